```python
import jax, jax.numpy as jnp
from jax import lax
import numpy as np

D_MODEL = 1024
BATCH = 8
SEQ = 16384
DEPTH = 4

N_MIXERS = 2
N_LAYERS_A = (DEPTH + N_MIXERS - 1) // N_MIXERS
N_LAYERS_B = DEPTH // N_MIXERS
D_FF = 2816
FFN_RESIDUAL_SCALE = 0.5
N_SUBLAYER_NORMS = 6
SGU_DIM = 2 * D_MODEL
SGU_GROUPS = 8
SGU_GROUP_DIM = SGU_DIM // SGU_GROUPS
CHUNK = 128
CONV_DIM = D_MODEL
CONV_WIDTH = 31
EPS = 1e-6

kernel_name = "hybrid_sgu_conformer_conv_macaron"


def rms_norm(x, g):
    xf = x.astype(jnp.float32)
    y = xf * lax.rsqrt(jnp.mean(xf * xf, axis=-1, keepdims=True) + EPS)
    return (y * g.astype(jnp.float32)).astype(x.dtype)


def layer_norm(x, g, b):
    xf = x.astype(jnp.float32)
    mu = jnp.mean(xf, axis=-1, keepdims=True)
    xc = xf - mu
    var = jnp.mean(xc * xc, axis=-1, keepdims=True)
    y = xc * lax.rsqrt(var + EPS) * g.astype(jnp.float32) + b.astype(jnp.float32)
    return y.astype(x.dtype)


def swiglu(h, w_gate, w_up, w_down):
    return (jax.nn.silu(h @ w_gate) * (h @ w_up)) @ w_down


def sgu_mixer(h, w_in, ln_g, ln_b, w_spatial, b_spatial, w_out):
    bsz, seq, _ = h.shape
    z = jax.nn.gelu(h @ w_in)
    u, v = jnp.split(z, 2, axis=-1)
    v = layer_norm(v, ln_g, ln_b)
    vc = v.reshape(bsz, seq // CHUNK, CHUNK, SGU_GROUPS, SGU_GROUP_DIM)
    causal = jnp.tril(jnp.ones((CHUNK, CHUNK), dtype=bool))
    ws = jnp.where(causal[None], w_spatial, 0)
    mixed = jnp.einsum('gts,bcsgd->bctgd', ws, vc)
    mixed = mixed + b_spatial.T[None, None, :, :, None]
    gated = u * mixed.reshape(bsz, seq, SGU_DIM)
    return gated @ w_out


def conv_mixer(h, w_pw1, w_dw, b_dw, ln_g, ln_b, w_pw2):
    a, gate = jnp.split(h @ w_pw1, 2, axis=-1)
    y = a * jax.nn.sigmoid(gate)
    y = lax.conv_general_dilated(
        y, w_dw[:, None, :].astype(y.dtype),
        window_strides=(1,),
        padding=[(CONV_WIDTH - 1, 0)],
        dimension_numbers=('NWC', 'WIO', 'NWC'),
        feature_group_count=CONV_DIM) + b_dw
    y = jax.nn.silu(layer_norm(y, ln_g, ln_b))
    return y @ w_pw2


def _fwd_setup_inputs(seed: int = 0) -> dict:
    key = jax.random.key(seed)
    ks = jax.random.split(key, 18)
    f32 = jnp.float32
    nrm = lambda k, shape, scale: (jax.random.normal(k, shape, f32) * scale).astype(f32)
    x = jax.random.normal(ks[0], (BATCH, SEQ, D_MODEL), f32)
    norm_g = 1.0 + nrm(ks[1], (DEPTH, N_SUBLAYER_NORMS, D_MODEL), 0.05)
    ff_w_gate = nrm(ks[2], (DEPTH, 2, D_MODEL, D_FF), D_MODEL ** -0.5)
    ff_w_up = nrm(ks[3], (DEPTH, 2, D_MODEL, D_FF), D_MODEL ** -0.5)
    ff_w_down = nrm(ks[4], (DEPTH, 2, D_FF, D_MODEL), D_FF ** -0.5)
    sgu_w_in = nrm(ks[5], (N_LAYERS_A, D_MODEL, 2 * SGU_DIM), D_MODEL ** -0.5)
    sgu_ln_g = 1.0 + nrm(ks[6], (N_LAYERS_A, SGU_DIM), 0.05)
    sgu_ln_b = nrm(ks[7], (N_LAYERS_A, SGU_DIM), 0.02)
    sgu_w_spatial = nrm(ks[8], (N_LAYERS_A, SGU_GROUPS, CHUNK, CHUNK), CHUNK ** -0.5)
    sgu_b_spatial = 1.0 + nrm(ks[9], (N_LAYERS_A, SGU_GROUPS, CHUNK), 0.1)
    sgu_w_out = nrm(ks[10], (N_LAYERS_A, SGU_DIM, D_MODEL), SGU_DIM ** -0.5)
    conv_w_pw1 = nrm(ks[11], (N_LAYERS_B, D_MODEL, 2 * CONV_DIM), D_MODEL ** -0.5)
    conv_w_dw = nrm(ks[12], (N_LAYERS_B, CONV_WIDTH, CONV_DIM), CONV_WIDTH ** -0.5)
    conv_b_dw = nrm(ks[13], (N_LAYERS_B, CONV_DIM), 0.02)
    conv_ln_g = 1.0 + nrm(ks[14], (N_LAYERS_B, CONV_DIM), 0.05)
    conv_ln_b = nrm(ks[15], (N_LAYERS_B, CONV_DIM), 0.02)
    conv_w_pw2 = nrm(ks[16], (N_LAYERS_B, CONV_DIM, D_MODEL), CONV_DIM ** -0.5)
    return {
        "x": x, "norm_g": norm_g,
        "ff_w_gate": ff_w_gate, "ff_w_up": ff_w_up, "ff_w_down": ff_w_down,
        "sgu_w_in": sgu_w_in, "sgu_ln_g": sgu_ln_g, "sgu_ln_b": sgu_ln_b,
        "sgu_w_spatial": sgu_w_spatial, "sgu_b_spatial": sgu_b_spatial, "sgu_w_out": sgu_w_out,
        "conv_w_pw1": conv_w_pw1, "conv_w_dw": conv_w_dw, "conv_b_dw": conv_b_dw,
        "conv_ln_g": conv_ln_g, "conv_ln_b": conv_ln_b, "conv_w_pw2": conv_w_pw2,
    }


def _fwd_reference(x, norm_g, ff_w_gate, ff_w_up, ff_w_down,
              sgu_w_in, sgu_ln_g, sgu_ln_b, sgu_w_spatial, sgu_b_spatial, sgu_w_out,
              conv_w_pw1, conv_w_dw, conv_b_dw, conv_ln_g, conv_ln_b, conv_w_pw2):
    for i in range(DEPTH):
        g = norm_g[i]
        h = swiglu(rms_norm(x, g[0]), ff_w_gate[i, 0], ff_w_up[i, 0], ff_w_down[i, 0])
        x = x + FFN_RESIDUAL_SCALE * rms_norm(h, g[1])
        hn = rms_norm(x, g[2])
        j = i // N_MIXERS
        if i % N_MIXERS == 0:
            m = sgu_mixer(hn, sgu_w_in[j], sgu_ln_g[j], sgu_ln_b[j],
                          sgu_w_spatial[j], sgu_b_spatial[j], sgu_w_out[j])
        else:
            m = conv_mixer(hn, conv_w_pw1[j], conv_w_dw[j], conv_b_dw[j],
                           conv_ln_g[j], conv_ln_b[j], conv_w_pw2[j])
        x = x + rms_norm(m, g[3])
        h = swiglu(rms_norm(x, g[4]), ff_w_gate[i, 1], ff_w_up[i, 1], ff_w_down[i, 1])
        x = x + FFN_RESIDUAL_SCALE * rms_norm(h, g[5])
    return x


import jax as _jax
import jax.numpy as _jnp

TWIN_FORMAT = 'train_step'
FWD_PARAMS = ['x', 'norm_g', 'ff_w_gate', 'ff_w_up', 'ff_w_down', 'sgu_w_in', 'sgu_ln_g', 'sgu_ln_b', 'sgu_w_spatial', 'sgu_b_spatial', 'sgu_w_out', 'conv_w_pw1', 'conv_w_dw', 'conv_b_dw', 'conv_ln_g', 'conv_ln_b', 'conv_w_pw2']
TWIN_WEIGHTS = ['norm_g', 'ff_w_gate', 'ff_w_up', 'ff_w_down', 'sgu_w_in', 'sgu_ln_g', 'sgu_ln_b', 'sgu_w_spatial', 'sgu_b_spatial', 'sgu_w_out', 'conv_w_pw1', 'conv_w_dw', 'conv_b_dw', 'conv_ln_g', 'conv_ln_b', 'conv_w_pw2']
TWIN_DIFF_INPUT = 'x'
TWIN_INPUTS = ['x', 'norm_g', 'ff_w_gate', 'ff_w_up', 'ff_w_down', 'sgu_w_in', 'sgu_ln_g', 'sgu_ln_b', 'sgu_w_spatial', 'sgu_b_spatial', 'sgu_w_out', 'conv_w_pw1', 'conv_w_dw', 'conv_b_dw', 'conv_ln_g', 'conv_ln_b', 'conv_w_pw2', 'loss_target', 'm_norm_g', 'm_ff_w_gate', 'm_ff_w_up', 'm_ff_w_down', 'm_sgu_w_in', 'm_sgu_ln_g', 'm_sgu_ln_b', 'm_sgu_w_spatial', 'm_sgu_b_spatial', 'm_sgu_w_out', 'm_conv_w_pw1', 'm_conv_w_dw', 'm_conv_b_dw', 'm_conv_ln_g', 'm_conv_ln_b', 'm_conv_w_pw2', 'v_norm_g', 'v_ff_w_gate', 'v_ff_w_up', 'v_ff_w_down', 'v_sgu_w_in', 'v_sgu_ln_g', 'v_sgu_ln_b', 'v_sgu_w_spatial', 'v_sgu_b_spatial', 'v_sgu_w_out', 'v_conv_w_pw1', 'v_conv_w_dw', 'v_conv_b_dw', 'v_conv_ln_g', 'v_conv_ln_b', 'v_conv_w_pw2']
TWIN_OUTPUTS = ['loss', 'grad_x', 'grad_norm_g', 'grad_ff_w_gate', 'grad_ff_w_up', 'grad_ff_w_down', 'grad_sgu_w_in', 'grad_sgu_ln_g', 'grad_sgu_ln_b', 'grad_sgu_w_spatial', 'grad_sgu_b_spatial', 'grad_sgu_w_out', 'grad_conv_w_pw1', 'grad_conv_w_dw', 'grad_conv_b_dw', 'grad_conv_ln_g', 'grad_conv_ln_b', 'grad_conv_w_pw2', 'delta_norm_g', 'delta_ff_w_gate', 'delta_ff_w_up', 'delta_ff_w_down', 'delta_sgu_w_in', 'delta_sgu_ln_g', 'delta_sgu_ln_b', 'delta_sgu_w_spatial', 'delta_sgu_b_spatial', 'delta_sgu_w_out', 'delta_conv_w_pw1', 'delta_conv_w_dw', 'delta_conv_b_dw', 'delta_conv_ln_g', 'delta_conv_ln_b', 'delta_conv_w_pw2', 'new_m_norm_g', 'new_m_ff_w_gate', 'new_m_ff_w_up', 'new_m_ff_w_down', 'new_m_sgu_w_in', 'new_m_sgu_ln_g', 'new_m_sgu_ln_b', 'new_m_sgu_w_spatial', 'new_m_sgu_b_spatial', 'new_m_sgu_w_out', 'new_m_conv_w_pw1', 'new_m_conv_w_dw', 'new_m_conv_b_dw', 'new_m_conv_ln_g', 'new_m_conv_ln_b', 'new_m_conv_w_pw2', 'new_v_norm_g', 'new_v_ff_w_gate', 'new_v_ff_w_up', 'new_v_ff_w_down', 'new_v_sgu_w_in', 'new_v_sgu_ln_g', 'new_v_sgu_ln_b', 'new_v_sgu_w_spatial', 'new_v_sgu_b_spatial', 'new_v_sgu_w_out', 'new_v_conv_w_pw1', 'new_v_conv_w_dw', 'new_v_conv_b_dw', 'new_v_conv_ln_g', 'new_v_conv_ln_b', 'new_v_conv_w_pw2']
TWIN_LEAF_KINDS = {'loss': 'loss', 'grad_x': 'grad_x', 'grad_norm_g': 'grad_w', 'grad_ff_w_gate': 'grad_w', 'grad_ff_w_up': 'grad_w', 'grad_ff_w_down': 'grad_w', 'grad_sgu_w_in': 'grad_w', 'grad_sgu_ln_g': 'grad_w', 'grad_sgu_ln_b': 'grad_w', 'grad_sgu_w_spatial': 'grad_w', 'grad_sgu_b_spatial': 'grad_w', 'grad_sgu_w_out': 'grad_w', 'grad_conv_w_pw1': 'grad_w', 'grad_conv_w_dw': 'grad_w', 'grad_conv_b_dw': 'grad_w', 'grad_conv_ln_g': 'grad_w', 'grad_conv_ln_b': 'grad_w', 'grad_conv_w_pw2': 'grad_w', 'delta_norm_g': 'delta_w', 'delta_ff_w_gate': 'delta_w', 'delta_ff_w_up': 'delta_w', 'delta_ff_w_down': 'delta_w', 'delta_sgu_w_in': 'delta_w', 'delta_sgu_ln_g': 'delta_w', 'delta_sgu_ln_b': 'delta_w', 'delta_sgu_w_spatial': 'delta_w', 'delta_sgu_b_spatial': 'delta_w', 'delta_sgu_w_out': 'delta_w', 'delta_conv_w_pw1': 'delta_w', 'delta_conv_w_dw': 'delta_w', 'delta_conv_b_dw': 'delta_w', 'delta_conv_ln_g': 'delta_w', 'delta_conv_ln_b': 'delta_w', 'delta_conv_w_pw2': 'delta_w', 'new_m_norm_g': 'new_m', 'new_m_ff_w_gate': 'new_m', 'new_m_ff_w_up': 'new_m', 'new_m_ff_w_down': 'new_m', 'new_m_sgu_w_in': 'new_m', 'new_m_sgu_ln_g': 'new_m', 'new_m_sgu_ln_b': 'new_m', 'new_m_sgu_w_spatial': 'new_m', 'new_m_sgu_b_spatial': 'new_m', 'new_m_sgu_w_out': 'new_m', 'new_m_conv_w_pw1': 'new_m', 'new_m_conv_w_dw': 'new_m', 'new_m_conv_b_dw': 'new_m', 'new_m_conv_ln_g': 'new_m', 'new_m_conv_ln_b': 'new_m', 'new_m_conv_w_pw2': 'new_m', 'new_v_norm_g': 'new_v', 'new_v_ff_w_gate': 'new_v', 'new_v_ff_w_up': 'new_v', 'new_v_ff_w_down': 'new_v', 'new_v_sgu_w_in': 'new_v', 'new_v_sgu_ln_g': 'new_v', 'new_v_sgu_ln_b': 'new_v', 'new_v_sgu_w_spatial': 'new_v', 'new_v_sgu_b_spatial': 'new_v', 'new_v_sgu_w_out': 'new_v', 'new_v_conv_w_pw1': 'new_v', 'new_v_conv_w_dw': 'new_v', 'new_v_conv_b_dw': 'new_v', 'new_v_conv_ln_g': 'new_v', 'new_v_conv_ln_b': 'new_v', 'new_v_conv_w_pw2': 'new_v'}


def _forward(args):
    return _fwd_reference(*[args[k] for k in FWD_PARAMS])


def _output_shape():
    def fwd():
        inp = _fwd_setup_inputs(0)
        return _fwd_reference(*[inp[k] for k in FWD_PARAMS])
    out = _jax.eval_shape(fwd)
    return out.shape, out.dtype

N_MICROBATCH = 1
ADAM_LR = 0.001
ADAM_B1 = 0.9
ADAM_B2 = 0.999
ADAM_EPS = 1e-08
ADAM_WD = 0.01
ADAM_STEP = 10
PER_EXAMPLE_BATCH_AXIS = {'x': 0, 'loss_target': 0}
SHARED_INPUTS = []
_WEIGHT_DTYPES = {'norm_g': _jnp.float32, 'ff_w_gate': _jnp.float32, 'ff_w_up': _jnp.float32, 'ff_w_down': _jnp.float32, 'sgu_w_in': _jnp.float32, 'sgu_ln_g': _jnp.float32, 'sgu_ln_b': _jnp.float32, 'sgu_w_spatial': _jnp.float32, 'sgu_b_spatial': _jnp.float32, 'sgu_w_out': _jnp.float32, 'conv_w_pw1': _jnp.float32, 'conv_w_dw': _jnp.float32, 'conv_b_dw': _jnp.float32, 'conv_ln_g': _jnp.float32, 'conv_ln_b': _jnp.float32, 'conv_w_pw2': _jnp.float32}
MOMENT_SCALE = {'norm_g': 5.851875e+01, 'ff_w_gate': 1.459611e+00, 'ff_w_up': 2.717076e+00, 'ff_w_down': 4.523064e+00, 'sgu_w_in': 4.074183e+00, 'sgu_ln_g': 7.787218e-01, 'sgu_ln_b': 9.181933e-01, 'sgu_w_spatial': 1.067764e+00, 'sgu_b_spatial': 2.324669e+00, 'sgu_w_out': 3.873029e+01, 'conv_w_pw1': 1.270341e+01, 'conv_w_dw': 1.979896e+01, 'conv_b_dw': 1.310405e+02, 'conv_ln_g': 5.804267e+01, 'conv_ln_b': 8.060577e+01, 'conv_w_pw2': 3.818326e+01}


def _to_microbatches(a, axis):
    t = _jnp.moveaxis(a, axis, 0)
    t = t.reshape((N_MICROBATCH, t.shape[0] // N_MICROBATCH) + t.shape[1:])
    return _jnp.moveaxis(t, 1, axis + 1)


def setup_inputs(seed: int = 0) -> dict:
    inp = _fwd_setup_inputs(seed)
    key = _jax.random.fold_in(_jax.random.key(seed), 7919)
    shape, _ = _output_shape()
    out = dict(inp)
    out["loss_target"] = _jax.random.normal(_jax.random.fold_in(key, 0), shape, _jnp.float32)
    for i, name in enumerate(TWIN_WEIGHTS):
        w = inp[name].astype(_jnp.float32)
        if MOMENT_SCALE is None:
            s = _jnp.sqrt(_jnp.mean(_jnp.square(w)) + 1e-30)
        else:
            s = MOMENT_SCALE[name]
        km, kv = _jax.random.split(_jax.random.fold_in(key, i + 1))
        out[name] = w
        out["m_" + name] = s * _jax.random.normal(km, w.shape, _jnp.float32)
        out["v_" + name] = (s * s) * _jax.random.uniform(kv, w.shape, _jnp.float32, 0.5, 1.5)
    if N_MICROBATCH > 1:
        for name, axis in PER_EXAMPLE_BATCH_AXIS.items():
            out[name] = _to_microbatches(out[name], axis)
    return {'x': out['x'], 'norm_g': out['norm_g'], 'ff_w_gate': out['ff_w_gate'], 'ff_w_up': out['ff_w_up'], 'ff_w_down': out['ff_w_down'], 'sgu_w_in': out['sgu_w_in'], 'sgu_ln_g': out['sgu_ln_g'], 'sgu_ln_b': out['sgu_ln_b'], 'sgu_w_spatial': out['sgu_w_spatial'], 'sgu_b_spatial': out['sgu_b_spatial'], 'sgu_w_out': out['sgu_w_out'], 'conv_w_pw1': out['conv_w_pw1'], 'conv_w_dw': out['conv_w_dw'], 'conv_b_dw': out['conv_b_dw'], 'conv_ln_g': out['conv_ln_g'], 'conv_ln_b': out['conv_ln_b'], 'conv_w_pw2': out['conv_w_pw2'], 'loss_target': out['loss_target'], 'm_norm_g': out['m_norm_g'], 'm_ff_w_gate': out['m_ff_w_gate'], 'm_ff_w_up': out['m_ff_w_up'], 'm_ff_w_down': out['m_ff_w_down'], 'm_sgu_w_in': out['m_sgu_w_in'], 'm_sgu_ln_g': out['m_sgu_ln_g'], 'm_sgu_ln_b': out['m_sgu_ln_b'], 'm_sgu_w_spatial': out['m_sgu_w_spatial'], 'm_sgu_b_spatial': out['m_sgu_b_spatial'], 'm_sgu_w_out': out['m_sgu_w_out'], 'm_conv_w_pw1': out['m_conv_w_pw1'], 'm_conv_w_dw': out['m_conv_w_dw'], 'm_conv_b_dw': out['m_conv_b_dw'], 'm_conv_ln_g': out['m_conv_ln_g'], 'm_conv_ln_b': out['m_conv_ln_b'], 'm_conv_w_pw2': out['m_conv_w_pw2'], 'v_norm_g': out['v_norm_g'], 'v_ff_w_gate': out['v_ff_w_gate'], 'v_ff_w_up': out['v_ff_w_up'], 'v_ff_w_down': out['v_ff_w_down'], 'v_sgu_w_in': out['v_sgu_w_in'], 'v_sgu_ln_g': out['v_sgu_ln_g'], 'v_sgu_ln_b': out['v_sgu_ln_b'], 'v_sgu_w_spatial': out['v_sgu_w_spatial'], 'v_sgu_b_spatial': out['v_sgu_b_spatial'], 'v_sgu_w_out': out['v_sgu_w_out'], 'v_conv_w_pw1': out['v_conv_w_pw1'], 'v_conv_w_dw': out['v_conv_w_dw'], 'v_conv_b_dw': out['v_conv_b_dw'], 'v_conv_ln_g': out['v_conv_ln_g'], 'v_conv_ln_b': out['v_conv_ln_b'], 'v_conv_w_pw2': out['v_conv_w_pw2']}


def _loss(weights, diff, rest, loss_target):
    with _jax.named_scope("forward"):
        args = {**rest, TWIN_DIFF_INPUT: diff, **{k: w.astype(_WEIGHT_DTYPES[k]) for k, w in weights.items()}}
        y = _forward(args)
    with _jax.named_scope("loss_head"):
        err = _jnp.square(y.astype(_jnp.float32) - loss_target)
        return 0.5 * _jnp.sum(_jnp.mean(err, axis=-1)) if err.ndim else 0.5 * err


def _adamw(w, g, m, v):
    m = ADAM_B1 * m + (1.0 - ADAM_B1) * g
    v = ADAM_B2 * v + (1.0 - ADAM_B2) * _jnp.square(g)
    m_hat = m / (1.0 - ADAM_B1 ** ADAM_STEP)
    v_hat = v / (1.0 - ADAM_B2 ** ADAM_STEP)
    delta = -ADAM_LR * (m_hat / (_jnp.sqrt(v_hat) + ADAM_EPS) + ADAM_WD * w)
    return delta, m, v


def reference(x, norm_g, ff_w_gate, ff_w_up, ff_w_down, sgu_w_in, sgu_ln_g, sgu_ln_b, sgu_w_spatial, sgu_b_spatial, sgu_w_out, conv_w_pw1, conv_w_dw, conv_b_dw, conv_ln_g, conv_ln_b, conv_w_pw2, loss_target, m_norm_g, m_ff_w_gate, m_ff_w_up, m_ff_w_down, m_sgu_w_in, m_sgu_ln_g, m_sgu_ln_b, m_sgu_w_spatial, m_sgu_b_spatial, m_sgu_w_out, m_conv_w_pw1, m_conv_w_dw, m_conv_b_dw, m_conv_ln_g, m_conv_ln_b, m_conv_w_pw2, v_norm_g, v_ff_w_gate, v_ff_w_up, v_ff_w_down, v_sgu_w_in, v_sgu_ln_g, v_sgu_ln_b, v_sgu_w_spatial, v_sgu_b_spatial, v_sgu_w_out, v_conv_w_pw1, v_conv_w_dw, v_conv_b_dw, v_conv_ln_g, v_conv_ln_b, v_conv_w_pw2):
    given = dict(x=x, norm_g=norm_g, ff_w_gate=ff_w_gate, ff_w_up=ff_w_up, ff_w_down=ff_w_down, sgu_w_in=sgu_w_in, sgu_ln_g=sgu_ln_g, sgu_ln_b=sgu_ln_b, sgu_w_spatial=sgu_w_spatial, sgu_b_spatial=sgu_b_spatial, sgu_w_out=sgu_w_out, conv_w_pw1=conv_w_pw1, conv_w_dw=conv_w_dw, conv_b_dw=conv_b_dw, conv_ln_g=conv_ln_g, conv_ln_b=conv_ln_b, conv_w_pw2=conv_w_pw2, loss_target=loss_target, m_norm_g=m_norm_g, m_ff_w_gate=m_ff_w_gate, m_ff_w_up=m_ff_w_up, m_ff_w_down=m_ff_w_down, m_sgu_w_in=m_sgu_w_in, m_sgu_ln_g=m_sgu_ln_g, m_sgu_ln_b=m_sgu_ln_b, m_sgu_w_spatial=m_sgu_w_spatial, m_sgu_b_spatial=m_sgu_b_spatial, m_sgu_w_out=m_sgu_w_out, m_conv_w_pw1=m_conv_w_pw1, m_conv_w_dw=m_conv_w_dw, m_conv_b_dw=m_conv_b_dw, m_conv_ln_g=m_conv_ln_g, m_conv_ln_b=m_conv_ln_b, m_conv_w_pw2=m_conv_w_pw2, v_norm_g=v_norm_g, v_ff_w_gate=v_ff_w_gate, v_ff_w_up=v_ff_w_up, v_ff_w_down=v_ff_w_down, v_sgu_w_in=v_sgu_w_in, v_sgu_ln_g=v_sgu_ln_g, v_sgu_ln_b=v_sgu_ln_b, v_sgu_w_spatial=v_sgu_w_spatial, v_sgu_b_spatial=v_sgu_b_spatial, v_sgu_w_out=v_sgu_w_out, v_conv_w_pw1=v_conv_w_pw1, v_conv_w_dw=v_conv_w_dw, v_conv_b_dw=v_conv_b_dw, v_conv_ln_g=v_conv_ln_g, v_conv_ln_b=v_conv_ln_b, v_conv_w_pw2=v_conv_w_pw2)
    weights = {n: given[n] for n in TWIN_WEIGHTS}
    shared = {n: given[n] for n in SHARED_INPUTS}
    per_example = {n: given[n] for n in ['x']}
    grad_fn = _jax.value_and_grad(_loss, argnums=(0, 1))

    def one_microbatch(ex, loss_target):
        ex = dict(ex)
        diff = ex.pop(TWIN_DIFF_INPUT)
        return grad_fn(weights, diff, {**shared, **ex}, loss_target)

    if N_MICROBATCH == 1:
        loss, (grad_w, grad_x) = one_microbatch(per_example, given["loss_target"])
    else:
        def body(carry, xs):
            loss_sum, grad_sum = carry
            l_k, (gw_k, gx_k) = one_microbatch(xs[0], xs[1])
            with _jax.named_scope("update"):
                return (loss_sum + l_k, _jax.tree.map(_jnp.add, grad_sum, gw_k)), gx_k

        init = (_jnp.zeros((), _jnp.float32), _jax.tree.map(_jnp.zeros_like, weights))
        (loss, grad_w), grad_x = _jax.lax.scan(body, init, (per_example, given["loss_target"]))
    with _jax.named_scope("update"):
        delta_w, new_m, new_v = {}, {}, {}
        for n in TWIN_WEIGHTS:
            delta_w[n], new_m[n], new_v[n] = _adamw(weights[n], grad_w[n], given["m_" + n], given["v_" + n])
    return (loss, grad_x, *[grad_w[n] for n in TWIN_WEIGHTS], *[delta_w[n] for n in TWIN_WEIGHTS],
            *[new_m[n] for n in TWIN_WEIGHTS], *[new_v[n] for n in TWIN_WEIGHTS])
```

```python
import functools

import jax
import jax.numpy as jnp
from jax import lax
from jax.experimental import pallas as pl
from jax.experimental.pallas import tpu as pltpu

F32 = jnp.float32
BF16 = jnp.bfloat16
EPS = 1e-6
N_CHIPS = 4
N_DEV = 8
N_GROUPS = 8
CHUNK = 128
CONV_W = 31
HALO = 32
CONV_RB = 64
CONV_CB = 256
VMEM_LIMIT_V7X = 56 * 1024 * 1024
MESH = pl.DeviceIdType.MESH

ADAM_LR = 0.001
ADAM_B1 = 0.9
ADAM_B2 = 0.999
ADAM_EPS = 1e-08
ADAM_WD = 0.01
ADAM_STEP = 10
FFN_SCALE = 0.5


def _cparams(*sem):
    return pltpu.CompilerParams(dimension_semantics=sem, vmem_limit_bytes=VMEM_LIMIT_V7X)


def _resident(shape):
    return pl.BlockSpec(shape, lambda *_: (0,) * len(shape), pipeline_mode=pl.Buffered(1))


def _dot(a, b):
    return jnp.dot(a, b, preferred_element_type=F32)


def _dot_nt(a, b):
    return lax.dot_general(a, b, (((1,), (1,)), ((), ())), preferred_element_type=F32)


def _dot_tn(a, b):
    return lax.dot_general(a, b, (((0,), (0,)), ((), ())), preferred_element_type=F32)


def _rms_stats(x):
    r = lax.rsqrt(jnp.mean(x * x, axis=-1, keepdims=True) + EPS)
    return x * r, r


def _rms_bwd(xh, r, g, dy):
    dxh = dy * g
    dx = r * (dxh - xh * jnp.mean(dxh * xh, axis=-1, keepdims=True))
    return dx, jnp.sum(dy * xh, axis=0, keepdims=True)


def _ln_stats(parts, width):
    mu = sum(jnp.sum(p, axis=-1, keepdims=True) for p in parts) / width
    cen = [p - mu for p in parts]
    var = sum(jnp.sum(c * c, axis=-1, keepdims=True) for c in cen) / width
    rstd = lax.rsqrt(var + EPS)
    return [c * rstd for c in cen], rstd


def _ln_bwd(vh_parts, rstd, dvh_parts, width):
    m1 = sum(jnp.sum(d, axis=-1, keepdims=True) for d in dvh_parts) / width
    m2 = sum(jnp.sum(d * v, axis=-1, keepdims=True) for d, v in zip(dvh_parts, vh_parts)) / width
    return [rstd * (d - m1 - v * m2) for d, v in zip(dvh_parts, vh_parts)]


_GELU_C = 0.7978845608028654
_GELU_A = 0.044715


def _gelu(x):
    return 0.5 * x * (1.0 + jnp.tanh(_GELU_C * (x + _GELU_A * x * x * x)))


def _gelu_grad(x):
    t = jnp.tanh(_GELU_C * (x + _GELU_A * x * x * x))
    return 0.5 * (1.0 + t) + 0.5 * x * (1.0 - t * t) * _GELU_C * (1.0 + 3.0 * _GELU_A * x * x)


def _acc_out(ref, first, val):
    @pl.when(first)
    def _():
        ref[...] = val

    @pl.when(jnp.logical_not(first))
    def _():
        ref[...] += val


def _ffn_fwd(x, g_pre, g_post, wg, wu, wd, tm):
    T, D = x.shape
    nj, _, F = wg.shape
    tm = min(tm, T)

    def body(x_ref, gpre_ref, gpost_ref, wg_ref, wu_ref, wd_ref, xo_ref, a_ref, b_ref, f_ref, h_scr, acc_scr):
        j = pl.program_id(1)

        @pl.when(j == 0)
        def _():
            xh, _ = _rms_stats(x_ref[...])
            h_scr[...] = (xh * gpre_ref[...]).astype(BF16)
            acc_scr[...] = jnp.zeros_like(acc_scr)

        h = h_scr[...]
        a = _dot(h, wg_ref[...])
        b = _dot(h, wu_ref[...])
        a_ref[...] = a.astype(BF16)
        b_ref[...] = b.astype(BF16)
        s = (a * jax.nn.sigmoid(a)) * b
        acc_scr[...] += _dot(s.astype(BF16), wd_ref[...])

        @pl.when(j == nj - 1)
        def _():
            f = acc_scr[...]
            f_ref[...] = f
            fh, _ = _rms_stats(f)
            xo_ref[...] = x_ref[...] + FFN_SCALE * (fh * gpost_ref[...])

    row = pl.BlockSpec((tm, D), lambda i, j: (i, 0))
    vec = pl.BlockSpec((1, D), lambda i, j: (0, 0))
    w_in = pl.BlockSpec((None, D, F), lambda i, j: (j, 0, 0))
    w_out = pl.BlockSpec((None, F, D), lambda i, j: (j, 0, 0))
    act = pl.BlockSpec((None, tm, F), lambda i, j: (j, i, 0))
    return pl.pallas_call(
        body,
        name="ffn_fwd",
        grid=(T // tm, nj),
        in_specs=[row, vec, vec, w_in, w_in, w_out],
        out_specs=[row, act, act, row],
        out_shape=[
            jax.ShapeDtypeStruct((T, D), F32),
            jax.ShapeDtypeStruct((nj, T, F), BF16),
            jax.ShapeDtypeStruct((nj, T, F), BF16),
            jax.ShapeDtypeStruct((T, D), F32),
        ],
        scratch_shapes=[pltpu.VMEM((tm, D), BF16), pltpu.VMEM((tm, D), F32)],
        compiler_params=_cparams("parallel", "arbitrary"),
    )(x, g_pre, g_post, wg, wu, wd)


def _ffn_bwd(dy, x, f, a, b, g_pre, g_post, wg, wu, wd, tm):
    T, D = x.shape
    nj, _, F = wg.shape
    tm = min(tm, T)

    def body(dy_ref, x_ref, f_ref, a_ref, b_ref, gpre_ref, gpost_ref, wg_ref, wu_ref, wd_ref,
             dx_ref, h_ref, dz_ref, s_ref, da_ref, db_ref, dgpre_ref, dgpost_ref, dh_scr):
        i = pl.program_id(0)
        j = pl.program_id(1)

        @pl.when(j == 0)
        def _():
            fh, rf = _rms_stats(f_ref[...])
            dz, dg = _rms_bwd(fh, rf, gpost_ref[...], FFN_SCALE * dy_ref[...])
            dz_ref[...] = dz.astype(BF16)
            _acc_out(dgpost_ref, i == 0, dg)
            xh, _ = _rms_stats(x_ref[...])
            h_ref[...] = (xh * gpre_ref[...]).astype(BF16)
            dh_scr[...] = jnp.zeros_like(dh_scr)

        ds = _dot_nt(dz_ref[...], wd_ref[...])
        av = a_ref[...].astype(F32)
        bv = b_ref[...].astype(F32)
        sg = jax.nn.sigmoid(av)
        sl = av * sg
        s_ref[...] = (sl * bv).astype(BF16)
        da = (ds * bv * (sg * (1.0 + av * (1.0 - sg)))).astype(BF16)
        db = (ds * sl).astype(BF16)
        da_ref[...] = da
        db_ref[...] = db
        dh_scr[...] += _dot_nt(da, wg_ref[...]) + _dot_nt(db, wu_ref[...])

        @pl.when(j == nj - 1)
        def _():
            xh, rx = _rms_stats(x_ref[...])
            dxn, dg = _rms_bwd(xh, rx, gpre_ref[...], dh_scr[...])
            dx_ref[...] = dy_ref[...] + dxn
            _acc_out(dgpre_ref, i == 0, dg)

    row = pl.BlockSpec((tm, D), lambda i, j: (i, 0))
    vec = pl.BlockSpec((1, D), lambda i, j: (0, 0))
    w_in = pl.BlockSpec((None, D, F), lambda i, j: (j, 0, 0))
    w_out = pl.BlockSpec((None, F, D), lambda i, j: (j, 0, 0))
    act = pl.BlockSpec((None, tm, F), lambda i, j: (j, i, 0))
    act_shape = jax.ShapeDtypeStruct((nj, T, F), BF16)
    return pl.pallas_call(
        body,
        name="ffn_bwd",
        grid=(T // tm, nj),
        in_specs=[row, row, row, act, act, vec, vec, w_in, w_in, w_out],
        out_specs=[row, row, row, act, act, act, vec, vec],
        out_shape=[
            jax.ShapeDtypeStruct((T, D), F32),
            jax.ShapeDtypeStruct((T, D), BF16),
            jax.ShapeDtypeStruct((T, D), BF16),
            act_shape, act_shape, act_shape,
            jax.ShapeDtypeStruct((1, D), F32),
            jax.ShapeDtypeStruct((1, D), F32),
        ],
        scratch_shapes=[pltpu.VMEM((tm, D), F32)],
        compiler_params=_cparams("arbitrary", "arbitrary"),
    )(dy, x, f, a, b, g_pre, g_post, wg, wu, wd)


def _tn_matmul(a, b, tk):
    a_chunked = a.ndim == 3
    nj = a.shape[0] if a_chunked else b.shape[0]
    T, M, N = a.shape[-2], a.shape[-1], b.shape[-1]
    tk = min(tk, T)
    nk = T // tk

    def body(a_ref, b_ref, o_ref, acc_scr):
        k = pl.program_id(1)

        @pl.when(k == 0)
        def _():
            acc_scr[...] = jnp.zeros_like(acc_scr)

        acc_scr[...] += _dot_tn(a_ref[...], b_ref[...])

        @pl.when(k == nk - 1)
        def _():
            o_ref[...] = acc_scr[...].astype(BF16)

    def spec(chunked, width):
        if chunked:
            return pl.BlockSpec((None, tk, width), lambda j, k: (j, k, 0))
        return pl.BlockSpec((tk, width), lambda j, k: (k, 0))

    return pl.pallas_call(
        body,
        name="tn_matmul",
        grid=(nj, nk),
        in_specs=[spec(a_chunked, M), spec(not a_chunked, N)],
        out_specs=pl.BlockSpec((None, M, N), lambda j, k: (j, 0, 0)),
        out_shape=jax.ShapeDtypeStruct((nj, M, N), BF16),
        scratch_shapes=[pltpu.VMEM((M, N), F32)],
        compiler_params=_cparams("parallel", "arbitrary"),
    )(a, b)


def _sgu_fwd(x, g_pre, g_post, win, lng, lnb, wsm, bsb, wout, tm):
    T, D = x.shape
    nc, _, E = win.shape
    S = 2 * E
    dg = S // N_GROUPS
    wo_rows = wout.shape[1]
    tm = min(tm, T)
    nq = tm // CHUNK

    def body(x_ref, gpre_ref, gpost_ref, win_ref, lng_ref, lnb_ref, ws_ref, bsb_ref, wout_ref,
             xo_ref, zp_ref, m_ref, u_scr, vn_scr, gt_scr):
        x = x_ref[...]
        xh, _ = _rms_stats(x)
        hn = (xh * gpre_ref[...]).astype(BF16)
        v_parts = []
        for c in range(nc):
            zp = _dot(hn, win_ref[c])
            zp_ref[c] = zp.astype(BF16)
            z = _gelu(zp)
            if c < nc // 2:
                u_scr[:, c * E:(c + 1) * E] = z
            else:
                v_parts.append(z)
        vh_parts, _ = _ln_stats(v_parts, S)
        for c, vh in enumerate(vh_parts):
            cols = slice(c * E, (c + 1) * E)
            vn_scr[:, cols] = (vh * lng_ref[:, cols] + lnb_ref[:, cols]).astype(BF16)
        for q in range(nq):
            rows = slice(q * CHUNK, (q + 1) * CHUNK)
            for g in range(N_GROUPS):
                cols = slice(g * dg, (g + 1) * dg)
                mixed = _dot(ws_ref[g], vn_scr[rows, cols]) + bsb_ref[g]
                gt_scr[rows, cols] = (u_scr[rows, cols] * mixed).astype(BF16)
        m = _dot(gt_scr[:, 0:wo_rows], wout_ref[0])
        for c in range(1, nc):
            m += _dot(gt_scr[:, c * wo_rows:(c + 1) * wo_rows], wout_ref[c])
        m_ref[...] = m
        mh, _ = _rms_stats(m)
        xo_ref[...] = x + mh * gpost_ref[...]

    row = pl.BlockSpec((tm, D), lambda i: (i, 0))
    return pl.pallas_call(
        body,
        name="sgu_fwd",
        grid=(T // tm,),
        in_specs=[row, _resident((1, D)), _resident((1, D)), _resident(win.shape), _resident((1, S)),
                  _resident((1, S)), _resident(wsm.shape), _resident(bsb.shape), _resident(wout.shape)],
        out_specs=[row, pl.BlockSpec((nc, tm, E), lambda i: (0, i, 0)), row],
        out_shape=[
            jax.ShapeDtypeStruct((T, D), F32),
            jax.ShapeDtypeStruct((nc, T, E), BF16),
            jax.ShapeDtypeStruct((T, D), F32),
        ],
        scratch_shapes=[pltpu.VMEM((tm, S), F32), pltpu.VMEM((tm, S), BF16), pltpu.VMEM((tm, S), BF16)],
        compiler_params=_cparams("parallel"),
    )(x, g_pre, g_post, win, lng, lnb, wsm, bsb, wout)


def _sgu_bwd(dy, x, m, zp, g_pre, g_post, win, lng, lnb, wsm, wsmt, bsb, wout, tm):
    T, D = x.shape
    nc, _, E = win.shape
    S = 2 * E
    dg = S // N_GROUPS
    wo_rows = wout.shape[1]
    tm = min(tm, T)
    nq = tm // CHUNK

    def body(dy_ref, x_ref, m_ref, zp_ref, gpre_ref, gpost_ref, win_ref, lng_ref, lnb_ref, ws_ref, wst_ref,
             bsb_ref, wout_ref,
             dx_ref, hn_ref, dzp_ref, gated_ref, dm_ref, dws_ref, dbs_ref, dlng_ref, dlnb_ref, dgpre_ref,
             dgpost_ref, u_scr, d_scr, vh_scr, vn_scr):
        first = pl.program_id(0) == 0
        dy = dy_ref[...]
        mh, rm = _rms_stats(m_ref[...])
        dm, dgp = _rms_bwd(mh, rm, gpost_ref[...], dy)
        _acc_out(dgpost_ref, first, dgp)
        dm = dm.astype(BF16)
        dm_ref[...] = dm
        for c in range(nc):
            d_scr[:, c * wo_rows:(c + 1) * wo_rows] = _dot_nt(dm, wout_ref[c])
        v_parts = []
        for c in range(nc):
            z = _gelu(zp_ref[c].astype(F32))
            if c < nc // 2:
                u_scr[:, c * E:(c + 1) * E] = z
            else:
                v_parts.append(z)
        vh_parts, rstd = _ln_stats(v_parts, S)
        for c, vh in enumerate(vh_parts):
            cols = slice(c * E, (c + 1) * E)
            vh_scr[:, cols] = vh
            vn_scr[:, cols] = (vh * lng_ref[:, cols] + lnb_ref[:, cols]).astype(BF16)

        @pl.when(first)
        def _():
            dws_ref[...] = jnp.zeros_like(dws_ref)
            dbs_ref[...] = jnp.zeros_like(dbs_ref)
            dlng_ref[...] = jnp.zeros_like(dlng_ref)
            dlnb_ref[...] = jnp.zeros_like(dlnb_ref)

        for q in range(nq):
            rows = slice(q * CHUNK, (q + 1) * CHUNK)
            for g in range(N_GROUPS):
                cols = slice(g * dg, (g + 1) * dg)
                vn = vn_scr[rows, cols]
                mixed = _dot(ws_ref[g], vn) + bsb_ref[g]
                u = u_scr[rows, cols]
                dgt = d_scr[rows, cols]
                gated_ref[(g * dg) // wo_rows, rows, (g * dg) % wo_rows:(g * dg) % wo_rows + dg] = (u * mixed).astype(BF16)
                dmix = dgt * u
                dbs_ref[:, cols] += dmix
                dmix = dmix.astype(BF16)
                dws_ref[g] += _dot_nt(dmix, vn)
                u_scr[rows, cols] = dgt * mixed
                d_scr[rows, cols] = _dot(wst_ref[g], dmix)
        dvn = [d_scr[:, c * E:(c + 1) * E] for c in range(nc // 2)]
        vh = [vh_scr[:, c * E:(c + 1) * E] for c in range(nc // 2)]
        for c, (d, v) in enumerate(zip(dvn, vh)):
            dlng_ref[:, c * E:(c + 1) * E] += jnp.sum(d * v, axis=0, keepdims=True)
            dlnb_ref[:, c * E:(c + 1) * E] += jnp.sum(d, axis=0, keepdims=True)
        dvh = [d * lng_ref[:, c * E:(c + 1) * E] for c, d in enumerate(dvn)]
        dv = _ln_bwd(vh, rstd, dvh, S)
        dhn = None
        for c in range(nc):
            dz = u_scr[:, c * E:(c + 1) * E] if c < nc // 2 else dv[c - nc // 2]
            dzp = (dz * _gelu_grad(zp_ref[c].astype(F32))).astype(BF16)
            dzp_ref[c] = dzp
            t = _dot_nt(dzp, win_ref[c])
            dhn = t if dhn is None else dhn + t
        xh, rx = _rms_stats(x_ref[...])
        hn_ref[...] = (xh * gpre_ref[...]).astype(BF16)
        dxn, dgq = _rms_bwd(xh, rx, gpre_ref[...], dhn)
        dx_ref[...] = dy + dxn
        _acc_out(dgpre_ref, first, dgq)

    row = pl.BlockSpec((tm, D), lambda i: (i, 0))

    def whole(shape):
        return pl.BlockSpec(shape, lambda i: (0,) * len(shape))

    return pl.pallas_call(
        body,
        name="sgu_bwd",
        grid=(T // tm,),
        in_specs=[row, row, row, pl.BlockSpec((nc, tm, E), lambda i: (0, i, 0)), _resident((1, D)), _resident((1, D)),
                  _resident(win.shape), _resident((1, S)), _resident((1, S)), _resident(wsm.shape),
                  _resident(wsmt.shape), _resident(bsb.shape), _resident(wout.shape)],
        out_specs=[row, row, pl.BlockSpec((nc, tm, E), lambda i: (0, i, 0)),
                   pl.BlockSpec((nc, tm, wo_rows), lambda i: (0, i, 0)), row,
                   whole((N_GROUPS, CHUNK, CHUNK)), whole((CHUNK, S)), whole((1, S)), whole((1, S)),
                   whole((1, D)), whole((1, D))],
        out_shape=[
            jax.ShapeDtypeStruct((T, D), F32),
            jax.ShapeDtypeStruct((T, D), BF16),
            jax.ShapeDtypeStruct((nc, T, E), BF16),
            jax.ShapeDtypeStruct((nc, T, wo_rows), BF16),
            jax.ShapeDtypeStruct((T, D), BF16),
            jax.ShapeDtypeStruct((N_GROUPS, CHUNK, CHUNK), F32),
            jax.ShapeDtypeStruct((CHUNK, S), F32),
            jax.ShapeDtypeStruct((1, S), F32),
            jax.ShapeDtypeStruct((1, S), F32),
            jax.ShapeDtypeStruct((1, D), F32),
            jax.ShapeDtypeStruct((1, D), F32),
        ],
        scratch_shapes=[pltpu.VMEM((tm, S), F32), pltpu.VMEM((tm, S), F32), pltpu.VMEM((tm, S), F32),
                        pltpu.VMEM((tm, S), BF16)],
        compiler_params=_cparams("arbitrary"),
    )(dy, x, m, zp, g_pre, g_post, win, lng, lnb, wsm, wsmt, bsb, wout)


def _conv_fwd_a(x, g_pre, wpw1, tm):
    T, D = x.shape
    nc, _, E = wpw1.shape
    C = 2 * E
    tm = min(tm, T)

    def body(x_ref, gpre_ref, w_ref, y_ref, p_ref):
        xh, _ = _rms_stats(x_ref[...])
        hn = (xh * gpre_ref[...]).astype(BF16)
        ps = []
        for c in range(nc):
            p = _dot(hn, w_ref[c])
            p_ref[c] = p.astype(BF16)
            ps.append(p)
        for c in range(nc // 2):
            y_ref[:, c * E:(c + 1) * E] = ps[c] * jax.nn.sigmoid(ps[c + nc // 2])

    row = pl.BlockSpec((tm, D), lambda i: (i, 0))
    return pl.pallas_call(
        body,
        name="conv_fwd_a",
        grid=(T // tm,),
        in_specs=[row, _resident((1, D)), _resident(wpw1.shape)],
        out_specs=[pl.BlockSpec((tm, C), lambda i: (i, 0)), pl.BlockSpec((nc, tm, E), lambda i: (0, i, 0))],
        out_shape=[jax.ShapeDtypeStruct((T, C), F32), jax.ShapeDtypeStruct((nc, T, E), BF16)],
        compiler_params=_cparams("parallel"),
    )(x, g_pre, wpw1)


def _conv_fwd_b(x, y, wdw, bdw, lng, lnb, wpw2, g_post, tm):
    T, D = x.shape
    C = y.shape[1]
    nc, E, _ = wpw2.shape
    tm = min(tm, T)
    per = tm // HALO

    def body(x_ref, y_ref, yprev_ref, wdw_ref, bdw_ref, lng_ref, lnb_ref, w_ref, gpost_ref,
             xo_ref, c_ref, m_ref, ybuf):
        i = pl.program_id(0)
        ybuf[0:HALO, :] = jnp.where(i > 0, yprev_ref[...], 0.0)
        ybuf[HALO:HALO + tm, :] = y_ref[...]
        off = HALO - (CONV_W - 1)
        for r0 in range(0, tm, CONV_RB):
            for c0 in range(0, C, CONV_CB):
                cols = slice(c0, c0 + CONV_CB)
                acc = jnp.broadcast_to(bdw_ref[:, cols], (CONV_RB, CONV_CB))
                for k in range(CONV_W):
                    acc = acc + wdw_ref[k:k + 1, cols] * ybuf[r0 + off + k:r0 + off + k + CONV_RB, cols]
                c_ref[r0:r0 + CONV_RB, cols] = acc
        (ch,), _ = _ln_stats([c_ref[...]], C)
        cn = ch * lng_ref[...] + lnb_ref[...]
        qv = (cn * jax.nn.sigmoid(cn)).astype(BF16)
        m = _dot(qv[:, 0:E], w_ref[0])
        for c in range(1, nc):
            m += _dot(qv[:, c * E:(c + 1) * E], w_ref[c])
        m_ref[...] = m
        mh, _ = _rms_stats(m)
        xo_ref[...] = x_ref[...] + mh * gpost_ref[...]

    row = pl.BlockSpec((tm, D), lambda i: (i, 0))
    crow = pl.BlockSpec((tm, C), lambda i: (i, 0))
    prev = pl.BlockSpec((HALO, C), lambda i: (jnp.maximum(i * per - 1, 0), 0))
    return pl.pallas_call(
        body,
        name="conv_fwd_b",
        grid=(T // tm,),
        in_specs=[row, crow, prev, _resident(wdw.shape), _resident((1, C)), _resident((1, C)), _resident((1, C)),
                  _resident(wpw2.shape), _resident((1, D))],
        out_specs=[row, crow, row],
        out_shape=[jax.ShapeDtypeStruct((T, D), F32), jax.ShapeDtypeStruct((T, C), F32),
                   jax.ShapeDtypeStruct((T, D), F32)],
        scratch_shapes=[pltpu.VMEM((HALO + tm, C), F32)],
        compiler_params=_cparams("parallel"),
    )(x, y, y, wdw, bdw, lng, lnb, wpw2, g_post)


def _conv_bwd_b(dy, m, c, lng, lnb, wpw2, g_post, tm):
    T, D = dy.shape
    C = c.shape[1]
    nc, E, _ = wpw2.shape
    tm = min(tm, T)

    def body(dy_ref, m_ref, c_ref, lng_ref, lnb_ref, w_ref, gpost_ref,
             dm_ref, q_ref, dc_ref, dlng_ref, dlnb_ref, dbdw_ref, dgpost_ref, dq_scr):
        first = pl.program_id(0) == 0
        mh, rm = _rms_stats(m_ref[...])
        dm, dgp = _rms_bwd(mh, rm, gpost_ref[...], dy_ref[...])
        _acc_out(dgpost_ref, first, dgp)
        dm = dm.astype(BF16)
        dm_ref[...] = dm
        for k in range(nc):
            dq_scr[:, k * E:(k + 1) * E] = _dot_nt(dm, w_ref[k])
        (ch,), rstd = _ln_stats([c_ref[...]], C)
        cn = ch * lng_ref[...] + lnb_ref[...]
        sg = jax.nn.sigmoid(cn)
        qv = (cn * sg).astype(BF16)
        for k in range(nc):
            q_ref[k] = qv[:, k * E:(k + 1) * E]
        dcn = dq_scr[...] * (sg * (1.0 + cn * (1.0 - sg)))
        _acc_out(dlng_ref, first, jnp.sum(dcn * ch, axis=0, keepdims=True))
        _acc_out(dlnb_ref, first, jnp.sum(dcn, axis=0, keepdims=True))
        (dc,) = _ln_bwd([ch], rstd, [dcn * lng_ref[...]], C)
        dc_ref[...] = dc
        _acc_out(dbdw_ref, first, jnp.sum(dc, axis=0, keepdims=True))

    row = pl.BlockSpec((tm, D), lambda i: (i, 0))
    crow = pl.BlockSpec((tm, C), lambda i: (i, 0))

    def whole(shape):
        return pl.BlockSpec(shape, lambda i: (0,) * len(shape))

    return pl.pallas_call(
        body,
        name="conv_bwd_b",
        grid=(T // tm,),
        in_specs=[row, row, crow, _resident((1, C)), _resident((1, C)), _resident(wpw2.shape), _resident((1, D))],
        out_specs=[row, pl.BlockSpec((nc, tm, E), lambda i: (0, i, 0)), crow, whole((1, C)), whole((1, C)),
                   whole((1, C)), whole((1, D))],
        out_shape=[jax.ShapeDtypeStruct((T, D), BF16), jax.ShapeDtypeStruct((nc, T, E), BF16),
                   jax.ShapeDtypeStruct((T, C), F32), jax.ShapeDtypeStruct((1, C), F32),
                   jax.ShapeDtypeStruct((1, C), F32), jax.ShapeDtypeStruct((1, C), F32),
                   jax.ShapeDtypeStruct((1, D), F32)],
        scratch_shapes=[pltpu.VMEM((tm, C), F32)],
        compiler_params=_cparams("arbitrary"),
    )(dy, m, c, lng, lnb, wpw2, g_post)


def _conv_bwd_a(dy, x, dc, y, p, g_pre, wdw, wpw1, tm):
    T, D = x.shape
    C = y.shape[1]
    nc, _, E = wpw1.shape
    tm = min(tm, T)
    per = tm // HALO
    n_tiles = T // tm
    KP = wdw.shape[0]

    def body(dy_ref, x_ref, dc_ref, dcnext_ref, y_ref, yprev_ref, p_ref, gpre_ref, wdw_ref, w_ref,
             dx_ref, hn_ref, dp_ref, dwdw_ref, dgpre_ref, ybuf, dcbuf, dyg_scr, dw8_scr):
        i = pl.program_id(0)
        first = i == 0
        ybuf[0:HALO, :] = jnp.where(i > 0, yprev_ref[...], 0.0)
        ybuf[HALO:HALO + tm, :] = y_ref[...]
        dcbuf[0:tm, :] = dc_ref[...]
        dcbuf[tm:tm + HALO, :] = jnp.where(i < n_tiles - 1, dcnext_ref[...], 0.0)
        off = HALO - (CONV_W - 1)
        @pl.when(first)
        def _():
            dw8_scr[...] = jnp.zeros_like(dw8_scr)

        for r0 in range(0, tm, CONV_RB):
            for c0 in range(0, C, CONV_CB):
                cols = slice(c0, c0 + CONV_CB)
                dcb = dcbuf[r0:r0 + CONV_RB, cols]
                acc = jnp.zeros((CONV_RB, CONV_CB), F32)
                for k in range(CONV_W):
                    back = CONV_W - 1 - k
                    acc = acc + wdw_ref[k:k + 1, cols] * dcbuf[r0 + back:r0 + back + CONV_RB, cols]
                    prod = dcb * ybuf[r0 + off + k:r0 + off + k + CONV_RB, cols]
                    dw8_scr[k, :, cols] += jnp.sum(prod.reshape(CONV_RB // 8, 8, CONV_CB), axis=0)
                dyg_scr[r0:r0 + CONV_RB, cols] = acc

        @pl.when(i == n_tiles - 1)
        def _():
            dwdw_ref[...] = jnp.sum(dw8_scr[...], axis=1)

        dhn = None
        for c in range(nc // 2):
            cols = slice(c * E, (c + 1) * E)
            av = p_ref[c].astype(F32)
            sg = jax.nn.sigmoid(p_ref[c + nc // 2].astype(F32))
            dygc = dyg_scr[:, cols]
            da = (dygc * sg).astype(BF16)
            dgt = (dygc * av * sg * (1.0 - sg)).astype(BF16)
            dp_ref[c] = da
            dp_ref[c + nc // 2] = dgt
            t = _dot_nt(da, w_ref[c]) + _dot_nt(dgt, w_ref[c + nc // 2])
            dhn = t if dhn is None else dhn + t
        xh, rx = _rms_stats(x_ref[...])
        hn_ref[...] = (xh * gpre_ref[...]).astype(BF16)
        dxn, dgq = _rms_bwd(xh, rx, gpre_ref[...], dhn)
        dx_ref[...] = dy_ref[...] + dxn
        _acc_out(dgpre_ref, first, dgq)

    row = pl.BlockSpec((tm, D), lambda i: (i, 0))
    crow = pl.BlockSpec((tm, C), lambda i: (i, 0))
    prev = pl.BlockSpec((HALO, C), lambda i: (jnp.maximum(i * per - 1, 0), 0))
    nxt = pl.BlockSpec((HALO, C), lambda i: (jnp.minimum((i + 1) * per, T // HALO - 1), 0))
    chunks = pl.BlockSpec((nc, tm, E), lambda i: (0, i, 0))

    def whole(shape):
        return pl.BlockSpec(shape, lambda i: (0,) * len(shape))

    return pl.pallas_call(
        body,
        name="conv_bwd_a",
        grid=(n_tiles,),
        in_specs=[row, row, crow, nxt, crow, prev, chunks, _resident((1, D)), _resident(wdw.shape),
                  _resident(wpw1.shape)],
        out_specs=[row, row, chunks, whole((KP, C)), whole((1, D))],
        out_shape=[jax.ShapeDtypeStruct((T, D), F32), jax.ShapeDtypeStruct((T, D), BF16),
                   jax.ShapeDtypeStruct((nc, T, E), BF16), jax.ShapeDtypeStruct((KP, C), F32),
                   jax.ShapeDtypeStruct((1, D), F32)],
        scratch_shapes=[pltpu.VMEM((HALO + tm, C), F32), pltpu.VMEM((tm + HALO, C), F32),
                        pltpu.VMEM((tm, C), F32), pltpu.VMEM((KP, 8, C), F32)],
        compiler_params=_cparams("arbitrary"),
    )(dy, x, dc, dc, y, y, p, g_pre, wdw, wpw1)


def _loss_head(y, target, tm):
    T, D = y.shape
    tm = min(tm, T)

    def body(y_ref, t_ref, dy_ref, loss_ref):
        e = y_ref[...] - t_ref[...]
        dy_ref[...] = e * (1.0 / D)
        part = jnp.sum(jnp.sum(e * e, axis=-1, keepdims=True), axis=0, keepdims=True) * (0.5 / D)
        _acc_out(loss_ref, pl.program_id(0) == 0, jnp.broadcast_to(part, loss_ref.shape))

    row = pl.BlockSpec((tm, D), lambda i: (i, 0))
    return pl.pallas_call(
        body,
        name="loss_head",
        grid=(T // tm,),
        in_specs=[row, row],
        out_specs=[row, pl.BlockSpec((8, 128), lambda i: (0, 0))],
        out_shape=[jax.ShapeDtypeStruct((T, D), F32), jax.ShapeDtypeStruct((8, 128), F32)],
        compiler_params=_cparams("arbitrary"),
    )(y, target)


def _row_tile(rows, cols, itemsize_budget=2 * 1024 * 1024):
    want = max(16, itemsize_budget // (4 * cols))
    if rows <= want:
        return rows
    t = (want // 16) * 16
    while t > 16 and rows % t:
        t -= 16
    return t if rows % t == 0 else rows


def _sum_parts(parts):
    n, R, C = parts.shape
    tr = _row_tile(R, C * n // 2 if parts.dtype == BF16 else C * n)

    def body(p_ref, o_ref):
        acc = p_ref[0].astype(F32)
        for s in range(1, n):
            acc = acc + p_ref[s].astype(F32)
        o_ref[...] = acc

    return pl.pallas_call(
        body,
        name="sum_parts",
        grid=(R // tr,),
        in_specs=[pl.BlockSpec((n, tr, C), lambda i: (0, i, 0))],
        out_specs=pl.BlockSpec((tr, C), lambda i: (i, 0)),
        out_shape=jax.ShapeDtypeStruct((R, C), F32),
        compiler_params=_cparams("parallel"),
    )(parts)


def _adamw(w, g, m, v):
    R, C = w.shape
    tr = _row_tile(R, C * 7 // 2)
    c1 = 1.0 - ADAM_B1 ** ADAM_STEP
    c2 = 1.0 - ADAM_B2 ** ADAM_STEP

    def body(w_ref, g_ref, m_ref, v_ref, d_ref, mo_ref, vo_ref):
        g = g_ref[...]
        mn = ADAM_B1 * m_ref[...] + (1.0 - ADAM_B1) * g
        vn = ADAM_B2 * v_ref[...] + (1.0 - ADAM_B2) * (g * g)
        mo_ref[...] = mn
        vo_ref[...] = vn
        d_ref[...] = -ADAM_LR * ((mn / c1) / (jnp.sqrt(vn / c2) + ADAM_EPS) + ADAM_WD * w_ref[...])

    blk = pl.BlockSpec((tr, C), lambda i: (i, 0))
    shp = jax.ShapeDtypeStruct((R, C), F32)
    return pl.pallas_call(
        body,
        name="adamw",
        grid=(R // tr,),
        in_specs=[blk, blk, blk, blk],
        out_specs=[blk, blk, blk],
        out_shape=[shp, shp, shp],
        compiler_params=_cparams("parallel"),
    )(w, g, m, v)


def _my_place():
    return lax.axis_index("x"), lax.axis_index("y"), lax.axis_index("c")


def _gather_weights(halved, whole):
    nh, nw = len(halved), len(whole)

    def body(*refs):
        h_in, w_in = refs[:nh], refs[nh:nh + nw]
        h_out, w_out = refs[nh + nw:2 * nh + nw], refs[2 * nh + nw:2 * (nh + nw)]
        hs_send, hs_recv, fw_send, fw_recv, ws_send, ws_recv, loc_sem = refs[2 * (nh + nw):]
        x, y, c = _my_place()
        me_chip = 2 * x + y
        sib = (x, y, 1 - c)
        chips = [(1 - x, y), (x, 1 - y), (1 - x, 1 - y)]

        def chip_no(ch):
            return 2 * ch[0] + ch[1]

        local = []
        for a in range(nh):
            cp = pltpu.make_async_copy(h_in[a], h_out[a].at[me_chip], loc_sem.at[a])
            cp.start()
            local.append(cp)
        for a in range(nw):
            cp = pltpu.make_async_copy(w_in[a], w_out[a].at[me_chip], loc_sem.at[nh + a])
            cp.start()
            local.append(cp)

        def landed(a, j, ch, half, to, sems):
            spot = h_out[a].at[chip_no(ch), pl.ds(half, 1)]
            return pltpu.make_async_remote_copy(src_ref=spot, dst_ref=spot, send_sem=sems[0].at[a, j],
                                                recv_sem=sems[1].at[a, j], device_id=to, device_id_type=MESH)

        def own_half_copy(a, j, to):
            return pltpu.make_async_remote_copy(src_ref=h_in[a].at[pl.ds(c, 1)],
                                                dst_ref=h_out[a].at[me_chip, pl.ds(c, 1)],
                                                send_sem=hs_send.at[a, j], recv_sem=hs_recv.at[a, j],
                                                device_id=to, device_id_type=MESH)

        sends = []
        for j, ch in enumerate(chips):
            for a in range(nh):
                cp = own_half_copy(a, j, (*ch, c))
                cp.start()
                sends.append(cp)
            for a in range(nw):
                cp = pltpu.make_async_remote_copy(src_ref=w_in[a], dst_ref=w_out[a].at[me_chip],
                                                  send_sem=ws_send.at[a, j], recv_sem=ws_recv.at[a, j],
                                                  device_id=(*ch, c), device_id_type=MESH)
                cp.start()
                sends.append(cp)
        for j, ch in enumerate(chips):
            for a in range(nh):
                landed(a, j, ch, c, (x, y, c), (hs_send, hs_recv)).wait_recv()
                cp = landed(a, j, ch, c, sib, (fw_send, fw_recv))
                cp.start()
                sends.append(cp)
        for j, ch in enumerate(chips):
            for a in range(nh):
                landed(a, j, ch, 1 - c, (x, y, c), (fw_send, fw_recv)).wait_recv()
            for a in range(nw):
                pltpu.make_async_remote_copy(src_ref=w_in[a], dst_ref=w_out[a].at[chip_no(ch)],
                                             send_sem=ws_send.at[a, j], recv_sem=ws_recv.at[a, j],
                                             device_id=(x, y, c), device_id_type=MESH).wait_recv()
        for cp in sends:
            cp.wait_send()
        for cp in local:
            cp.wait()

    hbm = pl.BlockSpec(memory_space=pl.ANY)
    outs = pl.pallas_call(
        body,
        name="gather_weights",
        in_specs=[hbm] * (nh + nw),
        out_specs=[hbm] * (nh + nw),
        out_shape=[jax.ShapeDtypeStruct((N_CHIPS, *a.shape), a.dtype) for a in (*halved, *whole)],
        scratch_shapes=[
            pltpu.SemaphoreType.DMA((nh, 3)), pltpu.SemaphoreType.DMA((nh, 3)),
            pltpu.SemaphoreType.DMA((nh, 3)), pltpu.SemaphoreType.DMA((nh, 3)),
            pltpu.SemaphoreType.DMA((max(nw, 1), 3)), pltpu.SemaphoreType.DMA((max(nw, 1), 3)),
            pltpu.SemaphoreType.DMA((nh + nw,)),
        ],
    )(*halved, *whole)
    return outs[:nh], outs[nh:]


def _scatter_grads(grads):
    n = len(grads)

    def body(*refs):
        g_in, g_out = refs[:n], refs[n:2 * n]
        send_sem, recv_sem = refs[2 * n:]
        x, y, c = _my_place()
        me = 4 * x + 2 * y + c
        for a in range(n):
            for k in range(N_CHIPS):
                for h in range(2):
                    to = (k // 2, k % 2, h)
                    d = 2 * k + h
                    src = g_in[a].at[k, h]
                    dst = g_out[a].at[me]

                    @pl.when(d == me)
                    def _():
                        pltpu.make_async_copy(src, dst, recv_sem.at[a, d]).start()

                    @pl.when(d != me)
                    def _():
                        pltpu.make_async_remote_copy(src_ref=src, dst_ref=dst, send_sem=send_sem.at[a, d],
                                                     recv_sem=recv_sem.at[a, me], device_id=to,
                                                     device_id_type=MESH).start()
        for a in range(n):
            for d in range(N_DEV):
                src = g_in[a].at[d // 2, d % 2]
                slot = g_out[a].at[d]

                @pl.when(d == me)
                def _():
                    pltpu.make_async_copy(src, slot, recv_sem.at[a, d]).wait()

                @pl.when(d != me)
                def _():
                    cp = pltpu.make_async_remote_copy(src_ref=src, dst_ref=slot, send_sem=send_sem.at[a, d],
                                                      recv_sem=recv_sem.at[a, d], device_id=(x, y, c),
                                                      device_id_type=MESH)
                    cp.wait_send()
                    cp.wait_recv()

    hbm = pl.BlockSpec(memory_space=pl.ANY)
    return pl.pallas_call(
        body,
        name="scatter_grads",
        in_specs=[hbm] * n,
        out_specs=[hbm] * n,
        out_shape=[jax.ShapeDtypeStruct((N_DEV, *g.shape[2:]), g.dtype) for g in grads],
        scratch_shapes=[pltpu.SemaphoreType.DMA((n, N_DEV)), pltpu.SemaphoreType.DMA((n, N_DEV))],
    )(*grads)


def _swap_halves(halves):
    n = len(halves)

    def body(*refs):
        h_in, h_out = refs[:n], refs[n:2 * n]
        send_sem, recv_sem, loc_sem = refs[2 * n:]
        x, y, c = _my_place()
        sib = (x, y, 1 - c)
        for a in range(n):
            pltpu.make_async_copy(h_in[a], h_out[a].at[c], loc_sem.at[a]).start()
            pltpu.make_async_remote_copy(src_ref=h_in[a], dst_ref=h_out[a].at[c], send_sem=send_sem.at[a],
                                         recv_sem=recv_sem.at[a], device_id=sib, device_id_type=MESH).start()
        for a in range(n):
            pltpu.make_async_copy(h_in[a], h_out[a].at[c], loc_sem.at[a]).wait()
            cp = pltpu.make_async_remote_copy(src_ref=h_in[a], dst_ref=h_out[a].at[1 - c], send_sem=send_sem.at[a],
                                              recv_sem=recv_sem.at[a], device_id=sib, device_id_type=MESH)
            cp.wait_send()
            cp.wait_recv()

    hbm = pl.BlockSpec(memory_space=pl.ANY)
    return pl.pallas_call(
        body,
        name="swap_halves",
        in_specs=[hbm] * n,
        out_specs=[hbm] * n,
        out_shape=[jax.ShapeDtypeStruct((2, *h.shape), h.dtype) for h in halves],
        scratch_shapes=[pltpu.SemaphoreType.DMA((n,)), pltpu.SemaphoreType.DMA((n,)), pltpu.SemaphoreType.DMA((n,))],
    )(*halves)


def _share_all(buf):
    def body(b_in, b_out, send_sem, recv_sem):
        x, y, c = _my_place()
        me = 4 * x + 2 * y + c
        for d in range(N_DEV):
            to = (d // 4, (d // 2) % 2, d % 2)

            @pl.when(d == me)
            def _():
                pltpu.make_async_copy(b_in, b_out.at[me], recv_sem.at[d]).start()

            @pl.when(d != me)
            def _():
                pltpu.make_async_remote_copy(src_ref=b_in, dst_ref=b_out.at[me], send_sem=send_sem.at[d],
                                             recv_sem=recv_sem.at[me], device_id=to, device_id_type=MESH).start()
        for d in range(N_DEV):
            @pl.when(d == me)
            def _():
                pltpu.make_async_copy(b_in, b_out.at[d], recv_sem.at[d]).wait()

            @pl.when(d != me)
            def _():
                cp = pltpu.make_async_remote_copy(src_ref=b_in, dst_ref=b_out.at[d], send_sem=send_sem.at[d],
                                                  recv_sem=recv_sem.at[d], device_id=(x, y, c),
                                                  device_id_type=MESH)
                cp.wait_send()
                cp.wait_recv()

    hbm = pl.BlockSpec(memory_space=pl.ANY)
    return pl.pallas_call(
        body,
        name="share_all",
        in_specs=[hbm],
        out_specs=hbm,
        out_shape=jax.ShapeDtypeStruct((N_DEV, *buf.shape), buf.dtype),
        scratch_shapes=[pltpu.SemaphoreType.DMA((N_DEV,)), pltpu.SemaphoreType.DMA((N_DEV,))],
    )(buf)


TM_FFN = 512
TM_SGU = 256
TM_CONV = 256
TK_WGRAD = 1024
TM_LOSS = 1024


def _local_step(x, target, W):
    depth = W["norm_g"].shape[0]
    saved = []
    n_a = n_b = 0
    vec = lambda v: v.reshape(1, -1)
    wsm, wsmt, bsb = [], [], []
    n_sgu = W["sgu_w_spatial"].shape[0]
    causal = jnp.tril(jnp.ones((CHUNK, CHUNK), dtype=bool))
    dgrp = W["sgu_ln_g"].shape[1] // N_GROUPS
    for jx in range(n_sgu):
        ws = jnp.where(causal[None], W["sgu_w_spatial"][jx], 0.0).astype(BF16)
        wsm.append(ws)
        wsmt.append(jnp.swapaxes(ws, 1, 2))
        bsb.append(jnp.broadcast_to(W["sgu_b_spatial"][jx][:, :, None], (N_GROUPS, CHUNK, dgrp)))
    kp = HALO
    wdw = [jnp.pad(W["conv_w_dw"][jx], ((0, kp - CONV_W), (0, 0))) for jx in range(W["conv_w_dw"].shape[0])]

    for i in range(depth):
        g = W["norm_g"][i]
        rec = {"x0": x}
        x, rec["a1"], rec["b1"], rec["f1"] = _ffn_fwd(
            x, vec(g[0]), vec(g[1]), W["ff_w_gate"][i][0], W["ff_w_up"][i][0], W["ff_w_down"][i][0], TM_FFN)
        rec["x1"] = x
        j = i // 2
        if i % 2 == 0:
            x, rec["zp"], rec["m"] = _sgu_fwd(
                x, vec(g[2]), vec(g[3]), W["sgu_w_in"][j], vec(W["sgu_ln_g"][j]), vec(W["sgu_ln_b"][j]),
                wsm[j], bsb[j], W["sgu_w_out"][j], TM_SGU)
        else:
            rec["y"], rec["p"] = _conv_fwd_a(x, vec(g[2]), W["conv_w_pw1"][j], TM_CONV)
            x, rec["c"], rec["m"] = _conv_fwd_b(
                x, rec["y"], wdw[j], vec(W["conv_b_dw"][j]), vec(W["conv_ln_g"][j]), vec(W["conv_ln_b"][j]),
                W["conv_w_pw2"][j], vec(g[3]), TM_CONV)
        rec["x2"] = x
        x, rec["a2"], rec["b2"], rec["f2"] = _ffn_fwd(
            x, vec(g[4]), vec(g[5]), W["ff_w_gate"][i][1], W["ff_w_up"][i][1], W["ff_w_down"][i][1], TM_FFN)
        saved.append(rec)

    dx, loss_tile = _loss_head(x, target, TM_LOSS)
    loss = loss_tile[0, 0]

    big = {k: [None] * W[k].shape[0] for k in
           ("ff_w_gate", "ff_w_up", "ff_w_down", "sgu_w_in", "sgu_w_out", "conv_w_pw1", "conv_w_pw2")}
    for k in ("ff_w_gate", "ff_w_up", "ff_w_down"):
        big[k] = [[None, None] for _ in range(depth)]
    small = {k: [None] * W[k].shape[0] for k in
             ("sgu_ln_g", "sgu_ln_b", "sgu_w_spatial", "sgu_b_spatial", "conv_w_dw", "conv_b_dw", "conv_ln_g",
              "conv_ln_b")}
    dnorm = [[None] * 6 for _ in range(depth)]

    def ffn_back(dx, i, f_idx, xin, a, b, f):
        g = W["norm_g"][i]
        dx, h, dz, s, da, db, dgpre, dgpost = _ffn_bwd(
            dx, xin, f, a, b, vec(g[4 * f_idx]), vec(g[4 * f_idx + 1]),
            W["ff_w_gate"][i][f_idx], W["ff_w_up"][i][f_idx], W["ff_w_down"][i][f_idx], TM_FFN)
        big["ff_w_gate"][i][f_idx] = _tn_matmul(h, da, TK_WGRAD)
        big["ff_w_up"][i][f_idx] = _tn_matmul(h, db, TK_WGRAD)
        big["ff_w_down"][i][f_idx] = _tn_matmul(s, dz, TK_WGRAD)
        dnorm[i][4 * f_idx] = dgpre[0]
        dnorm[i][4 * f_idx + 1] = dgpost[0]
        return dx

    for i in reversed(range(depth)):
        rec = saved[i]
        g = W["norm_g"][i]
        j = i // 2
        dx = ffn_back(dx, i, 1, rec["x2"], rec["a2"], rec["b2"], rec["f2"])
        if i % 2 == 0:
            (dx, hn, dzp, gated, dm, dws, dbs_acc, dlng, dlnb, dgpre, dgpost) = _sgu_bwd(
                dx, rec["x1"], rec["m"], rec["zp"], vec(g[2]), vec(g[3]), W["sgu_w_in"][j], vec(W["sgu_ln_g"][j]),
                vec(W["sgu_ln_b"][j]), wsm[j], wsmt[j], bsb[j], W["sgu_w_out"][j], TM_SGU)
            big["sgu_w_in"][j] = _tn_matmul(hn, dzp, TK_WGRAD)
            big["sgu_w_out"][j] = _tn_matmul(gated, dm, TK_WGRAD)
            small["sgu_w_spatial"][j] = jnp.where(causal[None], dws, 0.0)
            small["sgu_b_spatial"][j] = dbs_acc.reshape(CHUNK, N_GROUPS, dgrp).sum(-1).T
            small["sgu_ln_g"][j] = dlng[0]
            small["sgu_ln_b"][j] = dlnb[0]
        else:
            dm, q, dc, dlng, dlnb, dbdw, dgpost = _conv_bwd_b(
                dx, rec["m"], rec["c"], vec(W["conv_ln_g"][j]), vec(W["conv_ln_b"][j]), W["conv_w_pw2"][j],
                vec(g[3]), TM_CONV)
            dx, hn, dp, dwdw, dgpre = _conv_bwd_a(
                dx, rec["x1"], dc, rec["y"], rec["p"], vec(g[2]), wdw[j], W["conv_w_pw1"][j], TM_CONV)
            big["conv_w_pw1"][j] = _tn_matmul(hn, dp, TK_WGRAD)
            big["conv_w_pw2"][j] = _tn_matmul(q, dm, TK_WGRAD)
            small["conv_w_dw"][j] = dwdw[:CONV_W]
            small["conv_b_dw"][j] = dbdw[0]
            small["conv_ln_g"][j] = dlng[0]
            small["conv_ln_b"][j] = dlnb[0]
        dnorm[i][2] = dgpre[0]
        dnorm[i][3] = dgpost[0]
        dx = ffn_back(dx, i, 0, rec["x0"], rec["a1"], rec["b1"], rec["f1"])

    for k in ("ff_w_gate", "ff_w_up", "ff_w_down"):
        big[k] = jnp.stack([jnp.stack(pair, axis=1) for pair in big[k]], axis=1)
    for k in ("sgu_w_in", "sgu_w_out", "conv_w_pw1", "conv_w_pw2"):
        big[k] = jnp.stack(big[k], axis=1)
    small = {k: jnp.stack(v) for k, v in small.items()}
    small["norm_g"] = jnp.stack([jnp.stack(r) for r in dnorm])
    return loss, dx, big, small


BIG = ("ff_w_gate", "ff_w_up", "ff_w_down", "sgu_w_in", "sgu_w_out", "conv_w_pw1", "conv_w_pw2")
SHARDED_SMALL = ("norm_g", "conv_w_dw", "conv_b_dw", "conv_ln_g", "conv_ln_b")
REPLICATED = ("sgu_ln_g", "sgu_ln_b", "sgu_w_spatial", "sgu_b_spatial")
WEIGHTS = ("norm_g", "ff_w_gate", "ff_w_up", "ff_w_down", "sgu_w_in", "sgu_ln_g", "sgu_ln_b", "sgu_w_spatial",
           "sgu_b_spatial", "sgu_w_out", "conv_w_pw1", "conv_w_dw", "conv_b_dw", "conv_ln_g", "conv_ln_b",
           "conv_w_pw2")


def _rows8(a, width):
    r = a.reshape(-1, width)
    pad = (-r.shape[0]) % 8
    return jnp.pad(r, ((0, pad), (0, 0))) if pad else r


def _pack(arrs, width):
    parts = [_rows8(a, width) for a in arrs]
    return jnp.concatenate(parts, axis=0), [p.shape[0] for p in parts]


def _unpack(buf, like):
    out, r0 = [], 0
    width = buf.shape[-1]
    for a in like:
        n = -(-(a.size // width) // 8) * 8
        rows = a.size // width
        out.append(buf[..., r0:r0 + rows, :].reshape(*buf.shape[:-2], *a.shape))
        r0 += n
    return out


def kernel(x, norm_g, ff_w_gate, ff_w_up, ff_w_down, sgu_w_in, sgu_ln_g, sgu_ln_b, sgu_w_spatial, sgu_b_spatial, sgu_w_out, conv_w_pw1, conv_w_dw, conv_b_dw, conv_ln_g, conv_ln_b, conv_w_pw2, loss_target, m_norm_g, m_ff_w_gate, m_ff_w_up, m_ff_w_down, m_sgu_w_in, m_sgu_ln_g, m_sgu_ln_b, m_sgu_w_spatial, m_sgu_b_spatial, m_sgu_w_out, m_conv_w_pw1, m_conv_w_dw, m_conv_b_dw, m_conv_ln_g, m_conv_ln_b, m_conv_w_pw2, v_norm_g, v_ff_w_gate, v_ff_w_up, v_ff_w_down, v_sgu_w_in, v_sgu_ln_g, v_sgu_ln_b, v_sgu_w_spatial, v_sgu_b_spatial, v_sgu_w_out, v_conv_w_pw1, v_conv_w_dw, v_conv_b_dw, v_conv_ln_g, v_conv_ln_b, v_conv_w_pw2):
    w = dict(norm_g=norm_g, ff_w_gate=ff_w_gate, ff_w_up=ff_w_up, ff_w_down=ff_w_down, sgu_w_in=sgu_w_in,
             sgu_ln_g=sgu_ln_g, sgu_ln_b=sgu_ln_b, sgu_w_spatial=sgu_w_spatial, sgu_b_spatial=sgu_b_spatial,
             sgu_w_out=sgu_w_out, conv_w_pw1=conv_w_pw1, conv_w_dw=conv_w_dw, conv_b_dw=conv_b_dw,
             conv_ln_g=conv_ln_g, conv_ln_b=conv_ln_b, conv_w_pw2=conv_w_pw2)
    mom = dict(norm_g=m_norm_g, ff_w_gate=m_ff_w_gate, ff_w_up=m_ff_w_up, ff_w_down=m_ff_w_down,
               sgu_w_in=m_sgu_w_in, sgu_ln_g=m_sgu_ln_g, sgu_ln_b=m_sgu_ln_b, sgu_w_spatial=m_sgu_w_spatial,
               sgu_b_spatial=m_sgu_b_spatial, sgu_w_out=m_sgu_w_out, conv_w_pw1=m_conv_w_pw1,
               conv_w_dw=m_conv_w_dw, conv_b_dw=m_conv_b_dw, conv_ln_g=m_conv_ln_g, conv_ln_b=m_conv_ln_b,
               conv_w_pw2=m_conv_w_pw2)
    vel = dict(norm_g=v_norm_g, ff_w_gate=v_ff_w_gate, ff_w_up=v_ff_w_up, ff_w_down=v_ff_w_down,
               sgu_w_in=v_sgu_w_in, sgu_ln_g=v_sgu_ln_g, sgu_ln_b=v_sgu_ln_b, sgu_w_spatial=v_sgu_w_spatial,
               sgu_b_spatial=v_sgu_b_spatial, sgu_w_out=v_sgu_w_out, conv_w_pw1=v_conv_w_pw1,
               conv_w_dw=v_conv_w_dw, conv_b_dw=v_conv_b_dw, conv_ln_g=v_conv_ln_g, conv_ln_b=v_conv_ln_b,
               conv_w_pw2=v_conv_w_pw2)
    T, D = x.shape[1], x.shape[2]
    shard_w = conv_b_dw.shape[1]

    halved = [w[k].astype(BF16).reshape(2, w[k].shape[0] // 2, *w[k].shape[1:]) for k in BIG]
    small_buf, _ = _pack([w[k] for k in SHARDED_SMALL], shard_w)
    gathered, (small_all,) = _gather_weights(halved, [small_buf])
    W = {}
    for k, gth in zip(BIG, gathered):
        gth = gth.reshape(N_CHIPS, *w[k].shape)
        W[k] = jnp.moveaxis(gth, 0, 1 if w[k].ndim == 3 else 2)
    for k, part in zip(SHARDED_SMALL, _unpack(small_all, [w[k] for k in SHARDED_SMALL])):
        W[k] = jnp.moveaxis(part, 0, -2).reshape(*w[k].shape[:-1], N_CHIPS * shard_w)
    for k in REPLICATED:
        W[k] = w[k]

    loss, dx, big, small = _local_step(x[0], loss_target[0], W)
    loss = lax.psum(loss, ("x", "y", "c"))

    pieces = []
    for k in BIG:
        g = big[k]
        pieces.append(g.reshape(N_CHIPS, 2, -1, g.shape[-1]))
    arrived = _scatter_grads(pieces)
    halves = [_sum_parts(a) for a in arrived]
    both = _swap_halves(halves)
    grads = {k: b.reshape(w[k].shape) for k, b in zip(BIG, both)}

    sbuf, _ = _pack([small[k] for k in SHARDED_SMALL + REPLICATED], D)
    ssum = _sum_parts(_share_all(sbuf))
    me_chip = 2 * lax.axis_index("x") + lax.axis_index("y")
    for k, gfull in zip(SHARDED_SMALL + REPLICATED, _unpack(ssum, [small[k] for k in SHARDED_SMALL + REPLICATED])):
        if k in SHARDED_SMALL:
            gfull = lax.dynamic_slice_in_dim(gfull, me_chip * shard_w, shard_w, axis=gfull.ndim - 1)
        grads[k] = gfull

    delta, new_m, new_v = {}, {}, {}
    for k in BIG:
        w2 = lambda a: a.reshape(-1, a.shape[-1])
        d, mn, vn = _adamw(w2(w[k]), w2(grads[k]), w2(mom[k]), w2(vel[k]))
        delta[k], new_m[k], new_v[k] = (t.reshape(w[k].shape) for t in (d, mn, vn))
    for names, width in ((SHARDED_SMALL, shard_w), (REPLICATED, CHUNK)):
        packed = [_pack([src[k] for k in names], width)[0] for src in (w, grads, mom, vel)]
        outs = _adamw(*packed)
        for res, out in zip((delta, new_m, new_v), outs):
            for k, a in zip(names, _unpack(out, [w[k] for k in names])):
                res[k] = a

    return (loss, dx[None], *[grads[k] for k in WEIGHTS], *[delta[k] for k in WEIGHTS],
            *[new_m[k] for k in WEIGHTS], *[new_v[k] for k in WEIGHTS])
```

```python
import functools

import jax
import jax.numpy as jnp
from jax import lax
from jax.experimental import pallas as pl
from jax.experimental.pallas import tpu as pltpu

F32 = jnp.float32
BF16 = jnp.bfloat16
EPS = 1e-6
N_CHIPS = 4
N_DEV = 8
N_GROUPS = 8
CHUNK = 128
CONV_W = 31
HALO = 32
CONV_RB = 64
CONV_CB = 256
VMEM_LIMIT_V7X = 56 * 1024 * 1024
MESH = pl.DeviceIdType.MESH

ADAM_LR = 0.001
ADAM_B1 = 0.9
ADAM_B2 = 0.999
ADAM_EPS = 1e-08
ADAM_WD = 0.01
ADAM_STEP = 10
FFN_SCALE = 0.5


def _cparams(*sem):
    return pltpu.CompilerParams(dimension_semantics=sem, vmem_limit_bytes=VMEM_LIMIT_V7X)


def _resident(shape):
    return pl.BlockSpec(shape, lambda *_: (0,) * len(shape), pipeline_mode=pl.Buffered(1))


def _dot(a, b):
    return jnp.dot(a, b, preferred_element_type=F32)


def _dot_nt(a, b):
    return lax.dot_general(a, b, (((1,), (1,)), ((), ())), preferred_element_type=F32)


def _dot_tn(a, b):
    return lax.dot_general(a, b, (((0,), (0,)), ((), ())), preferred_element_type=F32)


def _rms_stats(x):
    r = lax.rsqrt(jnp.mean(x * x, axis=-1, keepdims=True) + EPS)
    return x * r, r


def _rms_bwd(xh, r, g, dy):
    dxh = dy * g
    dx = r * (dxh - xh * jnp.mean(dxh * xh, axis=-1, keepdims=True))
    return dx, jnp.sum(dy * xh, axis=0, keepdims=True)


def _ln_stats(parts, width):
    mu = sum(jnp.sum(p, axis=-1, keepdims=True) for p in parts) / width
    cen = [p - mu for p in parts]
    var = sum(jnp.sum(c * c, axis=-1, keepdims=True) for c in cen) / width
    rstd = lax.rsqrt(var + EPS)
    return [c * rstd for c in cen], rstd


def _ln_bwd(vh_parts, rstd, dvh_parts, width):
    m1 = sum(jnp.sum(d, axis=-1, keepdims=True) for d in dvh_parts) / width
    m2 = sum(jnp.sum(d * v, axis=-1, keepdims=True) for d, v in zip(dvh_parts, vh_parts)) / width
    return [rstd * (d - m1 - v * m2) for d, v in zip(dvh_parts, vh_parts)]


_GELU_C = 0.7978845608028654
_GELU_A = 0.044715


def _gelu(x):
    return 0.5 * x * (1.0 + jnp.tanh(_GELU_C * (x + _GELU_A * x * x * x)))


def _gelu_grad(x):
    t = jnp.tanh(_GELU_C * (x + _GELU_A * x * x * x))
    return 0.5 * (1.0 + t) + 0.5 * x * (1.0 - t * t) * _GELU_C * (1.0 + 3.0 * _GELU_A * x * x)


def _sigmoid_pair(a):
    e = jnp.exp(jnp.minimum(-a, 80.0))
    sg = 1.0 / (1.0 + e)
    return sg, e * sg


def _acc_out(ref, first, val):
    @pl.when(first)
    def _():
        ref[...] = val

    @pl.when(jnp.logical_not(first))
    def _():
        ref[...] += val


def _chunk_spec(sel, rows, cols):
    return pl.BlockSpec((None,) * (1 + len(sel)) + (rows, cols), lambda i, j: (j, *sel, 0, 0))


def _chunks_spec(w, sel):
    return pl.BlockSpec((w.shape[0],) + (None,) * len(sel) + w.shape[-2:], lambda *_: (0, *sel, 0, 0),
                        pipeline_mode=pl.Buffered(1))


def _ffn_fwd(x, g_pre, g_post, wg, wu, wd, sel, tm):
    T, D = x.shape
    nj, F = wg.shape[0], wg.shape[-1]
    tm = min(tm, T)

    def body(x_ref, gpre_ref, gpost_ref, wg_ref, wu_ref, wd_ref, xo_ref, a_ref, b_ref, f_ref, h_scr, acc_scr):
        j = pl.program_id(1)

        @pl.when(j == 0)
        def _():
            xh, _ = _rms_stats(x_ref[...])
            h_scr[...] = (xh * gpre_ref[...]).astype(BF16)
            acc_scr[...] = jnp.zeros_like(acc_scr)

        h = h_scr[...]
        a = _dot(h, wg_ref[...]).astype(BF16)
        b = _dot(h, wu_ref[...]).astype(BF16)
        a_ref[...] = a
        b_ref[...] = b
        sg, _ = _sigmoid_pair(a)
        acc_scr[...] += _dot((a * sg) * b, wd_ref[...])

        @pl.when(j == nj - 1)
        def _():
            f = acc_scr[...]
            f_ref[...] = f
            fh, _ = _rms_stats(f)
            xo_ref[...] = x_ref[...] + FFN_SCALE * (fh * gpost_ref[...])

    row = pl.BlockSpec((tm, D), lambda i, j: (i, 0))
    vec = pl.BlockSpec((1, D), lambda i, j: (0, 0))
    w_in = _chunk_spec(sel, D, F)
    w_out = _chunk_spec(sel, F, D)
    act = pl.BlockSpec((None, tm, F), lambda i, j: (j, i, 0))
    return pl.pallas_call(
        body,
        name="ffn_fwd",
        grid=(T // tm, nj),
        in_specs=[row, vec, vec, w_in, w_in, w_out],
        out_specs=[row, act, act, row],
        out_shape=[
            jax.ShapeDtypeStruct((T, D), F32),
            jax.ShapeDtypeStruct((nj, T, F), BF16),
            jax.ShapeDtypeStruct((nj, T, F), BF16),
            jax.ShapeDtypeStruct((T, D), F32),
        ],
        scratch_shapes=[pltpu.VMEM((tm, D), BF16), pltpu.VMEM((tm, D), F32)],
        compiler_params=_cparams("parallel", "arbitrary"),
    )(x, g_pre, g_post, wg, wu, wd)


def _ffn_bwd(dy, x, f, a, b, g_pre, g_post, wg, wu, wd, sel, tm):
    T, D = x.shape
    nj, F = wg.shape[0], wg.shape[-1]
    tm = min(tm, T)

    def body(dy_ref, x_ref, f_ref, a_ref, b_ref, gpre_ref, gpost_ref, wg_ref, wu_ref, wd_ref,
             dx_ref, h_ref, dz_ref, s_ref, da_ref, db_ref, dgpre_ref, dgpost_ref, dh_scr):
        i = pl.program_id(0)
        j = pl.program_id(1)

        @pl.when(j == 0)
        def _():
            fh, rf = _rms_stats(f_ref[...])
            dz, dg = _rms_bwd(fh, rf, gpost_ref[...], FFN_SCALE * dy_ref[...])
            dz_ref[...] = dz.astype(BF16)
            _acc_out(dgpost_ref, i == 0, dg)
            xh, _ = _rms_stats(x_ref[...])
            h_ref[...] = (xh * gpre_ref[...]).astype(BF16)
            dh_scr[...] = jnp.zeros_like(dh_scr)

        ds = _dot_nt(dz_ref[...], wd_ref[...]).astype(BF16)
        av = a_ref[...]
        bv = b_ref[...]
        sg, one_minus_sg = _sigmoid_pair(av)
        sl = av * sg
        s_ref[...] = sl * bv
        da = (ds * bv) * (sg + sl * one_minus_sg)
        db = ds * sl
        da_ref[...] = da
        db_ref[...] = db
        dh_scr[...] += _dot_nt(da, wg_ref[...]) + _dot_nt(db, wu_ref[...])

        @pl.when(j == nj - 1)
        def _():
            xh, rx = _rms_stats(x_ref[...])
            dxn, dg = _rms_bwd(xh, rx, gpre_ref[...], dh_scr[...])
            dx_ref[...] = dy_ref[...] + dxn
            _acc_out(dgpre_ref, i == 0, dg)

    row = pl.BlockSpec((tm, D), lambda i, j: (i, 0))
    vec = pl.BlockSpec((1, D), lambda i, j: (0, 0))
    w_in = _chunk_spec(sel, D, F)
    w_out = _chunk_spec(sel, F, D)
    act = pl.BlockSpec((None, tm, F), lambda i, j: (j, i, 0))
    act_shape = jax.ShapeDtypeStruct((nj, T, F), BF16)
    return pl.pallas_call(
        body,
        name="ffn_bwd",
        grid=(T // tm, nj),
        in_specs=[row, row, row, act, act, vec, vec, w_in, w_in, w_out],
        out_specs=[row, row, row, act, act, act, vec, vec],
        out_shape=[
            jax.ShapeDtypeStruct((T, D), F32),
            jax.ShapeDtypeStruct((T, D), BF16),
            jax.ShapeDtypeStruct((T, D), BF16),
            act_shape, act_shape, act_shape,
            jax.ShapeDtypeStruct((1, D), F32),
            jax.ShapeDtypeStruct((1, D), F32),
        ],
        scratch_shapes=[pltpu.VMEM((tm, D), F32)],
        compiler_params=_cparams("arbitrary", "arbitrary"),
    )(dy, x, f, a, b, g_pre, g_post, wg, wu, wd)


def _tn_matmul(a, b, buf, like, sel, tk):
    a_chunked = a.ndim == 3
    nj = a.shape[0] if a_chunked else b.shape[0]
    T, M, N = a.shape[-2], a.shape[-1], b.shape[-1]
    tk = min(tk, T)
    nk = T // tk

    def body(a_ref, b_ref, *rest):
        o_ref, acc_scr = rest[-2:]
        k = pl.program_id(1)

        @pl.when(k == 0)
        def _():
            acc_scr[...] = jnp.zeros_like(acc_scr)

        acc_scr[...] += _dot_tn(a_ref[...], b_ref[...])

        @pl.when(k == nk - 1)
        def _():
            o_ref[...] = acc_scr[...].astype(BF16)

    def spec(chunked, width):
        if chunked:
            return pl.BlockSpec((None, tk, width), lambda j, k: (j, k, 0))
        return pl.BlockSpec((tk, width), lambda j, k: (k, 0))

    have = buf is not None
    return pl.pallas_call(
        body,
        name="tn_matmul",
        grid=(nj, nk),
        in_specs=[spec(a_chunked, M), spec(not a_chunked, N)] + ([pl.BlockSpec(memory_space=pl.ANY)] if have else []),
        out_specs=pl.BlockSpec((None,) * (1 + len(sel)) + (M, N), lambda j, k: (j, *sel, 0, 0)),
        out_shape=jax.ShapeDtypeStruct(like.shape, BF16),
        input_output_aliases={2: 0} if have else {},
        scratch_shapes=[pltpu.VMEM((M, N), F32)],
        compiler_params=_cparams("parallel", "arbitrary"),
    )(a, b, *([buf] if have else []))


def _sgu_fwd(x, g_pre, g_post, win, lng, lnb, wsm, bsb, wout, sel, tm):
    T, D = x.shape
    nc, E = win.shape[0], win.shape[-1]
    S = 2 * E
    dg = S // N_GROUPS
    wo_rows = wout.shape[-2]
    tm = min(tm, T)
    nq = tm // CHUNK

    def body(x_ref, gpre_ref, gpost_ref, win_ref, lng_ref, lnb_ref, ws_ref, bsb_ref, wout_ref,
             xo_ref, zp_ref, m_ref, u_scr, vn_scr, gt_scr):
        x = x_ref[...]
        xh, _ = _rms_stats(x)
        hn = (xh * gpre_ref[...]).astype(BF16)
        v_parts = []
        for c in range(nc):
            zp = _dot(hn, win_ref[c])
            zp_ref[c] = zp.astype(BF16)
            z = _gelu(zp)
            if c < nc // 2:
                u_scr[:, c * E:(c + 1) * E] = z
            else:
                v_parts.append(z)
        vh_parts, _ = _ln_stats(v_parts, S)
        for c, vh in enumerate(vh_parts):
            cols = slice(c * E, (c + 1) * E)
            vn_scr[:, cols] = (vh * lng_ref[:, cols] + lnb_ref[:, cols]).astype(BF16)
        for q in range(nq):
            rows = slice(q * CHUNK, (q + 1) * CHUNK)
            for g in range(N_GROUPS):
                cols = slice(g * dg, (g + 1) * dg)
                mixed = _dot(ws_ref[g], vn_scr[rows, cols]) + bsb_ref[g]
                gt_scr[rows, cols] = (u_scr[rows, cols] * mixed).astype(BF16)
        m = _dot(gt_scr[:, 0:wo_rows], wout_ref[0])
        for c in range(1, nc):
            m += _dot(gt_scr[:, c * wo_rows:(c + 1) * wo_rows], wout_ref[c])
        m_ref[...] = m
        mh, _ = _rms_stats(m)
        xo_ref[...] = x + mh * gpost_ref[...]

    row = pl.BlockSpec((tm, D), lambda i: (i, 0))
    return pl.pallas_call(
        body,
        name="sgu_fwd",
        grid=(T // tm,),
        in_specs=[row, _resident((1, D)), _resident((1, D)), _chunks_spec(win, sel), _resident((1, S)),
                  _resident((1, S)), _resident(wsm.shape), _resident(bsb.shape), _chunks_spec(wout, sel)],
        out_specs=[row, pl.BlockSpec((nc, tm, E), lambda i: (0, i, 0)), row],
        out_shape=[
            jax.ShapeDtypeStruct((T, D), F32),
            jax.ShapeDtypeStruct((nc, T, E), BF16),
            jax.ShapeDtypeStruct((T, D), F32),
        ],
        scratch_shapes=[pltpu.VMEM((tm, S), F32), pltpu.VMEM((tm, S), BF16), pltpu.VMEM((tm, S), BF16)],
        compiler_params=_cparams("parallel"),
    )(x, g_pre, g_post, win, lng, lnb, wsm, bsb, wout)


def _sgu_bwd(dy, x, m, zp, g_pre, g_post, win, lng, lnb, wsm, wsmt, bsb, wout, sel, tm):
    T, D = x.shape
    nc, E = win.shape[0], win.shape[-1]
    S = 2 * E
    dg = S // N_GROUPS
    wo_rows = wout.shape[-2]
    tm = min(tm, T)
    nq = tm // CHUNK

    def body(dy_ref, x_ref, m_ref, zp_ref, gpre_ref, gpost_ref, win_ref, lng_ref, lnb_ref, ws_ref, wst_ref,
             bsb_ref, wout_ref,
             dx_ref, hn_ref, dzp_ref, gated_ref, dm_ref, dws_ref, dbs_ref, dlng_ref, dlnb_ref, dgpre_ref,
             dgpost_ref, u_scr, d_scr, vh_scr, vn_scr):
        first = pl.program_id(0) == 0
        dy = dy_ref[...]
        mh, rm = _rms_stats(m_ref[...])
        dm, dgp = _rms_bwd(mh, rm, gpost_ref[...], dy)
        _acc_out(dgpost_ref, first, dgp)
        dm = dm.astype(BF16)
        dm_ref[...] = dm
        for c in range(nc):
            d_scr[:, c * wo_rows:(c + 1) * wo_rows] = _dot_nt(dm, wout_ref[c])
        v_parts = []
        for c in range(nc):
            z = _gelu(zp_ref[c].astype(F32))
            if c < nc // 2:
                u_scr[:, c * E:(c + 1) * E] = z
            else:
                v_parts.append(z)
        vh_parts, rstd = _ln_stats(v_parts, S)
        for c, vh in enumerate(vh_parts):
            cols = slice(c * E, (c + 1) * E)
            vh_scr[:, cols] = vh
            vn_scr[:, cols] = (vh * lng_ref[:, cols] + lnb_ref[:, cols]).astype(BF16)

        @pl.when(first)
        def _():
            dws_ref[...] = jnp.zeros_like(dws_ref)
            dbs_ref[...] = jnp.zeros_like(dbs_ref)
            dlng_ref[...] = jnp.zeros_like(dlng_ref)
            dlnb_ref[...] = jnp.zeros_like(dlnb_ref)

        for q in range(nq):
            rows = slice(q * CHUNK, (q + 1) * CHUNK)
            for g in range(N_GROUPS):
                cols = slice(g * dg, (g + 1) * dg)
                vn = vn_scr[rows, cols]
                mixed = _dot(ws_ref[g], vn) + bsb_ref[g]
                u = u_scr[rows, cols]
                dgt = d_scr[rows, cols]
                gated_ref[(g * dg) // wo_rows, rows, (g * dg) % wo_rows:(g * dg) % wo_rows + dg] = (u * mixed).astype(BF16)
                dmix = dgt * u
                dbs_ref[:, cols] += dmix
                dmix = dmix.astype(BF16)
                dws_ref[g] += _dot_nt(dmix, vn)
                u_scr[rows, cols] = dgt * mixed
                d_scr[rows, cols] = _dot(wst_ref[g], dmix)
        dvn = [d_scr[:, c * E:(c + 1) * E] for c in range(nc // 2)]
        vh = [vh_scr[:, c * E:(c + 1) * E] for c in range(nc // 2)]
        for c, (d, v) in enumerate(zip(dvn, vh)):
            dlng_ref[:, c * E:(c + 1) * E] += jnp.sum(d * v, axis=0, keepdims=True)
            dlnb_ref[:, c * E:(c + 1) * E] += jnp.sum(d, axis=0, keepdims=True)
        dvh = [d * lng_ref[:, c * E:(c + 1) * E] for c, d in enumerate(dvn)]
        dv = _ln_bwd(vh, rstd, dvh, S)
        dhn = None
        for c in range(nc):
            dz = u_scr[:, c * E:(c + 1) * E] if c < nc // 2 else dv[c - nc // 2]
            dzp = (dz * _gelu_grad(zp_ref[c].astype(F32))).astype(BF16)
            dzp_ref[c] = dzp
            t = _dot_nt(dzp, win_ref[c])
            dhn = t if dhn is None else dhn + t
        xh, rx = _rms_stats(x_ref[...])
        hn_ref[...] = (xh * gpre_ref[...]).astype(BF16)
        dxn, dgq = _rms_bwd(xh, rx, gpre_ref[...], dhn)
        dx_ref[...] = dy + dxn
        _acc_out(dgpre_ref, first, dgq)

    row = pl.BlockSpec((tm, D), lambda i: (i, 0))

    def whole(shape):
        return pl.BlockSpec(shape, lambda i: (0,) * len(shape))

    return pl.pallas_call(
        body,
        name="sgu_bwd",
        grid=(T // tm,),
        in_specs=[row, row, row, pl.BlockSpec((nc, tm, E), lambda i: (0, i, 0)), _resident((1, D)), _resident((1, D)),
                  _chunks_spec(win, sel), _resident((1, S)), _resident((1, S)), _resident(wsm.shape),
                  _resident(wsmt.shape), _resident(bsb.shape), _chunks_spec(wout, sel)],
        out_specs=[row, row, pl.BlockSpec((nc, tm, E), lambda i: (0, i, 0)),
                   pl.BlockSpec((nc, tm, wo_rows), lambda i: (0, i, 0)), row,
                   whole((N_GROUPS, CHUNK, CHUNK)), whole((CHUNK, S)), whole((1, S)), whole((1, S)),
                   whole((1, D)), whole((1, D))],
        out_shape=[
            jax.ShapeDtypeStruct((T, D), F32),
            jax.ShapeDtypeStruct((T, D), BF16),
            jax.ShapeDtypeStruct((nc, T, E), BF16),
            jax.ShapeDtypeStruct((nc, T, wo_rows), BF16),
            jax.ShapeDtypeStruct((T, D), BF16),
            jax.ShapeDtypeStruct((N_GROUPS, CHUNK, CHUNK), F32),
            jax.ShapeDtypeStruct((CHUNK, S), F32),
            jax.ShapeDtypeStruct((1, S), F32),
            jax.ShapeDtypeStruct((1, S), F32),
            jax.ShapeDtypeStruct((1, D), F32),
            jax.ShapeDtypeStruct((1, D), F32),
        ],
        scratch_shapes=[pltpu.VMEM((tm, S), F32), pltpu.VMEM((tm, S), F32), pltpu.VMEM((tm, S), F32),
                        pltpu.VMEM((tm, S), BF16)],
        compiler_params=_cparams("arbitrary"),
    )(dy, x, m, zp, g_pre, g_post, win, lng, lnb, wsm, wsmt, bsb, wout)


def _conv_fwd_a(x, g_pre, wpw1, sel, tm):
    T, D = x.shape
    nc, E = wpw1.shape[0], wpw1.shape[-1]
    C = 2 * E
    tm = min(tm, T)

    def body(x_ref, gpre_ref, w_ref, y_ref, p_ref):
        xh, _ = _rms_stats(x_ref[...])
        hn = (xh * gpre_ref[...]).astype(BF16)
        ps = []
        for c in range(nc):
            p = _dot(hn, w_ref[c])
            p_ref[c] = p.astype(BF16)
            ps.append(p)
        for c in range(nc // 2):
            y_ref[:, c * E:(c + 1) * E] = ps[c] * jax.nn.sigmoid(ps[c + nc // 2])

    row = pl.BlockSpec((tm, D), lambda i: (i, 0))
    return pl.pallas_call(
        body,
        name="conv_fwd_a",
        grid=(T // tm,),
        in_specs=[row, _resident((1, D)), _chunks_spec(wpw1, sel)],
        out_specs=[pl.BlockSpec((tm, C), lambda i: (i, 0)), pl.BlockSpec((nc, tm, E), lambda i: (0, i, 0))],
        out_shape=[jax.ShapeDtypeStruct((T, C), F32), jax.ShapeDtypeStruct((nc, T, E), BF16)],
        compiler_params=_cparams("parallel"),
    )(x, g_pre, wpw1)


def _conv_fwd_b(x, y, wdw, bdw, lng, lnb, wpw2, g_post, sel, tm):
    T, D = x.shape
    C = y.shape[1]
    nc, E = wpw2.shape[0], wpw2.shape[-2]
    tm = min(tm, T)
    per = tm // HALO

    def body(x_ref, y_ref, yprev_ref, wdw_ref, bdw_ref, lng_ref, lnb_ref, w_ref, gpost_ref,
             xo_ref, c_ref, m_ref, ybuf):
        i = pl.program_id(0)
        ybuf[0:HALO, :] = jnp.where(i > 0, yprev_ref[...], 0.0)
        ybuf[HALO:HALO + tm, :] = y_ref[...]
        off = HALO - (CONV_W - 1)
        for r0 in range(0, tm, CONV_RB):
            for c0 in range(0, C, CONV_CB):
                cols = slice(c0, c0 + CONV_CB)
                acc = jnp.broadcast_to(bdw_ref[:, cols], (CONV_RB, CONV_CB))
                for k in range(CONV_W):
                    acc = acc + wdw_ref[k:k + 1, cols] * ybuf[r0 + off + k:r0 + off + k + CONV_RB, cols]
                c_ref[r0:r0 + CONV_RB, cols] = acc
        (ch,), _ = _ln_stats([c_ref[...]], C)
        cn = ch * lng_ref[...] + lnb_ref[...]
        qv = (cn * jax.nn.sigmoid(cn)).astype(BF16)
        m = _dot(qv[:, 0:E], w_ref[0])
        for c in range(1, nc):
            m += _dot(qv[:, c * E:(c + 1) * E], w_ref[c])
        m_ref[...] = m
        mh, _ = _rms_stats(m)
        xo_ref[...] = x_ref[...] + mh * gpost_ref[...]

    row = pl.BlockSpec((tm, D), lambda i: (i, 0))
    crow = pl.BlockSpec((tm, C), lambda i: (i, 0))
    prev = pl.BlockSpec((HALO, C), lambda i: (jnp.maximum(i * per - 1, 0), 0))
    return pl.pallas_call(
        body,
        name="conv_fwd_b",
        grid=(T // tm,),
        in_specs=[row, crow, prev, _resident(wdw.shape), _resident((1, C)), _resident((1, C)), _resident((1, C)),
                  _chunks_spec(wpw2, sel), _resident((1, D))],
        out_specs=[row, crow, row],
        out_shape=[jax.ShapeDtypeStruct((T, D), F32), jax.ShapeDtypeStruct((T, C), F32),
                   jax.ShapeDtypeStruct((T, D), F32)],
        scratch_shapes=[pltpu.VMEM((HALO + tm, C), F32)],
        compiler_params=_cparams("parallel"),
    )(x, y, y, wdw, bdw, lng, lnb, wpw2, g_post)


def _conv_bwd_b(dy, m, c, lng, lnb, wpw2, g_post, sel, tm):
    T, D = dy.shape
    C = c.shape[1]
    nc, E = wpw2.shape[0], wpw2.shape[-2]
    tm = min(tm, T)

    def body(dy_ref, m_ref, c_ref, lng_ref, lnb_ref, w_ref, gpost_ref,
             dm_ref, q_ref, dc_ref, dlng_ref, dlnb_ref, dbdw_ref, dgpost_ref, dq_scr):
        first = pl.program_id(0) == 0
        mh, rm = _rms_stats(m_ref[...])
        dm, dgp = _rms_bwd(mh, rm, gpost_ref[...], dy_ref[...])
        _acc_out(dgpost_ref, first, dgp)
        dm = dm.astype(BF16)
        dm_ref[...] = dm
        for k in range(nc):
            dq_scr[:, k * E:(k + 1) * E] = _dot_nt(dm, w_ref[k])
        (ch,), rstd = _ln_stats([c_ref[...]], C)
        cn = ch * lng_ref[...] + lnb_ref[...]
        sg = jax.nn.sigmoid(cn)
        qv = (cn * sg).astype(BF16)
        for k in range(nc):
            q_ref[k] = qv[:, k * E:(k + 1) * E]
        dcn = dq_scr[...] * (sg * (1.0 + cn * (1.0 - sg)))
        _acc_out(dlng_ref, first, jnp.sum(dcn * ch, axis=0, keepdims=True))
        _acc_out(dlnb_ref, first, jnp.sum(dcn, axis=0, keepdims=True))
        (dc,) = _ln_bwd([ch], rstd, [dcn * lng_ref[...]], C)
        dc_ref[...] = dc
        _acc_out(dbdw_ref, first, jnp.sum(dc, axis=0, keepdims=True))

    row = pl.BlockSpec((tm, D), lambda i: (i, 0))
    crow = pl.BlockSpec((tm, C), lambda i: (i, 0))

    def whole(shape):
        return pl.BlockSpec(shape, lambda i: (0,) * len(shape))

    return pl.pallas_call(
        body,
        name="conv_bwd_b",
        grid=(T // tm,),
        in_specs=[row, row, crow, _resident((1, C)), _resident((1, C)), _chunks_spec(wpw2, sel), _resident((1, D))],
        out_specs=[row, pl.BlockSpec((nc, tm, E), lambda i: (0, i, 0)), crow, whole((1, C)), whole((1, C)),
                   whole((1, C)), whole((1, D))],
        out_shape=[jax.ShapeDtypeStruct((T, D), BF16), jax.ShapeDtypeStruct((nc, T, E), BF16),
                   jax.ShapeDtypeStruct((T, C), F32), jax.ShapeDtypeStruct((1, C), F32),
                   jax.ShapeDtypeStruct((1, C), F32), jax.ShapeDtypeStruct((1, C), F32),
                   jax.ShapeDtypeStruct((1, D), F32)],
        scratch_shapes=[pltpu.VMEM((tm, C), F32)],
        compiler_params=_cparams("arbitrary"),
    )(dy, m, c, lng, lnb, wpw2, g_post)


def _conv_bwd_a(dy, x, dc, y, p, g_pre, wdw, wpw1, sel, tm):
    T, D = x.shape
    C = y.shape[1]
    nc, E = wpw1.shape[0], wpw1.shape[-1]
    tm = min(tm, T)
    per = tm // HALO
    n_tiles = T // tm
    KP = wdw.shape[0]

    def body(dy_ref, x_ref, dc_ref, dcnext_ref, y_ref, yprev_ref, p_ref, gpre_ref, wdw_ref, w_ref,
             dx_ref, hn_ref, dp_ref, dwdw_ref, dgpre_ref, ybuf, dcbuf, dyg_scr, dw8_scr):
        i = pl.program_id(0)
        first = i == 0
        ybuf[0:HALO, :] = jnp.where(i > 0, yprev_ref[...], 0.0)
        ybuf[HALO:HALO + tm, :] = y_ref[...]
        dcbuf[0:tm, :] = dc_ref[...]
        dcbuf[tm:tm + HALO, :] = jnp.where(i < n_tiles - 1, dcnext_ref[...], 0.0)
        off = HALO - (CONV_W - 1)
        @pl.when(first)
        def _():
            dw8_scr[...] = jnp.zeros_like(dw8_scr)

        for r0 in range(0, tm, CONV_RB):
            for c0 in range(0, C, CONV_CB):
                cols = slice(c0, c0 + CONV_CB)
                dcb = dcbuf[r0:r0 + CONV_RB, cols]
                acc = jnp.zeros((CONV_RB, CONV_CB), F32)
                for k in range(CONV_W):
                    back = CONV_W - 1 - k
                    acc = acc + wdw_ref[k:k + 1, cols] * dcbuf[r0 + back:r0 + back + CONV_RB, cols]
                    prod = dcb * ybuf[r0 + off + k:r0 + off + k + CONV_RB, cols]
                    dw8_scr[k, :, cols] += jnp.sum(prod.reshape(CONV_RB // 8, 8, CONV_CB), axis=0)
                dyg_scr[r0:r0 + CONV_RB, cols] = acc

        @pl.when(i == n_tiles - 1)
        def _():
            dwdw_ref[...] = jnp.sum(dw8_scr[...], axis=1)

        dhn = None
        for c in range(nc // 2):
            cols = slice(c * E, (c + 1) * E)
            av = p_ref[c].astype(F32)
            sg = jax.nn.sigmoid(p_ref[c + nc // 2].astype(F32))
            dygc = dyg_scr[:, cols]
            da = (dygc * sg).astype(BF16)
            dgt = (dygc * av * sg * (1.0 - sg)).astype(BF16)
            dp_ref[c] = da
            dp_ref[c + nc // 2] = dgt
            t = _dot_nt(da, w_ref[c]) + _dot_nt(dgt, w_ref[c + nc // 2])
            dhn = t if dhn is None else dhn + t
        xh, rx = _rms_stats(x_ref[...])
        hn_ref[...] = (xh * gpre_ref[...]).astype(BF16)
        dxn, dgq = _rms_bwd(xh, rx, gpre_ref[...], dhn)
        dx_ref[...] = dy_ref[...] + dxn
        _acc_out(dgpre_ref, first, dgq)

    row = pl.BlockSpec((tm, D), lambda i: (i, 0))
    crow = pl.BlockSpec((tm, C), lambda i: (i, 0))
    prev = pl.BlockSpec((HALO, C), lambda i: (jnp.maximum(i * per - 1, 0), 0))
    nxt = pl.BlockSpec((HALO, C), lambda i: (jnp.minimum((i + 1) * per, T // HALO - 1), 0))
    chunks = pl.BlockSpec((nc, tm, E), lambda i: (0, i, 0))

    def whole(shape):
        return pl.BlockSpec(shape, lambda i: (0,) * len(shape))

    return pl.pallas_call(
        body,
        name="conv_bwd_a",
        grid=(n_tiles,),
        in_specs=[row, row, crow, nxt, crow, prev, chunks, _resident((1, D)), _resident(wdw.shape),
                  _chunks_spec(wpw1, sel)],
        out_specs=[row, row, chunks, whole((KP, C)), whole((1, D))],
        out_shape=[jax.ShapeDtypeStruct((T, D), F32), jax.ShapeDtypeStruct((T, D), BF16),
                   jax.ShapeDtypeStruct((nc, T, E), BF16), jax.ShapeDtypeStruct((KP, C), F32),
                   jax.ShapeDtypeStruct((1, D), F32)],
        scratch_shapes=[pltpu.VMEM((HALO + tm, C), F32), pltpu.VMEM((tm + HALO, C), F32),
                        pltpu.VMEM((tm, C), F32), pltpu.VMEM((KP, 8, C), F32)],
        compiler_params=_cparams("arbitrary"),
    )(dy, x, dc, dc, y, y, p, g_pre, wdw, wpw1)


def _loss_head(y, target, tm):
    T, D = y.shape
    tm = min(tm, T)

    def body(y_ref, t_ref, dy_ref, loss_ref):
        e = y_ref[...] - t_ref[...]
        dy_ref[...] = e * (1.0 / D)
        part = jnp.sum(jnp.sum(e * e, axis=-1, keepdims=True), axis=0, keepdims=True) * (0.5 / D)
        _acc_out(loss_ref, pl.program_id(0) == 0, jnp.broadcast_to(part, loss_ref.shape))

    row = pl.BlockSpec((tm, D), lambda i: (i, 0))
    return pl.pallas_call(
        body,
        name="loss_head",
        grid=(T // tm,),
        in_specs=[row, row],
        out_specs=[row, pl.BlockSpec((8, 128), lambda i: (0, 0))],
        out_shape=[jax.ShapeDtypeStruct((T, D), F32), jax.ShapeDtypeStruct((8, 128), F32)],
        compiler_params=_cparams("arbitrary"),
    )(y, target)


def _row_tile(rows, cols, itemsize_budget=2 * 1024 * 1024):
    want = max(16, itemsize_budget // (4 * cols))
    if rows <= want:
        return rows
    t = (want // 16) * 16
    while t > 16 and rows % t:
        t -= 16
    return t if rows % t == 0 else rows


def _sum_parts(parts):
    n, R, C = parts.shape
    tr = _row_tile(R, C * n // 2 if parts.dtype == BF16 else C * n)

    def body(p_ref, o_ref):
        acc = p_ref[0].astype(F32)
        for s in range(1, n):
            acc = acc + p_ref[s].astype(F32)
        o_ref[...] = acc

    return pl.pallas_call(
        body,
        name="sum_parts",
        grid=(R // tr,),
        in_specs=[pl.BlockSpec((n, tr, C), lambda i: (0, i, 0))],
        out_specs=pl.BlockSpec((tr, C), lambda i: (i, 0)),
        out_shape=jax.ShapeDtypeStruct((R, C), F32),
        compiler_params=_cparams("parallel"),
    )(parts)


def _cast_into_slot(w, slot):
    _, R, C = w.shape
    tr = _row_tile(R, C)

    def body(slot_ref, w_ref, o_ref):
        o_ref[...] = w_ref[...].astype(BF16)

    return pl.pallas_call(
        body,
        name="cast_into_slot",
        grid_spec=pltpu.PrefetchScalarGridSpec(
            num_scalar_prefetch=1,
            grid=(2, R // tr),
            in_specs=[pl.BlockSpec((None, tr, C), lambda h, i, s: (h, i, 0))],
            out_specs=pl.BlockSpec((None, None, tr, C), lambda h, i, s: (s[0], h, i, 0)),
        ),
        out_shape=jax.ShapeDtypeStruct((N_CHIPS, 2, R, C), BF16),
        compiler_params=_cparams("parallel", "parallel"),
    )(slot, w)


def _sum_with_own(arrived, own, place):
    n, R, C = arrived.shape
    tr = _row_tile(R, C * (n + 1) // 2)

    def body(place_ref, a_ref, own_ref, o_ref):
        acc = own_ref[...].astype(F32)
        for s in range(n):
            acc = acc + a_ref[s].astype(F32)
        o_ref[...] = acc

    return pl.pallas_call(
        body,
        name="sum_with_own",
        grid_spec=pltpu.PrefetchScalarGridSpec(
            num_scalar_prefetch=1,
            grid=(R // tr,),
            in_specs=[pl.BlockSpec((n, tr, C), lambda i, p: (0, i, 0)),
                      pl.BlockSpec((None, None, tr, C), lambda i, p: (p[0], p[1], i, 0))],
            out_specs=pl.BlockSpec((None, tr, C), lambda i, p: (p[1], i, 0)),
        ),
        out_shape=jax.ShapeDtypeStruct((2, R, C), F32),
        compiler_params=_cparams("parallel"),
    )(place, arrived, own)


def _adamw(w, g, m, v):
    R, C = w.shape
    tr = _row_tile(R, C * 7 // 2)
    c1 = 1.0 - ADAM_B1 ** ADAM_STEP
    c2 = 1.0 - ADAM_B2 ** ADAM_STEP

    def body(w_ref, g_ref, m_ref, v_ref, d_ref, mo_ref, vo_ref):
        g = g_ref[...]
        mn = ADAM_B1 * m_ref[...] + (1.0 - ADAM_B1) * g
        vn = ADAM_B2 * v_ref[...] + (1.0 - ADAM_B2) * (g * g)
        mo_ref[...] = mn
        vo_ref[...] = vn
        d_ref[...] = -ADAM_LR * ((mn / c1) / (jnp.sqrt(vn / c2) + ADAM_EPS) + ADAM_WD * w_ref[...])

    blk = pl.BlockSpec((tr, C), lambda i: (i, 0))
    shp = jax.ShapeDtypeStruct((R, C), F32)
    return pl.pallas_call(
        body,
        name="adamw",
        grid=(R // tr,),
        in_specs=[blk, blk, blk, blk],
        out_specs=[blk, blk, blk],
        out_shape=[shp, shp, shp],
        compiler_params=_cparams("parallel"),
    )(w, g, m, v)


def _my_place():
    return lax.axis_index("x"), lax.axis_index("y"), lax.axis_index("c")


def _gather_weights(halved, whole):
    nh, nw = len(halved), len(whole)

    def body(*refs):
        w_in = refs[nh:nh + nw]
        h_out, w_out = refs[nh + nw:2 * nh + nw], refs[2 * nh + nw:2 * (nh + nw)]
        hs_send, hs_recv, fw_send, fw_recv, ws_send, ws_recv, loc_sem = refs[2 * (nh + nw):]
        x, y, c = _my_place()
        me_chip = 2 * x + y
        sib = (x, y, 1 - c)
        chips = [(1 - x, y), (x, 1 - y), (1 - x, 1 - y)]

        def chip_no(ch):
            return 2 * ch[0] + ch[1]

        local = []
        for a in range(nw):
            cp = pltpu.make_async_copy(w_in[a], w_out[a].at[me_chip], loc_sem.at[a])
            cp.start()
            local.append(cp)

        def landed(a, j, ch, half, to, sems):
            spot = h_out[a].at[chip_no(ch), pl.ds(half, 1)]
            return pltpu.make_async_remote_copy(src_ref=spot, dst_ref=spot, send_sem=sems[0].at[a, j],
                                                recv_sem=sems[1].at[a, j], device_id=to, device_id_type=MESH)

        def own_half_copy(a, j, to):
            spot = h_out[a].at[me_chip, pl.ds(c, 1)]
            return pltpu.make_async_remote_copy(src_ref=spot, dst_ref=spot,
                                                send_sem=hs_send.at[a, j], recv_sem=hs_recv.at[a, j],
                                                device_id=to, device_id_type=MESH)

        sends = []
        for j, ch in enumerate(chips):
            for a in range(nh):
                cp = own_half_copy(a, j, (*ch, c))
                cp.start()
                sends.append(cp)
            for a in range(nw):
                cp = pltpu.make_async_remote_copy(src_ref=w_in[a], dst_ref=w_out[a].at[me_chip],
                                                  send_sem=ws_send.at[a, j], recv_sem=ws_recv.at[a, j],
                                                  device_id=(*ch, c), device_id_type=MESH)
                cp.start()
                sends.append(cp)
        for j, ch in enumerate(chips):
            for a in range(nh):
                landed(a, j, ch, c, (x, y, c), (hs_send, hs_recv)).wait_recv()
                cp = landed(a, j, ch, c, sib, (fw_send, fw_recv))
                cp.start()
                sends.append(cp)
        for j, ch in enumerate(chips):
            for a in range(nh):
                landed(a, j, ch, 1 - c, (x, y, c), (fw_send, fw_recv)).wait_recv()
            for a in range(nw):
                pltpu.make_async_remote_copy(src_ref=w_in[a], dst_ref=w_out[a].at[chip_no(ch)],
                                             send_sem=ws_send.at[a, j], recv_sem=ws_recv.at[a, j],
                                             device_id=(x, y, c), device_id_type=MESH).wait_recv()
        for cp in sends:
            cp.wait_send()
        for cp in local:
            cp.wait()

    hbm = pl.BlockSpec(memory_space=pl.ANY)
    outs = pl.pallas_call(
        body,
        name="gather_weights",
        in_specs=[hbm] * (nh + nw),
        out_specs=[hbm] * (nh + nw),
        out_shape=[jax.ShapeDtypeStruct(a.shape, a.dtype) for a in halved]
        + [jax.ShapeDtypeStruct((N_CHIPS, *a.shape), a.dtype) for a in whole],
        input_output_aliases={a: a for a in range(nh)},
        scratch_shapes=[
            pltpu.SemaphoreType.DMA((nh, 3)), pltpu.SemaphoreType.DMA((nh, 3)),
            pltpu.SemaphoreType.DMA((nh, 3)), pltpu.SemaphoreType.DMA((nh, 3)),
            pltpu.SemaphoreType.DMA((max(nw, 1), 3)), pltpu.SemaphoreType.DMA((max(nw, 1), 3)),
            pltpu.SemaphoreType.DMA((max(nw, 1),)),
        ],
    )(*halved, *whole)
    return outs[:nh], outs[nh:]


def _scatter_grads(grads):
    n = len(grads)

    def body(*refs):
        g_in, g_out = refs[:n], refs[n:2 * n]
        send_sem, recv_sem = refs[2 * n:]
        x, y, c = _my_place()
        me = 4 * x + 2 * y + c
        def piece(a, d, slot):
            return pltpu.make_async_remote_copy(src_ref=g_in[a].at[d // 2, d % 2], dst_ref=g_out[a].at[slot],
                                                send_sem=send_sem.at[a, d], recv_sem=recv_sem.at[a, slot],
                                                device_id=(d // 4, (d // 2) % 2, d % 2), device_id_type=MESH)

        for a in range(n):
            for d in range(N_DEV):
                @pl.when(d != me)
                def _():
                    piece(a, d, lax.rem(me - d - 1 + N_DEV, N_DEV)).start()
        for a in range(n):
            for slot in range(N_DEV - 1):
                piece(a, 0, slot).wait_recv()
            for d in range(N_DEV):
                @pl.when(d != me)
                def _():
                    piece(a, d, 0).wait_send()

    hbm = pl.BlockSpec(memory_space=pl.ANY)
    return pl.pallas_call(
        body,
        name="scatter_grads",
        in_specs=[hbm] * n,
        out_specs=[hbm] * n,
        out_shape=[jax.ShapeDtypeStruct((N_DEV - 1, *g.shape[2:]), g.dtype) for g in grads],
        scratch_shapes=[pltpu.SemaphoreType.DMA((n, N_DEV)), pltpu.SemaphoreType.DMA((n, N_DEV - 1))],
    )(*grads)


def _swap_halves(halves):
    n = len(halves)

    def body(*refs):
        h_out = refs[n:2 * n]
        send_sem, recv_sem = refs[2 * n:]
        x, y, c = _my_place()
        sib = (x, y, 1 - c)
        for a in range(n):
            pltpu.make_async_remote_copy(src_ref=h_out[a].at[c], dst_ref=h_out[a].at[c], send_sem=send_sem.at[a],
                                         recv_sem=recv_sem.at[a], device_id=sib, device_id_type=MESH).start()
        for a in range(n):
            cp = pltpu.make_async_remote_copy(src_ref=h_out[a].at[c], dst_ref=h_out[a].at[1 - c],
                                              send_sem=send_sem.at[a], recv_sem=recv_sem.at[a], device_id=sib,
                                              device_id_type=MESH)
            cp.wait_send()
            cp.wait_recv()

    hbm = pl.BlockSpec(memory_space=pl.ANY)
    return pl.pallas_call(
        body,
        name="swap_halves",
        in_specs=[hbm] * n,
        out_specs=[hbm] * n,
        out_shape=[jax.ShapeDtypeStruct(h.shape, h.dtype) for h in halves],
        input_output_aliases={a: a for a in range(n)},
        scratch_shapes=[pltpu.SemaphoreType.DMA((n,)), pltpu.SemaphoreType.DMA((n,))],
    )(*halves)


def _share_all(buf):
    def body(b_in, b_out, send_sem, recv_sem):
        x, y, c = _my_place()
        me = 4 * x + 2 * y + c
        for d in range(N_DEV):
            @pl.when(d != me)
            def _():
                pltpu.make_async_remote_copy(src_ref=b_out.at[me], dst_ref=b_out.at[me], send_sem=send_sem.at[d],
                                             recv_sem=recv_sem.at[me], device_id=(d // 4, (d // 2) % 2, d % 2),
                                             device_id_type=MESH).start()
        for d in range(N_DEV):
            @pl.when(d != me)
            def _():
                cp = pltpu.make_async_remote_copy(src_ref=b_out.at[me], dst_ref=b_out.at[d], send_sem=send_sem.at[d],
                                                  recv_sem=recv_sem.at[d], device_id=(x, y, c),
                                                  device_id_type=MESH)
                cp.wait_send()
                cp.wait_recv()

    hbm = pl.BlockSpec(memory_space=pl.ANY)
    return pl.pallas_call(
        body,
        name="share_all",
        in_specs=[hbm],
        out_specs=hbm,
        out_shape=jax.ShapeDtypeStruct(buf.shape, buf.dtype),
        input_output_aliases={0: 0},
        scratch_shapes=[pltpu.SemaphoreType.DMA((N_DEV,)), pltpu.SemaphoreType.DMA((N_DEV,))],
    )(buf)


TM_FFN = 512
TM_SGU = 256
TM_CONV = 256
TK_WGRAD = 2048
TM_LOSS = 1024


def _local_step(x, target, W):
    depth = W["norm_g"].shape[0]
    saved = []
    n_a = n_b = 0
    vec = lambda v: v.reshape(1, -1)
    wsm, wsmt, bsb = [], [], []
    n_sgu = W["sgu_w_spatial"].shape[0]
    causal = jnp.tril(jnp.ones((CHUNK, CHUNK), dtype=bool))
    dgrp = W["sgu_ln_g"].shape[1] // N_GROUPS
    for jx in range(n_sgu):
        ws = jnp.where(causal[None], W["sgu_w_spatial"][jx], 0.0).astype(BF16)
        wsm.append(ws)
        wsmt.append(jnp.swapaxes(ws, 1, 2))
        bsb.append(jnp.broadcast_to(W["sgu_b_spatial"][jx][:, :, None], (N_GROUPS, CHUNK, dgrp)))
    kp = HALO
    wdw = [jnp.pad(W["conv_w_dw"][jx], ((0, kp - CONV_W), (0, 0))) for jx in range(W["conv_w_dw"].shape[0])]

    def ffn(x, i, f_idx):
        g = W["norm_g"][i]
        return _ffn_fwd(x, vec(g[4 * f_idx]), vec(g[4 * f_idx + 1]), W["ff_w_gate"], W["ff_w_up"], W["ff_w_down"],
                        (i, f_idx), TM_FFN)

    for i in range(depth):
        g = W["norm_g"][i]
        rec = {"x0": x}
        x, rec["a1"], rec["b1"], rec["f1"] = ffn(x, i, 0)
        rec["x1"] = x
        j = i // 2
        if i % 2 == 0:
            x, rec["zp"], rec["m"] = _sgu_fwd(
                x, vec(g[2]), vec(g[3]), W["sgu_w_in"], vec(W["sgu_ln_g"][j]), vec(W["sgu_ln_b"][j]),
                wsm[j], bsb[j], W["sgu_w_out"], (j,), TM_SGU)
        else:
            rec["y"], rec["p"] = _conv_fwd_a(x, vec(g[2]), W["conv_w_pw1"], (j,), TM_CONV)
            x, rec["c"], rec["m"] = _conv_fwd_b(
                x, rec["y"], wdw[j], vec(W["conv_b_dw"][j]), vec(W["conv_ln_g"][j]), vec(W["conv_ln_b"][j]),
                W["conv_w_pw2"], vec(g[3]), (j,), TM_CONV)
        rec["x2"] = x
        x, rec["a2"], rec["b2"], rec["f2"] = ffn(x, i, 1)
        saved.append(rec)

    dx, loss_tile = _loss_head(x, target, TM_LOSS)
    loss = loss_tile[0, 0]

    big = {k: None for k in
           ("ff_w_gate", "ff_w_up", "ff_w_down", "sgu_w_in", "sgu_w_out", "conv_w_pw1", "conv_w_pw2")}
    small = {k: [None] * W[k].shape[0] for k in
             ("sgu_ln_g", "sgu_ln_b", "sgu_w_spatial", "sgu_b_spatial", "conv_w_dw", "conv_b_dw", "conv_ln_g",
              "conv_ln_b")}
    dnorm = [[None] * 6 for _ in range(depth)]

    def wgrad(k, a, b, sel):
        big[k] = _tn_matmul(a, b, big[k], W[k], sel, TK_WGRAD)

    def ffn_back(dx, i, f_idx, xin, a, b, f):
        g = W["norm_g"][i]
        dx, h, dz, s, da, db, dgpre, dgpost = _ffn_bwd(
            dx, xin, f, a, b, vec(g[4 * f_idx]), vec(g[4 * f_idx + 1]),
            W["ff_w_gate"], W["ff_w_up"], W["ff_w_down"], (i, f_idx), TM_FFN)
        wgrad("ff_w_gate", h, da, (i, f_idx))
        wgrad("ff_w_up", h, db, (i, f_idx))
        wgrad("ff_w_down", s, dz, (i, f_idx))
        dnorm[i][4 * f_idx] = dgpre[0]
        dnorm[i][4 * f_idx + 1] = dgpost[0]
        return dx

    for i in reversed(range(depth)):
        rec = saved[i]
        g = W["norm_g"][i]
        j = i // 2
        dx = ffn_back(dx, i, 1, rec["x2"], rec["a2"], rec["b2"], rec["f2"])
        if i % 2 == 0:
            (dx, hn, dzp, gated, dm, dws, dbs_acc, dlng, dlnb, dgpre, dgpost) = _sgu_bwd(
                dx, rec["x1"], rec["m"], rec["zp"], vec(g[2]), vec(g[3]), W["sgu_w_in"], vec(W["sgu_ln_g"][j]),
                vec(W["sgu_ln_b"][j]), wsm[j], wsmt[j], bsb[j], W["sgu_w_out"], (j,), TM_SGU)
            wgrad("sgu_w_in", hn, dzp, (j,))
            wgrad("sgu_w_out", gated, dm, (j,))
            small["sgu_w_spatial"][j] = jnp.where(causal[None], dws, 0.0)
            small["sgu_b_spatial"][j] = dbs_acc.reshape(CHUNK, N_GROUPS, dgrp).sum(-1).T
            small["sgu_ln_g"][j] = dlng[0]
            small["sgu_ln_b"][j] = dlnb[0]
        else:
            dm, q, dc, dlng, dlnb, dbdw, dgpost = _conv_bwd_b(
                dx, rec["m"], rec["c"], vec(W["conv_ln_g"][j]), vec(W["conv_ln_b"][j]), W["conv_w_pw2"],
                vec(g[3]), (j,), TM_CONV)
            dx, hn, dp, dwdw, dgpre = _conv_bwd_a(
                dx, rec["x1"], dc, rec["y"], rec["p"], vec(g[2]), wdw[j], W["conv_w_pw1"], (j,), TM_CONV)
            wgrad("conv_w_pw1", hn, dp, (j,))
            wgrad("conv_w_pw2", q, dm, (j,))
            small["conv_w_dw"][j] = dwdw[:CONV_W]
            small["conv_b_dw"][j] = dbdw[0]
            small["conv_ln_g"][j] = dlng[0]
            small["conv_ln_b"][j] = dlnb[0]
        dnorm[i][2] = dgpre[0]
        dnorm[i][3] = dgpost[0]
        dx = ffn_back(dx, i, 0, rec["x0"], rec["a1"], rec["b1"], rec["f1"])

    small = {k: jnp.stack(v) for k, v in small.items()}
    small["norm_g"] = jnp.stack([jnp.stack(r) for r in dnorm])
    return loss, dx, big, small


BIG = ("ff_w_gate", "ff_w_up", "ff_w_down", "sgu_w_in", "sgu_w_out", "conv_w_pw1", "conv_w_pw2")
SHARDED_SMALL = ("norm_g", "conv_w_dw", "conv_b_dw", "conv_ln_g", "conv_ln_b")
REPLICATED = ("sgu_ln_g", "sgu_ln_b", "sgu_w_spatial", "sgu_b_spatial")
WEIGHTS = ("norm_g", "ff_w_gate", "ff_w_up", "ff_w_down", "sgu_w_in", "sgu_ln_g", "sgu_ln_b", "sgu_w_spatial",
           "sgu_b_spatial", "sgu_w_out", "conv_w_pw1", "conv_w_dw", "conv_b_dw", "conv_ln_g", "conv_ln_b",
           "conv_w_pw2")


def _rows8(a, width):
    r = a.reshape(-1, width)
    pad = (-r.shape[0]) % 8
    return jnp.pad(r, ((0, pad), (0, 0))) if pad else r


def _pack(arrs, width):
    parts = [_rows8(a, width) for a in arrs]
    return jnp.concatenate(parts, axis=0), [p.shape[0] for p in parts]


def _unpack(buf, like):
    out, r0 = [], 0
    width = buf.shape[-1]
    for a in like:
        n = -(-(a.size // width) // 8) * 8
        rows = a.size // width
        out.append(buf[..., r0:r0 + rows, :].reshape(*buf.shape[:-2], *a.shape))
        r0 += n
    return out


def kernel(x, norm_g, ff_w_gate, ff_w_up, ff_w_down, sgu_w_in, sgu_ln_g, sgu_ln_b, sgu_w_spatial, sgu_b_spatial, sgu_w_out, conv_w_pw1, conv_w_dw, conv_b_dw, conv_ln_g, conv_ln_b, conv_w_pw2, loss_target, m_norm_g, m_ff_w_gate, m_ff_w_up, m_ff_w_down, m_sgu_w_in, m_sgu_ln_g, m_sgu_ln_b, m_sgu_w_spatial, m_sgu_b_spatial, m_sgu_w_out, m_conv_w_pw1, m_conv_w_dw, m_conv_b_dw, m_conv_ln_g, m_conv_ln_b, m_conv_w_pw2, v_norm_g, v_ff_w_gate, v_ff_w_up, v_ff_w_down, v_sgu_w_in, v_sgu_ln_g, v_sgu_ln_b, v_sgu_w_spatial, v_sgu_b_spatial, v_sgu_w_out, v_conv_w_pw1, v_conv_w_dw, v_conv_b_dw, v_conv_ln_g, v_conv_ln_b, v_conv_w_pw2):
    w = dict(norm_g=norm_g, ff_w_gate=ff_w_gate, ff_w_up=ff_w_up, ff_w_down=ff_w_down, sgu_w_in=sgu_w_in,
             sgu_ln_g=sgu_ln_g, sgu_ln_b=sgu_ln_b, sgu_w_spatial=sgu_w_spatial, sgu_b_spatial=sgu_b_spatial,
             sgu_w_out=sgu_w_out, conv_w_pw1=conv_w_pw1, conv_w_dw=conv_w_dw, conv_b_dw=conv_b_dw,
             conv_ln_g=conv_ln_g, conv_ln_b=conv_ln_b, conv_w_pw2=conv_w_pw2)
    mom = dict(norm_g=m_norm_g, ff_w_gate=m_ff_w_gate, ff_w_up=m_ff_w_up, ff_w_down=m_ff_w_down,
               sgu_w_in=m_sgu_w_in, sgu_ln_g=m_sgu_ln_g, sgu_ln_b=m_sgu_ln_b, sgu_w_spatial=m_sgu_w_spatial,
               sgu_b_spatial=m_sgu_b_spatial, sgu_w_out=m_sgu_w_out, conv_w_pw1=m_conv_w_pw1,
               conv_w_dw=m_conv_w_dw, conv_b_dw=m_conv_b_dw, conv_ln_g=m_conv_ln_g, conv_ln_b=m_conv_ln_b,
               conv_w_pw2=m_conv_w_pw2)
    vel = dict(norm_g=v_norm_g, ff_w_gate=v_ff_w_gate, ff_w_up=v_ff_w_up, ff_w_down=v_ff_w_down,
               sgu_w_in=v_sgu_w_in, sgu_ln_g=v_sgu_ln_g, sgu_ln_b=v_sgu_ln_b, sgu_w_spatial=v_sgu_w_spatial,
               sgu_b_spatial=v_sgu_b_spatial, sgu_w_out=v_sgu_w_out, conv_w_pw1=v_conv_w_pw1,
               conv_w_dw=v_conv_w_dw, conv_b_dw=v_conv_b_dw, conv_ln_g=v_conv_ln_g, conv_ln_b=v_conv_ln_b,
               conv_w_pw2=v_conv_w_pw2)
    T, D = x.shape[1], x.shape[2]
    shard_w = conv_b_dw.shape[1]

    xi, yi, ci = _my_place()
    me_chip = (2 * xi + yi).astype(jnp.int32)
    me = (4 * xi + 2 * yi + ci).astype(jnp.int32)

    halved = [_cast_into_slot(w[k].reshape(2, -1, w[k].shape[-1]), me_chip.reshape(1)) for k in BIG]
    small_buf, _ = _pack([w[k] for k in SHARDED_SMALL], shard_w)
    gathered, (small_all,) = _gather_weights(halved, [small_buf])
    W = {k: gth.reshape(N_CHIPS, *w[k].shape) for k, gth in zip(BIG, gathered)}
    for k, part in zip(SHARDED_SMALL, _unpack(small_all, [w[k] for k in SHARDED_SMALL])):
        W[k] = jnp.moveaxis(part, 0, -2).reshape(*w[k].shape[:-1], N_CHIPS * shard_w)
    for k in REPLICATED:
        W[k] = w[k]

    loss, dx, big, small = _local_step(x[0], loss_target[0], W)
    loss = lax.psum(loss, ("x", "y", "c"))

    pieces = []
    for k in BIG:
        g = big[k]
        pieces.append(g.reshape(N_CHIPS, 2, -1, g.shape[-1]))
    arrived = _scatter_grads(pieces)
    place = jnp.stack([me_chip, ci.astype(jnp.int32)])
    both = _swap_halves([_sum_with_own(a, p, place) for a, p in zip(arrived, pieces)])
    grads = {k: b.reshape(w[k].shape) for k, b in zip(BIG, both)}

    sbuf, _ = _pack([small[k] for k in SHARDED_SMALL + REPLICATED], D)
    slots = lax.dynamic_update_slice(jnp.zeros((N_DEV, *sbuf.shape), F32), sbuf[None], (me, 0, 0))
    ssum = _sum_parts(_share_all(slots))
    for k, gfull in zip(SHARDED_SMALL + REPLICATED, _unpack(ssum, [small[k] for k in SHARDED_SMALL + REPLICATED])):
        if k in SHARDED_SMALL:
            gfull = lax.dynamic_slice_in_dim(gfull, me_chip * shard_w, shard_w, axis=gfull.ndim - 1)
        grads[k] = gfull

    delta, new_m, new_v = {}, {}, {}
    for k in BIG:
        w2 = lambda a: a.reshape(-1, a.shape[-1])
        d, mn, vn = _adamw(w2(w[k]), w2(grads[k]), w2(mom[k]), w2(vel[k]))
        delta[k], new_m[k], new_v[k] = (t.reshape(w[k].shape) for t in (d, mn, vn))
    for names, width in ((SHARDED_SMALL, shard_w), (REPLICATED, CHUNK)):
        packed = [_pack([src[k] for k in names], width)[0] for src in (w, grads, mom, vel)]
        outs = _adamw(*packed)
        for res, out in zip((delta, new_m, new_v), outs):
            for k, a in zip(names, _unpack(out, [w[k] for k in names])):
                res[k] = a

    return (loss, dx[None], *[grads[k] for k in WEIGHTS], *[delta[k] for k in WEIGHTS],
            *[new_m[k] for k in WEIGHTS], *[new_v[k] for k in WEIGHTS])
```

```python
import functools

import jax
import jax.numpy as jnp
from jax import lax
from jax.experimental import pallas as pl
from jax.experimental.pallas import tpu as pltpu

F32 = jnp.float32
BF16 = jnp.bfloat16
EPS = 1e-6
N_CHIPS = 4
N_DEV = 8
N_GROUPS = 8
CHUNK = 128
CONV_W = 31
HALO = 32
CONV_RB = 64
CONV_CB = 256
VMEM_LIMIT_V7X = 56 * 1024 * 1024
MESH = pl.DeviceIdType.MESH

ADAM_LR = 0.001
ADAM_B1 = 0.9
ADAM_B2 = 0.999
ADAM_EPS = 1e-08
ADAM_WD = 0.01
ADAM_STEP = 10
FFN_SCALE = 0.5


def _cparams(*sem, **kw):
    return pltpu.CompilerParams(dimension_semantics=sem, vmem_limit_bytes=VMEM_LIMIT_V7X, **kw)


def _resident(shape):
    return pl.BlockSpec(shape, lambda *_: (0,) * len(shape), pipeline_mode=pl.Buffered(1))


def _dot(a, b):
    return jnp.dot(a, b, preferred_element_type=F32)


def _dot_nt(a, b):
    return lax.dot_general(a, b, (((1,), (1,)), ((), ())), preferred_element_type=F32)


def _dot_tn(a, b):
    return lax.dot_general(a, b, (((0,), (0,)), ((), ())), preferred_element_type=F32)


def _rms_stats(x):
    r = lax.rsqrt(jnp.mean(x * x, axis=-1, keepdims=True) + EPS)
    return x * r, r


def _rms_bwd(xh, r, g, dy):
    dxh = dy * g
    dx = r * (dxh - xh * jnp.mean(dxh * xh, axis=-1, keepdims=True))
    return dx, jnp.sum(dy * xh, axis=0, keepdims=True)


def _ln_stats(parts, width):
    mu = sum(jnp.sum(p, axis=-1, keepdims=True) for p in parts) / width
    cen = [p - mu for p in parts]
    var = sum(jnp.sum(c * c, axis=-1, keepdims=True) for c in cen) / width
    rstd = lax.rsqrt(var + EPS)
    return [c * rstd for c in cen], rstd


def _ln_bwd(vh_parts, rstd, dvh_parts, width):
    m1 = sum(jnp.sum(d, axis=-1, keepdims=True) for d in dvh_parts) / width
    m2 = sum(jnp.sum(d * v, axis=-1, keepdims=True) for d, v in zip(dvh_parts, vh_parts)) / width
    return [rstd * (d - m1 - v * m2) for d, v in zip(dvh_parts, vh_parts)]


_GELU_C = 0.7978845608028654
_GELU_A = 0.044715


def _gelu(x):
    return 0.5 * x * (1.0 + jnp.tanh(_GELU_C * (x + _GELU_A * x * x * x)))


def _gelu_grad(x):
    t = jnp.tanh(_GELU_C * (x + _GELU_A * x * x * x))
    return 0.5 * (1.0 + t) + 0.5 * x * (1.0 - t * t) * _GELU_C * (1.0 + 3.0 * _GELU_A * x * x)


def _sigmoid_pair(a):
    e = jnp.exp(jnp.minimum(-a, 80.0))
    sg = 1.0 / (1.0 + e)
    return sg, e * sg


def _acc_out(ref, first, val):
    @pl.when(first)
    def _():
        ref[...] = val

    @pl.when(jnp.logical_not(first))
    def _():
        ref[...] += val


def _my_place():
    return lax.axis_index("x"), lax.axis_index("y"), lax.axis_index("c")


class _Gather:
    def __init__(self, bufs, sems):
        self.bufs = bufs
        self.own_sems, self.fwd_sems = sems[:2], sems[2:4]
        self.x, self.y, self.c = _my_place()
        x, y = self.x, self.y
        self.chips = [(1 - x, y), (x, 1 - y), (1 - x, 1 - y)]

    def _copy(self, a, j, chip, half, to, sems):
        spot = self.bufs[a].at[2 * chip[0] + chip[1], pl.ds(half, 1)]
        return pltpu.make_async_remote_copy(src_ref=spot, dst_ref=spot, send_sem=sems[0].at[a, j],
                                            recv_sem=sems[1].at[a, j], device_id=to, device_id_type=MESH)

    def _own(self, a, j):
        return self._copy(a, j, (self.x, self.y), self.c, (*self.chips[j], self.c), self.own_sems)

    def _passed_on(self, a, j):
        return self._copy(a, j, self.chips[j], self.c, (self.x, self.y, 1 - self.c), self.fwd_sems)

    def start(self):
        for j in range(3):
            for a in range(len(self.bufs)):
                self._own(a, j).start()

    def forward(self):
        me = (self.x, self.y, self.c)
        for j in range(3):
            for a in range(len(self.bufs)):
                self._copy(a, j, self.chips[j], self.c, me, self.own_sems).wait_recv()
                self._passed_on(a, j).start()

    def finish(self):
        me = (self.x, self.y, self.c)
        for j in range(3):
            for a in range(len(self.bufs)):
                self._copy(a, j, self.chips[j], 1 - self.c, me, self.fwd_sems).wait_recv()
        for j in range(3):
            for a in range(len(self.bufs)):
                self._own(a, j).wait_send()
                self._passed_on(a, j).wait_send()

    @staticmethod
    def semaphores(n):
        return [pltpu.SemaphoreType.DMA((n, 3)) for _ in range(4)]


class _Scatter:
    def __init__(self, g_in, g_out, sems):
        self.g_in, self.g_out = g_in, g_out
        self.send_sem, self.recv_sem = sems
        x, y, c = _my_place()
        self.me = 4 * x + 2 * y + c

    def _piece(self, a, d, slot):
        return pltpu.make_async_remote_copy(
            src_ref=self.g_in[a].at[d // 2, d % 2], dst_ref=self.g_out[a].at[slot],
            send_sem=self.send_sem.at[a, d], recv_sem=self.recv_sem.at[a, slot],
            device_id=(d // 4, (d // 2) % 2, d % 2), device_id_type=MESH)

    def start(self):
        for a in range(len(self.g_in)):
            for d in range(N_DEV):
                @pl.when(d != self.me)
                def _():
                    self._piece(a, d, lax.rem(self.me - d - 1 + N_DEV, N_DEV)).start()

    def finish(self):
        for a in range(len(self.g_in)):
            for slot in range(N_DEV - 1):
                self._piece(a, 0, slot).wait_recv()
            for d in range(N_DEV):
                @pl.when(d != self.me)
                def _():
                    self._piece(a, d, 0).wait_send()

    @staticmethod
    def semaphores(n):
        return [pltpu.SemaphoreType.DMA((n, N_DEV)), pltpu.SemaphoreType.DMA((n, N_DEV - 1))]


def _chunk_spec(sel, rows, cols):
    return pl.BlockSpec((None,) * (1 + len(sel)) + (rows, cols), lambda i, j: (j, *sel, 0, 0))


def _chunks_spec(w, sel):
    return pl.BlockSpec((w.shape[0],) + (None,) * len(sel) + w.shape[-2:], lambda *_: (0, *sel, 0, 0),
                        pipeline_mode=pl.Buffered(1))


def _ffn_fwd(x, g_pre, g_post, wg, wu, wd, sel, tm, gather=()):
    T, D = x.shape
    nj, F = wg.shape[0], wg.shape[-1]
    tm = min(tm, T)
    ni = T // tm
    ng = len(gather)

    def body(*refs):
        x_ref, gpre_ref, gpost_ref, wg_ref, wu_ref, wd_ref = refs[:6]
        xo_ref, a_ref, b_ref, f_ref = refs[6 + ng:10 + ng]
        h_scr, acc_scr = refs[10 + 2 * ng:12 + 2 * ng]
        i = pl.program_id(0)
        j = pl.program_id(1)
        if ng:
            plan = _Gather(refs[10 + ng:10 + 2 * ng], refs[12 + 2 * ng:])
            pl.when(jnp.logical_and(i == 0, j == 0))(plan.start)
            pl.when(jnp.logical_and(i == (5 * ni) // 8, j == nj - 1))(plan.forward)

        @pl.when(j == 0)
        def _():
            xh, _ = _rms_stats(x_ref[...])
            h_scr[...] = (xh * gpre_ref[...]).astype(BF16)
            acc_scr[...] = jnp.zeros_like(acc_scr)

        h = h_scr[...]
        a = _dot(h, wg_ref[...]).astype(BF16)
        b = _dot(h, wu_ref[...]).astype(BF16)
        a_ref[...] = a
        b_ref[...] = b
        sg, _ = _sigmoid_pair(a)
        acc_scr[...] += _dot((a * sg) * b, wd_ref[...])

        @pl.when(j == nj - 1)
        def _():
            f = acc_scr[...]
            f_ref[...] = f
            fh, _ = _rms_stats(f)
            xo_ref[...] = x_ref[...] + FFN_SCALE * (fh * gpost_ref[...])

        if ng:
            pl.when(jnp.logical_and(i == ni - 1, j == nj - 1))(plan.finish)

    row = pl.BlockSpec((tm, D), lambda i, j: (i, 0))
    vec = pl.BlockSpec((1, D), lambda i, j: (0, 0))
    w_in = _chunk_spec(sel, D, F)
    w_out = _chunk_spec(sel, F, D)
    act = pl.BlockSpec((None, tm, F), lambda i, j: (j, i, 0))
    hbm = pl.BlockSpec(memory_space=pl.ANY)
    return pl.pallas_call(
        body,
        name="ffn_fwd_gather" if ng else "ffn_fwd",
        grid=(ni, nj),
        in_specs=[row, vec, vec, w_in, w_in, w_out] + [hbm] * ng,
        out_specs=[row, act, act, row] + [hbm] * ng,
        out_shape=[
            jax.ShapeDtypeStruct((T, D), F32),
            jax.ShapeDtypeStruct((nj, T, F), BF16),
            jax.ShapeDtypeStruct((nj, T, F), BF16),
            jax.ShapeDtypeStruct((T, D), F32),
        ] + [jax.ShapeDtypeStruct(g.shape, g.dtype) for g in gather],
        input_output_aliases={6 + a: 4 + a for a in range(ng)},
        scratch_shapes=[pltpu.VMEM((tm, D), BF16), pltpu.VMEM((tm, D), F32)] + (_Gather.semaphores(ng) if ng else []),
        compiler_params=_cparams("arbitrary", "arbitrary"),
    )(x, g_pre, g_post, wg, wu, wd, *gather)


def _ffn_bwd(dy, x, f, a, b, g_pre, g_post, wg, wu, wd, sel, tm, scatter=()):
    T, D = x.shape
    nj, F = wg.shape[0], wg.shape[-1]
    tm = min(tm, T)
    ni = T // tm
    rb = min(RB_FFN_BWD, tm)
    ns = len(scatter)

    def body(*refs):
        dy_ref, x_ref, f_ref, a_ref, b_ref, gpre_ref, gpost_ref, wg_ref, wu_ref, wd_ref = refs[:10]
        dx_ref, h_ref, dz_ref, s_ref, da_ref, db_ref, dgpre_ref, dgpost_ref = refs[10 + ns:18 + ns]
        dh_scr = refs[18 + 2 * ns]
        i = pl.program_id(0)
        j = pl.program_id(1)
        if ns:
            plan = _Scatter(refs[10:10 + ns], refs[18 + ns:18 + 2 * ns], refs[19 + 2 * ns:])
            pl.when(jnp.logical_and(i == 0, j == 0))(plan.start)

        @pl.when(j == 0)
        def _():
            fh, rf = _rms_stats(f_ref[...])
            dz, dg = _rms_bwd(fh, rf, gpost_ref[...], FFN_SCALE * dy_ref[...])
            dz_ref[...] = dz.astype(BF16)
            _acc_out(dgpost_ref, i == 0, dg)
            xh, _ = _rms_stats(x_ref[...])
            h_ref[...] = (xh * gpre_ref[...]).astype(BF16)
            dh_scr[...] = jnp.zeros_like(dh_scr)

        for r0 in range(0, tm, rb):
            rows = slice(r0, r0 + rb)
            ds = _dot_nt(dz_ref[rows, :], wd_ref[...]).astype(BF16)
            av = a_ref[rows, :]
            bv = b_ref[rows, :]
            sg, one_minus_sg = _sigmoid_pair(av)
            sl = av * sg
            s_ref[rows, :] = sl * bv
            da = (ds * bv) * (sg + sl * one_minus_sg)
            db = ds * sl
            da_ref[rows, :] = da
            db_ref[rows, :] = db
            dh_scr[rows, :] += _dot_nt(da, wg_ref[...]) + _dot_nt(db, wu_ref[...])

        @pl.when(j == nj - 1)
        def _():
            xh, rx = _rms_stats(x_ref[...])
            dxn, dg = _rms_bwd(xh, rx, gpre_ref[...], dh_scr[...])
            dx_ref[...] = dy_ref[...] + dxn
            _acc_out(dgpre_ref, i == 0, dg)

        if ns:
            pl.when(jnp.logical_and(i == ni - 1, j == nj - 1))(plan.finish)

    row = pl.BlockSpec((tm, D), lambda i, j: (i, 0))
    vec = pl.BlockSpec((1, D), lambda i, j: (0, 0))
    w_in = _chunk_spec(sel, D, F)
    w_out = _chunk_spec(sel, F, D)
    act = pl.BlockSpec((None, tm, F), lambda i, j: (j, i, 0))
    act_shape = jax.ShapeDtypeStruct((nj, T, F), BF16)
    hbm = pl.BlockSpec(memory_space=pl.ANY)
    return pl.pallas_call(
        body,
        name="ffn_bwd_scatter" if ns else "ffn_bwd",
        grid=(ni, nj),
        in_specs=[row, row, row, act, act, vec, vec, w_in, w_in, w_out] + [hbm] * ns,
        out_specs=[row, row, row, act, act, act, vec, vec] + [hbm] * ns,
        out_shape=[
            jax.ShapeDtypeStruct((T, D), F32),
            jax.ShapeDtypeStruct((T, D), BF16),
            jax.ShapeDtypeStruct((T, D), BF16),
            act_shape, act_shape, act_shape,
            jax.ShapeDtypeStruct((1, D), F32),
            jax.ShapeDtypeStruct((1, D), F32),
        ] + [jax.ShapeDtypeStruct((N_DEV - 1, *g.shape[2:]), g.dtype) for g in scatter],
        scratch_shapes=[pltpu.VMEM((tm, D), F32)] + (_Scatter.semaphores(ns) if ns else []),
        compiler_params=_cparams("arbitrary", "arbitrary"),
    )(dy, x, f, a, b, g_pre, g_post, wg, wu, wd, *scatter)


def _tn_matmul(a, b, buf, like, sel, tk):
    a_chunked = a.ndim == 3
    nj = a.shape[0] if a_chunked else b.shape[0]
    T, M, N = a.shape[-2], a.shape[-1], b.shape[-1]
    tk = min(tk, T)
    nk = T // tk

    def body(a_ref, b_ref, *rest):
        o_ref, acc_scr = rest[-2:]
        k = pl.program_id(1)

        @pl.when(k == 0)
        def _():
            acc_scr[...] = jnp.zeros_like(acc_scr)

        acc_scr[...] += _dot_tn(a_ref[...], b_ref[...])

        @pl.when(k == nk - 1)
        def _():
            o_ref[...] = acc_scr[...].astype(BF16)

    def spec(chunked, width):
        if chunked:
            return pl.BlockSpec((None, tk, width), lambda j, k: (j, k, 0))
        return pl.BlockSpec((tk, width), lambda j, k: (k, 0))

    have = buf is not None
    return pl.pallas_call(
        body,
        name="tn_matmul",
        grid=(nj, nk),
        in_specs=[spec(a_chunked, M), spec(not a_chunked, N)] + ([pl.BlockSpec(memory_space=pl.ANY)] if have else []),
        out_specs=pl.BlockSpec((None,) * (1 + len(sel)) + (M, N), lambda j, k: (j, *sel, 0, 0)),
        out_shape=jax.ShapeDtypeStruct(like.shape, BF16),
        input_output_aliases={2: 0} if have else {},
        scratch_shapes=[pltpu.VMEM((M, N), F32)],
        compiler_params=_cparams("parallel", "arbitrary"),
    )(a, b, *([buf] if have else []))


def _sgu_fwd(x, g_pre, g_post, win, lng, lnb, wsm, bsb, wout, sel, tm):
    T, D = x.shape
    nc, E = win.shape[0], win.shape[-1]
    S = 2 * E
    dg = S // N_GROUPS
    wo_rows = wout.shape[-2]
    tm = min(tm, T)
    nq = tm // CHUNK

    def body(x_ref, gpre_ref, gpost_ref, win_ref, lng_ref, lnb_ref, ws_ref, bsb_ref, wout_ref,
             xo_ref, zp_ref, m_ref, u_scr, vn_scr, gt_scr):
        x = x_ref[...]
        xh, _ = _rms_stats(x)
        hn = (xh * gpre_ref[...]).astype(BF16)
        v_parts = []
        for c in range(nc):
            zp = _dot(hn, win_ref[c])
            zp_ref[c] = zp.astype(BF16)
            z = _gelu(zp)
            if c < nc // 2:
                u_scr[:, c * E:(c + 1) * E] = z
            else:
                v_parts.append(z)
        vh_parts, _ = _ln_stats(v_parts, S)
        for c, vh in enumerate(vh_parts):
            cols = slice(c * E, (c + 1) * E)
            vn_scr[:, cols] = (vh * lng_ref[:, cols] + lnb_ref[:, cols]).astype(BF16)
        for q in range(nq):
            rows = slice(q * CHUNK, (q + 1) * CHUNK)
            for g in range(N_GROUPS):
                cols = slice(g * dg, (g + 1) * dg)
                mixed = _dot(ws_ref[g], vn_scr[rows, cols]) + bsb_ref[g]
                gt_scr[rows, cols] = (u_scr[rows, cols] * mixed).astype(BF16)
        m = _dot(gt_scr[:, 0:wo_rows], wout_ref[0])
        for c in range(1, nc):
            m += _dot(gt_scr[:, c * wo_rows:(c + 1) * wo_rows], wout_ref[c])
        m_ref[...] = m
        mh, _ = _rms_stats(m)
        xo_ref[...] = x + mh * gpost_ref[...]

    row = pl.BlockSpec((tm, D), lambda i: (i, 0))
    return pl.pallas_call(
        body,
        name="sgu_fwd",
        grid=(T // tm,),
        in_specs=[row, _resident((1, D)), _resident((1, D)), _chunks_spec(win, sel), _resident((1, S)),
                  _resident((1, S)), _resident(wsm.shape), _resident(bsb.shape), _chunks_spec(wout, sel)],
        out_specs=[row, pl.BlockSpec((nc, tm, E), lambda i: (0, i, 0)), row],
        out_shape=[
            jax.ShapeDtypeStruct((T, D), F32),
            jax.ShapeDtypeStruct((nc, T, E), BF16),
            jax.ShapeDtypeStruct((T, D), F32),
        ],
        scratch_shapes=[pltpu.VMEM((tm, S), F32), pltpu.VMEM((tm, S), BF16), pltpu.VMEM((tm, S), BF16)],
        compiler_params=_cparams("parallel"),
    )(x, g_pre, g_post, win, lng, lnb, wsm, bsb, wout)


def _sgu_bwd(dy, x, m, zp, g_pre, g_post, win, lng, lnb, wsm, wsmt, bsb, wout, sel, tm):
    T, D = x.shape
    nc, E = win.shape[0], win.shape[-1]
    S = 2 * E
    dg = S // N_GROUPS
    wo_rows = wout.shape[-2]
    tm = min(tm, T)
    nq = tm // CHUNK

    def body(dy_ref, x_ref, m_ref, zp_ref, gpre_ref, gpost_ref, win_ref, lng_ref, lnb_ref, ws_ref, wst_ref,
             bsb_ref, wout_ref,
             dx_ref, hn_ref, dzp_ref, gated_ref, dm_ref, dws_ref, dbs_ref, dlng_ref, dlnb_ref, dgpre_ref,
             dgpost_ref, u_scr, d_scr, vh_scr, vn_scr):
        first = pl.program_id(0) == 0
        dy = dy_ref[...]
        mh, rm = _rms_stats(m_ref[...])
        dm, dgp = _rms_bwd(mh, rm, gpost_ref[...], dy)
        _acc_out(dgpost_ref, first, dgp)
        dm = dm.astype(BF16)
        dm_ref[...] = dm
        for c in range(nc):
            d_scr[:, c * wo_rows:(c + 1) * wo_rows] = _dot_nt(dm, wout_ref[c])
        v_parts = []
        for c in range(nc):
            z = _gelu(zp_ref[c].astype(F32))
            if c < nc // 2:
                u_scr[:, c * E:(c + 1) * E] = z
            else:
                v_parts.append(z)
        vh_parts, rstd = _ln_stats(v_parts, S)
        for c, vh in enumerate(vh_parts):
            cols = slice(c * E, (c + 1) * E)
            vh_scr[:, cols] = vh
            vn_scr[:, cols] = (vh * lng_ref[:, cols] + lnb_ref[:, cols]).astype(BF16)

        @pl.when(first)
        def _():
            dws_ref[...] = jnp.zeros_like(dws_ref)
            dbs_ref[...] = jnp.zeros_like(dbs_ref)
            dlng_ref[...] = jnp.zeros_like(dlng_ref)
            dlnb_ref[...] = jnp.zeros_like(dlnb_ref)

        for q in range(nq):
            rows = slice(q * CHUNK, (q + 1) * CHUNK)
            for g in range(N_GROUPS):
                cols = slice(g * dg, (g + 1) * dg)
                vn = vn_scr[rows, cols]
                mixed = _dot(ws_ref[g], vn) + bsb_ref[g]
                u = u_scr[rows, cols]
                dgt = d_scr[rows, cols]
                gated_ref[(g * dg) // wo_rows, rows, (g * dg) % wo_rows:(g * dg) % wo_rows + dg] = (u * mixed).astype(BF16)
                dmix = dgt * u
                dbs_ref[:, cols] += dmix
                dmix = dmix.astype(BF16)
                dws_ref[g] += _dot_nt(dmix, vn)
                u_scr[rows, cols] = dgt * mixed
                d_scr[rows, cols] = _dot(wst_ref[g], dmix)
        dvn = [d_scr[:, c * E:(c + 1) * E] for c in range(nc // 2)]
        vh = [vh_scr[:, c * E:(c + 1) * E] for c in range(nc // 2)]
        for c, (d, v) in enumerate(zip(dvn, vh)):
            dlng_ref[:, c * E:(c + 1) * E] += jnp.sum(d * v, axis=0, keepdims=True)
            dlnb_ref[:, c * E:(c + 1) * E] += jnp.sum(d, axis=0, keepdims=True)
        dvh = [d * lng_ref[:, c * E:(c + 1) * E] for c, d in enumerate(dvn)]
        dv = _ln_bwd(vh, rstd, dvh, S)
        dhn = None
        for c in range(nc):
            dz = u_scr[:, c * E:(c + 1) * E] if c < nc // 2 else dv[c - nc // 2]
            dzp = (dz * _gelu_grad(zp_ref[c].astype(F32))).astype(BF16)
            dzp_ref[c] = dzp
            t = _dot_nt(dzp, win_ref[c])
            dhn = t if dhn is None else dhn + t
        xh, rx = _rms_stats(x_ref[...])
        hn_ref[...] = (xh * gpre_ref[...]).astype(BF16)
        dxn, dgq = _rms_bwd(xh, rx, gpre_ref[...], dhn)
        dx_ref[...] = dy + dxn
        _acc_out(dgpre_ref, first, dgq)

    row = pl.BlockSpec((tm, D), lambda i: (i, 0))

    def whole(shape):
        return pl.BlockSpec(shape, lambda i: (0,) * len(shape))

    return pl.pallas_call(
        body,
        name="sgu_bwd",
        grid=(T // tm,),
        in_specs=[row, row, row, pl.BlockSpec((nc, tm, E), lambda i: (0, i, 0)), _resident((1, D)), _resident((1, D)),
                  _chunks_spec(win, sel), _resident((1, S)), _resident((1, S)), _resident(wsm.shape),
                  _resident(wsmt.shape), _resident(bsb.shape), _chunks_spec(wout, sel)],
        out_specs=[row, row, pl.BlockSpec((nc, tm, E), lambda i: (0, i, 0)),
                   pl.BlockSpec((nc, tm, wo_rows), lambda i: (0, i, 0)), row,
                   whole((N_GROUPS, CHUNK, CHUNK)), whole((CHUNK, S)), whole((1, S)), whole((1, S)),
                   whole((1, D)), whole((1, D))],
        out_shape=[
            jax.ShapeDtypeStruct((T, D), F32),
            jax.ShapeDtypeStruct((T, D), BF16),
            jax.ShapeDtypeStruct((nc, T, E), BF16),
            jax.ShapeDtypeStruct((nc, T, wo_rows), BF16),
            jax.ShapeDtypeStruct((T, D), BF16),
            jax.ShapeDtypeStruct((N_GROUPS, CHUNK, CHUNK), F32),
            jax.ShapeDtypeStruct((CHUNK, S), F32),
            jax.ShapeDtypeStruct((1, S), F32),
            jax.ShapeDtypeStruct((1, S), F32),
            jax.ShapeDtypeStruct((1, D), F32),
            jax.ShapeDtypeStruct((1, D), F32),
        ],
        scratch_shapes=[pltpu.VMEM((tm, S), F32), pltpu.VMEM((tm, S), F32), pltpu.VMEM((tm, S), F32),
                        pltpu.VMEM((tm, S), BF16)],
        compiler_params=_cparams("arbitrary"),
    )(dy, x, m, zp, g_pre, g_post, win, lng, lnb, wsm, wsmt, bsb, wout)


def _conv_fwd_a(x, g_pre, wpw1, sel, tm):
    T, D = x.shape
    nc, E = wpw1.shape[0], wpw1.shape[-1]
    C = 2 * E
    tm = min(tm, T)

    def body(x_ref, gpre_ref, w_ref, y_ref, p_ref):
        xh, _ = _rms_stats(x_ref[...])
        hn = (xh * gpre_ref[...]).astype(BF16)
        ps = []
        for c in range(nc):
            p = _dot(hn, w_ref[c])
            p_ref[c] = p.astype(BF16)
            ps.append(p)
        for c in range(nc // 2):
            y_ref[:, c * E:(c + 1) * E] = ps[c] * jax.nn.sigmoid(ps[c + nc // 2])

    row = pl.BlockSpec((tm, D), lambda i: (i, 0))
    return pl.pallas_call(
        body,
        name="conv_fwd_a",
        grid=(T // tm,),
        in_specs=[row, _resident((1, D)), _chunks_spec(wpw1, sel)],
        out_specs=[pl.BlockSpec((tm, C), lambda i: (i, 0)), pl.BlockSpec((nc, tm, E), lambda i: (0, i, 0))],
        out_shape=[jax.ShapeDtypeStruct((T, C), F32), jax.ShapeDtypeStruct((nc, T, E), BF16)],
        compiler_params=_cparams("parallel"),
    )(x, g_pre, wpw1)


def _conv_fwd_b(x, y, wdw, bdw, lng, lnb, wpw2, g_post, sel, tm):
    T, D = x.shape
    C = y.shape[1]
    nc, E = wpw2.shape[0], wpw2.shape[-2]
    tm = min(tm, T)
    per = tm // HALO

    def body(x_ref, y_ref, yprev_ref, wdw_ref, bdw_ref, lng_ref, lnb_ref, w_ref, gpost_ref,
             xo_ref, c_ref, m_ref, ybuf):
        i = pl.program_id(0)
        ybuf[0:HALO, :] = jnp.where(i > 0, yprev_ref[...], 0.0)
        ybuf[HALO:HALO + tm, :] = y_ref[...]
        off = HALO - (CONV_W - 1)
        for r0 in range(0, tm, CONV_RB):
            for c0 in range(0, C, CONV_CB):
                cols = slice(c0, c0 + CONV_CB)
                acc = jnp.broadcast_to(bdw_ref[:, cols], (CONV_RB, CONV_CB))
                for k in range(CONV_W):
                    acc = acc + wdw_ref[k:k + 1, cols] * ybuf[r0 + off + k:r0 + off + k + CONV_RB, cols]
                c_ref[r0:r0 + CONV_RB, cols] = acc
        (ch,), _ = _ln_stats([c_ref[...]], C)
        cn = ch * lng_ref[...] + lnb_ref[...]
        qv = (cn * jax.nn.sigmoid(cn)).astype(BF16)
        m = _dot(qv[:, 0:E], w_ref[0])
        for c in range(1, nc):
            m += _dot(qv[:, c * E:(c + 1) * E], w_ref[c])
        m_ref[...] = m
        mh, _ = _rms_stats(m)
        xo_ref[...] = x_ref[...] + mh * gpost_ref[...]

    row = pl.BlockSpec((tm, D), lambda i: (i, 0))
    crow = pl.BlockSpec((tm, C), lambda i: (i, 0))
    prev = pl.BlockSpec((HALO, C), lambda i: (jnp.maximum(i * per - 1, 0), 0))
    return pl.pallas_call(
        body,
        name="conv_fwd_b",
        grid=(T // tm,),
        in_specs=[row, crow, prev, _resident(wdw.shape), _resident((1, C)), _resident((1, C)), _resident((1, C)),
                  _chunks_spec(wpw2, sel), _resident((1, D))],
        out_specs=[row, crow, row],
        out_shape=[jax.ShapeDtypeStruct((T, D), F32), jax.ShapeDtypeStruct((T, C), F32),
                   jax.ShapeDtypeStruct((T, D), F32)],
        scratch_shapes=[pltpu.VMEM((HALO + tm, C), F32)],
        compiler_params=_cparams("parallel"),
    )(x, y, y, wdw, bdw, lng, lnb, wpw2, g_post)


def _conv_bwd_b(dy, m, c, lng, lnb, wpw2, g_post, sel, tm):
    T, D = dy.shape
    C = c.shape[1]
    nc, E = wpw2.shape[0], wpw2.shape[-2]
    tm = min(tm, T)

    def body(dy_ref, m_ref, c_ref, lng_ref, lnb_ref, w_ref, gpost_ref,
             dm_ref, q_ref, dc_ref, dlng_ref, dlnb_ref, dbdw_ref, dgpost_ref, dq_scr):
        first = pl.program_id(0) == 0
        mh, rm = _rms_stats(m_ref[...])
        dm, dgp = _rms_bwd(mh, rm, gpost_ref[...], dy_ref[...])
        _acc_out(dgpost_ref, first, dgp)
        dm = dm.astype(BF16)
        dm_ref[...] = dm
        for k in range(nc):
            dq_scr[:, k * E:(k + 1) * E] = _dot_nt(dm, w_ref[k])
        (ch,), rstd = _ln_stats([c_ref[...]], C)
        cn = ch * lng_ref[...] + lnb_ref[...]
        sg = jax.nn.sigmoid(cn)
        qv = (cn * sg).astype(BF16)
        for k in range(nc):
            q_ref[k] = qv[:, k * E:(k + 1) * E]
        dcn = dq_scr[...] * (sg * (1.0 + cn * (1.0 - sg)))
        _acc_out(dlng_ref, first, jnp.sum(dcn * ch, axis=0, keepdims=True))
        _acc_out(dlnb_ref, first, jnp.sum(dcn, axis=0, keepdims=True))
        (dc,) = _ln_bwd([ch], rstd, [dcn * lng_ref[...]], C)
        dc_ref[...] = dc
        _acc_out(dbdw_ref, first, jnp.sum(dc, axis=0, keepdims=True))

    row = pl.BlockSpec((tm, D), lambda i: (i, 0))
    crow = pl.BlockSpec((tm, C), lambda i: (i, 0))

    def whole(shape):
        return pl.BlockSpec(shape, lambda i: (0,) * len(shape))

    return pl.pallas_call(
        body,
        name="conv_bwd_b",
        grid=(T // tm,),
        in_specs=[row, row, crow, _resident((1, C)), _resident((1, C)), _chunks_spec(wpw2, sel), _resident((1, D))],
        out_specs=[row, pl.BlockSpec((nc, tm, E), lambda i: (0, i, 0)), crow, whole((1, C)), whole((1, C)),
                   whole((1, C)), whole((1, D))],
        out_shape=[jax.ShapeDtypeStruct((T, D), BF16), jax.ShapeDtypeStruct((nc, T, E), BF16),
                   jax.ShapeDtypeStruct((T, C), F32), jax.ShapeDtypeStruct((1, C), F32),
                   jax.ShapeDtypeStruct((1, C), F32), jax.ShapeDtypeStruct((1, C), F32),
                   jax.ShapeDtypeStruct((1, D), F32)],
        scratch_shapes=[pltpu.VMEM((tm, C), F32)],
        compiler_params=_cparams("arbitrary"),
    )(dy, m, c, lng, lnb, wpw2, g_post)


def _conv_bwd_a(dy, x, dc, y, p, g_pre, wdw, wpw1, sel, tm):
    T, D = x.shape
    C = y.shape[1]
    nc, E = wpw1.shape[0], wpw1.shape[-1]
    tm = min(tm, T)
    per = tm // HALO
    n_tiles = T // tm
    KP = wdw.shape[0]

    def body(dy_ref, x_ref, dc_ref, dcnext_ref, y_ref, yprev_ref, p_ref, gpre_ref, wdw_ref, w_ref,
             dx_ref, hn_ref, dp_ref, dwdw_ref, dgpre_ref, ybuf, dcbuf, dyg_scr, dw8_scr):
        i = pl.program_id(0)
        first = i == 0
        ybuf[0:HALO, :] = jnp.where(i > 0, yprev_ref[...], 0.0)
        ybuf[HALO:HALO + tm, :] = y_ref[...]
        dcbuf[0:tm, :] = dc_ref[...]
        dcbuf[tm:tm + HALO, :] = jnp.where(i < n_tiles - 1, dcnext_ref[...], 0.0)
        off = HALO - (CONV_W - 1)
        @pl.when(first)
        def _():
            dw8_scr[...] = jnp.zeros_like(dw8_scr)

        for r0 in range(0, tm, CONV_RB):
            for c0 in range(0, C, CONV_CB):
                cols = slice(c0, c0 + CONV_CB)
                dcb = dcbuf[r0:r0 + CONV_RB, cols]
                acc = jnp.zeros((CONV_RB, CONV_CB), F32)
                for k in range(CONV_W):
                    back = CONV_W - 1 - k
                    acc = acc + wdw_ref[k:k + 1, cols] * dcbuf[r0 + back:r0 + back + CONV_RB, cols]
                    prod = dcb * ybuf[r0 + off + k:r0 + off + k + CONV_RB, cols]
                    dw8_scr[k, :, cols] += jnp.sum(prod.reshape(CONV_RB // 8, 8, CONV_CB), axis=0)
                dyg_scr[r0:r0 + CONV_RB, cols] = acc

        @pl.when(i == n_tiles - 1)
        def _():
            dwdw_ref[...] = jnp.sum(dw8_scr[...], axis=1)

        dhn = None
        for c in range(nc // 2):
            cols = slice(c * E, (c + 1) * E)
            av = p_ref[c].astype(F32)
            sg = jax.nn.sigmoid(p_ref[c + nc // 2].astype(F32))
            dygc = dyg_scr[:, cols]
            da = (dygc * sg).astype(BF16)
            dgt = (dygc * av * sg * (1.0 - sg)).astype(BF16)
            dp_ref[c] = da
            dp_ref[c + nc // 2] = dgt
            t = _dot_nt(da, w_ref[c]) + _dot_nt(dgt, w_ref[c + nc // 2])
            dhn = t if dhn is None else dhn + t
        xh, rx = _rms_stats(x_ref[...])
        hn_ref[...] = (xh * gpre_ref[...]).astype(BF16)
        dxn, dgq = _rms_bwd(xh, rx, gpre_ref[...], dhn)
        dx_ref[...] = dy_ref[...] + dxn
        _acc_out(dgpre_ref, first, dgq)

    row = pl.BlockSpec((tm, D), lambda i: (i, 0))
    crow = pl.BlockSpec((tm, C), lambda i: (i, 0))
    prev = pl.BlockSpec((HALO, C), lambda i: (jnp.maximum(i * per - 1, 0), 0))
    nxt = pl.BlockSpec((HALO, C), lambda i: (jnp.minimum((i + 1) * per, T // HALO - 1), 0))
    chunks = pl.BlockSpec((nc, tm, E), lambda i: (0, i, 0))

    def whole(shape):
        return pl.BlockSpec(shape, lambda i: (0,) * len(shape))

    return pl.pallas_call(
        body,
        name="conv_bwd_a",
        grid=(n_tiles,),
        in_specs=[row, row, crow, nxt, crow, prev, chunks, _resident((1, D)), _resident(wdw.shape),
                  _chunks_spec(wpw1, sel)],
        out_specs=[row, row, chunks, whole((KP, C)), whole((1, D))],
        out_shape=[jax.ShapeDtypeStruct((T, D), F32), jax.ShapeDtypeStruct((T, D), BF16),
                   jax.ShapeDtypeStruct((nc, T, E), BF16), jax.ShapeDtypeStruct((KP, C), F32),
                   jax.ShapeDtypeStruct((1, D), F32)],
        scratch_shapes=[pltpu.VMEM((HALO + tm, C), F32), pltpu.VMEM((tm + HALO, C), F32),
                        pltpu.VMEM((tm, C), F32), pltpu.VMEM((KP, 8, C), F32)],
        compiler_params=_cparams("arbitrary"),
    )(dy, x, dc, dc, y, y, p, g_pre, wdw, wpw1)


def _loss_head(y, target, tm):
    T, D = y.shape
    tm = min(tm, T)

    def body(y_ref, t_ref, dy_ref, loss_ref):
        e = y_ref[...] - t_ref[...]
        dy_ref[...] = e * (1.0 / D)
        part = jnp.sum(jnp.sum(e * e, axis=-1, keepdims=True), axis=0, keepdims=True) * (0.5 / D)
        _acc_out(loss_ref, pl.program_id(0) == 0, jnp.broadcast_to(part, loss_ref.shape))

    row = pl.BlockSpec((tm, D), lambda i: (i, 0))
    return pl.pallas_call(
        body,
        name="loss_head",
        grid=(T // tm,),
        in_specs=[row, row],
        out_specs=[row, pl.BlockSpec((8, 128), lambda i: (0, 0))],
        out_shape=[jax.ShapeDtypeStruct((T, D), F32), jax.ShapeDtypeStruct((8, 128), F32)],
        compiler_params=_cparams("arbitrary"),
    )(y, target)


def _row_tile(rows, cols, itemsize_budget=2 * 1024 * 1024):
    want = max(16, itemsize_budget // (4 * cols))
    if rows <= want:
        return rows
    t = (want // 16) * 16
    while t > 16 and rows % t:
        t -= 16
    return t if rows % t == 0 else rows


def _sum_parts(parts):
    n, R, C = parts.shape
    tr = _row_tile(R, C * n // 2 if parts.dtype == BF16 else C * n)

    def body(p_ref, o_ref):
        acc = p_ref[0].astype(F32)
        for s in range(1, n):
            acc = acc + p_ref[s].astype(F32)
        o_ref[...] = acc

    return pl.pallas_call(
        body,
        name="sum_parts",
        grid=(R // tr,),
        in_specs=[pl.BlockSpec((n, tr, C), lambda i: (0, i, 0))],
        out_specs=pl.BlockSpec((tr, C), lambda i: (i, 0)),
        out_shape=jax.ShapeDtypeStruct((R, C), F32),
        compiler_params=_cparams("parallel"),
    )(parts)


def _cast_into_slot(w, slot):
    _, R, C = w.shape
    tr = _row_tile(R, C)

    def body(slot_ref, w_ref, o_ref):
        o_ref[...] = w_ref[...].astype(BF16)

    return pl.pallas_call(
        body,
        name="cast_into_slot",
        grid_spec=pltpu.PrefetchScalarGridSpec(
            num_scalar_prefetch=1,
            grid=(2, R // tr),
            in_specs=[pl.BlockSpec((None, tr, C), lambda h, i, s: (h, i, 0))],
            out_specs=pl.BlockSpec((None, None, tr, C), lambda h, i, s: (s[0], h, i, 0)),
        ),
        out_shape=jax.ShapeDtypeStruct((N_CHIPS, 2, R, C), BF16),
        compiler_params=_cparams("parallel", "parallel"),
    )(slot, w)


def _sum_with_own(arrived, own, place):
    n, R, C = arrived.shape
    tr = _row_tile(R, C * (n + 1) // 2)

    def body(place_ref, a_ref, own_ref, o_ref):
        acc = own_ref[...].astype(F32)
        for s in range(n):
            acc = acc + a_ref[s].astype(F32)
        o_ref[...] = acc

    return pl.pallas_call(
        body,
        name="sum_with_own",
        grid_spec=pltpu.PrefetchScalarGridSpec(
            num_scalar_prefetch=1,
            grid=(R // tr,),
            in_specs=[pl.BlockSpec((n, tr, C), lambda i, p: (0, i, 0)),
                      pl.BlockSpec((None, None, tr, C), lambda i, p: (p[0], p[1], i, 0))],
            out_specs=pl.BlockSpec((None, tr, C), lambda i, p: (p[1], i, 0)),
        ),
        out_shape=jax.ShapeDtypeStruct((2, R, C), F32),
        compiler_params=_cparams("parallel"),
    )(place, arrived, own)


def _adamw(w, g, m, v):
    R, C = w.shape
    tr = _row_tile(R, C * 7 // 2)
    c1 = 1.0 - ADAM_B1 ** ADAM_STEP
    c2 = 1.0 - ADAM_B2 ** ADAM_STEP

    def body(w_ref, g_ref, m_ref, v_ref, d_ref, mo_ref, vo_ref):
        g = g_ref[...]
        mn = ADAM_B1 * m_ref[...] + (1.0 - ADAM_B1) * g
        vn = ADAM_B2 * v_ref[...] + (1.0 - ADAM_B2) * (g * g)
        mo_ref[...] = mn
        vo_ref[...] = vn
        d_ref[...] = -ADAM_LR * ((mn / c1) / (jnp.sqrt(vn / c2) + ADAM_EPS) + ADAM_WD * w_ref[...])

    blk = pl.BlockSpec((tr, C), lambda i: (i, 0))
    shp = jax.ShapeDtypeStruct((R, C), F32)
    return pl.pallas_call(
        body,
        name="adamw",
        grid=(R // tr,),
        in_specs=[blk, blk, blk, blk],
        out_specs=[blk, blk, blk],
        out_shape=[shp, shp, shp],
        compiler_params=_cparams("parallel"),
    )(w, g, m, v)


def _adamw_into(w, g, m, v, outs, sel):
    n, R, C = w.shape
    tr = _row_tile(R, C * 4)
    c1 = 1.0 - ADAM_B1 ** ADAM_STEP
    c2 = 1.0 - ADAM_B2 ** ADAM_STEP

    def body(w_ref, g_ref, m_ref, v_ref, *rest):
        go_ref, d_ref, mo_ref, vo_ref = rest[-4:]
        g = g_ref[...]
        mn = ADAM_B1 * m_ref[...] + (1.0 - ADAM_B1) * g
        vn = ADAM_B2 * v_ref[...] + (1.0 - ADAM_B2) * (g * g)
        go_ref[...] = g
        mo_ref[...] = mn
        vo_ref[...] = vn
        d_ref[...] = -ADAM_LR * ((mn / c1) / (jnp.sqrt(vn / c2) + ADAM_EPS) + ADAM_WD * w_ref[...])

    entry = pl.BlockSpec((None, tr, C), lambda i: (sel, i, 0))
    hbm = pl.BlockSpec(memory_space=pl.ANY)
    have = outs is not None
    shp = jax.ShapeDtypeStruct((n, R, C), F32)
    return pl.pallas_call(
        body,
        name="adamw_into",
        grid=(R // tr,),
        in_specs=[entry, pl.BlockSpec((tr, C), lambda i: (i, 0)), entry, entry] + ([hbm] * 4 if have else []),
        out_specs=[entry] * 4,
        out_shape=[shp] * 4,
        input_output_aliases={4 + t: t for t in range(4)} if have else {},
        compiler_params=_cparams("parallel"),
    )(w, g, m, v, *(outs if have else ()))


def _gather_weights(halved, whole):
    nh, nw = len(halved), len(whole)

    def body(*refs):
        w_in = refs[nh:nh + nw]
        h_out, w_out = refs[nh + nw:2 * nh + nw], refs[2 * nh + nw:2 * (nh + nw)]
        ws_send, ws_recv, loc_sem = refs[2 * (nh + nw):2 * (nh + nw) + 3]
        plan = _Gather(h_out, refs[2 * (nh + nw) + 3:])
        x, y, c = _my_place()
        me_chip = 2 * x + y
        plan.start()

        def small(a, j, slot, to):
            return pltpu.make_async_remote_copy(src_ref=w_in[a], dst_ref=w_out[a].at[slot],
                                                send_sem=ws_send.at[a, j], recv_sem=ws_recv.at[a, j],
                                                device_id=to, device_id_type=MESH)

        for a in range(nw):
            pltpu.make_async_copy(w_in[a], w_out[a].at[me_chip], loc_sem.at[a]).start()
            for j, ch in enumerate(plan.chips):
                small(a, j, me_chip, (*ch, c)).start()
        plan.forward()
        plan.finish()
        for a in range(nw):
            for j, ch in enumerate(plan.chips):
                cp = small(a, j, 2 * ch[0] + ch[1], (x, y, c))
                cp.wait_recv()
                cp.wait_send()
            pltpu.make_async_copy(w_in[a], w_out[a].at[me_chip], loc_sem.at[a]).wait()

    hbm = pl.BlockSpec(memory_space=pl.ANY)
    outs = pl.pallas_call(
        body,
        name="gather_weights",
        in_specs=[hbm] * (nh + nw),
        out_specs=[hbm] * (nh + nw),
        out_shape=[jax.ShapeDtypeStruct(a.shape, a.dtype) for a in halved]
        + [jax.ShapeDtypeStruct((N_CHIPS, *a.shape), a.dtype) for a in whole],
        input_output_aliases={a: a for a in range(nh)},
        scratch_shapes=[pltpu.SemaphoreType.DMA((max(nw, 1), 3)), pltpu.SemaphoreType.DMA((max(nw, 1), 3)),
                        pltpu.SemaphoreType.DMA((max(nw, 1),))] + _Gather.semaphores(nh),
    )(*halved, *whole)
    return outs[:nh], outs[nh:]


def _scatter_grads(grads):
    n = len(grads)

    def body(*refs):
        plan = _Scatter(refs[:n], refs[n:2 * n], refs[2 * n:])
        plan.start()
        plan.finish()

    hbm = pl.BlockSpec(memory_space=pl.ANY)
    return pl.pallas_call(
        body,
        name="scatter_grads",
        in_specs=[hbm] * n,
        out_specs=[hbm] * n,
        out_shape=[jax.ShapeDtypeStruct((N_DEV - 1, *g.shape[2:]), g.dtype) for g in grads],
        scratch_shapes=_Scatter.semaphores(n),
    )(*grads)


def _swap_halves(halves):
    n = len(halves)

    def body(*refs):
        h_out = refs[n:2 * n]
        send_sem, recv_sem = refs[2 * n:]
        x, y, c = _my_place()
        sib = (x, y, 1 - c)
        for a in range(n):
            pltpu.make_async_remote_copy(src_ref=h_out[a].at[c], dst_ref=h_out[a].at[c], send_sem=send_sem.at[a],
                                         recv_sem=recv_sem.at[a], device_id=sib, device_id_type=MESH).start()
        for a in range(n):
            cp = pltpu.make_async_remote_copy(src_ref=h_out[a].at[c], dst_ref=h_out[a].at[1 - c],
                                              send_sem=send_sem.at[a], recv_sem=recv_sem.at[a], device_id=sib,
                                              device_id_type=MESH)
            cp.wait_send()
            cp.wait_recv()

    hbm = pl.BlockSpec(memory_space=pl.ANY)
    return pl.pallas_call(
        body,
        name="swap_halves",
        in_specs=[hbm] * n,
        out_specs=[hbm] * n,
        out_shape=[jax.ShapeDtypeStruct(h.shape, h.dtype) for h in halves],
        input_output_aliases={a: a for a in range(n)},
        scratch_shapes=[pltpu.SemaphoreType.DMA((n,)), pltpu.SemaphoreType.DMA((n,))],
    )(*halves)


def _share_all(buf):
    def body(b_in, b_out, send_sem, recv_sem):
        x, y, c = _my_place()
        me = 4 * x + 2 * y + c
        for d in range(N_DEV):
            @pl.when(d != me)
            def _():
                pltpu.make_async_remote_copy(src_ref=b_out.at[me], dst_ref=b_out.at[me], send_sem=send_sem.at[d],
                                             recv_sem=recv_sem.at[me], device_id=(d // 4, (d // 2) % 2, d % 2),
                                             device_id_type=MESH).start()
        for d in range(N_DEV):
            @pl.when(d != me)
            def _():
                cp = pltpu.make_async_remote_copy(src_ref=b_out.at[me], dst_ref=b_out.at[d], send_sem=send_sem.at[d],
                                                  recv_sem=recv_sem.at[d], device_id=(x, y, c),
                                                  device_id_type=MESH)
                cp.wait_send()
                cp.wait_recv()

    hbm = pl.BlockSpec(memory_space=pl.ANY)
    return pl.pallas_call(
        body,
        name="share_all",
        in_specs=[hbm],
        out_specs=hbm,
        out_shape=jax.ShapeDtypeStruct(buf.shape, buf.dtype),
        input_output_aliases={0: 0},
        scratch_shapes=[pltpu.SemaphoreType.DMA((N_DEV,)), pltpu.SemaphoreType.DMA((N_DEV,))],
    )(buf)


TM_FFN = 512
RB_FFN_BWD = 256
TM_SGU = 256
TM_CONV = 256
TK_WGRAD = 2048
TM_LOSS = 1024


FFN_KINDS = ("ff_w_gate", "ff_w_up", "ff_w_down")


def _layer_kinds(i):
    return FFN_KINDS + (("sgu_w_in", "sgu_w_out") if i % 2 == 0 else ("conv_w_pw1", "conv_w_pw2"))


def _local_step(x, target, G, W, exchange=None):
    depth = W["norm_g"].shape[0]
    G = [dict(g) for g in G]
    saved = []
    vec = lambda v: v.reshape(1, -1)

    def mixer_w(i, k):
        w = G[i][k]
        return w.reshape(w.shape[0], -1, w.shape[-1])
    wsm, wsmt, bsb = [], [], []
    n_sgu = W["sgu_w_spatial"].shape[0]
    causal = jnp.tril(jnp.ones((CHUNK, CHUNK), dtype=bool))
    dgrp = W["sgu_ln_g"].shape[1] // N_GROUPS
    for jx in range(n_sgu):
        ws = jnp.where(causal[None], W["sgu_w_spatial"][jx], 0.0).astype(BF16)
        wsm.append(ws)
        wsmt.append(jnp.swapaxes(ws, 1, 2))
        bsb.append(jnp.broadcast_to(W["sgu_b_spatial"][jx][:, :, None], (N_GROUPS, CHUNK, dgrp)))
    kp = HALO
    wdw = [jnp.pad(W["conv_w_dw"][jx], ((0, kp - CONV_W), (0, 0))) for jx in range(W["conv_w_dw"].shape[0])]

    def ffn(x, i, f_idx, gather=()):
        g = W["norm_g"][i]
        return _ffn_fwd(x, vec(g[4 * f_idx]), vec(g[4 * f_idx + 1]), G[i]["ff_w_gate"], G[i]["ff_w_up"],
                        G[i]["ff_w_down"], (f_idx,), TM_FFN, gather)

    for i in range(depth):
        g = W["norm_g"][i]
        rec = {"x0": x}
        if exchange is not None and i + 1 < depth:
            kinds = _layer_kinds(i + 1)
            x, rec["a1"], rec["b1"], rec["f1"], *filled = ffn(x, i, 0, [G[i + 1][k] for k in kinds])
            G[i + 1] = dict(zip(kinds, filled))
        else:
            x, rec["a1"], rec["b1"], rec["f1"] = ffn(x, i, 0)
        rec["x1"] = x
        j = i // 2
        if i % 2 == 0:
            x, rec["zp"], rec["m"] = _sgu_fwd(
                x, vec(g[2]), vec(g[3]), mixer_w(i, "sgu_w_in"), vec(W["sgu_ln_g"][j]), vec(W["sgu_ln_b"][j]),
                wsm[j], bsb[j], mixer_w(i, "sgu_w_out"), (), TM_SGU)
        else:
            rec["y"], rec["p"] = _conv_fwd_a(x, vec(g[2]), mixer_w(i, "conv_w_pw1"), (), TM_CONV)
            x, rec["c"], rec["m"] = _conv_fwd_b(
                x, rec["y"], wdw[j], vec(W["conv_b_dw"][j]), vec(W["conv_ln_g"][j]), vec(W["conv_ln_b"][j]),
                mixer_w(i, "conv_w_pw2"), vec(g[3]), (), TM_CONV)
        rec["x2"] = x
        x, rec["a2"], rec["b2"], rec["f2"] = ffn(x, i, 1)
        saved.append(rec)

    dx, loss_tile = _loss_head(x, target, TM_LOSS)
    loss = loss_tile[0, 0]

    big = [{k: None for k in _layer_kinds(i)} for i in range(depth)]
    small = {k: [None] * W[k].shape[0] for k in
             ("sgu_ln_g", "sgu_ln_b", "sgu_w_spatial", "sgu_b_spatial", "conv_w_dw", "conv_b_dw", "conv_ln_g",
              "conv_ln_b")}
    dnorm = [[None] * 6 for _ in range(depth)]
    pieces = [None] * depth
    waiting = []

    def wgrad(i, k, a, b, sel):
        like = G[i][k] if sel else mixer_w(i, k)
        big[i][k] = _tn_matmul(a, b, big[i][k], like, sel, TK_WGRAD)

    def ffn_back(dx, i, f_idx, xin, a, b, f):
        g = W["norm_g"][i]
        sending = waiting.pop() if exchange is not None and waiting else None
        dx, h, dz, s, da, db, dgpre, dgpost, *arrived = _ffn_bwd(
            dx, xin, f, a, b, vec(g[4 * f_idx]), vec(g[4 * f_idx + 1]),
            G[i]["ff_w_gate"], G[i]["ff_w_up"], G[i]["ff_w_down"], (f_idx,), TM_FFN,
            list(pieces[sending].values()) if sending is not None else ())
        if sending is not None:
            exchange(sending, pieces[sending], dict(zip(pieces[sending], arrived)))
        wgrad(i, "ff_w_gate", h, da, (f_idx,))
        wgrad(i, "ff_w_up", h, db, (f_idx,))
        wgrad(i, "ff_w_down", s, dz, (f_idx,))
        dnorm[i][4 * f_idx] = dgpre[0]
        dnorm[i][4 * f_idx + 1] = dgpost[0]
        return dx

    for i in reversed(range(depth)):
        rec = saved[i]
        g = W["norm_g"][i]
        j = i // 2
        dx = ffn_back(dx, i, 1, rec["x2"], rec["a2"], rec["b2"], rec["f2"])
        if i % 2 == 0:
            (dx, hn, dzp, gated, dm, dws, dbs_acc, dlng, dlnb, dgpre, dgpost) = _sgu_bwd(
                dx, rec["x1"], rec["m"], rec["zp"], vec(g[2]), vec(g[3]), mixer_w(i, "sgu_w_in"),
                vec(W["sgu_ln_g"][j]), vec(W["sgu_ln_b"][j]), wsm[j], wsmt[j], bsb[j], mixer_w(i, "sgu_w_out"), (),
                TM_SGU)
            wgrad(i, "sgu_w_in", hn, dzp, ())
            wgrad(i, "sgu_w_out", gated, dm, ())
            small["sgu_w_spatial"][j] = jnp.where(causal[None], dws, 0.0)
            small["sgu_b_spatial"][j] = dbs_acc.reshape(CHUNK, N_GROUPS, dgrp).sum(-1).T
            small["sgu_ln_g"][j] = dlng[0]
            small["sgu_ln_b"][j] = dlnb[0]
        else:
            dm, q, dc, dlng, dlnb, dbdw, dgpost = _conv_bwd_b(
                dx, rec["m"], rec["c"], vec(W["conv_ln_g"][j]), vec(W["conv_ln_b"][j]), mixer_w(i, "conv_w_pw2"),
                vec(g[3]), (), TM_CONV)
            dx, hn, dp, dwdw, dgpre = _conv_bwd_a(
                dx, rec["x1"], dc, rec["y"], rec["p"], vec(g[2]), wdw[j], mixer_w(i, "conv_w_pw1"), (), TM_CONV)
            wgrad(i, "conv_w_pw1", hn, dp, ())
            wgrad(i, "conv_w_pw2", q, dm, ())
            small["conv_w_dw"][j] = dwdw[:CONV_W]
            small["conv_b_dw"][j] = dbdw[0]
            small["conv_ln_g"][j] = dlng[0]
            small["conv_ln_b"][j] = dlnb[0]
        dnorm[i][2] = dgpre[0]
        dnorm[i][3] = dgpost[0]
        dx = ffn_back(dx, i, 0, rec["x0"], rec["a1"], rec["b1"], rec["f1"])
        pieces[i] = {k: b.reshape(N_CHIPS, 2, -1, b.shape[-1]) for k, b in big[i].items()}
        waiting.append(i)

    if exchange is not None:
        last = waiting.pop()
        exchange(last, pieces[last], dict(zip(pieces[last], _scatter_grads(list(pieces[last].values())))))

    small = {k: jnp.stack(v) for k, v in small.items()}
    small["norm_g"] = jnp.stack([jnp.stack(r) for r in dnorm])
    return loss, dx, pieces, small


BIG = ("ff_w_gate", "ff_w_up", "ff_w_down", "sgu_w_in", "sgu_w_out", "conv_w_pw1", "conv_w_pw2")
SHARDED_SMALL = ("norm_g", "conv_w_dw", "conv_b_dw", "conv_ln_g", "conv_ln_b")
REPLICATED = ("sgu_ln_g", "sgu_ln_b", "sgu_w_spatial", "sgu_b_spatial")
WEIGHTS = ("norm_g", "ff_w_gate", "ff_w_up", "ff_w_down", "sgu_w_in", "sgu_ln_g", "sgu_ln_b", "sgu_w_spatial",
           "sgu_b_spatial", "sgu_w_out", "conv_w_pw1", "conv_w_dw", "conv_b_dw", "conv_ln_g", "conv_ln_b",
           "conv_w_pw2")


def _rows8(a, width):
    r = a.reshape(-1, width)
    pad = (-r.shape[0]) % 8
    return jnp.pad(r, ((0, pad), (0, 0))) if pad else r


def _pack(arrs, width):
    parts = [_rows8(a, width) for a in arrs]
    return jnp.concatenate(parts, axis=0), [p.shape[0] for p in parts]


def _unpack(buf, like):
    out, r0 = [], 0
    width = buf.shape[-1]
    for a in like:
        n = -(-(a.size // width) // 8) * 8
        rows = a.size // width
        out.append(buf[..., r0:r0 + rows, :].reshape(*buf.shape[:-2], *a.shape))
        r0 += n
    return out


def kernel(x, norm_g, ff_w_gate, ff_w_up, ff_w_down, sgu_w_in, sgu_ln_g, sgu_ln_b, sgu_w_spatial, sgu_b_spatial, sgu_w_out, conv_w_pw1, conv_w_dw, conv_b_dw, conv_ln_g, conv_ln_b, conv_w_pw2, loss_target, m_norm_g, m_ff_w_gate, m_ff_w_up, m_ff_w_down, m_sgu_w_in, m_sgu_ln_g, m_sgu_ln_b, m_sgu_w_spatial, m_sgu_b_spatial, m_sgu_w_out, m_conv_w_pw1, m_conv_w_dw, m_conv_b_dw, m_conv_ln_g, m_conv_ln_b, m_conv_w_pw2, v_norm_g, v_ff_w_gate, v_ff_w_up, v_ff_w_down, v_sgu_w_in, v_sgu_ln_g, v_sgu_ln_b, v_sgu_w_spatial, v_sgu_b_spatial, v_sgu_w_out, v_conv_w_pw1, v_conv_w_dw, v_conv_b_dw, v_conv_ln_g, v_conv_ln_b, v_conv_w_pw2):
    w = dict(norm_g=norm_g, ff_w_gate=ff_w_gate, ff_w_up=ff_w_up, ff_w_down=ff_w_down, sgu_w_in=sgu_w_in,
             sgu_ln_g=sgu_ln_g, sgu_ln_b=sgu_ln_b, sgu_w_spatial=sgu_w_spatial, sgu_b_spatial=sgu_b_spatial,
             sgu_w_out=sgu_w_out, conv_w_pw1=conv_w_pw1, conv_w_dw=conv_w_dw, conv_b_dw=conv_b_dw,
             conv_ln_g=conv_ln_g, conv_ln_b=conv_ln_b, conv_w_pw2=conv_w_pw2)
    mom = dict(norm_g=m_norm_g, ff_w_gate=m_ff_w_gate, ff_w_up=m_ff_w_up, ff_w_down=m_ff_w_down,
               sgu_w_in=m_sgu_w_in, sgu_ln_g=m_sgu_ln_g, sgu_ln_b=m_sgu_ln_b, sgu_w_spatial=m_sgu_w_spatial,
               sgu_b_spatial=m_sgu_b_spatial, sgu_w_out=m_sgu_w_out, conv_w_pw1=m_conv_w_pw1,
               conv_w_dw=m_conv_w_dw, conv_b_dw=m_conv_b_dw, conv_ln_g=m_conv_ln_g, conv_ln_b=m_conv_ln_b,
               conv_w_pw2=m_conv_w_pw2)
    vel = dict(norm_g=v_norm_g, ff_w_gate=v_ff_w_gate, ff_w_up=v_ff_w_up, ff_w_down=v_ff_w_down,
               sgu_w_in=v_sgu_w_in, sgu_ln_g=v_sgu_ln_g, sgu_ln_b=v_sgu_ln_b, sgu_w_spatial=v_sgu_w_spatial,
               sgu_b_spatial=v_sgu_b_spatial, sgu_w_out=v_sgu_w_out, conv_w_pw1=v_conv_w_pw1,
               conv_w_dw=v_conv_w_dw, conv_b_dw=v_conv_b_dw, conv_ln_g=v_conv_ln_g, conv_ln_b=v_conv_ln_b,
               conv_w_pw2=v_conv_w_pw2)
    T, D = x.shape[1], x.shape[2]
    shard_w = conv_b_dw.shape[1]

    xi, yi, ci = _my_place()
    me_chip = (2 * xi + yi).astype(jnp.int32)
    me = (4 * xi + 2 * yi + ci).astype(jnp.int32)

    depth = norm_g.shape[0]
    place = jnp.stack([me_chip, ci.astype(jnp.int32)])

    def entry(k, i):
        return i if k in FFN_KINDS else i // 2

    def stacked(a):
        return a.reshape(a.shape[0], -1, a.shape[-1])

    G = []
    for i in range(depth):
        G.append({k: _cast_into_slot(w[k][entry(k, i)].reshape(2, -1, w[k].shape[-1]), me_chip.reshape(1))
                  for k in _layer_kinds(i)})
    small_buf, _ = _pack([w[k] for k in SHARDED_SMALL], shard_w)
    first, (small_all,) = _gather_weights(list(G[0].values()), [small_buf])
    G[0] = dict(zip(G[0], first))
    W = {}
    for k, part in zip(SHARDED_SMALL, _unpack(small_all, [w[k] for k in SHARDED_SMALL])):
        W[k] = jnp.moveaxis(part, 0, -2).reshape(*w[k].shape[:-1], N_CHIPS * shard_w)
    for k in REPLICATED:
        W[k] = w[k]

    results = {k: None for k in BIG}

    def reduce_and_update(i, pieces, arrived):
        kinds = list(pieces)
        both = _swap_halves([_sum_with_own(arrived[k], pieces[k], place) for k in kinds])
        for k, g in zip(kinds, both):
            results[k] = _adamw_into(stacked(w[k]), g.reshape(-1, g.shape[-1]), stacked(mom[k]), stacked(vel[k]),
                                     results[k], entry(k, i))

    loss, dx, _, small = _local_step(x[0], loss_target[0], G, W, reduce_and_update)
    loss = lax.psum(loss, ("x", "y", "c"))
    grads, delta, new_m, new_v = {}, {}, {}, {}
    for k in BIG:
        grads[k], delta[k], new_m[k], new_v[k] = (t.reshape(w[k].shape) for t in results[k])

    sbuf, _ = _pack([small[k] for k in SHARDED_SMALL + REPLICATED], D)
    slots = lax.dynamic_update_slice(jnp.zeros((N_DEV, *sbuf.shape), F32), sbuf[None], (me, 0, 0))
    ssum = _sum_parts(_share_all(slots))
    for k, gfull in zip(SHARDED_SMALL + REPLICATED, _unpack(ssum, [small[k] for k in SHARDED_SMALL + REPLICATED])):
        if k in SHARDED_SMALL:
            gfull = lax.dynamic_slice_in_dim(gfull, me_chip * shard_w, shard_w, axis=gfull.ndim - 1)
        grads[k] = gfull

    for names, width in ((SHARDED_SMALL, shard_w), (REPLICATED, CHUNK)):
        packed = [_pack([src[k] for k in names], width)[0] for src in (w, grads, mom, vel)]
        outs = _adamw(*packed)
        for res, out in zip((delta, new_m, new_v), outs):
            for k, a in zip(names, _unpack(out, [w[k] for k in names])):
                res[k] = a

    return (loss, dx[None], *[grads[k] for k in WEIGHTS], *[delta[k] for k in WEIGHTS],
            *[new_m[k] for k in WEIGHTS], *[new_v[k] for k in WEIGHTS])
```

```python
import functools

import jax
import jax.numpy as jnp
from jax import lax
from jax.experimental import pallas as pl
from jax.experimental.pallas import tpu as pltpu

F32 = jnp.float32
BF16 = jnp.bfloat16
EPS = 1e-6
N_CHIPS = 4
N_DEV = 8
N_GROUPS = 8
CHUNK = 128
CONV_W = 31
HALO = 32
CONV_RB = 64
CONV_CB = 256
VMEM_LIMIT_V7X = 56 * 1024 * 1024
MESH = pl.DeviceIdType.MESH

ADAM_LR = 0.001
ADAM_B1 = 0.9
ADAM_B2 = 0.999
ADAM_EPS = 1e-08
ADAM_WD = 0.01
ADAM_STEP = 10
FFN_SCALE = 0.5


def _cparams(*sem, **kw):
    return pltpu.CompilerParams(dimension_semantics=sem, vmem_limit_bytes=VMEM_LIMIT_V7X, **kw)


def _resident(shape):
    return pl.BlockSpec(shape, lambda *_: (0,) * len(shape), pipeline_mode=pl.Buffered(1))


def _dot(a, b):
    return jnp.dot(a, b, preferred_element_type=F32)


def _dot_nt(a, b):
    return lax.dot_general(a, b, (((1,), (1,)), ((), ())), preferred_element_type=F32)


def _dot_tn(a, b):
    return lax.dot_general(a, b, (((0,), (0,)), ((), ())), preferred_element_type=F32)


def _rms_stats(x):
    r = lax.rsqrt(jnp.mean(x * x, axis=-1, keepdims=True) + EPS)
    return x * r, r


def _rms_bwd(xh, r, g, dy):
    dxh = dy * g
    dx = r * (dxh - xh * jnp.mean(dxh * xh, axis=-1, keepdims=True))
    return dx, jnp.sum(dy * xh, axis=0, keepdims=True)


def _ln_stats(parts, width):
    mu = sum(jnp.sum(p, axis=-1, keepdims=True) for p in parts) / width
    cen = [p - mu for p in parts]
    var = sum(jnp.sum(c * c, axis=-1, keepdims=True) for c in cen) / width
    rstd = lax.rsqrt(var + EPS)
    return [c * rstd for c in cen], rstd


def _ln_bwd(vh_parts, rstd, dvh_parts, width):
    m1 = sum(jnp.sum(d, axis=-1, keepdims=True) for d in dvh_parts) / width
    m2 = sum(jnp.sum(d * v, axis=-1, keepdims=True) for d, v in zip(dvh_parts, vh_parts)) / width
    return [rstd * (d - m1 - v * m2) for d, v in zip(dvh_parts, vh_parts)]


_GELU_C = 0.7978845608028654
_GELU_A = 0.044715


def _gelu(x):
    return 0.5 * x * (1.0 + jnp.tanh(_GELU_C * (x + _GELU_A * x * x * x)))


def _gelu_pair(x):
    x2 = x * x
    t = jnp.tanh(_GELU_C * (x + _GELU_A * (x * x2)))
    half = 0.5 * (1.0 + t)
    return x * half, half + (0.5 * _GELU_C) * x * (1.0 - t * t) * (1.0 + (3.0 * _GELU_A) * x2)


def _sigmoid_pair(a):
    e = jnp.exp(jnp.minimum(-a, 80.0))
    sg = 1.0 / (1.0 + e)
    return sg, e * sg


def _acc_out(ref, first, val):
    @pl.when(first)
    def _():
        ref[...] = val

    @pl.when(jnp.logical_not(first))
    def _():
        ref[...] += val


def _my_place():
    return lax.axis_index("x"), lax.axis_index("y"), lax.axis_index("c")


class _Gather:
    def __init__(self, bufs, sems):
        self.bufs = bufs
        self.own_sems, self.fwd_sems = sems[:2], sems[2:4]
        self.x, self.y, self.c = _my_place()
        x, y = self.x, self.y
        self.chips = [(1 - x, y), (x, 1 - y), (1 - x, 1 - y)]

    def _copy(self, a, j, chip, half, to, sems):
        spot = self.bufs[a].at[2 * chip[0] + chip[1], pl.ds(half, 1)]
        return pltpu.make_async_remote_copy(src_ref=spot, dst_ref=spot, send_sem=sems[0].at[a, j],
                                            recv_sem=sems[1].at[a, j], device_id=to, device_id_type=MESH)

    def _own(self, a, j):
        return self._copy(a, j, (self.x, self.y), self.c, (*self.chips[j], self.c), self.own_sems)

    def _passed_on(self, a, j):
        return self._copy(a, j, self.chips[j], self.c, (self.x, self.y, 1 - self.c), self.fwd_sems)

    def start(self):
        for j in range(3):
            for a in range(len(self.bufs)):
                self._own(a, j).start()

    def forward(self):
        me = (self.x, self.y, self.c)
        for j in range(3):
            for a in range(len(self.bufs)):
                self._copy(a, j, self.chips[j], self.c, me, self.own_sems).wait_recv()
                self._passed_on(a, j).start()

    def finish(self):
        me = (self.x, self.y, self.c)
        for j in range(3):
            for a in range(len(self.bufs)):
                self._copy(a, j, self.chips[j], 1 - self.c, me, self.fwd_sems).wait_recv()
        for j in range(3):
            for a in range(len(self.bufs)):
                self._own(a, j).wait_send()
                self._passed_on(a, j).wait_send()

    @staticmethod
    def semaphores(n):
        return [pltpu.SemaphoreType.DMA((n, 3)) for _ in range(4)]


class _Scatter:
    def __init__(self, g_in, g_out, sems, halves):
        self.g_in, self.g_out = g_in, g_out
        self.send_sem, self.recv_sem = sems
        self.halves = halves
        x, y, c = _my_place()
        self.c = c
        self.me = 4 * x + 2 * y + c

    def _piece(self, a, d, slot):
        return pltpu.make_async_remote_copy(
            src_ref=self.g_in[a].at[d // 2, d % 2], dst_ref=self.g_out[a].at[slot],
            send_sem=self.send_sem.at[a, d], recv_sem=self.recv_sem.at[a, slot],
            device_id=(d // 4, (d // 2) % 2, d % 2), device_id_type=MESH)

    def _to(self, a):
        return [d for d in range(N_DEV) if d % 2 in self.halves[a]]

    def start(self):
        for a in range(len(self.g_in)):
            for d in self._to(a):
                @pl.when(d != self.me)
                def _():
                    self._piece(a, d, lax.rem(self.me - d - 1 + N_DEV, N_DEV)).start()

    def finish(self):
        for a in range(len(self.g_in)):
            for h in self.halves[a]:
                @pl.when(self.c == h)
                def _():
                    for slot in range(N_DEV - 1):
                        self._piece(a, 0, slot).wait_recv()
            for d in self._to(a):
                @pl.when(d != self.me)
                def _():
                    self._piece(a, d, 0).wait_send()

    @staticmethod
    def semaphores(n):
        return [pltpu.SemaphoreType.DMA((n, N_DEV)), pltpu.SemaphoreType.DMA((n, N_DEV - 1))]


def _chunk_spec(sel, rows, cols):
    return pl.BlockSpec((None,) * (1 + len(sel)) + (rows, cols), lambda i, j: (j, *sel, 0, 0))


def _chunks_spec(w, sel):
    return pl.BlockSpec((w.shape[0],) + (None,) * len(sel) + w.shape[-2:], lambda *_: (0, *sel, 0, 0),
                        pipeline_mode=pl.Buffered(1))


def _ffn_fwd(x, g_pre, g_post, wg, wu, wd, sel, tm, gather=()):
    T, D = x.shape
    nj, F = wg.shape[0], wg.shape[-1]
    tm = min(tm, T)
    ni = T // tm
    ng = len(gather)

    def body(*refs):
        x_ref, gpre_ref, gpost_ref, wg_ref, wu_ref, wd_ref = refs[:6]
        xo_ref, a_ref, b_ref, f_ref = refs[6 + ng:10 + ng]
        h_scr, acc_scr = refs[10 + 2 * ng:12 + 2 * ng]
        i = pl.program_id(0)
        j = pl.program_id(1)
        if ng:
            plan = _Gather(refs[10 + ng:10 + 2 * ng], refs[12 + 2 * ng:])
            pl.when(jnp.logical_and(i == 0, j == 0))(plan.start)
            pl.when(jnp.logical_and(i == (5 * ni) // 8, j == nj - 1))(plan.forward)

        @pl.when(j == 0)
        def _():
            xh, _ = _rms_stats(x_ref[...])
            h_scr[...] = (xh * gpre_ref[...]).astype(BF16)
            acc_scr[...] = jnp.zeros_like(acc_scr)

        h = h_scr[...]
        a = _dot(h, wg_ref[...]).astype(BF16)
        b = _dot(h, wu_ref[...]).astype(BF16)
        a_ref[...] = a
        b_ref[...] = b
        sg, _ = _sigmoid_pair(a)
        acc_scr[...] += _dot((a * sg) * b, wd_ref[...])

        @pl.when(j == nj - 1)
        def _():
            f = acc_scr[...]
            f_ref[...] = f
            fh, _ = _rms_stats(f)
            xo_ref[...] = x_ref[...] + FFN_SCALE * (fh * gpost_ref[...])

        if ng:
            pl.when(jnp.logical_and(i == ni - 1, j == nj - 1))(plan.finish)

    row = pl.BlockSpec((tm, D), lambda i, j: (i, 0))
    vec = pl.BlockSpec((1, D), lambda i, j: (0, 0))
    w_in = _chunk_spec(sel, D, F)
    w_out = _chunk_spec(sel, F, D)
    act = pl.BlockSpec((None, tm, F), lambda i, j: (j, i, 0))
    hbm = pl.BlockSpec(memory_space=pl.ANY)
    return pl.pallas_call(
        body,
        name="ffn_fwd_gather" if ng else "ffn_fwd",
        grid=(ni, nj),
        in_specs=[row, vec, vec, w_in, w_in, w_out] + [hbm] * ng,
        out_specs=[row, act, act, row] + [hbm] * ng,
        out_shape=[
            jax.ShapeDtypeStruct((T, D), F32),
            jax.ShapeDtypeStruct((nj, T, F), BF16),
            jax.ShapeDtypeStruct((nj, T, F), BF16),
            jax.ShapeDtypeStruct((T, D), F32),
        ] + [jax.ShapeDtypeStruct(g.shape, g.dtype) for g in gather],
        input_output_aliases={6 + a: 4 + a for a in range(ng)},
        scratch_shapes=[pltpu.VMEM((tm, D), BF16), pltpu.VMEM((tm, D), F32)] + (_Gather.semaphores(ng) if ng else []),
        compiler_params=_cparams("arbitrary", "arbitrary"),
    )(x, g_pre, g_post, wg, wu, wd, *gather)


def _ffn_bwd(dy, x, f, a, b, g_pre, g_post, wg, wu, wd, sel, tm, scatter=()):
    T, D = x.shape
    nj, F = wg.shape[0], wg.shape[-1]
    tm = min(tm, T)
    ni = T // tm
    rb = min(RB_FFN_BWD, tm)
    ns = len(scatter)
    halves = [h for _, h in scatter]
    scatter = [g for g, _ in scatter]

    def body(*refs):
        dy_ref, x_ref, f_ref, a_ref, b_ref, gpre_ref, gpost_ref, wg_ref, wu_ref, wd_ref = refs[:10]
        dx_ref, h_ref, dz_ref, s_ref, da_ref, db_ref, dgpre_ref, dgpost_ref = refs[10 + ns:18 + ns]
        dh_scr = refs[18 + 2 * ns]
        i = pl.program_id(0)
        j = pl.program_id(1)
        if ns:
            plan = _Scatter(refs[10:10 + ns], refs[18 + ns:18 + 2 * ns], refs[19 + 2 * ns:], halves)
            pl.when(jnp.logical_and(i == 0, j == 0))(plan.start)

        @pl.when(j == 0)
        def _():
            fh, rf = _rms_stats(f_ref[...])
            dz, dg = _rms_bwd(fh, rf, gpost_ref[...], FFN_SCALE * dy_ref[...])
            dz_ref[...] = dz.astype(BF16)
            _acc_out(dgpost_ref, i == 0, dg)
            xh, _ = _rms_stats(x_ref[...])
            h_ref[...] = (xh * gpre_ref[...]).astype(BF16)
            dh_scr[...] = jnp.zeros_like(dh_scr)

        for r0 in range(0, tm, rb):
            rows = slice(r0, r0 + rb)
            ds = _dot_nt(dz_ref[rows, :], wd_ref[...]).astype(BF16)
            av = a_ref[rows, :]
            bv = b_ref[rows, :]
            sg, one_minus_sg = _sigmoid_pair(av)
            sl = av * sg
            s_ref[rows, :] = sl * bv
            da = (ds * bv) * (sg + sl * one_minus_sg)
            db = ds * sl
            da_ref[rows, :] = da
            db_ref[rows, :] = db
            dh_scr[rows, :] += _dot_nt(da, wg_ref[...]) + _dot_nt(db, wu_ref[...])

        @pl.when(j == nj - 1)
        def _():
            xh, rx = _rms_stats(x_ref[...])
            dxn, dg = _rms_bwd(xh, rx, gpre_ref[...], dh_scr[...])
            dx_ref[...] = dy_ref[...] + dxn
            _acc_out(dgpre_ref, i == 0, dg)

        if ns:
            pl.when(jnp.logical_and(i == ni - 1, j == nj - 1))(plan.finish)

    row = pl.BlockSpec((tm, D), lambda i, j: (i, 0))
    vec = pl.BlockSpec((1, D), lambda i, j: (0, 0))
    w_in = _chunk_spec(sel, D, F)
    w_out = _chunk_spec(sel, F, D)
    act = pl.BlockSpec((None, tm, F), lambda i, j: (j, i, 0))
    act_shape = jax.ShapeDtypeStruct((nj, T, F), BF16)
    hbm = pl.BlockSpec(memory_space=pl.ANY)
    return pl.pallas_call(
        body,
        name="ffn_bwd_scatter" if ns else "ffn_bwd",
        grid=(ni, nj),
        in_specs=[row, row, row, act, act, vec, vec, w_in, w_in, w_out] + [hbm] * ns,
        out_specs=[row, row, row, act, act, act, vec, vec] + [hbm] * ns,
        out_shape=[
            jax.ShapeDtypeStruct((T, D), F32),
            jax.ShapeDtypeStruct((T, D), BF16),
            jax.ShapeDtypeStruct((T, D), BF16),
            act_shape, act_shape, act_shape,
            jax.ShapeDtypeStruct((1, D), F32),
            jax.ShapeDtypeStruct((1, D), F32),
        ] + [jax.ShapeDtypeStruct((N_DEV - 1, *g.shape[2:]), g.dtype) for g in scatter],
        scratch_shapes=[pltpu.VMEM((tm, D), F32)] + (_Scatter.semaphores(ns) if ns else []),
        compiler_params=_cparams("arbitrary", "arbitrary"),
    )(dy, x, f, a, b, g_pre, g_post, wg, wu, wd, *scatter)


def _tn_matmul(a, b, buf, like, sel, tk):
    a_chunked = a.ndim == 3
    nj = a.shape[0] if a_chunked else b.shape[0]
    T, M, N = a.shape[-2], a.shape[-1], b.shape[-1]
    tk = min(tk, T)
    nk = T // tk

    def body(a_ref, b_ref, *rest):
        o_ref, acc_scr = rest[-2:]
        k = pl.program_id(1)

        @pl.when(k == 0)
        def _():
            acc_scr[...] = jnp.zeros_like(acc_scr)

        acc_scr[...] += _dot_tn(a_ref[...], b_ref[...])

        @pl.when(k == nk - 1)
        def _():
            o_ref[...] = acc_scr[...].astype(BF16)

    def spec(chunked, width):
        if chunked:
            return pl.BlockSpec((None, tk, width), lambda j, k: (j, k, 0))
        return pl.BlockSpec((tk, width), lambda j, k: (k, 0))

    have = buf is not None
    return pl.pallas_call(
        body,
        name="tn_matmul",
        grid=(nj, nk),
        in_specs=[spec(a_chunked, M), spec(not a_chunked, N)] + ([pl.BlockSpec(memory_space=pl.ANY)] if have else []),
        out_specs=pl.BlockSpec((None,) * (1 + len(sel)) + (M, N), lambda j, k: (j, *sel, 0, 0)),
        out_shape=jax.ShapeDtypeStruct(like.shape, BF16),
        input_output_aliases={2: 0} if have else {},
        scratch_shapes=[pltpu.VMEM((M, N), F32)],
        compiler_params=_cparams("parallel", "arbitrary"),
    )(a, b, *([buf] if have else []))


def _sgu_fwd(x, g_pre, g_post, win, lng, lnb, wsm, bsb, wout, sel, tm):
    T, D = x.shape
    nc, E = win.shape[0], win.shape[-1]
    S = 2 * E
    dg = S // N_GROUPS
    wo_rows = wout.shape[-2]
    tm = min(tm, T)
    nq = tm // CHUNK

    def body(x_ref, gpre_ref, gpost_ref, win_ref, lng_ref, lnb_ref, ws_ref, bsb_ref, wout_ref,
             xo_ref, zp_ref, m_ref, u_scr, vn_scr, gt_scr):
        x = x_ref[...]
        xh, _ = _rms_stats(x)
        hn = (xh * gpre_ref[...]).astype(BF16)
        v_parts = []
        for c in range(nc):
            zp = _dot(hn, win_ref[c])
            zp_ref[c] = zp.astype(BF16)
            z = _gelu(zp)
            if c < nc // 2:
                u_scr[:, c * E:(c + 1) * E] = z
            else:
                v_parts.append(z)
        vh_parts, _ = _ln_stats(v_parts, S)
        for c, vh in enumerate(vh_parts):
            cols = slice(c * E, (c + 1) * E)
            vn_scr[:, cols] = (vh * lng_ref[:, cols] + lnb_ref[:, cols]).astype(BF16)
        for q in range(nq):
            rows = slice(q * CHUNK, (q + 1) * CHUNK)
            for g in range(N_GROUPS):
                cols = slice(g * dg, (g + 1) * dg)
                mixed = _dot(ws_ref[g], vn_scr[rows, cols]) + bsb_ref[g]
                gt_scr[rows, cols] = (u_scr[rows, cols] * mixed).astype(BF16)
        m = _dot(gt_scr[:, 0:wo_rows], wout_ref[0])
        for c in range(1, nc):
            m += _dot(gt_scr[:, c * wo_rows:(c + 1) * wo_rows], wout_ref[c])
        m_ref[...] = m
        mh, _ = _rms_stats(m)
        xo_ref[...] = x + mh * gpost_ref[...]

    row = pl.BlockSpec((tm, D), lambda i: (i, 0))
    return pl.pallas_call(
        body,
        name="sgu_fwd",
        grid=(T // tm,),
        in_specs=[row, _resident((1, D)), _resident((1, D)), _chunks_spec(win, sel), _resident((1, S)),
                  _resident((1, S)), _resident(wsm.shape), _resident(bsb.shape), _chunks_spec(wout, sel)],
        out_specs=[row, pl.BlockSpec((nc, tm, E), lambda i: (0, i, 0)), row],
        out_shape=[
            jax.ShapeDtypeStruct((T, D), F32),
            jax.ShapeDtypeStruct((nc, T, E), BF16),
            jax.ShapeDtypeStruct((T, D), F32),
        ],
        scratch_shapes=[pltpu.VMEM((tm, S), F32), pltpu.VMEM((tm, S), BF16), pltpu.VMEM((tm, S), BF16)],
        compiler_params=_cparams("parallel"),
    )(x, g_pre, g_post, win, lng, lnb, wsm, bsb, wout)


def _sgu_bwd(dy, x, m, zp, g_pre, g_post, win, lng, lnb, wsm, wsmt, bsb, wout, sel, tm):
    T, D = x.shape
    nc, E = win.shape[0], win.shape[-1]
    S = 2 * E
    dg = S // N_GROUPS
    wo_rows = wout.shape[-2]
    tm = min(tm, T)
    nq = tm // CHUNK

    def body(dy_ref, x_ref, m_ref, zp_ref, gpre_ref, gpost_ref, win_ref, lng_ref, lnb_ref, ws_ref, wst_ref,
             bsb_ref, wout_ref,
             dx_ref, hn_ref, dzp_ref, gated_ref, dm_ref, dws_ref, dbs_ref, dlng_ref, dlnb_ref, dgpre_ref,
             dgpost_ref, u_scr, d_scr, vh_scr, vn_scr, gg_scr):
        first = pl.program_id(0) == 0
        dy = dy_ref[...]
        mh, rm = _rms_stats(m_ref[...])
        dm, dgp = _rms_bwd(mh, rm, gpost_ref[...], dy)
        _acc_out(dgpost_ref, first, dgp)
        dm = dm.astype(BF16)
        dm_ref[...] = dm
        for c in range(nc):
            d_scr[:, c * wo_rows:(c + 1) * wo_rows] = _dot_nt(dm, wout_ref[c])
        v_parts = []
        for c in range(nc):
            z, gg_scr[c] = _gelu_pair(zp_ref[c])
            if c < nc // 2:
                u_scr[:, c * E:(c + 1) * E] = z.astype(F32)
            else:
                v_parts.append(z.astype(F32))
        vh_parts, rstd = _ln_stats(v_parts, S)
        for c, vh in enumerate(vh_parts):
            cols = slice(c * E, (c + 1) * E)
            vh_scr[:, cols] = vh
            vn_scr[:, cols] = (vh * lng_ref[:, cols] + lnb_ref[:, cols]).astype(BF16)

        @pl.when(first)
        def _():
            dws_ref[...] = jnp.zeros_like(dws_ref)
            dbs_ref[...] = jnp.zeros_like(dbs_ref)
            dlng_ref[...] = jnp.zeros_like(dlng_ref)
            dlnb_ref[...] = jnp.zeros_like(dlnb_ref)

        for q in range(nq):
            rows = slice(q * CHUNK, (q + 1) * CHUNK)
            for g in range(N_GROUPS):
                cols = slice(g * dg, (g + 1) * dg)
                vn = vn_scr[rows, cols]
                mixed = _dot(ws_ref[g], vn) + bsb_ref[g]
                u = u_scr[rows, cols]
                dgt = d_scr[rows, cols]
                gated_ref[(g * dg) // wo_rows, rows, (g * dg) % wo_rows:(g * dg) % wo_rows + dg] = (u * mixed).astype(BF16)
                dmix = dgt * u
                dbs_ref[:, cols] += dmix
                dmix = dmix.astype(BF16)
                dws_ref[g] += _dot_nt(dmix, vn)
                u_scr[rows, cols] = dgt * mixed
                d_scr[rows, cols] = _dot(wst_ref[g], dmix)
        dvn = [d_scr[:, c * E:(c + 1) * E] for c in range(nc // 2)]
        vh = [vh_scr[:, c * E:(c + 1) * E] for c in range(nc // 2)]
        for c, (d, v) in enumerate(zip(dvn, vh)):
            dlng_ref[:, c * E:(c + 1) * E] += jnp.sum(d * v, axis=0, keepdims=True)
            dlnb_ref[:, c * E:(c + 1) * E] += jnp.sum(d, axis=0, keepdims=True)
        dvh = [d * lng_ref[:, c * E:(c + 1) * E] for c, d in enumerate(dvn)]
        dv = _ln_bwd(vh, rstd, dvh, S)
        dhn = None
        for c in range(nc):
            dz = u_scr[:, c * E:(c + 1) * E] if c < nc // 2 else dv[c - nc // 2]
            dzp = dz.astype(BF16) * gg_scr[c]
            dzp_ref[c] = dzp
            t = _dot_nt(dzp, win_ref[c])
            dhn = t if dhn is None else dhn + t
        xh, rx = _rms_stats(x_ref[...])
        hn_ref[...] = (xh * gpre_ref[...]).astype(BF16)
        dxn, dgq = _rms_bwd(xh, rx, gpre_ref[...], dhn)
        dx_ref[...] = dy + dxn
        _acc_out(dgpre_ref, first, dgq)

    row = pl.BlockSpec((tm, D), lambda i: (i, 0))

    def whole(shape):
        return pl.BlockSpec(shape, lambda i: (0,) * len(shape))

    return pl.pallas_call(
        body,
        name="sgu_bwd",
        grid=(T // tm,),
        in_specs=[row, row, row, pl.BlockSpec((nc, tm, E), lambda i: (0, i, 0)), _resident((1, D)), _resident((1, D)),
                  _chunks_spec(win, sel), _resident((1, S)), _resident((1, S)), _resident(wsm.shape),
                  _resident(wsmt.shape), _resident(bsb.shape), _chunks_spec(wout, sel)],
        out_specs=[row, row, pl.BlockSpec((nc, tm, E), lambda i: (0, i, 0)),
                   pl.BlockSpec((nc, tm, wo_rows), lambda i: (0, i, 0)), row,
                   whole((N_GROUPS, CHUNK, CHUNK)), whole((CHUNK, S)), whole((1, S)), whole((1, S)),
                   whole((1, D)), whole((1, D))],
        out_shape=[
            jax.ShapeDtypeStruct((T, D), F32),
            jax.ShapeDtypeStruct((T, D), BF16),
            jax.ShapeDtypeStruct((nc, T, E), BF16),
            jax.ShapeDtypeStruct((nc, T, wo_rows), BF16),
            jax.ShapeDtypeStruct((T, D), BF16),
            jax.ShapeDtypeStruct((N_GROUPS, CHUNK, CHUNK), F32),
            jax.ShapeDtypeStruct((CHUNK, S), F32),
            jax.ShapeDtypeStruct((1, S), F32),
            jax.ShapeDtypeStruct((1, S), F32),
            jax.ShapeDtypeStruct((1, D), F32),
            jax.ShapeDtypeStruct((1, D), F32),
        ],
        scratch_shapes=[pltpu.VMEM((tm, S), F32), pltpu.VMEM((tm, S), F32), pltpu.VMEM((tm, S), F32),
                        pltpu.VMEM((tm, S), BF16), pltpu.VMEM((nc, tm, E), BF16)],
        compiler_params=_cparams("arbitrary"),
    )(dy, x, m, zp, g_pre, g_post, win, lng, lnb, wsm, wsmt, bsb, wout)


def _shifted_windows(buf, r0, cols, lo, hi):
    n = CONV_RB + HALO
    base = buf[r0:r0 + n, cols]
    for r in range(8):
        rolled = base if r == 0 else pltpu.roll(base, n - r, axis=0)
        for s in range(r, hi, 8):
            if s >= lo:
                yield s, rolled[s - r:s - r + CONV_RB]


def _conv_fwd_a(x, g_pre, wpw1, sel, tm):
    T, D = x.shape
    nc, E = wpw1.shape[0], wpw1.shape[-1]
    C = 2 * E
    tm = min(tm, T)

    def body(x_ref, gpre_ref, w_ref, y_ref, p_ref):
        xh, _ = _rms_stats(x_ref[...])
        hn = (xh * gpre_ref[...]).astype(BF16)
        ps = []
        for c in range(nc):
            p = _dot(hn, w_ref[c])
            p_ref[c] = p.astype(BF16)
            ps.append(p)
        for c in range(nc // 2):
            y_ref[:, c * E:(c + 1) * E] = ps[c] * jax.nn.sigmoid(ps[c + nc // 2])

    row = pl.BlockSpec((tm, D), lambda i: (i, 0))
    return pl.pallas_call(
        body,
        name="conv_fwd_a",
        grid=(T // tm,),
        in_specs=[row, _resident((1, D)), _chunks_spec(wpw1, sel)],
        out_specs=[pl.BlockSpec((tm, C), lambda i: (i, 0)), pl.BlockSpec((nc, tm, E), lambda i: (0, i, 0))],
        out_shape=[jax.ShapeDtypeStruct((T, C), F32), jax.ShapeDtypeStruct((nc, T, E), BF16)],
        compiler_params=_cparams("parallel"),
    )(x, g_pre, wpw1)


def _conv_fwd_b(x, y, wdw, bdw, lng, lnb, wpw2, g_post, sel, tm):
    T, D = x.shape
    C = y.shape[1]
    nc, E = wpw2.shape[0], wpw2.shape[-2]
    tm = min(tm, T)
    per = tm // HALO

    def body(x_ref, y_ref, yprev_ref, wdw_ref, bdw_ref, lng_ref, lnb_ref, w_ref, gpost_ref,
             xo_ref, c_ref, m_ref, ybuf):
        i = pl.program_id(0)
        ybuf[0:HALO, :] = jnp.where(i > 0, yprev_ref[...], 0.0)
        ybuf[HALO:HALO + tm, :] = y_ref[...]
        off = HALO - (CONV_W - 1)
        for r0 in range(0, tm, CONV_RB):
            for c0 in range(0, C, CONV_CB):
                cols = slice(c0, c0 + CONV_CB)
                acc = jnp.broadcast_to(bdw_ref[:, cols], (CONV_RB, CONV_CB))
                for s, win in _shifted_windows(ybuf, r0, cols, off, off + CONV_W):
                    acc = acc + wdw_ref[s - off:s - off + 1, cols] * win
                c_ref[r0:r0 + CONV_RB, cols] = acc
        (ch,), _ = _ln_stats([c_ref[...]], C)
        cn = ch * lng_ref[...] + lnb_ref[...]
        qv = (cn * jax.nn.sigmoid(cn)).astype(BF16)
        m = _dot(qv[:, 0:E], w_ref[0])
        for c in range(1, nc):
            m += _dot(qv[:, c * E:(c + 1) * E], w_ref[c])
        m_ref[...] = m
        mh, _ = _rms_stats(m)
        xo_ref[...] = x_ref[...] + mh * gpost_ref[...]

    row = pl.BlockSpec((tm, D), lambda i: (i, 0))
    crow = pl.BlockSpec((tm, C), lambda i: (i, 0))
    prev = pl.BlockSpec((HALO, C), lambda i: (jnp.maximum(i * per - 1, 0), 0))
    return pl.pallas_call(
        body,
        name="conv_fwd_b",
        grid=(T // tm,),
        in_specs=[row, crow, prev, _resident(wdw.shape), _resident((1, C)), _resident((1, C)), _resident((1, C)),
                  _chunks_spec(wpw2, sel), _resident((1, D))],
        out_specs=[row, crow, row],
        out_shape=[jax.ShapeDtypeStruct((T, D), F32), jax.ShapeDtypeStruct((T, C), F32),
                   jax.ShapeDtypeStruct((T, D), F32)],
        scratch_shapes=[pltpu.VMEM((HALO + tm, C), F32)],
        compiler_params=_cparams("parallel"),
    )(x, y, y, wdw, bdw, lng, lnb, wpw2, g_post)


def _conv_bwd_b(dy, m, c, lng, lnb, wpw2, g_post, sel, tm):
    T, D = dy.shape
    C = c.shape[1]
    nc, E = wpw2.shape[0], wpw2.shape[-2]
    tm = min(tm, T)

    def body(dy_ref, m_ref, c_ref, lng_ref, lnb_ref, w_ref, gpost_ref,
             dm_ref, q_ref, dc_ref, dlng_ref, dlnb_ref, dbdw_ref, dgpost_ref, dq_scr):
        first = pl.program_id(0) == 0
        mh, rm = _rms_stats(m_ref[...])
        dm, dgp = _rms_bwd(mh, rm, gpost_ref[...], dy_ref[...])
        _acc_out(dgpost_ref, first, dgp)
        dm = dm.astype(BF16)
        dm_ref[...] = dm
        for k in range(nc):
            dq_scr[:, k * E:(k + 1) * E] = _dot_nt(dm, w_ref[k])
        (ch,), rstd = _ln_stats([c_ref[...]], C)
        cn = ch * lng_ref[...] + lnb_ref[...]
        sg = jax.nn.sigmoid(cn)
        qv = (cn * sg).astype(BF16)
        for k in range(nc):
            q_ref[k] = qv[:, k * E:(k + 1) * E]
        dcn = dq_scr[...] * (sg * (1.0 + cn * (1.0 - sg)))
        _acc_out(dlng_ref, first, jnp.sum(dcn * ch, axis=0, keepdims=True))
        _acc_out(dlnb_ref, first, jnp.sum(dcn, axis=0, keepdims=True))
        (dc,) = _ln_bwd([ch], rstd, [dcn * lng_ref[...]], C)
        dc_ref[...] = dc
        _acc_out(dbdw_ref, first, jnp.sum(dc, axis=0, keepdims=True))

    row = pl.BlockSpec((tm, D), lambda i: (i, 0))
    crow = pl.BlockSpec((tm, C), lambda i: (i, 0))

    def whole(shape):
        return pl.BlockSpec(shape, lambda i: (0,) * len(shape))

    return pl.pallas_call(
        body,
        name="conv_bwd_b",
        grid=(T // tm,),
        in_specs=[row, row, crow, _resident((1, C)), _resident((1, C)), _chunks_spec(wpw2, sel), _resident((1, D))],
        out_specs=[row, pl.BlockSpec((nc, tm, E), lambda i: (0, i, 0)), crow, whole((1, C)), whole((1, C)),
                   whole((1, C)), whole((1, D))],
        out_shape=[jax.ShapeDtypeStruct((T, D), BF16), jax.ShapeDtypeStruct((nc, T, E), BF16),
                   jax.ShapeDtypeStruct((T, C), F32), jax.ShapeDtypeStruct((1, C), F32),
                   jax.ShapeDtypeStruct((1, C), F32), jax.ShapeDtypeStruct((1, C), F32),
                   jax.ShapeDtypeStruct((1, D), F32)],
        scratch_shapes=[pltpu.VMEM((tm, C), F32)],
        compiler_params=_cparams("arbitrary"),
    )(dy, m, c, lng, lnb, wpw2, g_post)


def _conv_bwd_a(dy, x, dc, y, p, g_pre, wdw, wpw1, sel, tm):
    T, D = x.shape
    C = y.shape[1]
    nc, E = wpw1.shape[0], wpw1.shape[-1]
    tm = min(tm, T)
    per = tm // HALO
    n_tiles = T // tm
    KP = wdw.shape[0]

    def body(dy_ref, x_ref, dc_ref, dcnext_ref, y_ref, yprev_ref, p_ref, gpre_ref, wdw_ref, w_ref,
             dx_ref, hn_ref, dp_ref, dwdw_ref, dgpre_ref, ybuf, dcbuf, dyg_scr, dw8_scr):
        i = pl.program_id(0)
        first = i == 0
        ybuf[0:HALO, :] = jnp.where(i > 0, yprev_ref[...], 0.0)
        ybuf[HALO:HALO + tm, :] = y_ref[...]
        dcbuf[0:tm, :] = dc_ref[...]
        dcbuf[tm:tm + HALO, :] = jnp.where(i < n_tiles - 1, dcnext_ref[...], 0.0)
        off = HALO - (CONV_W - 1)
        @pl.when(first)
        def _():
            dw8_scr[...] = jnp.zeros_like(dw8_scr)

        for r0 in range(0, tm, CONV_RB):
            for c0 in range(0, C, CONV_CB):
                cols = slice(c0, c0 + CONV_CB)
                dcb = dcbuf[r0:r0 + CONV_RB, cols]
                acc = jnp.zeros((CONV_RB, CONV_CB), F32)
                for s, win in _shifted_windows(dcbuf, r0, cols, 0, CONV_W):
                    k = CONV_W - 1 - s
                    acc = acc + wdw_ref[k:k + 1, cols] * win
                dyg_scr[r0:r0 + CONV_RB, cols] = acc
                for s, win in _shifted_windows(ybuf, r0, cols, off, off + CONV_W):
                    dw8_scr[s - off, :, cols] += jnp.sum((dcb * win).reshape(CONV_RB // 8, 8, CONV_CB), axis=0)

        @pl.when(i == n_tiles - 1)
        def _():
            dwdw_ref[...] = jnp.sum(dw8_scr[...], axis=1)

        dhn = None
        for c in range(nc // 2):
            cols = slice(c * E, (c + 1) * E)
            av = p_ref[c].astype(F32)
            sg = jax.nn.sigmoid(p_ref[c + nc // 2].astype(F32))
            dygc = dyg_scr[:, cols]
            da = (dygc * sg).astype(BF16)
            dgt = (dygc * av * sg * (1.0 - sg)).astype(BF16)
            dp_ref[c] = da
            dp_ref[c + nc // 2] = dgt
            t = _dot_nt(da, w_ref[c]) + _dot_nt(dgt, w_ref[c + nc // 2])
            dhn = t if dhn is None else dhn + t
        xh, rx = _rms_stats(x_ref[...])
        hn_ref[...] = (xh * gpre_ref[...]).astype(BF16)
        dxn, dgq = _rms_bwd(xh, rx, gpre_ref[...], dhn)
        dx_ref[...] = dy_ref[...] + dxn
        _acc_out(dgpre_ref, first, dgq)

    row = pl.BlockSpec((tm, D), lambda i: (i, 0))
    crow = pl.BlockSpec((tm, C), lambda i: (i, 0))
    prev = pl.BlockSpec((HALO, C), lambda i: (jnp.maximum(i * per - 1, 0), 0))
    nxt = pl.BlockSpec((HALO, C), lambda i: (jnp.minimum((i + 1) * per, T // HALO - 1), 0))
    chunks = pl.BlockSpec((nc, tm, E), lambda i: (0, i, 0))

    def whole(shape):
        return pl.BlockSpec(shape, lambda i: (0,) * len(shape))

    return pl.pallas_call(
        body,
        name="conv_bwd_a",
        grid=(n_tiles,),
        in_specs=[row, row, crow, nxt, crow, prev, chunks, _resident((1, D)), _resident(wdw.shape),
                  _chunks_spec(wpw1, sel)],
        out_specs=[row, row, chunks, whole((KP, C)), whole((1, D))],
        out_shape=[jax.ShapeDtypeStruct((T, D), F32), jax.ShapeDtypeStruct((T, D), BF16),
                   jax.ShapeDtypeStruct((nc, T, E), BF16), jax.ShapeDtypeStruct((KP, C), F32),
                   jax.ShapeDtypeStruct((1, D), F32)],
        scratch_shapes=[pltpu.VMEM((HALO + tm, C), F32), pltpu.VMEM((tm + HALO, C), F32),
                        pltpu.VMEM((tm, C), F32), pltpu.VMEM((KP, 8, C), F32)],
        compiler_params=_cparams("arbitrary"),
    )(dy, x, dc, dc, y, y, p, g_pre, wdw, wpw1)


def _loss_head(y, target, tm):
    T, D = y.shape
    tm = min(tm, T)

    def body(y_ref, t_ref, dy_ref, loss_ref):
        e = y_ref[...] - t_ref[...]
        dy_ref[...] = e * (1.0 / D)
        part = jnp.sum(jnp.sum(e * e, axis=-1, keepdims=True), axis=0, keepdims=True) * (0.5 / D)
        _acc_out(loss_ref, pl.program_id(0) == 0, jnp.broadcast_to(part, loss_ref.shape))

    row = pl.BlockSpec((tm, D), lambda i: (i, 0))
    return pl.pallas_call(
        body,
        name="loss_head",
        grid=(T // tm,),
        in_specs=[row, row],
        out_specs=[row, pl.BlockSpec((8, 128), lambda i: (0, 0))],
        out_shape=[jax.ShapeDtypeStruct((T, D), F32), jax.ShapeDtypeStruct((8, 128), F32)],
        compiler_params=_cparams("arbitrary"),
    )(y, target)


def _row_tile(rows, cols, itemsize_budget=2 * 1024 * 1024):
    want = max(16, itemsize_budget // (4 * cols))
    if rows <= want:
        return rows
    t = (want // 16) * 16
    while t > 16 and rows % t:
        t -= 16
    return t if rows % t == 0 else rows


def _sum_parts(parts):
    n, R, C = parts.shape
    tr = _row_tile(R, C * n // 2 if parts.dtype == BF16 else C * n)

    def body(p_ref, o_ref):
        acc = p_ref[0].astype(F32)
        for s in range(1, n):
            acc = acc + p_ref[s].astype(F32)
        o_ref[...] = acc

    return pl.pallas_call(
        body,
        name="sum_parts",
        grid=(R // tr,),
        in_specs=[pl.BlockSpec((n, tr, C), lambda i: (0, i, 0))],
        out_specs=pl.BlockSpec((tr, C), lambda i: (i, 0)),
        out_shape=jax.ShapeDtypeStruct((R, C), F32),
        compiler_params=_cparams("parallel"),
    )(parts)


def _cast_into_slot(w):
    _, R, C = w.shape
    tr = _row_tile(R, C)

    def body(w_ref, o_ref):
        o_ref[...] = w_ref[...].astype(BF16)

    def own_slot(h, i):
        return 2 * lax.axis_index("x") + lax.axis_index("y"), h, i, 0

    return pl.pallas_call(
        body,
        name="cast_into_slot",
        grid=(2, R // tr),
        in_specs=[pl.BlockSpec((None, tr, C), lambda h, i: (h, i, 0))],
        out_specs=pl.BlockSpec((None, None, tr, C), own_slot),
        out_shape=jax.ShapeDtypeStruct((N_CHIPS, 2, R, C), BF16),
        compiler_params=_cparams("parallel", "parallel"),
    )(w)


def _sum_with_own(arrived, own):
    n, R, C = arrived.shape
    tr = _row_tile(R, C * (n + 1) // 2)

    def body(a_ref, own_ref, o_ref):
        acc = own_ref[...].astype(F32)
        for s in range(n):
            acc = acc + a_ref[s].astype(F32)
        o_ref[...] = acc

    def own_piece(i):
        return 2 * lax.axis_index("x") + lax.axis_index("y"), lax.axis_index("c"), i, 0

    return pl.pallas_call(
        body,
        name="sum_with_own",
        grid=(R // tr,),
        in_specs=[pl.BlockSpec((n, tr, C), lambda i: (0, i, 0)), pl.BlockSpec((None, None, tr, C), own_piece)],
        out_specs=pl.BlockSpec((None, tr, C), lambda i: (lax.axis_index("c"), i, 0)),
        out_shape=jax.ShapeDtypeStruct((2, R, C), F32),
        compiler_params=_cparams("parallel"),
    )(arrived, own)


def _adamw(w, g, m, v):
    R, C = w.shape
    tr = _row_tile(R, C * 7 // 2)
    c1 = 1.0 - ADAM_B1 ** ADAM_STEP
    c2 = 1.0 - ADAM_B2 ** ADAM_STEP

    def body(w_ref, g_ref, m_ref, v_ref, d_ref, mo_ref, vo_ref):
        g = g_ref[...]
        mn = ADAM_B1 * m_ref[...] + (1.0 - ADAM_B1) * g
        vn = ADAM_B2 * v_ref[...] + (1.0 - ADAM_B2) * (g * g)
        mo_ref[...] = mn
        vo_ref[...] = vn
        d_ref[...] = -ADAM_LR * ((mn / c1) / (jnp.sqrt(vn / c2) + ADAM_EPS) + ADAM_WD * w_ref[...])

    blk = pl.BlockSpec((tr, C), lambda i: (i, 0))
    shp = jax.ShapeDtypeStruct((R, C), F32)
    return pl.pallas_call(
        body,
        name="adamw",
        grid=(R // tr,),
        in_specs=[blk, blk, blk, blk],
        out_specs=[blk, blk, blk],
        out_shape=[shp, shp, shp],
        compiler_params=_cparams("parallel"),
    )(w, g, m, v)


def _adamw_into(w, g, m, v, outs, sel):
    n, R, C = w.shape
    tr = _row_tile(R, C * 4)
    c1 = 1.0 - ADAM_B1 ** ADAM_STEP
    c2 = 1.0 - ADAM_B2 ** ADAM_STEP

    def body(w_ref, g_ref, m_ref, v_ref, *rest):
        go_ref, d_ref, mo_ref, vo_ref = rest[-4:]
        g = g_ref[...]
        mn = ADAM_B1 * m_ref[...] + (1.0 - ADAM_B1) * g
        vn = ADAM_B2 * v_ref[...] + (1.0 - ADAM_B2) * (g * g)
        go_ref[...] = g
        mo_ref[...] = mn
        vo_ref[...] = vn
        d_ref[...] = -ADAM_LR * ((mn / c1) / (jnp.sqrt(vn / c2) + ADAM_EPS) + ADAM_WD * w_ref[...])

    entry = pl.BlockSpec((None, tr, C), lambda i: (sel, i, 0))
    hbm = pl.BlockSpec(memory_space=pl.ANY)
    have = outs is not None
    shp = jax.ShapeDtypeStruct((n, R, C), F32)
    return pl.pallas_call(
        body,
        name="adamw_into",
        grid=(R // tr,),
        in_specs=[entry, pl.BlockSpec((tr, C), lambda i: (i, 0)), entry, entry] + ([hbm] * 4 if have else []),
        out_specs=[entry] * 4,
        out_shape=[shp] * 4,
        input_output_aliases={4 + t: t for t in range(4)} if have else {},
        compiler_params=_cparams("parallel"),
    )(w, g, m, v, *(outs if have else ()))


def _gather_weights(halved, whole):
    nh, nw = len(halved), len(whole)

    def body(*refs):
        w_in = refs[nh:nh + nw]
        h_out, w_out = refs[nh + nw:2 * nh + nw], refs[2 * nh + nw:2 * (nh + nw)]
        ws_send, ws_recv, loc_sem = refs[2 * (nh + nw):2 * (nh + nw) + 3]
        plan = _Gather(h_out, refs[2 * (nh + nw) + 3:])
        x, y, c = _my_place()
        me_chip = 2 * x + y
        plan.start()

        def small(a, j, slot, to):
            return pltpu.make_async_remote_copy(src_ref=w_in[a], dst_ref=w_out[a].at[slot],
                                                send_sem=ws_send.at[a, j], recv_sem=ws_recv.at[a, j],
                                                device_id=to, device_id_type=MESH)

        for a in range(nw):
            pltpu.make_async_copy(w_in[a], w_out[a].at[me_chip], loc_sem.at[a]).start()
            for j, ch in enumerate(plan.chips):
                small(a, j, me_chip, (*ch, c)).start()
        plan.forward()
        plan.finish()
        for a in range(nw):
            for j, ch in enumerate(plan.chips):
                cp = small(a, j, 2 * ch[0] + ch[1], (x, y, c))
                cp.wait_recv()
                cp.wait_send()
            pltpu.make_async_copy(w_in[a], w_out[a].at[me_chip], loc_sem.at[a]).wait()

    hbm = pl.BlockSpec(memory_space=pl.ANY)
    outs = pl.pallas_call(
        body,
        name="gather_weights",
        in_specs=[hbm] * (nh + nw),
        out_specs=[hbm] * (nh + nw),
        out_shape=[jax.ShapeDtypeStruct(a.shape, a.dtype) for a in halved]
        + [jax.ShapeDtypeStruct((N_CHIPS, *a.shape), a.dtype) for a in whole],
        input_output_aliases={a: a for a in range(nh)},
        scratch_shapes=[pltpu.SemaphoreType.DMA((max(nw, 1), 3)), pltpu.SemaphoreType.DMA((max(nw, 1), 3)),
                        pltpu.SemaphoreType.DMA((max(nw, 1),))] + _Gather.semaphores(nh),
    )(*halved, *whole)
    return outs[:nh], outs[nh:]


def _scatter_grads(grads, halves, into):
    n = len(grads)

    def body(*refs):
        plan = _Scatter(refs[:n], refs[2 * n:3 * n], refs[3 * n:], halves)
        plan.start()
        plan.finish()

    hbm = pl.BlockSpec(memory_space=pl.ANY)
    return pl.pallas_call(
        body,
        name="scatter_grads",
        in_specs=[hbm] * (2 * n),
        out_specs=[hbm] * n,
        out_shape=[jax.ShapeDtypeStruct(t.shape, t.dtype) for t in into],
        input_output_aliases={n + a: a for a in range(n)},
        scratch_shapes=_Scatter.semaphores(n),
    )(*grads, *into)


def _swap_halves(halves):
    n = len(halves)

    def body(*refs):
        h_out = refs[n:2 * n]
        send_sem, recv_sem = refs[2 * n:]
        x, y, c = _my_place()
        sib = (x, y, 1 - c)
        for a in range(n):
            pltpu.make_async_remote_copy(src_ref=h_out[a].at[c], dst_ref=h_out[a].at[c], send_sem=send_sem.at[a],
                                         recv_sem=recv_sem.at[a], device_id=sib, device_id_type=MESH).start()
        for a in range(n):
            cp = pltpu.make_async_remote_copy(src_ref=h_out[a].at[c], dst_ref=h_out[a].at[1 - c],
                                              send_sem=send_sem.at[a], recv_sem=recv_sem.at[a], device_id=sib,
                                              device_id_type=MESH)
            cp.wait_send()
            cp.wait_recv()

    hbm = pl.BlockSpec(memory_space=pl.ANY)
    return pl.pallas_call(
        body,
        name="swap_halves",
        in_specs=[hbm] * n,
        out_specs=[hbm] * n,
        out_shape=[jax.ShapeDtypeStruct(h.shape, h.dtype) for h in halves],
        input_output_aliases={a: a for a in range(n)},
        scratch_shapes=[pltpu.SemaphoreType.DMA((n,)), pltpu.SemaphoreType.DMA((n,))],
    )(*halves)


def _share_all(buf):
    def body(b_in, b_out, send_sem, recv_sem):
        x, y, c = _my_place()
        me = 4 * x + 2 * y + c
        for d in range(N_DEV):
            @pl.when(d != me)
            def _():
                pltpu.make_async_remote_copy(src_ref=b_out.at[me], dst_ref=b_out.at[me], send_sem=send_sem.at[d],
                                             recv_sem=recv_sem.at[me], device_id=(d // 4, (d // 2) % 2, d % 2),
                                             device_id_type=MESH).start()
        for d in range(N_DEV):
            @pl.when(d != me)
            def _():
                cp = pltpu.make_async_remote_copy(src_ref=b_out.at[me], dst_ref=b_out.at[d], send_sem=send_sem.at[d],
                                                  recv_sem=recv_sem.at[d], device_id=(x, y, c),
                                                  device_id_type=MESH)
                cp.wait_send()
                cp.wait_recv()

    hbm = pl.BlockSpec(memory_space=pl.ANY)
    return pl.pallas_call(
        body,
        name="share_all",
        in_specs=[hbm],
        out_specs=hbm,
        out_shape=jax.ShapeDtypeStruct(buf.shape, buf.dtype),
        input_output_aliases={0: 0},
        scratch_shapes=[pltpu.SemaphoreType.DMA((N_DEV,)), pltpu.SemaphoreType.DMA((N_DEV,))],
    )(buf)


TM_FFN = 512
RB_FFN_BWD = 256
TM_SGU = 256
TM_CONV = 256
TK_WGRAD = 2048
TM_LOSS = 1024


FFN_KINDS = ("ff_w_gate", "ff_w_up", "ff_w_down")


def _layer_kinds(i):
    return FFN_KINDS + (("sgu_w_in", "sgu_w_out") if i % 2 == 0 else ("conv_w_pw1", "conv_w_pw2"))


def _local_step(x, target, G, W, exchange=None):
    depth = W["norm_g"].shape[0]
    G = [dict(g) for g in G]
    saved = []
    vec = lambda v: v.reshape(1, -1)

    def mixer_w(i, k):
        w = G[i][k]
        return w.reshape(w.shape[0], -1, w.shape[-1])
    wsm, wsmt, bsb = [], [], []
    n_sgu = W["sgu_w_spatial"].shape[0]
    causal = jnp.tril(jnp.ones((CHUNK, CHUNK), dtype=bool))
    dgrp = W["sgu_ln_g"].shape[1] // N_GROUPS
    for jx in range(n_sgu):
        ws = jnp.where(causal[None], W["sgu_w_spatial"][jx], 0.0).astype(BF16)
        wsm.append(ws)
        wsmt.append(jnp.swapaxes(ws, 1, 2))
        bsb.append(jnp.broadcast_to(W["sgu_b_spatial"][jx][:, :, None], (N_GROUPS, CHUNK, dgrp)))
    kp = HALO
    wdw = [jnp.pad(W["conv_w_dw"][jx], ((0, kp - CONV_W), (0, 0))) for jx in range(W["conv_w_dw"].shape[0])]

    def ffn(x, i, f_idx, gather=()):
        g = W["norm_g"][i]
        return _ffn_fwd(x, vec(g[4 * f_idx]), vec(g[4 * f_idx + 1]), G[i]["ff_w_gate"], G[i]["ff_w_up"],
                        G[i]["ff_w_down"], (f_idx,), TM_FFN, gather)

    for i in range(depth):
        g = W["norm_g"][i]
        rec = {"x0": x}
        if exchange is not None and i + 1 < depth:
            kinds = _layer_kinds(i + 1)
            x, rec["a1"], rec["b1"], rec["f1"], *filled = ffn(x, i, 0, [G[i + 1][k] for k in kinds])
            G[i + 1] = dict(zip(kinds, filled))
        else:
            x, rec["a1"], rec["b1"], rec["f1"] = ffn(x, i, 0)
        rec["x1"] = x
        j = i // 2
        if i % 2 == 0:
            x, rec["zp"], rec["m"] = _sgu_fwd(
                x, vec(g[2]), vec(g[3]), mixer_w(i, "sgu_w_in"), vec(W["sgu_ln_g"][j]), vec(W["sgu_ln_b"][j]),
                wsm[j], bsb[j], mixer_w(i, "sgu_w_out"), (), TM_SGU)
        else:
            rec["y"], rec["p"] = _conv_fwd_a(x, vec(g[2]), mixer_w(i, "conv_w_pw1"), (), TM_CONV)
            x, rec["c"], rec["m"] = _conv_fwd_b(
                x, rec["y"], wdw[j], vec(W["conv_b_dw"][j]), vec(W["conv_ln_g"][j]), vec(W["conv_ln_b"][j]),
                mixer_w(i, "conv_w_pw2"), vec(g[3]), (), TM_CONV)
        rec["x2"] = x
        x, rec["a2"], rec["b2"], rec["f2"] = ffn(x, i, 1)
        saved.append(rec)

    dx, loss_tile = _loss_head(x, target, TM_LOSS)
    loss = loss_tile[0, 0]

    big = [{k: None for k in _layer_kinds(i)} for i in range(depth)]
    small = {k: [None] * W[k].shape[0] for k in
             ("sgu_ln_g", "sgu_ln_b", "sgu_w_spatial", "sgu_b_spatial", "conv_w_dw", "conv_b_dw", "conv_ln_g",
              "conv_ln_b")}
    dnorm = [[None] * 6 for _ in range(depth)]
    pieces = [None] * depth
    waiting = []

    def wgrad(i, k, a, b, sel):
        like = G[i][k] if sel else mixer_w(i, k)
        big[i][k] = _tn_matmul(a, b, big[i][k], like, sel, TK_WGRAD)

    def as_pieces(b):
        return b.reshape(N_CHIPS, 2, -1, b.shape[-1])

    def ffn_back(dx, i, f_idx, xin, a, b, f, send):
        g = W["norm_g"][i]
        dx, h, dz, s, da, db, dgpre, dgpost, *arrived = _ffn_bwd(
            dx, xin, f, a, b, vec(g[4 * f_idx]), vec(g[4 * f_idx + 1]),
            G[i]["ff_w_gate"], G[i]["ff_w_up"], G[i]["ff_w_down"], (f_idx,), TM_FFN, list(send.values()))
        wgrad(i, "ff_w_gate", h, da, (f_idx,))
        wgrad(i, "ff_w_up", h, db, (f_idx,))
        wgrad(i, "ff_w_down", s, dz, (f_idx,))
        dnorm[i][4 * f_idx] = dgpre[0]
        dnorm[i][4 * f_idx + 1] = dgpost[0]
        return dx, dict(zip(send, arrived))

    for i in reversed(range(depth)):
        rec = saved[i]
        g = W["norm_g"][i]
        j = i // 2
        if exchange is not None and waiting:
            sent = waiting.pop()
            dx, arrived = ffn_back(dx, i, 1, rec["x2"], rec["a2"], rec["b2"], rec["f2"],
                                   {k: (p, (0, 1)) for k, p in pieces[sent].items()})
            exchange(sent, pieces[sent], arrived)
        else:
            dx, _ = ffn_back(dx, i, 1, rec["x2"], rec["a2"], rec["b2"], rec["f2"], {})
        if i % 2 == 0:
            (dx, hn, dzp, gated, dm, dws, dbs_acc, dlng, dlnb, dgpre, dgpost) = _sgu_bwd(
                dx, rec["x1"], rec["m"], rec["zp"], vec(g[2]), vec(g[3]), mixer_w(i, "sgu_w_in"),
                vec(W["sgu_ln_g"][j]), vec(W["sgu_ln_b"][j]), wsm[j], wsmt[j], bsb[j], mixer_w(i, "sgu_w_out"), (),
                TM_SGU)
            wgrad(i, "sgu_w_in", hn, dzp, ())
            wgrad(i, "sgu_w_out", gated, dm, ())
            small["sgu_w_spatial"][j] = jnp.where(causal[None], dws, 0.0)
            small["sgu_b_spatial"][j] = dbs_acc.reshape(CHUNK, N_GROUPS, dgrp).sum(-1).T
            small["sgu_ln_g"][j] = dlng[0]
            small["sgu_ln_b"][j] = dlnb[0]
        else:
            dm, q, dc, dlng, dlnb, dbdw, dgpost = _conv_bwd_b(
                dx, rec["m"], rec["c"], vec(W["conv_ln_g"][j]), vec(W["conv_ln_b"][j]), mixer_w(i, "conv_w_pw2"),
                vec(g[3]), (), TM_CONV)
            dx, hn, dp, dwdw, dgpre = _conv_bwd_a(
                dx, rec["x1"], dc, rec["y"], rec["p"], vec(g[2]), wdw[j], mixer_w(i, "conv_w_pw1"), (), TM_CONV)
            wgrad(i, "conv_w_pw1", hn, dp, ())
            wgrad(i, "conv_w_pw2", q, dm, ())
            small["conv_w_dw"][j] = dwdw[:CONV_W]
            small["conv_b_dw"][j] = dbdw[0]
            small["conv_ln_g"][j] = dlng[0]
            small["conv_ln_b"][j] = dlnb[0]
        dnorm[i][2] = dgpre[0]
        dnorm[i][3] = dgpost[0]
        if exchange is not None and i == 0:
            dx, arrived = ffn_back(dx, i, 0, rec["x0"], rec["a1"], rec["b1"], rec["f1"],
                                   {k: (as_pieces(b), (1,) if k in FFN_KINDS else (0, 1)) for k, b in big[i].items()})
            pieces[i] = {k: as_pieces(b) for k, b in big[i].items()}
            rest = _scatter_grads([pieces[i][k] for k in FFN_KINDS], [(0,)] * len(FFN_KINDS),
                                  [arrived[k] for k in FFN_KINDS])
            arrived.update(zip(FFN_KINDS, rest))
            exchange(i, pieces[i], arrived)
        else:
            dx, _ = ffn_back(dx, i, 0, rec["x0"], rec["a1"], rec["b1"], rec["f1"], {})
            pieces[i] = {k: as_pieces(b) for k, b in big[i].items()}
            waiting.append(i)

    small = {k: jnp.stack(v) for k, v in small.items()}
    small["norm_g"] = jnp.stack([jnp.stack(r) for r in dnorm])
    return loss, dx, pieces, small


BIG = ("ff_w_gate", "ff_w_up", "ff_w_down", "sgu_w_in", "sgu_w_out", "conv_w_pw1", "conv_w_pw2")
SHARDED_SMALL = ("norm_g", "conv_w_dw", "conv_b_dw", "conv_ln_g", "conv_ln_b")
REPLICATED = ("sgu_ln_g", "sgu_ln_b", "sgu_w_spatial", "sgu_b_spatial")
WEIGHTS = ("norm_g", "ff_w_gate", "ff_w_up", "ff_w_down", "sgu_w_in", "sgu_ln_g", "sgu_ln_b", "sgu_w_spatial",
           "sgu_b_spatial", "sgu_w_out", "conv_w_pw1", "conv_w_dw", "conv_b_dw", "conv_ln_g", "conv_ln_b",
           "conv_w_pw2")


def _rows8(a, width):
    r = a.reshape(-1, width)
    pad = (-r.shape[0]) % 8
    return jnp.pad(r, ((0, pad), (0, 0))) if pad else r


def _pack(arrs, width):
    parts = [_rows8(a, width) for a in arrs]
    return jnp.concatenate(parts, axis=0), [p.shape[0] for p in parts]


def _unpack(buf, like):
    out, r0 = [], 0
    width = buf.shape[-1]
    for a in like:
        n = -(-(a.size // width) // 8) * 8
        rows = a.size // width
        out.append(buf[..., r0:r0 + rows, :].reshape(*buf.shape[:-2], *a.shape))
        r0 += n
    return out


def kernel(x, norm_g, ff_w_gate, ff_w_up, ff_w_down, sgu_w_in, sgu_ln_g, sgu_ln_b, sgu_w_spatial, sgu_b_spatial, sgu_w_out, conv_w_pw1, conv_w_dw, conv_b_dw, conv_ln_g, conv_ln_b, conv_w_pw2, loss_target, m_norm_g, m_ff_w_gate, m_ff_w_up, m_ff_w_down, m_sgu_w_in, m_sgu_ln_g, m_sgu_ln_b, m_sgu_w_spatial, m_sgu_b_spatial, m_sgu_w_out, m_conv_w_pw1, m_conv_w_dw, m_conv_b_dw, m_conv_ln_g, m_conv_ln_b, m_conv_w_pw2, v_norm_g, v_ff_w_gate, v_ff_w_up, v_ff_w_down, v_sgu_w_in, v_sgu_ln_g, v_sgu_ln_b, v_sgu_w_spatial, v_sgu_b_spatial, v_sgu_w_out, v_conv_w_pw1, v_conv_w_dw, v_conv_b_dw, v_conv_ln_g, v_conv_ln_b, v_conv_w_pw2):
    w = dict(norm_g=norm_g, ff_w_gate=ff_w_gate, ff_w_up=ff_w_up, ff_w_down=ff_w_down, sgu_w_in=sgu_w_in,
             sgu_ln_g=sgu_ln_g, sgu_ln_b=sgu_ln_b, sgu_w_spatial=sgu_w_spatial, sgu_b_spatial=sgu_b_spatial,
             sgu_w_out=sgu_w_out, conv_w_pw1=conv_w_pw1, conv_w_dw=conv_w_dw, conv_b_dw=conv_b_dw,
             conv_ln_g=conv_ln_g, conv_ln_b=conv_ln_b, conv_w_pw2=conv_w_pw2)
    mom = dict(norm_g=m_norm_g, ff_w_gate=m_ff_w_gate, ff_w_up=m_ff_w_up, ff_w_down=m_ff_w_down,
               sgu_w_in=m_sgu_w_in, sgu_ln_g=m_sgu_ln_g, sgu_ln_b=m_sgu_ln_b, sgu_w_spatial=m_sgu_w_spatial,
               sgu_b_spatial=m_sgu_b_spatial, sgu_w_out=m_sgu_w_out, conv_w_pw1=m_conv_w_pw1,
               conv_w_dw=m_conv_w_dw, conv_b_dw=m_conv_b_dw, conv_ln_g=m_conv_ln_g, conv_ln_b=m_conv_ln_b,
               conv_w_pw2=m_conv_w_pw2)
    vel = dict(norm_g=v_norm_g, ff_w_gate=v_ff_w_gate, ff_w_up=v_ff_w_up, ff_w_down=v_ff_w_down,
               sgu_w_in=v_sgu_w_in, sgu_ln_g=v_sgu_ln_g, sgu_ln_b=v_sgu_ln_b, sgu_w_spatial=v_sgu_w_spatial,
               sgu_b_spatial=v_sgu_b_spatial, sgu_w_out=v_sgu_w_out, conv_w_pw1=v_conv_w_pw1,
               conv_w_dw=v_conv_w_dw, conv_b_dw=v_conv_b_dw, conv_ln_g=v_conv_ln_g, conv_ln_b=v_conv_ln_b,
               conv_w_pw2=v_conv_w_pw2)
    T, D = x.shape[1], x.shape[2]
    shard_w = conv_b_dw.shape[1]

    xi, yi, ci = _my_place()
    me_chip = (2 * xi + yi).astype(jnp.int32)
    me = (4 * xi + 2 * yi + ci).astype(jnp.int32)

    depth = norm_g.shape[0]

    def entry(k, i):
        return i if k in FFN_KINDS else i // 2

    def stacked(a):
        return a.reshape(a.shape[0], -1, a.shape[-1])

    G = []
    for i in range(depth):
        G.append({k: _cast_into_slot(w[k][entry(k, i)].reshape(2, -1, w[k].shape[-1])) for k in _layer_kinds(i)})
    small_buf, _ = _pack([w[k] for k in SHARDED_SMALL], shard_w)
    first, (small_all,) = _gather_weights(list(G[0].values()), [small_buf])
    G[0] = dict(zip(G[0], first))
    W = {}
    for k, part in zip(SHARDED_SMALL, _unpack(small_all, [w[k] for k in SHARDED_SMALL])):
        W[k] = jnp.moveaxis(part, 0, -2).reshape(*w[k].shape[:-1], N_CHIPS * shard_w)
    for k in REPLICATED:
        W[k] = w[k]

    results = {k: None for k in BIG}

    def reduce_and_update(i, pieces, arrived):
        kinds = list(pieces)
        both = _swap_halves([_sum_with_own(arrived[k], pieces[k]) for k in kinds])
        for k, g in zip(kinds, both):
            results[k] = _adamw_into(stacked(w[k]), g.reshape(-1, g.shape[-1]), stacked(mom[k]), stacked(vel[k]),
                                     results[k], entry(k, i))

    loss, dx, _, small = _local_step(x[0], loss_target[0], G, W, reduce_and_update)
    loss = lax.psum(loss, ("x", "y", "c"))
    grads, delta, new_m, new_v = {}, {}, {}, {}
    for k in BIG:
        grads[k], delta[k], new_m[k], new_v[k] = (t.reshape(w[k].shape) for t in results[k])

    sbuf, _ = _pack([small[k] for k in SHARDED_SMALL + REPLICATED], D)
    slots = lax.dynamic_update_slice(jnp.zeros((N_DEV, *sbuf.shape), F32), sbuf[None], (me, 0, 0))
    ssum = _sum_parts(_share_all(slots))
    for k, gfull in zip(SHARDED_SMALL + REPLICATED, _unpack(ssum, [small[k] for k in SHARDED_SMALL + REPLICATED])):
        if k in SHARDED_SMALL:
            gfull = lax.dynamic_slice_in_dim(gfull, me_chip * shard_w, shard_w, axis=gfull.ndim - 1)
        grads[k] = gfull

    for names, width in ((SHARDED_SMALL, shard_w), (REPLICATED, CHUNK)):
        packed = [_pack([src[k] for k in names], width)[0] for src in (w, grads, mom, vel)]
        outs = _adamw(*packed)
        for res, out in zip((delta, new_m, new_v), outs):
            for k, a in zip(names, _unpack(out, [w[k] for k in names])):
                res[k] = a

    return (loss, dx[None], *[grads[k] for k in WEIGHTS], *[delta[k] for k in WEIGHTS],
            *[new_m[k] for k in WEIGHTS], *[new_v[k] for k in WEIGHTS])
```

```python
import functools

import jax
import jax.numpy as jnp
from jax import lax
from jax.experimental import pallas as pl
from jax.experimental.pallas import tpu as pltpu

F32 = jnp.float32
BF16 = jnp.bfloat16
EPS = 1e-6
N_CHIPS = 4
N_DEV = 8
N_GROUPS = 8
CHUNK = 128
CONV_W = 31
HALO = 32
CONV_RB = 64
CONV_CB = 256
VMEM_LIMIT_V7X = 60 * 1024 * 1024
MESH = pl.DeviceIdType.MESH

ADAM_LR = 0.001
ADAM_B1 = 0.9
ADAM_B2 = 0.999
ADAM_EPS = 1e-08
ADAM_WD = 0.01
ADAM_STEP = 10
FFN_SCALE = 0.5


def _cparams(*sem, **kw):
    return pltpu.CompilerParams(dimension_semantics=sem, vmem_limit_bytes=VMEM_LIMIT_V7X, **kw)


def _resident(shape):
    return pl.BlockSpec(shape, lambda *_: (0,) * len(shape), pipeline_mode=pl.Buffered(1))


def _dot(a, b):
    return jnp.dot(a, b, preferred_element_type=F32)


def _dot_nt(a, b):
    return lax.dot_general(a, b, (((1,), (1,)), ((), ())), preferred_element_type=F32)


def _dot_tn(a, b):
    return lax.dot_general(a, b, (((0,), (0,)), ((), ())), preferred_element_type=F32)


def _rms_stats(x):
    r = lax.rsqrt(jnp.mean(x * x, axis=-1, keepdims=True) + EPS)
    return x * r, r


def _rms_bwd(xh, r, g, dy):
    dxh = dy * g
    dx = r * (dxh - xh * jnp.mean(dxh * xh, axis=-1, keepdims=True))
    return dx, jnp.sum(dy * xh, axis=0, keepdims=True)


def _ln_stats(parts, width):
    mu = sum(jnp.sum(p, axis=-1, keepdims=True) for p in parts) / width
    cen = [p - mu for p in parts]
    var = sum(jnp.sum(c * c, axis=-1, keepdims=True) for c in cen) / width
    rstd = lax.rsqrt(var + EPS)
    return [c * rstd for c in cen], rstd


def _ln_bwd(vh_parts, rstd, dvh_parts, width):
    m1 = sum(jnp.sum(d, axis=-1, keepdims=True) for d in dvh_parts) / width
    m2 = sum(jnp.sum(d * v, axis=-1, keepdims=True) for d, v in zip(dvh_parts, vh_parts)) / width
    return [rstd * (d - m1 - v * m2) for d, v in zip(dvh_parts, vh_parts)]


_GELU_C = 0.7978845608028654
_GELU_A = 0.044715


def _gelu(x):
    return 0.5 * x * (1.0 + jnp.tanh(_GELU_C * (x + _GELU_A * x * x * x)))


def _gelu_pair(x):
    x2 = x * x
    t = jnp.tanh(_GELU_C * (x + _GELU_A * (x * x2)))
    half = 0.5 * (1.0 + t)
    return x * half, half + (0.5 * _GELU_C) * x * (1.0 - t * t) * (1.0 + (3.0 * _GELU_A) * x2)


def _sigmoid_pair(a):
    e = jnp.exp(jnp.minimum(-a, 80.0))
    sg = 1.0 / (1.0 + e)
    return sg, e * sg


def _acc_out(ref, first, val):
    @pl.when(first)
    def _():
        ref[...] = val

    @pl.when(jnp.logical_not(first))
    def _():
        ref[...] += val


def _my_place():
    return lax.axis_index("x"), lax.axis_index("y"), lax.axis_index("c")


class _Gather:
    def __init__(self, bufs, sems):
        self.bufs = bufs
        self.own_sems, self.fwd_sems = sems[:2], sems[2:4]
        self.x, self.y, self.c = _my_place()
        x, y = self.x, self.y
        self.chips = [(1 - x, y), (x, 1 - y), (1 - x, 1 - y)]

    def _copy(self, a, j, chip, half, to, sems):
        spot = self.bufs[a].at[2 * chip[0] + chip[1], pl.ds(half, 1)]
        return pltpu.make_async_remote_copy(src_ref=spot, dst_ref=spot, send_sem=sems[0].at[a, j],
                                            recv_sem=sems[1].at[a, j], device_id=to, device_id_type=MESH)

    def _own(self, a, j):
        return self._copy(a, j, (self.x, self.y), self.c, (*self.chips[j], self.c), self.own_sems)

    def _passed_on(self, a, j):
        return self._copy(a, j, self.chips[j], self.c, (self.x, self.y, 1 - self.c), self.fwd_sems)

    def start(self):
        for j in range(3):
            for a in range(len(self.bufs)):
                self._own(a, j).start()

    def forward(self):
        me = (self.x, self.y, self.c)
        for j in range(3):
            for a in range(len(self.bufs)):
                self._copy(a, j, self.chips[j], self.c, me, self.own_sems).wait_recv()
                self._passed_on(a, j).start()

    def finish(self):
        me = (self.x, self.y, self.c)
        for j in range(3):
            for a in range(len(self.bufs)):
                self._copy(a, j, self.chips[j], 1 - self.c, me, self.fwd_sems).wait_recv()
        for j in range(3):
            for a in range(len(self.bufs)):
                self._own(a, j).wait_send()
                self._passed_on(a, j).wait_send()

    @staticmethod
    def semaphores(n):
        return [pltpu.SemaphoreType.DMA((n, 3)) for _ in range(4)]


class _Scatter:
    def __init__(self, g_in, g_out, sems, halves):
        self.g_in, self.g_out = g_in, g_out
        self.send_sem, self.recv_sem = sems
        self.halves = halves
        x, y, c = _my_place()
        self.c = c
        self.me = 4 * x + 2 * y + c

    def _piece(self, a, d, slot):
        return pltpu.make_async_remote_copy(
            src_ref=self.g_in[a].at[d // 2, d % 2], dst_ref=self.g_out[a].at[slot],
            send_sem=self.send_sem.at[a, d], recv_sem=self.recv_sem.at[a, slot],
            device_id=(d // 4, (d // 2) % 2, d % 2), device_id_type=MESH)

    def _to(self, a):
        return [d for d in range(N_DEV) if d % 2 in self.halves[a]]

    def start(self):
        for a in range(len(self.g_in)):
            for d in self._to(a):
                @pl.when(d != self.me)
                def _():
                    self._piece(a, d, lax.rem(self.me - d - 1 + N_DEV, N_DEV)).start()

    def finish(self):
        for a in range(len(self.g_in)):
            for h in self.halves[a]:
                @pl.when(self.c == h)
                def _():
                    for slot in range(N_DEV - 1):
                        self._piece(a, 0, slot).wait_recv()
            for d in self._to(a):
                @pl.when(d != self.me)
                def _():
                    self._piece(a, d, 0).wait_send()

    @staticmethod
    def semaphores(n):
        return [pltpu.SemaphoreType.DMA((n, N_DEV)), pltpu.SemaphoreType.DMA((n, N_DEV - 1))]


def _chunk_spec(sel, rows, cols):
    return pl.BlockSpec((None,) * (1 + len(sel)) + (rows, cols), lambda i, j: (j, *sel, 0, 0))


def _chunks_spec(w, sel):
    return pl.BlockSpec((w.shape[0],) + (None,) * len(sel) + w.shape[-2:], lambda *_: (0, *sel, 0, 0),
                        pipeline_mode=pl.Buffered(1))


def _ffn_fwd(x, g_pre, g_post, wg, wu, wd, sel, tm, gather=()):
    T, D = x.shape
    nj, F = wg.shape[0], wg.shape[-1]
    tm = min(tm, T)
    ni = T // tm
    rb = min(RB_FFN_FWD, tm)
    ng = len(gather)

    def body(*refs):
        x_ref, gpre_ref, gpost_ref, wg_ref, wu_ref, wd_ref = refs[:6]
        xo_ref, a_ref, b_ref, f_ref = refs[6 + ng:10 + ng]
        h_scr, acc_scr = refs[10 + 2 * ng:12 + 2 * ng]
        i = pl.program_id(0)
        j = pl.program_id(1)
        if ng:
            plan = _Gather(refs[10 + ng:10 + 2 * ng], refs[12 + 2 * ng:])
            pl.when(jnp.logical_and(i == 0, j == 0))(plan.start)
            pl.when(jnp.logical_and(i == (5 * ni) // 8, j == nj - 1))(plan.forward)

        @pl.when(j == 0)
        def _():
            xh, _ = _rms_stats(x_ref[...])
            h_scr[...] = (xh * gpre_ref[...]).astype(BF16)
            acc_scr[...] = jnp.zeros_like(acc_scr)

        for r0 in range(0, tm, rb):
            rows = slice(r0, r0 + rb)
            h = h_scr[rows, :]
            a = _dot(h, wg_ref[...]).astype(BF16)
            b = _dot(h, wu_ref[...]).astype(BF16)
            a_ref[rows, :] = a
            b_ref[rows, :] = b
            sg, _ = _sigmoid_pair(a)
            acc_scr[rows, :] += _dot((a * sg) * b, wd_ref[...])

        @pl.when(j == nj - 1)
        def _():
            f = acc_scr[...]
            f_ref[...] = f
            fh, _ = _rms_stats(f)
            xo_ref[...] = x_ref[...] + FFN_SCALE * (fh * gpost_ref[...])

        if ng:
            pl.when(jnp.logical_and(i == ni - 1, j == nj - 1))(plan.finish)

    row = pl.BlockSpec((tm, D), lambda i, j: (i, 0))
    vec = pl.BlockSpec((1, D), lambda i, j: (0, 0))
    w_in = _chunk_spec(sel, D, F)
    w_out = _chunk_spec(sel, F, D)
    act = pl.BlockSpec((None, tm, F), lambda i, j: (j, i, 0))
    hbm = pl.BlockSpec(memory_space=pl.ANY)
    return pl.pallas_call(
        body,
        name="ffn_fwd_gather" if ng else "ffn_fwd",
        grid=(ni, nj),
        in_specs=[row, vec, vec, w_in, w_in, w_out] + [hbm] * ng,
        out_specs=[row, act, act, row] + [hbm] * ng,
        out_shape=[
            jax.ShapeDtypeStruct((T, D), F32),
            jax.ShapeDtypeStruct((nj, T, F), BF16),
            jax.ShapeDtypeStruct((nj, T, F), BF16),
            jax.ShapeDtypeStruct((T, D), F32),
        ] + [jax.ShapeDtypeStruct(g.shape, g.dtype) for g in gather],
        input_output_aliases={6 + a: 4 + a for a in range(ng)},
        scratch_shapes=[pltpu.VMEM((tm, D), BF16), pltpu.VMEM((tm, D), F32)] + (_Gather.semaphores(ng) if ng else []),
        compiler_params=_cparams("arbitrary", "arbitrary"),
    )(x, g_pre, g_post, wg, wu, wd, *gather)


def _ffn_bwd(dy, x, f, a, b, g_pre, g_post, wg, wu, wd, sel, tm, scatter=()):
    T, D = x.shape
    nj, F = wg.shape[0], wg.shape[-1]
    tm = min(tm, T)
    ni = T // tm
    rb = min(RB_FFN_BWD, tm)
    ns = len(scatter)
    halves = [h for _, h in scatter]
    scatter = [g for g, _ in scatter]

    def body(*refs):
        dy_ref, x_ref, f_ref, a_ref, b_ref, gpre_ref, gpost_ref, wg_ref, wu_ref, wd_ref = refs[:10]
        dx_ref, h_ref, dz_ref, s_ref, da_ref, db_ref, dgpre_ref, dgpost_ref = refs[10 + ns:18 + ns]
        dh_scr = refs[18 + 2 * ns]
        i = pl.program_id(0)
        j = pl.program_id(1)
        if ns:
            plan = _Scatter(refs[10:10 + ns], refs[18 + ns:18 + 2 * ns], refs[19 + 2 * ns:], halves)
            pl.when(jnp.logical_and(i == 0, j == 0))(plan.start)

        @pl.when(j == 0)
        def _():
            fh, rf = _rms_stats(f_ref[...])
            dz, dg = _rms_bwd(fh, rf, gpost_ref[...], FFN_SCALE * dy_ref[...])
            dz_ref[...] = dz.astype(BF16)
            _acc_out(dgpost_ref, i == 0, dg)
            xh, _ = _rms_stats(x_ref[...])
            h_ref[...] = (xh * gpre_ref[...]).astype(BF16)
            dh_scr[...] = jnp.zeros_like(dh_scr)

        for r0 in range(0, tm, rb):
            rows = slice(r0, r0 + rb)
            ds = _dot_nt(dz_ref[rows, :], wd_ref[j]).astype(BF16)
            av = a_ref[rows, :]
            bv = b_ref[rows, :]
            sg, one_minus_sg = _sigmoid_pair(av)
            sl = av * sg
            s_ref[rows, :] = sl * bv
            da = (ds * bv) * (sg + sl * one_minus_sg)
            db = ds * sl
            da_ref[rows, :] = da
            db_ref[rows, :] = db
            dh_scr[rows, :] += _dot_nt(da, wg_ref[j]) + _dot_nt(db, wu_ref[j])

        @pl.when(j == nj - 1)
        def _():
            xh, rx = _rms_stats(x_ref[...])
            dxn, dg = _rms_bwd(xh, rx, gpre_ref[...], dh_scr[...])
            dx_ref[...] = dy_ref[...] + dxn
            _acc_out(dgpre_ref, i == 0, dg)

        if ns:
            pl.when(jnp.logical_and(i == ni - 1, j == nj - 1))(plan.finish)

    row = pl.BlockSpec((tm, D), lambda i, j: (i, 0))
    vec = pl.BlockSpec((1, D), lambda i, j: (0, 0))
    w_in = _chunks_spec(wg, sel)
    w_out = _chunks_spec(wd, sel)
    act = pl.BlockSpec((None, tm, F), lambda i, j: (j, i, 0))
    act_shape = jax.ShapeDtypeStruct((nj, T, F), BF16)
    hbm = pl.BlockSpec(memory_space=pl.ANY)
    return pl.pallas_call(
        body,
        name="ffn_bwd_scatter" if ns else "ffn_bwd",
        grid=(ni, nj),
        in_specs=[row, row, row, act, act, vec, vec, w_in, w_in, w_out] + [hbm] * ns,
        out_specs=[row, row, row, act, act, act, vec, vec] + [hbm] * ns,
        out_shape=[
            jax.ShapeDtypeStruct((T, D), F32),
            jax.ShapeDtypeStruct((T, D), BF16),
            jax.ShapeDtypeStruct((T, D), BF16),
            act_shape, act_shape, act_shape,
            jax.ShapeDtypeStruct((1, D), F32),
            jax.ShapeDtypeStruct((1, D), F32),
        ] + [jax.ShapeDtypeStruct((N_DEV - 1, *g.shape[2:]), g.dtype) for g in scatter],
        scratch_shapes=[pltpu.VMEM((tm, D), F32)] + (_Scatter.semaphores(ns) if ns else []),
        compiler_params=_cparams("arbitrary", "arbitrary"),
    )(dy, x, f, a, b, g_pre, g_post, wg, wu, wd, *scatter)


def _tn_matmul(a, b, buf, like, sel, tk):
    a_chunked = a.ndim == 3
    nj = a.shape[0] if a_chunked else b.shape[0]
    T, M, N = a.shape[-2], a.shape[-1], b.shape[-1]
    tk = min(tk, T)
    nk = T // tk

    def body(a_ref, b_ref, *rest):
        o_ref, acc_scr = rest[-2:]
        k = pl.program_id(1)

        @pl.when(k == 0)
        def _():
            acc_scr[...] = jnp.zeros_like(acc_scr)

        acc_scr[...] += _dot_tn(a_ref[...], b_ref[...])

        @pl.when(k == nk - 1)
        def _():
            o_ref[...] = acc_scr[...].astype(BF16)

    def spec(chunked, width):
        if chunked:
            return pl.BlockSpec((None, tk, width), lambda j, k: (j, k, 0))
        return pl.BlockSpec((tk, width), lambda j, k: (k, 0))

    have = buf is not None
    return pl.pallas_call(
        body,
        name="tn_matmul",
        grid=(nj, nk),
        in_specs=[spec(a_chunked, M), spec(not a_chunked, N)] + ([pl.BlockSpec(memory_space=pl.ANY)] if have else []),
        out_specs=pl.BlockSpec((None,) * (1 + len(sel)) + (M, N), lambda j, k: (j, *sel, 0, 0)),
        out_shape=jax.ShapeDtypeStruct(like.shape, BF16),
        input_output_aliases={2: 0} if have else {},
        scratch_shapes=[pltpu.VMEM((M, N), F32)],
        compiler_params=_cparams("parallel", "arbitrary"),
    )(a, b, *([buf] if have else []))


def _sgu_fwd(x, g_pre, g_post, win, lng, lnb, wsm, bsb, wout, sel, tm):
    T, D = x.shape
    nc, E = win.shape[0], win.shape[-1]
    S = 2 * E
    dg = S // N_GROUPS
    wo_rows = wout.shape[-2]
    tm = min(tm, T)
    nq = tm // CHUNK

    def body(x_ref, gpre_ref, gpost_ref, win_ref, lng_ref, lnb_ref, ws_ref, bsb_ref, wout_ref,
             xo_ref, zp_ref, m_ref, u_scr, vn_scr, gt_scr):
        x = x_ref[...]
        xh, _ = _rms_stats(x)
        hn = (xh * gpre_ref[...]).astype(BF16)
        v_parts = []
        for c in range(nc):
            zp = _dot(hn, win_ref[c])
            zp_ref[c] = zp.astype(BF16)
            z = _gelu(zp)
            if c < nc // 2:
                u_scr[:, c * E:(c + 1) * E] = z
            else:
                v_parts.append(z)
        vh_parts, _ = _ln_stats(v_parts, S)
        for c, vh in enumerate(vh_parts):
            cols = slice(c * E, (c + 1) * E)
            vn_scr[:, cols] = (vh * lng_ref[:, cols] + lnb_ref[:, cols]).astype(BF16)
        for q in range(nq):
            rows = slice(q * CHUNK, (q + 1) * CHUNK)
            for g in range(N_GROUPS):
                cols = slice(g * dg, (g + 1) * dg)
                mixed = _dot(ws_ref[g], vn_scr[rows, cols]) + bsb_ref[g]
                gt_scr[rows, cols] = (u_scr[rows, cols] * mixed).astype(BF16)
        m = _dot(gt_scr[:, 0:wo_rows], wout_ref[0])
        for c in range(1, nc):
            m += _dot(gt_scr[:, c * wo_rows:(c + 1) * wo_rows], wout_ref[c])
        m_ref[...] = m
        mh, _ = _rms_stats(m)
        xo_ref[...] = x + mh * gpost_ref[...]

    row = pl.BlockSpec((tm, D), lambda i: (i, 0))
    return pl.pallas_call(
        body,
        name="sgu_fwd",
        grid=(T // tm,),
        in_specs=[row, _resident((1, D)), _resident((1, D)), _chunks_spec(win, sel), _resident((1, S)),
                  _resident((1, S)), _resident(wsm.shape), _resident(bsb.shape), _chunks_spec(wout, sel)],
        out_specs=[row, pl.BlockSpec((nc, tm, E), lambda i: (0, i, 0)), row],
        out_shape=[
            jax.ShapeDtypeStruct((T, D), F32),
            jax.ShapeDtypeStruct((nc, T, E), BF16),
            jax.ShapeDtypeStruct((T, D), F32),
        ],
        scratch_shapes=[pltpu.VMEM((tm, S), F32), pltpu.VMEM((tm, S), BF16), pltpu.VMEM((tm, S), BF16)],
        compiler_params=_cparams("parallel"),
    )(x, g_pre, g_post, win, lng, lnb, wsm, bsb, wout)


def _sgu_bwd(dy, x, m, zp, g_pre, g_post, win, lng, lnb, wsm, wsmt, bsb, wout, sel, tm):
    T, D = x.shape
    nc, E = win.shape[0], win.shape[-1]
    S = 2 * E
    dg = S // N_GROUPS
    wo_rows = wout.shape[-2]
    tm = min(tm, T)
    nq = tm // CHUNK

    def body(dy_ref, x_ref, m_ref, zp_ref, gpre_ref, gpost_ref, win_ref, lng_ref, lnb_ref, ws_ref, wst_ref,
             bsb_ref, wout_ref,
             dx_ref, hn_ref, dzp_ref, gated_ref, dm_ref, dws_ref, dbs_ref, dlng_ref, dlnb_ref, dgpre_ref,
             dgpost_ref, u_scr, d_scr, vh_scr, vn_scr, gg_scr):
        first = pl.program_id(0) == 0
        dy = dy_ref[...]
        mh, rm = _rms_stats(m_ref[...])
        dm, dgp = _rms_bwd(mh, rm, gpost_ref[...], dy)
        _acc_out(dgpost_ref, first, dgp)
        dm = dm.astype(BF16)
        dm_ref[...] = dm
        for c in range(nc):
            d_scr[:, c * wo_rows:(c + 1) * wo_rows] = _dot_nt(dm, wout_ref[c])
        v_parts = []
        for c in range(nc):
            z, gg_scr[c] = _gelu_pair(zp_ref[c])
            if c < nc // 2:
                u_scr[:, c * E:(c + 1) * E] = z.astype(F32)
            else:
                v_parts.append(z.astype(F32))
        vh_parts, rstd = _ln_stats(v_parts, S)
        for c, vh in enumerate(vh_parts):
            cols = slice(c * E, (c + 1) * E)
            vh_scr[:, cols] = vh
            vn_scr[:, cols] = (vh * lng_ref[:, cols] + lnb_ref[:, cols]).astype(BF16)

        @pl.when(first)
        def _():
            dws_ref[...] = jnp.zeros_like(dws_ref)
            dbs_ref[...] = jnp.zeros_like(dbs_ref)
            dlng_ref[...] = jnp.zeros_like(dlng_ref)
            dlnb_ref[...] = jnp.zeros_like(dlnb_ref)

        for q in range(nq):
            rows = slice(q * CHUNK, (q + 1) * CHUNK)
            for g in range(N_GROUPS):
                cols = slice(g * dg, (g + 1) * dg)
                vn = vn_scr[rows, cols]
                mixed = _dot(ws_ref[g], vn) + bsb_ref[g]
                u = u_scr[rows, cols]
                dgt = d_scr[rows, cols]
                gated_ref[(g * dg) // wo_rows, rows, (g * dg) % wo_rows:(g * dg) % wo_rows + dg] = (u * mixed).astype(BF16)
                dmix = dgt * u
                dbs_ref[:, cols] += dmix
                dmix = dmix.astype(BF16)
                dws_ref[g] += _dot_nt(dmix, vn)
                u_scr[rows, cols] = dgt * mixed
                d_scr[rows, cols] = _dot(wst_ref[g], dmix)
        dvn = [d_scr[:, c * E:(c + 1) * E] for c in range(nc // 2)]
        vh = [vh_scr[:, c * E:(c + 1) * E] for c in range(nc // 2)]
        for c, (d, v) in enumerate(zip(dvn, vh)):
            dlng_ref[:, c * E:(c + 1) * E] += jnp.sum(d * v, axis=0, keepdims=True)
            dlnb_ref[:, c * E:(c + 1) * E] += jnp.sum(d, axis=0, keepdims=True)
        dvh = [d * lng_ref[:, c * E:(c + 1) * E] for c, d in enumerate(dvn)]
        dv = _ln_bwd(vh, rstd, dvh, S)
        dhn = None
        for c in range(nc):
            dz = u_scr[:, c * E:(c + 1) * E] if c < nc // 2 else dv[c - nc // 2]
            dzp = dz.astype(BF16) * gg_scr[c]
            dzp_ref[c] = dzp
            t = _dot_nt(dzp, win_ref[c])
            dhn = t if dhn is None else dhn + t
        xh, rx = _rms_stats(x_ref[...])
        hn_ref[...] = (xh * gpre_ref[...]).astype(BF16)
        dxn, dgq = _rms_bwd(xh, rx, gpre_ref[...], dhn)
        dx_ref[...] = dy + dxn
        _acc_out(dgpre_ref, first, dgq)

    row = pl.BlockSpec((tm, D), lambda i: (i, 0))

    def whole(shape):
        return pl.BlockSpec(shape, lambda i: (0,) * len(shape))

    return pl.pallas_call(
        body,
        name="sgu_bwd",
        grid=(T // tm,),
        in_specs=[row, row, row, pl.BlockSpec((nc, tm, E), lambda i: (0, i, 0)), _resident((1, D)), _resident((1, D)),
                  _chunks_spec(win, sel), _resident((1, S)), _resident((1, S)), _resident(wsm.shape),
                  _resident(wsmt.shape), _resident(bsb.shape), _chunks_spec(wout, sel)],
        out_specs=[row, row, pl.BlockSpec((nc, tm, E), lambda i: (0, i, 0)),
                   pl.BlockSpec((nc, tm, wo_rows), lambda i: (0, i, 0)), row,
                   whole((N_GROUPS, CHUNK, CHUNK)), whole((CHUNK, S)), whole((1, S)), whole((1, S)),
                   whole((1, D)), whole((1, D))],
        out_shape=[
            jax.ShapeDtypeStruct((T, D), F32),
            jax.ShapeDtypeStruct((T, D), BF16),
            jax.ShapeDtypeStruct((nc, T, E), BF16),
            jax.ShapeDtypeStruct((nc, T, wo_rows), BF16),
            jax.ShapeDtypeStruct((T, D), BF16),
            jax.ShapeDtypeStruct((N_GROUPS, CHUNK, CHUNK), F32),
            jax.ShapeDtypeStruct((CHUNK, S), F32),
            jax.ShapeDtypeStruct((1, S), F32),
            jax.ShapeDtypeStruct((1, S), F32),
            jax.ShapeDtypeStruct((1, D), F32),
            jax.ShapeDtypeStruct((1, D), F32),
        ],
        scratch_shapes=[pltpu.VMEM((tm, S), F32), pltpu.VMEM((tm, S), F32), pltpu.VMEM((tm, S), F32),
                        pltpu.VMEM((tm, S), BF16), pltpu.VMEM((nc, tm, E), BF16)],
        compiler_params=_cparams("arbitrary"),
    )(dy, x, m, zp, g_pre, g_post, win, lng, lnb, wsm, wsmt, bsb, wout)


def _shifted_windows(buf, r0, cols, lo, hi):
    n = CONV_RB + HALO
    base = buf[r0:r0 + n, cols]
    for r in range(8):
        rolled = base if r == 0 else pltpu.roll(base, n - r, axis=0)
        for s in range(r, hi, 8):
            if s >= lo:
                yield s, rolled[s - r:s - r + CONV_RB]


def _conv_fwd_a(x, g_pre, wpw1, sel, tm):
    T, D = x.shape
    nc, E = wpw1.shape[0], wpw1.shape[-1]
    C = 2 * E
    tm = min(tm, T)

    def body(x_ref, gpre_ref, w_ref, y_ref, p_ref):
        xh, _ = _rms_stats(x_ref[...])
        hn = (xh * gpre_ref[...]).astype(BF16)
        ps = []
        for c in range(nc):
            p = _dot(hn, w_ref[c])
            p_ref[c] = p.astype(BF16)
            ps.append(p)
        for c in range(nc // 2):
            y_ref[:, c * E:(c + 1) * E] = ps[c] * jax.nn.sigmoid(ps[c + nc // 2])

    row = pl.BlockSpec((tm, D), lambda i: (i, 0))
    return pl.pallas_call(
        body,
        name="conv_fwd_a",
        grid=(T // tm,),
        in_specs=[row, _resident((1, D)), _chunks_spec(wpw1, sel)],
        out_specs=[pl.BlockSpec((tm, C), lambda i: (i, 0)), pl.BlockSpec((nc, tm, E), lambda i: (0, i, 0))],
        out_shape=[jax.ShapeDtypeStruct((T, C), F32), jax.ShapeDtypeStruct((nc, T, E), BF16)],
        compiler_params=_cparams("parallel"),
    )(x, g_pre, wpw1)


def _conv_fwd_b(x, y, wdw, bdw, lng, lnb, wpw2, g_post, sel, tm):
    T, D = x.shape
    C = y.shape[1]
    nc, E = wpw2.shape[0], wpw2.shape[-2]
    tm = min(tm, T)
    per = tm // HALO

    def body(x_ref, y_ref, yprev_ref, wdw_ref, bdw_ref, lng_ref, lnb_ref, w_ref, gpost_ref,
             xo_ref, c_ref, m_ref, ybuf):
        i = pl.program_id(0)
        ybuf[0:HALO, :] = jnp.where(i > 0, yprev_ref[...], 0.0)
        ybuf[HALO:HALO + tm, :] = y_ref[...]
        off = HALO - (CONV_W - 1)
        for r0 in range(0, tm, CONV_RB):
            for c0 in range(0, C, CONV_CB):
                cols = slice(c0, c0 + CONV_CB)
                acc = jnp.broadcast_to(bdw_ref[:, cols], (CONV_RB, CONV_CB))
                for s, win in _shifted_windows(ybuf, r0, cols, off, off + CONV_W):
                    acc = acc + wdw_ref[s - off:s - off + 1, cols] * win
                c_ref[r0:r0 + CONV_RB, cols] = acc
        (ch,), _ = _ln_stats([c_ref[...]], C)
        cn = ch * lng_ref[...] + lnb_ref[...]
        qv = (cn * jax.nn.sigmoid(cn)).astype(BF16)
        m = _dot(qv[:, 0:E], w_ref[0])
        for c in range(1, nc):
            m += _dot(qv[:, c * E:(c + 1) * E], w_ref[c])
        m_ref[...] = m
        mh, _ = _rms_stats(m)
        xo_ref[...] = x_ref[...] + mh * gpost_ref[...]

    row = pl.BlockSpec((tm, D), lambda i: (i, 0))
    crow = pl.BlockSpec((tm, C), lambda i: (i, 0))
    prev = pl.BlockSpec((HALO, C), lambda i: (jnp.maximum(i * per - 1, 0), 0))
    return pl.pallas_call(
        body,
        name="conv_fwd_b",
        grid=(T // tm,),
        in_specs=[row, crow, prev, _resident(wdw.shape), _resident((1, C)), _resident((1, C)), _resident((1, C)),
                  _chunks_spec(wpw2, sel), _resident((1, D))],
        out_specs=[row, crow, row],
        out_shape=[jax.ShapeDtypeStruct((T, D), F32), jax.ShapeDtypeStruct((T, C), F32),
                   jax.ShapeDtypeStruct((T, D), F32)],
        scratch_shapes=[pltpu.VMEM((HALO + tm, C), F32)],
        compiler_params=_cparams("parallel"),
    )(x, y, y, wdw, bdw, lng, lnb, wpw2, g_post)


def _conv_bwd_b(dy, m, c, lng, lnb, wpw2, g_post, sel, tm):
    T, D = dy.shape
    C = c.shape[1]
    nc, E = wpw2.shape[0], wpw2.shape[-2]
    tm = min(tm, T)

    def body(dy_ref, m_ref, c_ref, lng_ref, lnb_ref, w_ref, gpost_ref,
             dm_ref, q_ref, dc_ref, dlng_ref, dlnb_ref, dbdw_ref, dgpost_ref, dq_scr):
        first = pl.program_id(0) == 0
        mh, rm = _rms_stats(m_ref[...])
        dm, dgp = _rms_bwd(mh, rm, gpost_ref[...], dy_ref[...])
        _acc_out(dgpost_ref, first, dgp)
        dm = dm.astype(BF16)
        dm_ref[...] = dm
        for k in range(nc):
            dq_scr[:, k * E:(k + 1) * E] = _dot_nt(dm, w_ref[k])
        (ch,), rstd = _ln_stats([c_ref[...]], C)
        cn = ch * lng_ref[...] + lnb_ref[...]
        sg = jax.nn.sigmoid(cn)
        qv = (cn * sg).astype(BF16)
        for k in range(nc):
            q_ref[k] = qv[:, k * E:(k + 1) * E]
        dcn = dq_scr[...] * (sg * (1.0 + cn * (1.0 - sg)))
        _acc_out(dlng_ref, first, jnp.sum(dcn * ch, axis=0, keepdims=True))
        _acc_out(dlnb_ref, first, jnp.sum(dcn, axis=0, keepdims=True))
        (dc,) = _ln_bwd([ch], rstd, [dcn * lng_ref[...]], C)
        dc_ref[...] = dc
        _acc_out(dbdw_ref, first, jnp.sum(dc, axis=0, keepdims=True))

    row = pl.BlockSpec((tm, D), lambda i: (i, 0))
    crow = pl.BlockSpec((tm, C), lambda i: (i, 0))

    def whole(shape):
        return pl.BlockSpec(shape, lambda i: (0,) * len(shape))

    return pl.pallas_call(
        body,
        name="conv_bwd_b",
        grid=(T // tm,),
        in_specs=[row, row, crow, _resident((1, C)), _resident((1, C)), _chunks_spec(wpw2, sel), _resident((1, D))],
        out_specs=[row, pl.BlockSpec((nc, tm, E), lambda i: (0, i, 0)), crow, whole((1, C)), whole((1, C)),
                   whole((1, C)), whole((1, D))],
        out_shape=[jax.ShapeDtypeStruct((T, D), BF16), jax.ShapeDtypeStruct((nc, T, E), BF16),
                   jax.ShapeDtypeStruct((T, C), F32), jax.ShapeDtypeStruct((1, C), F32),
                   jax.ShapeDtypeStruct((1, C), F32), jax.ShapeDtypeStruct((1, C), F32),
                   jax.ShapeDtypeStruct((1, D), F32)],
        scratch_shapes=[pltpu.VMEM((tm, C), F32)],
        compiler_params=_cparams("arbitrary"),
    )(dy, m, c, lng, lnb, wpw2, g_post)


def _conv_bwd_a(dy, x, dc, y, p, g_pre, wdw, wpw1, sel, tm):
    T, D = x.shape
    C = y.shape[1]
    nc, E = wpw1.shape[0], wpw1.shape[-1]
    tm = min(tm, T)
    per = tm // HALO
    n_tiles = T // tm
    KP = wdw.shape[0]

    def body(dy_ref, x_ref, dc_ref, dcnext_ref, y_ref, yprev_ref, p_ref, gpre_ref, wdw_ref, w_ref,
             dx_ref, hn_ref, dp_ref, dwdw_ref, dgpre_ref, ybuf, dcbuf, dyg_scr, dw8_scr):
        i = pl.program_id(0)
        first = i == 0
        ybuf[0:HALO, :] = jnp.where(i > 0, yprev_ref[...], 0.0)
        ybuf[HALO:HALO + tm, :] = y_ref[...]
        dcbuf[0:tm, :] = dc_ref[...]
        dcbuf[tm:tm + HALO, :] = jnp.where(i < n_tiles - 1, dcnext_ref[...], 0.0)
        off = HALO - (CONV_W - 1)
        @pl.when(first)
        def _():
            dw8_scr[...] = jnp.zeros_like(dw8_scr)

        for r0 in range(0, tm, CONV_RB):
            for c0 in range(0, C, CONV_CB):
                cols = slice(c0, c0 + CONV_CB)
                dcb = dcbuf[r0:r0 + CONV_RB, cols]
                acc = jnp.zeros((CONV_RB, CONV_CB), F32)
                for s, win in _shifted_windows(dcbuf, r0, cols, 0, CONV_W):
                    k = CONV_W - 1 - s
                    acc = acc + wdw_ref[k:k + 1, cols] * win
                dyg_scr[r0:r0 + CONV_RB, cols] = acc
                for s, win in _shifted_windows(ybuf, r0, cols, off, off + CONV_W):
                    dw8_scr[s - off, :, cols] += jnp.sum((dcb * win).reshape(CONV_RB // 8, 8, CONV_CB), axis=0)

        @pl.when(i == n_tiles - 1)
        def _():
            dwdw_ref[...] = jnp.sum(dw8_scr[...], axis=1)

        dhn = None
        for c in range(nc // 2):
            cols = slice(c * E, (c + 1) * E)
            av = p_ref[c].astype(F32)
            sg = jax.nn.sigmoid(p_ref[c + nc // 2].astype(F32))
            dygc = dyg_scr[:, cols]
            da = (dygc * sg).astype(BF16)
            dgt = (dygc * av * sg * (1.0 - sg)).astype(BF16)
            dp_ref[c] = da
            dp_ref[c + nc // 2] = dgt
            t = _dot_nt(da, w_ref[c]) + _dot_nt(dgt, w_ref[c + nc // 2])
            dhn = t if dhn is None else dhn + t
        xh, rx = _rms_stats(x_ref[...])
        hn_ref[...] = (xh * gpre_ref[...]).astype(BF16)
        dxn, dgq = _rms_bwd(xh, rx, gpre_ref[...], dhn)
        dx_ref[...] = dy_ref[...] + dxn
        _acc_out(dgpre_ref, first, dgq)

    row = pl.BlockSpec((tm, D), lambda i: (i, 0))
    crow = pl.BlockSpec((tm, C), lambda i: (i, 0))
    prev = pl.BlockSpec((HALO, C), lambda i: (jnp.maximum(i * per - 1, 0), 0))
    nxt = pl.BlockSpec((HALO, C), lambda i: (jnp.minimum((i + 1) * per, T // HALO - 1), 0))
    chunks = pl.BlockSpec((nc, tm, E), lambda i: (0, i, 0))

    def whole(shape):
        return pl.BlockSpec(shape, lambda i: (0,) * len(shape))

    return pl.pallas_call(
        body,
        name="conv_bwd_a",
        grid=(n_tiles,),
        in_specs=[row, row, crow, nxt, crow, prev, chunks, _resident((1, D)), _resident(wdw.shape),
                  _chunks_spec(wpw1, sel)],
        out_specs=[row, row, chunks, whole((KP, C)), whole((1, D))],
        out_shape=[jax.ShapeDtypeStruct((T, D), F32), jax.ShapeDtypeStruct((T, D), BF16),
                   jax.ShapeDtypeStruct((nc, T, E), BF16), jax.ShapeDtypeStruct((KP, C), F32),
                   jax.ShapeDtypeStruct((1, D), F32)],
        scratch_shapes=[pltpu.VMEM((HALO + tm, C), F32), pltpu.VMEM((tm + HALO, C), F32),
                        pltpu.VMEM((tm, C), F32), pltpu.VMEM((KP, 8, C), F32)],
        compiler_params=_cparams("arbitrary"),
    )(dy, x, dc, dc, y, y, p, g_pre, wdw, wpw1)


def _loss_head(y, target, tm):
    T, D = y.shape
    tm = min(tm, T)

    def body(y_ref, t_ref, dy_ref, loss_ref):
        e = y_ref[...] - t_ref[...]
        dy_ref[...] = e * (1.0 / D)
        part = jnp.sum(jnp.sum(e * e, axis=-1, keepdims=True), axis=0, keepdims=True) * (0.5 / D)
        _acc_out(loss_ref, pl.program_id(0) == 0, jnp.broadcast_to(part, loss_ref.shape))

    row = pl.BlockSpec((tm, D), lambda i: (i, 0))
    return pl.pallas_call(
        body,
        name="loss_head",
        grid=(T // tm,),
        in_specs=[row, row],
        out_specs=[row, pl.BlockSpec((8, 128), lambda i: (0, 0))],
        out_shape=[jax.ShapeDtypeStruct((T, D), F32), jax.ShapeDtypeStruct((8, 128), F32)],
        compiler_params=_cparams("arbitrary"),
    )(y, target)


def _row_tile(rows, cols, itemsize_budget=2 * 1024 * 1024):
    want = max(16, itemsize_budget // (4 * cols))
    if rows <= want:
        return rows
    t = (want // 16) * 16
    while t > 16 and rows % t:
        t -= 16
    return t if rows % t == 0 else rows


def _sum_parts(parts):
    n, R, C = parts.shape
    tr = _row_tile(R, C * n // 2 if parts.dtype == BF16 else C * n)

    def body(p_ref, o_ref):
        acc = p_ref[0].astype(F32)
        for s in range(1, n):
            acc = acc + p_ref[s].astype(F32)
        o_ref[...] = acc

    return pl.pallas_call(
        body,
        name="sum_parts",
        grid=(R // tr,),
        in_specs=[pl.BlockSpec((n, tr, C), lambda i: (0, i, 0))],
        out_specs=pl.BlockSpec((tr, C), lambda i: (i, 0)),
        out_shape=jax.ShapeDtypeStruct((R, C), F32),
        compiler_params=_cparams("parallel"),
    )(parts)


def _cast_into_slot(w):
    _, R, C = w.shape
    tr = _row_tile(R, C)

    def body(w_ref, o_ref):
        o_ref[...] = w_ref[...].astype(BF16)

    def own_slot(h, i):
        return 2 * lax.axis_index("x") + lax.axis_index("y"), h, i, 0

    return pl.pallas_call(
        body,
        name="cast_into_slot",
        grid=(2, R // tr),
        in_specs=[pl.BlockSpec((None, tr, C), lambda h, i: (h, i, 0))],
        out_specs=pl.BlockSpec((None, None, tr, C), own_slot),
        out_shape=jax.ShapeDtypeStruct((N_CHIPS, 2, R, C), BF16),
        compiler_params=_cparams("parallel", "parallel"),
    )(w)


def _sum_with_own(arrived, own):
    n, R, C = arrived.shape
    tr = _row_tile(R, C * (n + 1) // 2)

    def body(a_ref, own_ref, o_ref):
        acc = own_ref[...].astype(F32)
        for s in range(n):
            acc = acc + a_ref[s].astype(F32)
        o_ref[...] = acc

    def own_piece(i):
        return 2 * lax.axis_index("x") + lax.axis_index("y"), lax.axis_index("c"), i, 0

    return pl.pallas_call(
        body,
        name="sum_with_own",
        grid=(R // tr,),
        in_specs=[pl.BlockSpec((n, tr, C), lambda i: (0, i, 0)), pl.BlockSpec((None, None, tr, C), own_piece)],
        out_specs=pl.BlockSpec((None, tr, C), lambda i: (lax.axis_index("c"), i, 0)),
        out_shape=jax.ShapeDtypeStruct((2, R, C), F32),
        compiler_params=_cparams("parallel"),
    )(arrived, own)


def _adamw(w, g, m, v):
    R, C = w.shape
    tr = _row_tile(R, C * 7 // 2)
    c1 = 1.0 - ADAM_B1 ** ADAM_STEP
    c2 = 1.0 - ADAM_B2 ** ADAM_STEP

    def body(w_ref, g_ref, m_ref, v_ref, d_ref, mo_ref, vo_ref):
        g = g_ref[...]
        mn = ADAM_B1 * m_ref[...] + (1.0 - ADAM_B1) * g
        vn = ADAM_B2 * v_ref[...] + (1.0 - ADAM_B2) * (g * g)
        mo_ref[...] = mn
        vo_ref[...] = vn
        d_ref[...] = -ADAM_LR * ((mn / c1) / (jnp.sqrt(vn / c2) + ADAM_EPS) + ADAM_WD * w_ref[...])

    blk = pl.BlockSpec((tr, C), lambda i: (i, 0))
    shp = jax.ShapeDtypeStruct((R, C), F32)
    return pl.pallas_call(
        body,
        name="adamw",
        grid=(R // tr,),
        in_specs=[blk, blk, blk, blk],
        out_specs=[blk, blk, blk],
        out_shape=[shp, shp, shp],
        compiler_params=_cparams("parallel"),
    )(w, g, m, v)


def _adamw_into(w, g, m, v, outs, sel):
    n, R, C = w.shape
    tr = _row_tile(R, C * 4)
    c1 = 1.0 - ADAM_B1 ** ADAM_STEP
    c2 = 1.0 - ADAM_B2 ** ADAM_STEP

    def body(w_ref, g_ref, m_ref, v_ref, *rest):
        go_ref, d_ref, mo_ref, vo_ref = rest[-4:]
        g = g_ref[...]
        mn = ADAM_B1 * m_ref[...] + (1.0 - ADAM_B1) * g
        vn = ADAM_B2 * v_ref[...] + (1.0 - ADAM_B2) * (g * g)
        go_ref[...] = g
        mo_ref[...] = mn
        vo_ref[...] = vn
        d_ref[...] = -ADAM_LR * ((mn / c1) / (jnp.sqrt(vn / c2) + ADAM_EPS) + ADAM_WD * w_ref[...])

    entry = pl.BlockSpec((None, tr, C), lambda i: (sel, i, 0))
    hbm = pl.BlockSpec(memory_space=pl.ANY)
    have = outs is not None
    shp = jax.ShapeDtypeStruct((n, R, C), F32)
    return pl.pallas_call(
        body,
        name="adamw_into",
        grid=(R // tr,),
        in_specs=[entry, pl.BlockSpec((tr, C), lambda i: (i, 0)), entry, entry] + ([hbm] * 4 if have else []),
        out_specs=[entry] * 4,
        out_shape=[shp] * 4,
        input_output_aliases={4 + t: t for t in range(4)} if have else {},
        compiler_params=_cparams("parallel"),
    )(w, g, m, v, *(outs if have else ()))


def _gather_weights(halved, whole):
    nh, nw = len(halved), len(whole)

    def body(*refs):
        w_in = refs[nh:nh + nw]
        h_out, w_out = refs[nh + nw:2 * nh + nw], refs[2 * nh + nw:2 * (nh + nw)]
        ws_send, ws_recv, loc_sem = refs[2 * (nh + nw):2 * (nh + nw) + 3]
        plan = _Gather(h_out, refs[2 * (nh + nw) + 3:])
        x, y, c = _my_place()
        me_chip = 2 * x + y
        plan.start()

        def small(a, j, slot, to):
            return pltpu.make_async_remote_copy(src_ref=w_in[a], dst_ref=w_out[a].at[slot],
                                                send_sem=ws_send.at[a, j], recv_sem=ws_recv.at[a, j],
                                                device_id=to, device_id_type=MESH)

        for a in range(nw):
            pltpu.make_async_copy(w_in[a], w_out[a].at[me_chip], loc_sem.at[a]).start()
            for j, ch in enumerate(plan.chips):
                small(a, j, me_chip, (*ch, c)).start()
        plan.forward()
        plan.finish()
        for a in range(nw):
            for j, ch in enumerate(plan.chips):
                cp = small(a, j, 2 * ch[0] + ch[1], (x, y, c))
                cp.wait_recv()
                cp.wait_send()
            pltpu.make_async_copy(w_in[a], w_out[a].at[me_chip], loc_sem.at[a]).wait()

    hbm = pl.BlockSpec(memory_space=pl.ANY)
    outs = pl.pallas_call(
        body,
        name="gather_weights",
        in_specs=[hbm] * (nh + nw),
        out_specs=[hbm] * (nh + nw),
        out_shape=[jax.ShapeDtypeStruct(a.shape, a.dtype) for a in halved]
        + [jax.ShapeDtypeStruct((N_CHIPS, *a.shape), a.dtype) for a in whole],
        input_output_aliases={a: a for a in range(nh)},
        scratch_shapes=[pltpu.SemaphoreType.DMA((max(nw, 1), 3)), pltpu.SemaphoreType.DMA((max(nw, 1), 3)),
                        pltpu.SemaphoreType.DMA((max(nw, 1),))] + _Gather.semaphores(nh),
    )(*halved, *whole)
    return outs[:nh], outs[nh:]


def _scatter_grads(grads, halves, into):
    n = len(grads)

    def body(*refs):
        plan = _Scatter(refs[:n], refs[2 * n:3 * n], refs[3 * n:], halves)
        plan.start()
        plan.finish()

    hbm = pl.BlockSpec(memory_space=pl.ANY)
    return pl.pallas_call(
        body,
        name="scatter_grads",
        in_specs=[hbm] * (2 * n),
        out_specs=[hbm] * n,
        out_shape=[jax.ShapeDtypeStruct(t.shape, t.dtype) for t in into],
        input_output_aliases={n + a: a for a in range(n)},
        scratch_shapes=_Scatter.semaphores(n),
    )(*grads, *into)


def _swap_halves(halves):
    n = len(halves)

    def body(*refs):
        h_out = refs[n:2 * n]
        send_sem, recv_sem = refs[2 * n:]
        x, y, c = _my_place()
        sib = (x, y, 1 - c)
        for a in range(n):
            pltpu.make_async_remote_copy(src_ref=h_out[a].at[c], dst_ref=h_out[a].at[c], send_sem=send_sem.at[a],
                                         recv_sem=recv_sem.at[a], device_id=sib, device_id_type=MESH).start()
        for a in range(n):
            cp = pltpu.make_async_remote_copy(src_ref=h_out[a].at[c], dst_ref=h_out[a].at[1 - c],
                                              send_sem=send_sem.at[a], recv_sem=recv_sem.at[a], device_id=sib,
                                              device_id_type=MESH)
            cp.wait_send()
            cp.wait_recv()

    hbm = pl.BlockSpec(memory_space=pl.ANY)
    return pl.pallas_call(
        body,
        name="swap_halves",
        in_specs=[hbm] * n,
        out_specs=[hbm] * n,
        out_shape=[jax.ShapeDtypeStruct(h.shape, h.dtype) for h in halves],
        input_output_aliases={a: a for a in range(n)},
        scratch_shapes=[pltpu.SemaphoreType.DMA((n,)), pltpu.SemaphoreType.DMA((n,))],
    )(*halves)


def _share_all(buf):
    def body(b_in, b_out, send_sem, recv_sem):
        x, y, c = _my_place()
        me = 4 * x + 2 * y + c
        for d in range(N_DEV):
            @pl.when(d != me)
            def _():
                pltpu.make_async_remote_copy(src_ref=b_out.at[me], dst_ref=b_out.at[me], send_sem=send_sem.at[d],
                                             recv_sem=recv_sem.at[me], device_id=(d // 4, (d // 2) % 2, d % 2),
                                             device_id_type=MESH).start()
        for d in range(N_DEV):
            @pl.when(d != me)
            def _():
                cp = pltpu.make_async_remote_copy(src_ref=b_out.at[me], dst_ref=b_out.at[d], send_sem=send_sem.at[d],
                                                  recv_sem=recv_sem.at[d], device_id=(x, y, c),
                                                  device_id_type=MESH)
                cp.wait_send()
                cp.wait_recv()

    hbm = pl.BlockSpec(memory_space=pl.ANY)
    return pl.pallas_call(
        body,
        name="share_all",
        in_specs=[hbm],
        out_specs=hbm,
        out_shape=jax.ShapeDtypeStruct(buf.shape, buf.dtype),
        input_output_aliases={0: 0},
        scratch_shapes=[pltpu.SemaphoreType.DMA((N_DEV,)), pltpu.SemaphoreType.DMA((N_DEV,))],
    )(buf)


TM_FFN = 512
TM_FFN_FWD = 1024
RB_FFN_FWD = 512
RB_FFN_BWD = 256
TM_SGU = 256
TM_CONV = 256
TK_WGRAD = 4096
TM_LOSS = 1024


FFN_KINDS = ("ff_w_gate", "ff_w_up", "ff_w_down")


def _layer_kinds(i):
    return FFN_KINDS + (("sgu_w_in", "sgu_w_out") if i % 2 == 0 else ("conv_w_pw1", "conv_w_pw2"))


def _local_step(x, target, G, W, exchange=None):
    depth = W["norm_g"].shape[0]
    G = [dict(g) for g in G]
    saved = []
    vec = lambda v: v.reshape(1, -1)

    def mixer_w(i, k):
        w = G[i][k]
        return w.reshape(w.shape[0], -1, w.shape[-1])
    wsm, wsmt, bsb = [], [], []
    n_sgu = W["sgu_w_spatial"].shape[0]
    causal = jnp.tril(jnp.ones((CHUNK, CHUNK), dtype=bool))
    dgrp = W["sgu_ln_g"].shape[1] // N_GROUPS
    for jx in range(n_sgu):
        ws = jnp.where(causal[None], W["sgu_w_spatial"][jx], 0.0).astype(BF16)
        wsm.append(ws)
        wsmt.append(jnp.swapaxes(ws, 1, 2))
        bsb.append(jnp.broadcast_to(W["sgu_b_spatial"][jx][:, :, None], (N_GROUPS, CHUNK, dgrp)))
    kp = HALO
    wdw = [jnp.pad(W["conv_w_dw"][jx], ((0, kp - CONV_W), (0, 0))) for jx in range(W["conv_w_dw"].shape[0])]

    def ffn(x, i, f_idx, gather=()):
        g = W["norm_g"][i]
        return _ffn_fwd(x, vec(g[4 * f_idx]), vec(g[4 * f_idx + 1]), G[i]["ff_w_gate"], G[i]["ff_w_up"],
                        G[i]["ff_w_down"], (f_idx,), TM_FFN_FWD, gather)

    for i in range(depth):
        g = W["norm_g"][i]
        rec = {"x0": x}
        if exchange is not None and i + 1 < depth:
            kinds = _layer_kinds(i + 1)
            x, rec["a1"], rec["b1"], rec["f1"], *filled = ffn(x, i, 0, [G[i + 1][k] for k in kinds])
            G[i + 1] = dict(zip(kinds, filled))
        else:
            x, rec["a1"], rec["b1"], rec["f1"] = ffn(x, i, 0)
        rec["x1"] = x
        j = i // 2
        if i % 2 == 0:
            x, rec["zp"], rec["m"] = _sgu_fwd(
                x, vec(g[2]), vec(g[3]), mixer_w(i, "sgu_w_in"), vec(W["sgu_ln_g"][j]), vec(W["sgu_ln_b"][j]),
                wsm[j], bsb[j], mixer_w(i, "sgu_w_out"), (), TM_SGU)
        else:
            rec["y"], rec["p"] = _conv_fwd_a(x, vec(g[2]), mixer_w(i, "conv_w_pw1"), (), TM_CONV)
            x, rec["c"], rec["m"] = _conv_fwd_b(
                x, rec["y"], wdw[j], vec(W["conv_b_dw"][j]), vec(W["conv_ln_g"][j]), vec(W["conv_ln_b"][j]),
                mixer_w(i, "conv_w_pw2"), vec(g[3]), (), TM_CONV)
        rec["x2"] = x
        x, rec["a2"], rec["b2"], rec["f2"] = ffn(x, i, 1)
        saved.append(rec)

    dx, loss_tile = _loss_head(x, target, TM_LOSS)
    loss = loss_tile[0, 0]

    big = [{k: None for k in _layer_kinds(i)} for i in range(depth)]
    small = {k: [None] * W[k].shape[0] for k in
             ("sgu_ln_g", "sgu_ln_b", "sgu_w_spatial", "sgu_b_spatial", "conv_w_dw", "conv_b_dw", "conv_ln_g",
              "conv_ln_b")}
    dnorm = [[None] * 6 for _ in range(depth)]
    pieces = [None] * depth
    waiting = []

    def wgrad(i, k, a, b, sel):
        like = G[i][k] if sel else mixer_w(i, k)
        big[i][k] = _tn_matmul(a, b, big[i][k], like, sel, TK_WGRAD)

    def as_pieces(b):
        return b.reshape(N_CHIPS, 2, -1, b.shape[-1])

    def ffn_back(dx, i, f_idx, xin, a, b, f, send):
        g = W["norm_g"][i]
        dx, h, dz, s, da, db, dgpre, dgpost, *arrived = _ffn_bwd(
            dx, xin, f, a, b, vec(g[4 * f_idx]), vec(g[4 * f_idx + 1]),
            G[i]["ff_w_gate"], G[i]["ff_w_up"], G[i]["ff_w_down"], (f_idx,), TM_FFN, list(send.values()))
        wgrad(i, "ff_w_gate", h, da, (f_idx,))
        wgrad(i, "ff_w_up", h, db, (f_idx,))
        wgrad(i, "ff_w_down", s, dz, (f_idx,))
        dnorm[i][4 * f_idx] = dgpre[0]
        dnorm[i][4 * f_idx + 1] = dgpost[0]
        return dx, dict(zip(send, arrived))

    for i in reversed(range(depth)):
        rec = saved[i]
        g = W["norm_g"][i]
        j = i // 2
        if exchange is not None and waiting:
            sent = waiting.pop()
            dx, arrived = ffn_back(dx, i, 1, rec["x2"], rec["a2"], rec["b2"], rec["f2"],
                                   {k: (p, (0, 1)) for k, p in pieces[sent].items()})
            exchange(sent, pieces[sent], arrived)
        else:
            dx, _ = ffn_back(dx, i, 1, rec["x2"], rec["a2"], rec["b2"], rec["f2"], {})
        if i % 2 == 0:
            (dx, hn, dzp, gated, dm, dws, dbs_acc, dlng, dlnb, dgpre, dgpost) = _sgu_bwd(
                dx, rec["x1"], rec["m"], rec["zp"], vec(g[2]), vec(g[3]), mixer_w(i, "sgu_w_in"),
                vec(W["sgu_ln_g"][j]), vec(W["sgu_ln_b"][j]), wsm[j], wsmt[j], bsb[j], mixer_w(i, "sgu_w_out"), (),
                TM_SGU)
            wgrad(i, "sgu_w_in", hn, dzp, ())
            wgrad(i, "sgu_w_out", gated, dm, ())
            small["sgu_w_spatial"][j] = jnp.where(causal[None], dws, 0.0)
            small["sgu_b_spatial"][j] = dbs_acc.reshape(CHUNK, N_GROUPS, dgrp).sum(-1).T
            small["sgu_ln_g"][j] = dlng[0]
            small["sgu_ln_b"][j] = dlnb[0]
        else:
            dm, q, dc, dlng, dlnb, dbdw, dgpost = _conv_bwd_b(
                dx, rec["m"], rec["c"], vec(W["conv_ln_g"][j]), vec(W["conv_ln_b"][j]), mixer_w(i, "conv_w_pw2"),
                vec(g[3]), (), TM_CONV)
            dx, hn, dp, dwdw, dgpre = _conv_bwd_a(
                dx, rec["x1"], dc, rec["y"], rec["p"], vec(g[2]), wdw[j], mixer_w(i, "conv_w_pw1"), (), TM_CONV)
            wgrad(i, "conv_w_pw1", hn, dp, ())
            wgrad(i, "conv_w_pw2", q, dm, ())
            small["conv_w_dw"][j] = dwdw[:CONV_W]
            small["conv_b_dw"][j] = dbdw[0]
            small["conv_ln_g"][j] = dlng[0]
            small["conv_ln_b"][j] = dlnb[0]
        dnorm[i][2] = dgpre[0]
        dnorm[i][3] = dgpost[0]
        if exchange is not None and i == 0:
            dx, arrived = ffn_back(dx, i, 0, rec["x0"], rec["a1"], rec["b1"], rec["f1"],
                                   {k: (as_pieces(b), (1,) if k in FFN_KINDS else (0, 1)) for k, b in big[i].items()})
            pieces[i] = {k: as_pieces(b) for k, b in big[i].items()}
            rest = _scatter_grads([pieces[i][k] for k in FFN_KINDS], [(0,)] * len(FFN_KINDS),
                                  [arrived[k] for k in FFN_KINDS])
            arrived.update(zip(FFN_KINDS, rest))
            exchange(i, pieces[i], arrived)
        else:
            dx, _ = ffn_back(dx, i, 0, rec["x0"], rec["a1"], rec["b1"], rec["f1"], {})
            pieces[i] = {k: as_pieces(b) for k, b in big[i].items()}
            waiting.append(i)

    small = {k: jnp.stack(v) for k, v in small.items()}
    small["norm_g"] = jnp.stack([jnp.stack(r) for r in dnorm])
    return loss, dx, pieces, small


BIG = ("ff_w_gate", "ff_w_up", "ff_w_down", "sgu_w_in", "sgu_w_out", "conv_w_pw1", "conv_w_pw2")
SHARDED_SMALL = ("norm_g", "conv_w_dw", "conv_b_dw", "conv_ln_g", "conv_ln_b")
REPLICATED = ("sgu_ln_g", "sgu_ln_b", "sgu_w_spatial", "sgu_b_spatial")
WEIGHTS = ("norm_g", "ff_w_gate", "ff_w_up", "ff_w_down", "sgu_w_in", "sgu_ln_g", "sgu_ln_b", "sgu_w_spatial",
           "sgu_b_spatial", "sgu_w_out", "conv_w_pw1", "conv_w_dw", "conv_b_dw", "conv_ln_g", "conv_ln_b",
           "conv_w_pw2")


def _rows8(a, width):
    r = a.reshape(-1, width)
    pad = (-r.shape[0]) % 8
    return jnp.pad(r, ((0, pad), (0, 0))) if pad else r


def _pack(arrs, width):
    parts = [_rows8(a, width) for a in arrs]
    return jnp.concatenate(parts, axis=0), [p.shape[0] for p in parts]


def _unpack(buf, like):
    out, r0 = [], 0
    width = buf.shape[-1]
    for a in like:
        n = -(-(a.size // width) // 8) * 8
        rows = a.size // width
        out.append(buf[..., r0:r0 + rows, :].reshape(*buf.shape[:-2], *a.shape))
        r0 += n
    return out


def kernel(x, norm_g, ff_w_gate, ff_w_up, ff_w_down, sgu_w_in, sgu_ln_g, sgu_ln_b, sgu_w_spatial, sgu_b_spatial, sgu_w_out, conv_w_pw1, conv_w_dw, conv_b_dw, conv_ln_g, conv_ln_b, conv_w_pw2, loss_target, m_norm_g, m_ff_w_gate, m_ff_w_up, m_ff_w_down, m_sgu_w_in, m_sgu_ln_g, m_sgu_ln_b, m_sgu_w_spatial, m_sgu_b_spatial, m_sgu_w_out, m_conv_w_pw1, m_conv_w_dw, m_conv_b_dw, m_conv_ln_g, m_conv_ln_b, m_conv_w_pw2, v_norm_g, v_ff_w_gate, v_ff_w_up, v_ff_w_down, v_sgu_w_in, v_sgu_ln_g, v_sgu_ln_b, v_sgu_w_spatial, v_sgu_b_spatial, v_sgu_w_out, v_conv_w_pw1, v_conv_w_dw, v_conv_b_dw, v_conv_ln_g, v_conv_ln_b, v_conv_w_pw2):
    w = dict(norm_g=norm_g, ff_w_gate=ff_w_gate, ff_w_up=ff_w_up, ff_w_down=ff_w_down, sgu_w_in=sgu_w_in,
             sgu_ln_g=sgu_ln_g, sgu_ln_b=sgu_ln_b, sgu_w_spatial=sgu_w_spatial, sgu_b_spatial=sgu_b_spatial,
             sgu_w_out=sgu_w_out, conv_w_pw1=conv_w_pw1, conv_w_dw=conv_w_dw, conv_b_dw=conv_b_dw,
             conv_ln_g=conv_ln_g, conv_ln_b=conv_ln_b, conv_w_pw2=conv_w_pw2)
    mom = dict(norm_g=m_norm_g, ff_w_gate=m_ff_w_gate, ff_w_up=m_ff_w_up, ff_w_down=m_ff_w_down,
               sgu_w_in=m_sgu_w_in, sgu_ln_g=m_sgu_ln_g, sgu_ln_b=m_sgu_ln_b, sgu_w_spatial=m_sgu_w_spatial,
               sgu_b_spatial=m_sgu_b_spatial, sgu_w_out=m_sgu_w_out, conv_w_pw1=m_conv_w_pw1,
               conv_w_dw=m_conv_w_dw, conv_b_dw=m_conv_b_dw, conv_ln_g=m_conv_ln_g, conv_ln_b=m_conv_ln_b,
               conv_w_pw2=m_conv_w_pw2)
    vel = dict(norm_g=v_norm_g, ff_w_gate=v_ff_w_gate, ff_w_up=v_ff_w_up, ff_w_down=v_ff_w_down,
               sgu_w_in=v_sgu_w_in, sgu_ln_g=v_sgu_ln_g, sgu_ln_b=v_sgu_ln_b, sgu_w_spatial=v_sgu_w_spatial,
               sgu_b_spatial=v_sgu_b_spatial, sgu_w_out=v_sgu_w_out, conv_w_pw1=v_conv_w_pw1,
               conv_w_dw=v_conv_w_dw, conv_b_dw=v_conv_b_dw, conv_ln_g=v_conv_ln_g, conv_ln_b=v_conv_ln_b,
               conv_w_pw2=v_conv_w_pw2)
    T, D = x.shape[1], x.shape[2]
    shard_w = conv_b_dw.shape[1]

    xi, yi, ci = _my_place()
    me_chip = (2 * xi + yi).astype(jnp.int32)
    me = (4 * xi + 2 * yi + ci).astype(jnp.int32)

    depth = norm_g.shape[0]

    def entry(k, i):
        return i if k in FFN_KINDS else i // 2

    def stacked(a):
        return a.reshape(a.shape[0], -1, a.shape[-1])

    G = []
    for i in range(depth):
        G.append({k: _cast_into_slot(w[k][entry(k, i)].reshape(2, -1, w[k].shape[-1])) for k in _layer_kinds(i)})
    small_buf, _ = _pack([w[k] for k in SHARDED_SMALL], shard_w)
    first, (small_all,) = _gather_weights(list(G[0].values()), [small_buf])
    G[0] = dict(zip(G[0], first))
    W = {}
    for k, part in zip(SHARDED_SMALL, _unpack(small_all, [w[k] for k in SHARDED_SMALL])):
        W[k] = jnp.moveaxis(part, 0, -2).reshape(*w[k].shape[:-1], N_CHIPS * shard_w)
    for k in REPLICATED:
        W[k] = w[k]

    results = {k: None for k in BIG}

    def reduce_and_update(i, pieces, arrived):
        kinds = list(pieces)
        both = _swap_halves([_sum_with_own(arrived[k], pieces[k]) for k in kinds])
        for k, g in zip(kinds, both):
            results[k] = _adamw_into(stacked(w[k]), g.reshape(-1, g.shape[-1]), stacked(mom[k]), stacked(vel[k]),
                                     results[k], entry(k, i))

    loss, dx, _, small = _local_step(x[0], loss_target[0], G, W, reduce_and_update)
    loss = lax.psum(loss, ("x", "y", "c"))
    grads, delta, new_m, new_v = {}, {}, {}, {}
    for k in BIG:
        grads[k], delta[k], new_m[k], new_v[k] = (t.reshape(w[k].shape) for t in results[k])

    sbuf, _ = _pack([small[k] for k in SHARDED_SMALL + REPLICATED], D)
    slots = lax.dynamic_update_slice(jnp.zeros((N_DEV, *sbuf.shape), F32), sbuf[None], (me, 0, 0))
    ssum = _sum_parts(_share_all(slots))
    for k, gfull in zip(SHARDED_SMALL + REPLICATED, _unpack(ssum, [small[k] for k in SHARDED_SMALL + REPLICATED])):
        if k in SHARDED_SMALL:
            gfull = lax.dynamic_slice_in_dim(gfull, me_chip * shard_w, shard_w, axis=gfull.ndim - 1)
        grads[k] = gfull

    for names, width in ((SHARDED_SMALL, shard_w), (REPLICATED, CHUNK)):
        packed = [_pack([src[k] for k in names], width)[0] for src in (w, grads, mom, vel)]
        outs = _adamw(*packed)
        for res, out in zip((delta, new_m, new_v), outs):
            for k, a in zip(names, _unpack(out, [w[k] for k in names])):
                res[k] = a

    return (loss, dx[None], *[grads[k] for k in WEIGHTS], *[delta[k] for k in WEIGHTS],
            *[new_m[k] for k in WEIGHTS], *[new_v[k] for k in WEIGHTS])
```

```python
import functools

import jax
import jax.numpy as jnp
from jax import lax
from jax.experimental import pallas as pl
from jax.experimental.pallas import tpu as pltpu

F32 = jnp.float32
BF16 = jnp.bfloat16
EPS = 1e-6
N_CHIPS = 4
N_DEV = 8
N_GROUPS = 8
CHUNK = 128
CONV_W = 31
HALO = 32
CONV_RB = 64
CONV_CB = 256
VMEM_LIMIT_V7X = 60 * 1024 * 1024
MESH = pl.DeviceIdType.MESH

ADAM_LR = 0.001
ADAM_B1 = 0.9
ADAM_B2 = 0.999
ADAM_EPS = 1e-08
ADAM_WD = 0.01
ADAM_STEP = 10
FFN_SCALE = 0.5


def _cparams(*sem, **kw):
    return pltpu.CompilerParams(dimension_semantics=sem, vmem_limit_bytes=VMEM_LIMIT_V7X, **kw)


def _resident(shape):
    return pl.BlockSpec(shape, lambda *_: (0,) * len(shape), pipeline_mode=pl.Buffered(1))


def _dot(a, b):
    return jnp.dot(a, b, preferred_element_type=F32)


def _dot_nt(a, b):
    return lax.dot_general(a, b, (((1,), (1,)), ((), ())), preferred_element_type=F32)


def _dot_tn(a, b):
    return lax.dot_general(a, b, (((0,), (0,)), ((), ())), preferred_element_type=F32)


def _rms_stats(x):
    r = lax.rsqrt(jnp.mean(x * x, axis=-1, keepdims=True) + EPS)
    return x * r, r


def _rms_bwd(xh, r, g, dy):
    dxh = dy * g
    dx = r * (dxh - xh * jnp.mean(dxh * xh, axis=-1, keepdims=True))
    return dx, jnp.sum(dy * xh, axis=0, keepdims=True)


def _ln_stats(parts, width):
    mu = sum(jnp.sum(p, axis=-1, keepdims=True) for p in parts) / width
    cen = [p - mu for p in parts]
    var = sum(jnp.sum(c * c, axis=-1, keepdims=True) for c in cen) / width
    rstd = lax.rsqrt(var + EPS)
    return [c * rstd for c in cen], rstd


def _ln_bwd(vh_parts, rstd, dvh_parts, width):
    m1 = sum(jnp.sum(d, axis=-1, keepdims=True) for d in dvh_parts) / width
    m2 = sum(jnp.sum(d * v, axis=-1, keepdims=True) for d, v in zip(dvh_parts, vh_parts)) / width
    return [rstd * (d - m1 - v * m2) for d, v in zip(dvh_parts, vh_parts)]


_GELU_C = 0.7978845608028654
_GELU_A = 0.044715


def _gelu(x):
    return 0.5 * x * (1.0 + jnp.tanh(_GELU_C * (x + _GELU_A * x * x * x)))


def _gelu_pair(x):
    x2 = x * x
    t = jnp.tanh(_GELU_C * (x + _GELU_A * (x * x2)))
    half = 0.5 * (1.0 + t)
    return x * half, half + (0.5 * _GELU_C) * x * (1.0 - t * t) * (1.0 + (3.0 * _GELU_A) * x2)


def _sigmoid_pair(a):
    e = jnp.exp(jnp.minimum(-a, 80.0))
    sg = 1.0 / (1.0 + e)
    return sg, e * sg


def _acc_out(ref, first, val):
    @pl.when(first)
    def _():
        ref[...] = val

    @pl.when(jnp.logical_not(first))
    def _():
        ref[...] += val


def _my_place():
    return lax.axis_index("x"), lax.axis_index("y"), lax.axis_index("c")


class _Gather:
    def __init__(self, bufs, sems):
        self.bufs = bufs
        self.own_sems, self.fwd_sems = sems[:2], sems[2:4]
        self.x, self.y, self.c = _my_place()
        x, y = self.x, self.y
        self.chips = [(1 - x, y), (x, 1 - y), (1 - x, 1 - y)]

    def _copy(self, a, j, chip, half, to, sems):
        spot = self.bufs[a].at[2 * chip[0] + chip[1], pl.ds(half, 1)]
        return pltpu.make_async_remote_copy(src_ref=spot, dst_ref=spot, send_sem=sems[0].at[a, j],
                                            recv_sem=sems[1].at[a, j], device_id=to, device_id_type=MESH)

    def _own(self, a, j):
        return self._copy(a, j, (self.x, self.y), self.c, (*self.chips[j], self.c), self.own_sems)

    def _passed_on(self, a, j):
        return self._copy(a, j, self.chips[j], self.c, (self.x, self.y, 1 - self.c), self.fwd_sems)

    def start(self):
        for j in range(3):
            for a in range(len(self.bufs)):
                self._own(a, j).start()

    def forward(self):
        me = (self.x, self.y, self.c)
        for j in range(3):
            for a in range(len(self.bufs)):
                self._copy(a, j, self.chips[j], self.c, me, self.own_sems).wait_recv()
                self._passed_on(a, j).start()

    def finish(self):
        me = (self.x, self.y, self.c)
        for j in range(3):
            for a in range(len(self.bufs)):
                self._copy(a, j, self.chips[j], 1 - self.c, me, self.fwd_sems).wait_recv()
        for j in range(3):
            for a in range(len(self.bufs)):
                self._own(a, j).wait_send()
                self._passed_on(a, j).wait_send()

    @staticmethod
    def semaphores(n):
        return [pltpu.SemaphoreType.DMA((n, 3)) for _ in range(4)]


class _Scatter:
    def __init__(self, g_in, g_out, sems, halves):
        self.g_in, self.g_out = g_in, g_out
        self.send_sem, self.recv_sem = sems
        self.halves = halves
        x, y, c = _my_place()
        self.c = c
        self.me = 4 * x + 2 * y + c

    def _piece(self, a, d, slot):
        return pltpu.make_async_remote_copy(
            src_ref=self.g_in[a].at[d // 2, d % 2], dst_ref=self.g_out[a].at[slot],
            send_sem=self.send_sem.at[a, d], recv_sem=self.recv_sem.at[a, slot],
            device_id=(d // 4, (d // 2) % 2, d % 2), device_id_type=MESH)

    def _to(self, a):
        return [d for d in range(N_DEV) if d % 2 in self.halves[a]]

    def start(self):
        for a in range(len(self.g_in)):
            for d in self._to(a):
                @pl.when(d != self.me)
                def _():
                    self._piece(a, d, lax.rem(self.me - d - 1 + N_DEV, N_DEV)).start()

    def finish(self):
        for a in range(len(self.g_in)):
            for h in self.halves[a]:
                @pl.when(self.c == h)
                def _():
                    for slot in range(N_DEV - 1):
                        self._piece(a, 0, slot).wait_recv()
            for d in self._to(a):
                @pl.when(d != self.me)
                def _():
                    self._piece(a, d, 0).wait_send()

    @staticmethod
    def semaphores(n):
        return [pltpu.SemaphoreType.DMA((n, N_DEV)), pltpu.SemaphoreType.DMA((n, N_DEV - 1))]


class _ShareAll:
    def __init__(self, buf, sems):
        self.buf = buf
        self.send_sem, self.recv_sem = sems
        self.x, self.y, self.c = _my_place()
        self.me = 4 * self.x + 2 * self.y + self.c

    def start(self):
        mine = self.buf.at[self.me]
        for d in range(N_DEV):
            @pl.when(d != self.me)
            def _():
                pltpu.make_async_remote_copy(src_ref=mine, dst_ref=mine, send_sem=self.send_sem.at[d],
                                             recv_sem=self.recv_sem.at[self.me],
                                             device_id=(d // 4, (d // 2) % 2, d % 2), device_id_type=MESH).start()

    def finish(self):
        for d in range(N_DEV):
            @pl.when(d != self.me)
            def _():
                cp = pltpu.make_async_remote_copy(src_ref=self.buf.at[self.me], dst_ref=self.buf.at[d],
                                                  send_sem=self.send_sem.at[d], recv_sem=self.recv_sem.at[d],
                                                  device_id=(self.x, self.y, self.c), device_id_type=MESH)
                cp.wait_send()
                cp.wait_recv()

    @staticmethod
    def semaphores():
        return [pltpu.SemaphoreType.DMA((N_DEV,)), pltpu.SemaphoreType.DMA((N_DEV,))]


def _chunk_spec(sel, rows, cols):
    return pl.BlockSpec((None,) * (1 + len(sel)) + (rows, cols), lambda i, j: (j, *sel, 0, 0))


def _chunks_spec(w, sel):
    return pl.BlockSpec((w.shape[0],) + (None,) * len(sel) + w.shape[-2:], lambda *_: (0, *sel, 0, 0),
                        pipeline_mode=pl.Buffered(1))


def _ffn_fwd(x, g_pre, g_post, wg, wu, wd, sel, tm, gather=()):
    T, D = x.shape
    nj, F = wg.shape[0], wg.shape[-1]
    tm = min(tm, T)
    ni = T // tm
    rb = min(RB_FFN_FWD, tm)
    ng = len(gather)

    def body(*refs):
        x_ref, gpre_ref, gpost_ref, wg_ref, wu_ref, wd_ref = refs[:6]
        xo_ref, a_ref, b_ref, f_ref = refs[6 + ng:10 + ng]
        h_scr, acc_scr = refs[10 + 2 * ng:12 + 2 * ng]
        i = pl.program_id(0)
        j = pl.program_id(1)
        if ng:
            plan = _Gather(refs[10 + ng:10 + 2 * ng], refs[12 + 2 * ng:])
            pl.when(jnp.logical_and(i == 0, j == 0))(plan.start)
            pl.when(jnp.logical_and(i == (5 * ni) // 8, j == nj - 1))(plan.forward)

        @pl.when(j == 0)
        def _():
            xh, _ = _rms_stats(x_ref[...])
            h_scr[...] = (xh * gpre_ref[...]).astype(BF16)
            acc_scr[...] = jnp.zeros_like(acc_scr)

        for r0 in range(0, tm, rb):
            rows = slice(r0, r0 + rb)
            h = h_scr[rows, :]
            a = _dot(h, wg_ref[...]).astype(BF16)
            b = _dot(h, wu_ref[...]).astype(BF16)
            a_ref[rows, :] = a
            b_ref[rows, :] = b
            sg, _ = _sigmoid_pair(a)
            acc_scr[rows, :] += _dot((a * sg) * b, wd_ref[...])

        @pl.when(j == nj - 1)
        def _():
            f = acc_scr[...]
            f_ref[...] = f
            fh, _ = _rms_stats(f)
            xo_ref[...] = x_ref[...] + FFN_SCALE * (fh * gpost_ref[...])

        if ng:
            pl.when(jnp.logical_and(i == ni - 1, j == nj - 1))(plan.finish)

    row = pl.BlockSpec((tm, D), lambda i, j: (i, 0))
    vec = pl.BlockSpec((1, D), lambda i, j: (0, 0))
    w_in = _chunk_spec(sel, D, F)
    w_out = _chunk_spec(sel, F, D)
    act = pl.BlockSpec((None, tm, F), lambda i, j: (j, i, 0))
    hbm = pl.BlockSpec(memory_space=pl.ANY)
    return pl.pallas_call(
        body,
        name="ffn_fwd_gather" if ng else "ffn_fwd",
        grid=(ni, nj),
        in_specs=[row, vec, vec, w_in, w_in, w_out] + [hbm] * ng,
        out_specs=[row, act, act, row] + [hbm] * ng,
        out_shape=[
            jax.ShapeDtypeStruct((T, D), F32),
            jax.ShapeDtypeStruct((nj, T, F), BF16),
            jax.ShapeDtypeStruct((nj, T, F), BF16),
            jax.ShapeDtypeStruct((T, D), F32),
        ] + [jax.ShapeDtypeStruct(g.shape, g.dtype) for g in gather],
        input_output_aliases={6 + a: 4 + a for a in range(ng)},
        scratch_shapes=[pltpu.VMEM((tm, D), BF16), pltpu.VMEM((tm, D), F32)] + (_Gather.semaphores(ng) if ng else []),
        compiler_params=_cparams("arbitrary", "arbitrary"),
    )(x, g_pre, g_post, wg, wu, wd, *gather)


def _ffn_bwd(dy, x, f, a, b, g_pre, g_post, wg, wu, wd, sel, tm, scatter=()):
    T, D = x.shape
    nj, F = wg.shape[0], wg.shape[-1]
    tm = min(tm, T)
    ni = T // tm
    rb = min(RB_FFN_BWD, tm)
    ns = len(scatter)
    halves = [h for _, h in scatter]
    scatter = [g for g, _ in scatter]

    def body(*refs):
        dy_ref, x_ref, f_ref, a_ref, b_ref, gpre_ref, gpost_ref, wg_ref, wu_ref, wd_ref = refs[:10]
        dx_ref, h_ref, dz_ref, s_ref, da_ref, db_ref, dgpre_ref, dgpost_ref = refs[10 + ns:18 + ns]
        dh_scr = refs[18 + 2 * ns]
        i = pl.program_id(0)
        j = pl.program_id(1)
        if ns:
            plan = _Scatter(refs[10:10 + ns], refs[18 + ns:18 + 2 * ns], refs[19 + 2 * ns:], halves)
            pl.when(jnp.logical_and(i == 0, j == 0))(plan.start)

        @pl.when(j == 0)
        def _():
            fh, rf = _rms_stats(f_ref[...])
            dz, dg = _rms_bwd(fh, rf, gpost_ref[...], FFN_SCALE * dy_ref[...])
            dz_ref[...] = dz.astype(BF16)
            _acc_out(dgpost_ref, i == 0, dg)
            xh, _ = _rms_stats(x_ref[...])
            h_ref[...] = (xh * gpre_ref[...]).astype(BF16)
            dh_scr[...] = jnp.zeros_like(dh_scr)

        for r0 in range(0, tm, rb):
            rows = slice(r0, r0 + rb)
            ds = _dot_nt(dz_ref[rows, :], wd_ref[j]).astype(BF16)
            av = a_ref[rows, :]
            bv = b_ref[rows, :]
            sg, one_minus_sg = _sigmoid_pair(av)
            sl = av * sg
            s_ref[rows, :] = sl * bv
            da = (ds * bv) * (sg + sl * one_minus_sg)
            db = ds * sl
            da_ref[rows, :] = da
            db_ref[rows, :] = db
            dh_scr[rows, :] += _dot_nt(da, wg_ref[j]) + _dot_nt(db, wu_ref[j])

        @pl.when(j == nj - 1)
        def _():
            xh, rx = _rms_stats(x_ref[...])
            dxn, dg = _rms_bwd(xh, rx, gpre_ref[...], dh_scr[...])
            dx_ref[...] = dy_ref[...] + dxn
            _acc_out(dgpre_ref, i == 0, dg)

        if ns:
            pl.when(jnp.logical_and(i == ni - 1, j == nj - 1))(plan.finish)

    row = pl.BlockSpec((tm, D), lambda i, j: (i, 0))
    vec = pl.BlockSpec((1, D), lambda i, j: (0, 0))
    w_in = _chunks_spec(wg, sel)
    w_out = _chunks_spec(wd, sel)
    act = pl.BlockSpec((None, tm, F), lambda i, j: (j, i, 0))
    act_shape = jax.ShapeDtypeStruct((nj, T, F), BF16)
    hbm = pl.BlockSpec(memory_space=pl.ANY)
    return pl.pallas_call(
        body,
        name="ffn_bwd_scatter" if ns else "ffn_bwd",
        grid=(ni, nj),
        in_specs=[row, row, row, act, act, vec, vec, w_in, w_in, w_out] + [hbm] * ns,
        out_specs=[row, row, row, act, act, act, vec, vec] + [hbm] * ns,
        out_shape=[
            jax.ShapeDtypeStruct((T, D), F32),
            jax.ShapeDtypeStruct((T, D), BF16),
            jax.ShapeDtypeStruct((T, D), BF16),
            act_shape, act_shape, act_shape,
            jax.ShapeDtypeStruct((1, D), F32),
            jax.ShapeDtypeStruct((1, D), F32),
        ] + [jax.ShapeDtypeStruct((N_DEV - 1, *g.shape[2:]), g.dtype) for g in scatter],
        scratch_shapes=[pltpu.VMEM((tm, D), F32)] + (_Scatter.semaphores(ns) if ns else []),
        compiler_params=_cparams("arbitrary", "arbitrary"),
    )(dy, x, f, a, b, g_pre, g_post, wg, wu, wd, *scatter)


class _ScatterRider:
    name = "scatter"

    def __init__(self, grads, halves, into):
        n = len(grads)
        self.halves = halves
        self.operands = [*grads, *into]
        self.results = [jax.ShapeDtypeStruct(t.shape, t.dtype) for t in into]
        self.aliases = {n + a: a for a in range(n)}
        self.semaphores = _Scatter.semaphores(n)

    def plan(self, in_refs, out_refs, sems):
        return _Scatter(in_refs[:len(out_refs)], out_refs, sems, self.halves)


class _ShareRider:
    name = "share"

    def __init__(self, slots):
        self.operands = [slots]
        self.results = [jax.ShapeDtypeStruct(slots.shape, slots.dtype)]
        self.aliases = {0: 0}
        self.semaphores = _ShareAll.semaphores()

    def plan(self, in_refs, out_refs, sems):
        return _ShareAll(out_refs[0], sems)


def _tn_matmul(a, b, buf, like, sel, tk, rider=None):
    a_chunked = a.ndim == 3
    nj = a.shape[0] if a_chunked else b.shape[0]
    T, M, N = a.shape[-2], a.shape[-1], b.shape[-1]
    tk = min(tk, T)
    nk = T // tk
    have = buf is not None
    n_in = 2 + have + (len(rider.operands) if rider else 0)
    n_out = 1 + (len(rider.results) if rider else 0)

    def body(*refs):
        a_ref, b_ref = refs[:2]
        o_ref = refs[n_in]
        acc_scr = refs[n_in + n_out]
        j = pl.program_id(0)
        k = pl.program_id(1)
        if rider:
            plan = rider.plan(refs[2 + have:n_in], refs[n_in + 1:n_in + n_out], refs[n_in + n_out + 1:])
            pl.when(jnp.logical_and(j == 0, k == 0))(plan.start)

        @pl.when(k == 0)
        def _():
            acc_scr[...] = jnp.zeros_like(acc_scr)

        acc_scr[...] += _dot_tn(a_ref[...], b_ref[...])

        @pl.when(k == nk - 1)
        def _():
            o_ref[...] = acc_scr[...].astype(BF16)

        if rider:
            pl.when(jnp.logical_and(j == nj - 1, k == nk - 1))(plan.finish)

    def spec(chunked, width):
        if chunked:
            return pl.BlockSpec((None, tk, width), lambda j, k: (j, k, 0))
        return pl.BlockSpec((tk, width), lambda j, k: (k, 0))

    hbm = pl.BlockSpec(memory_space=pl.ANY)
    aliases = {2: 0} if have else {}
    if rider:
        aliases.update({2 + have + src: 1 + dst for src, dst in rider.aliases.items()})
    outs = pl.pallas_call(
        body,
        name="tn_matmul_" + rider.name if rider else "tn_matmul",
        grid=(nj, nk),
        in_specs=[spec(a_chunked, M), spec(not a_chunked, N)] + [hbm] * (n_in - 2),
        out_specs=[pl.BlockSpec((None,) * (1 + len(sel)) + (M, N), lambda j, k: (j, *sel, 0, 0))] + [hbm] * (n_out - 1),
        out_shape=[jax.ShapeDtypeStruct(like.shape, BF16)] + (list(rider.results) if rider else []),
        input_output_aliases=aliases,
        scratch_shapes=[pltpu.VMEM((M, N), F32)] + (rider.semaphores if rider else []),
        compiler_params=_cparams("arbitrary", "arbitrary"),
    )(a, b, *([buf] if have else []), *(rider.operands if rider else ()))
    return (outs[0], outs[1:]) if rider else outs[0]


def _sgu_fwd(x, g_pre, g_post, win, lng, lnb, wsm, bsb, wout, sel, tm):
    T, D = x.shape
    nc, E = win.shape[0], win.shape[-1]
    S = 2 * E
    dg = S // N_GROUPS
    wo_rows = wout.shape[-2]
    tm = min(tm, T)
    nq = tm // CHUNK

    def body(x_ref, gpre_ref, gpost_ref, win_ref, lng_ref, lnb_ref, ws_ref, bsb_ref, wout_ref,
             xo_ref, zp_ref, m_ref, u_scr, vn_scr, gt_scr):
        x = x_ref[...]
        xh, _ = _rms_stats(x)
        hn = (xh * gpre_ref[...]).astype(BF16)
        v_parts = []
        for c in range(nc):
            zp = _dot(hn, win_ref[c])
            zp_ref[c] = zp.astype(BF16)
            z = _gelu(zp)
            if c < nc // 2:
                u_scr[:, c * E:(c + 1) * E] = z
            else:
                v_parts.append(z)
        vh_parts, _ = _ln_stats(v_parts, S)
        for c, vh in enumerate(vh_parts):
            cols = slice(c * E, (c + 1) * E)
            vn_scr[:, cols] = (vh * lng_ref[:, cols] + lnb_ref[:, cols]).astype(BF16)
        for q in range(nq):
            rows = slice(q * CHUNK, (q + 1) * CHUNK)
            for g in range(N_GROUPS):
                cols = slice(g * dg, (g + 1) * dg)
                mixed = _dot(ws_ref[g], vn_scr[rows, cols]) + bsb_ref[g]
                gt_scr[rows, cols] = (u_scr[rows, cols] * mixed).astype(BF16)
        m = _dot(gt_scr[:, 0:wo_rows], wout_ref[0])
        for c in range(1, nc):
            m += _dot(gt_scr[:, c * wo_rows:(c + 1) * wo_rows], wout_ref[c])
        m_ref[...] = m
        mh, _ = _rms_stats(m)
        xo_ref[...] = x + mh * gpost_ref[...]

    row = pl.BlockSpec((tm, D), lambda i: (i, 0))
    return pl.pallas_call(
        body,
        name="sgu_fwd",
        grid=(T // tm,),
        in_specs=[row, _resident((1, D)), _resident((1, D)), _chunks_spec(win, sel), _resident((1, S)),
                  _resident((1, S)), _resident(wsm.shape), _resident(bsb.shape), _chunks_spec(wout, sel)],
        out_specs=[row, pl.BlockSpec((nc, tm, E), lambda i: (0, i, 0)), row],
        out_shape=[
            jax.ShapeDtypeStruct((T, D), F32),
            jax.ShapeDtypeStruct((nc, T, E), BF16),
            jax.ShapeDtypeStruct((T, D), F32),
        ],
        scratch_shapes=[pltpu.VMEM((tm, S), F32), pltpu.VMEM((tm, S), BF16), pltpu.VMEM((tm, S), BF16)],
        compiler_params=_cparams("parallel"),
    )(x, g_pre, g_post, win, lng, lnb, wsm, bsb, wout)


def _sgu_bwd(dy, x, m, zp, g_pre, g_post, win, lng, lnb, wsm, wsmt, bsb, wout, sel, tm):
    T, D = x.shape
    nc, E = win.shape[0], win.shape[-1]
    S = 2 * E
    dg = S // N_GROUPS
    wo_rows = wout.shape[-2]
    tm = min(tm, T)
    nq = tm // CHUNK

    def body(dy_ref, x_ref, m_ref, zp_ref, gpre_ref, gpost_ref, win_ref, lng_ref, lnb_ref, ws_ref, wst_ref,
             bsb_ref, wout_ref,
             dx_ref, hn_ref, dzp_ref, gated_ref, dm_ref, dws_ref, dbs_ref, dlng_ref, dlnb_ref, dgpre_ref,
             dgpost_ref, u_scr, d_scr, vh_scr, vn_scr, gg_scr):
        first = pl.program_id(0) == 0
        dy = dy_ref[...]
        mh, rm = _rms_stats(m_ref[...])
        dm, dgp = _rms_bwd(mh, rm, gpost_ref[...], dy)
        _acc_out(dgpost_ref, first, dgp)
        dm = dm.astype(BF16)
        dm_ref[...] = dm
        for c in range(nc):
            d_scr[:, c * wo_rows:(c + 1) * wo_rows] = _dot_nt(dm, wout_ref[c])
        v_parts = []
        for c in range(nc):
            z, gg_scr[c] = _gelu_pair(zp_ref[c])
            if c < nc // 2:
                u_scr[:, c * E:(c + 1) * E] = z.astype(F32)
            else:
                v_parts.append(z.astype(F32))
        vh_parts, rstd = _ln_stats(v_parts, S)
        for c, vh in enumerate(vh_parts):
            cols = slice(c * E, (c + 1) * E)
            vh_scr[:, cols] = vh
            vn_scr[:, cols] = (vh * lng_ref[:, cols] + lnb_ref[:, cols]).astype(BF16)

        @pl.when(first)
        def _():
            dws_ref[...] = jnp.zeros_like(dws_ref)
            dbs_ref[...] = jnp.zeros_like(dbs_ref)
            dlng_ref[...] = jnp.zeros_like(dlng_ref)
            dlnb_ref[...] = jnp.zeros_like(dlnb_ref)

        for q in range(nq):
            rows = slice(q * CHUNK, (q + 1) * CHUNK)
            for g in range(N_GROUPS):
                cols = slice(g * dg, (g + 1) * dg)
                vn = vn_scr[rows, cols]
                mixed = _dot(ws_ref[g], vn) + bsb_ref[g]
                u = u_scr[rows, cols]
                dgt = d_scr[rows, cols]
                gated_ref[(g * dg) // wo_rows, rows, (g * dg) % wo_rows:(g * dg) % wo_rows + dg] = (u * mixed).astype(BF16)
                dmix = dgt * u
                dbs_ref[:, cols] += dmix
                dmix = dmix.astype(BF16)
                dws_ref[g] += _dot_nt(dmix, vn)
                u_scr[rows, cols] = dgt * mixed
                d_scr[rows, cols] = _dot(wst_ref[g], dmix)
        dvn = [d_scr[:, c * E:(c + 1) * E] for c in range(nc // 2)]
        vh = [vh_scr[:, c * E:(c + 1) * E] for c in range(nc // 2)]
        for c, (d, v) in enumerate(zip(dvn, vh)):
            dlng_ref[:, c * E:(c + 1) * E] += jnp.sum(d * v, axis=0, keepdims=True)
            dlnb_ref[:, c * E:(c + 1) * E] += jnp.sum(d, axis=0, keepdims=True)
        dvh = [d * lng_ref[:, c * E:(c + 1) * E] for c, d in enumerate(dvn)]
        dv = _ln_bwd(vh, rstd, dvh, S)
        dhn = None
        for c in range(nc):
            dz = u_scr[:, c * E:(c + 1) * E] if c < nc // 2 else dv[c - nc // 2]
            dzp = dz.astype(BF16) * gg_scr[c]
            dzp_ref[c] = dzp
            t = _dot_nt(dzp, win_ref[c])
            dhn = t if dhn is None else dhn + t
        xh, rx = _rms_stats(x_ref[...])
        hn_ref[...] = (xh * gpre_ref[...]).astype(BF16)
        dxn, dgq = _rms_bwd(xh, rx, gpre_ref[...], dhn)
        dx_ref[...] = dy + dxn
        _acc_out(dgpre_ref, first, dgq)

    row = pl.BlockSpec((tm, D), lambda i: (i, 0))

    def whole(shape):
        return pl.BlockSpec(shape, lambda i: (0,) * len(shape))

    return pl.pallas_call(
        body,
        name="sgu_bwd",
        grid=(T // tm,),
        in_specs=[row, row, row, pl.BlockSpec((nc, tm, E), lambda i: (0, i, 0)), _resident((1, D)), _resident((1, D)),
                  _chunks_spec(win, sel), _resident((1, S)), _resident((1, S)), _resident(wsm.shape),
                  _resident(wsmt.shape), _resident(bsb.shape), _chunks_spec(wout, sel)],
        out_specs=[row, row, pl.BlockSpec((nc, tm, E), lambda i: (0, i, 0)),
                   pl.BlockSpec((nc, tm, wo_rows), lambda i: (0, i, 0)), row,
                   whole((N_GROUPS, CHUNK, CHUNK)), whole((CHUNK, S)), whole((1, S)), whole((1, S)),
                   whole((1, D)), whole((1, D))],
        out_shape=[
            jax.ShapeDtypeStruct((T, D), F32),
            jax.ShapeDtypeStruct((T, D), BF16),
            jax.ShapeDtypeStruct((nc, T, E), BF16),
            jax.ShapeDtypeStruct((nc, T, wo_rows), BF16),
            jax.ShapeDtypeStruct((T, D), BF16),
            jax.ShapeDtypeStruct((N_GROUPS, CHUNK, CHUNK), F32),
            jax.ShapeDtypeStruct((CHUNK, S), F32),
            jax.ShapeDtypeStruct((1, S), F32),
            jax.ShapeDtypeStruct((1, S), F32),
            jax.ShapeDtypeStruct((1, D), F32),
            jax.ShapeDtypeStruct((1, D), F32),
        ],
        scratch_shapes=[pltpu.VMEM((tm, S), F32), pltpu.VMEM((tm, S), F32), pltpu.VMEM((tm, S), F32),
                        pltpu.VMEM((tm, S), BF16), pltpu.VMEM((nc, tm, E), BF16)],
        compiler_params=_cparams("arbitrary"),
    )(dy, x, m, zp, g_pre, g_post, win, lng, lnb, wsm, wsmt, bsb, wout)


def _shifted_windows(buf, r0, cols, lo, hi):
    n = CONV_RB + HALO
    base = buf[r0:r0 + n, cols]
    for r in range(8):
        rolled = base if r == 0 else pltpu.roll(base, n - r, axis=0)
        for s in range(r, hi, 8):
            if s >= lo:
                yield s, rolled[s - r:s - r + CONV_RB]


def _conv_fwd_a(x, g_pre, wpw1, sel, tm):
    T, D = x.shape
    nc, E = wpw1.shape[0], wpw1.shape[-1]
    C = 2 * E
    tm = min(tm, T)

    def body(x_ref, gpre_ref, w_ref, y_ref, p_ref):
        xh, _ = _rms_stats(x_ref[...])
        hn = (xh * gpre_ref[...]).astype(BF16)
        ps = []
        for c in range(nc):
            p = _dot(hn, w_ref[c])
            p_ref[c] = p.astype(BF16)
            ps.append(p)
        for c in range(nc // 2):
            y_ref[:, c * E:(c + 1) * E] = ps[c] * jax.nn.sigmoid(ps[c + nc // 2])

    row = pl.BlockSpec((tm, D), lambda i: (i, 0))
    return pl.pallas_call(
        body,
        name="conv_fwd_a",
        grid=(T // tm,),
        in_specs=[row, _resident((1, D)), _chunks_spec(wpw1, sel)],
        out_specs=[pl.BlockSpec((tm, C), lambda i: (i, 0)), pl.BlockSpec((nc, tm, E), lambda i: (0, i, 0))],
        out_shape=[jax.ShapeDtypeStruct((T, C), F32), jax.ShapeDtypeStruct((nc, T, E), BF16)],
        compiler_params=_cparams("parallel"),
    )(x, g_pre, wpw1)


def _conv_fwd_b(x, y, wdw, bdw, lng, lnb, wpw2, g_post, sel, tm):
    T, D = x.shape
    C = y.shape[1]
    nc, E = wpw2.shape[0], wpw2.shape[-2]
    tm = min(tm, T)
    per = tm // HALO

    def body(x_ref, y_ref, yprev_ref, wdw_ref, bdw_ref, lng_ref, lnb_ref, w_ref, gpost_ref,
             xo_ref, c_ref, m_ref, ybuf):
        i = pl.program_id(0)
        ybuf[0:HALO, :] = jnp.where(i > 0, yprev_ref[...], 0.0)
        ybuf[HALO:HALO + tm, :] = y_ref[...]
        off = HALO - (CONV_W - 1)
        for r0 in range(0, tm, CONV_RB):
            for c0 in range(0, C, CONV_CB):
                cols = slice(c0, c0 + CONV_CB)
                acc = jnp.broadcast_to(bdw_ref[:, cols], (CONV_RB, CONV_CB))
                for s, win in _shifted_windows(ybuf, r0, cols, off, off + CONV_W):
                    acc = acc + wdw_ref[s - off:s - off + 1, cols] * win
                c_ref[r0:r0 + CONV_RB, cols] = acc
        (ch,), _ = _ln_stats([c_ref[...]], C)
        cn = ch * lng_ref[...] + lnb_ref[...]
        qv = (cn * jax.nn.sigmoid(cn)).astype(BF16)
        m = _dot(qv[:, 0:E], w_ref[0])
        for c in range(1, nc):
            m += _dot(qv[:, c * E:(c + 1) * E], w_ref[c])
        m_ref[...] = m
        mh, _ = _rms_stats(m)
        xo_ref[...] = x_ref[...] + mh * gpost_ref[...]

    row = pl.BlockSpec((tm, D), lambda i: (i, 0))
    crow = pl.BlockSpec((tm, C), lambda i: (i, 0))
    prev = pl.BlockSpec((HALO, C), lambda i: (jnp.maximum(i * per - 1, 0), 0))
    return pl.pallas_call(
        body,
        name="conv_fwd_b",
        grid=(T // tm,),
        in_specs=[row, crow, prev, _resident(wdw.shape), _resident((1, C)), _resident((1, C)), _resident((1, C)),
                  _chunks_spec(wpw2, sel), _resident((1, D))],
        out_specs=[row, crow, row],
        out_shape=[jax.ShapeDtypeStruct((T, D), F32), jax.ShapeDtypeStruct((T, C), F32),
                   jax.ShapeDtypeStruct((T, D), F32)],
        scratch_shapes=[pltpu.VMEM((HALO + tm, C), F32)],
        compiler_params=_cparams("parallel"),
    )(x, y, y, wdw, bdw, lng, lnb, wpw2, g_post)


def _conv_bwd_b(dy, m, c, lng, lnb, wpw2, g_post, sel, tm):
    T, D = dy.shape
    C = c.shape[1]
    nc, E = wpw2.shape[0], wpw2.shape[-2]
    tm = min(tm, T)

    def body(dy_ref, m_ref, c_ref, lng_ref, lnb_ref, w_ref, gpost_ref,
             dm_ref, q_ref, dc_ref, dlng_ref, dlnb_ref, dbdw_ref, dgpost_ref, dq_scr):
        first = pl.program_id(0) == 0
        mh, rm = _rms_stats(m_ref[...])
        dm, dgp = _rms_bwd(mh, rm, gpost_ref[...], dy_ref[...])
        _acc_out(dgpost_ref, first, dgp)
        dm = dm.astype(BF16)
        dm_ref[...] = dm
        for k in range(nc):
            dq_scr[:, k * E:(k + 1) * E] = _dot_nt(dm, w_ref[k])
        (ch,), rstd = _ln_stats([c_ref[...]], C)
        cn = ch * lng_ref[...] + lnb_ref[...]
        sg = jax.nn.sigmoid(cn)
        qv = (cn * sg).astype(BF16)
        for k in range(nc):
            q_ref[k] = qv[:, k * E:(k + 1) * E]
        dcn = dq_scr[...] * (sg * (1.0 + cn * (1.0 - sg)))
        _acc_out(dlng_ref, first, jnp.sum(dcn * ch, axis=0, keepdims=True))
        _acc_out(dlnb_ref, first, jnp.sum(dcn, axis=0, keepdims=True))
        (dc,) = _ln_bwd([ch], rstd, [dcn * lng_ref[...]], C)
        dc_ref[...] = dc
        _acc_out(dbdw_ref, first, jnp.sum(dc, axis=0, keepdims=True))

    row = pl.BlockSpec((tm, D), lambda i: (i, 0))
    crow = pl.BlockSpec((tm, C), lambda i: (i, 0))

    def whole(shape):
        return pl.BlockSpec(shape, lambda i: (0,) * len(shape))

    return pl.pallas_call(
        body,
        name="conv_bwd_b",
        grid=(T // tm,),
        in_specs=[row, row, crow, _resident((1, C)), _resident((1, C)), _chunks_spec(wpw2, sel), _resident((1, D))],
        out_specs=[row, pl.BlockSpec((nc, tm, E), lambda i: (0, i, 0)), crow, whole((1, C)), whole((1, C)),
                   whole((1, C)), whole((1, D))],
        out_shape=[jax.ShapeDtypeStruct((T, D), BF16), jax.ShapeDtypeStruct((nc, T, E), BF16),
                   jax.ShapeDtypeStruct((T, C), F32), jax.ShapeDtypeStruct((1, C), F32),
                   jax.ShapeDtypeStruct((1, C), F32), jax.ShapeDtypeStruct((1, C), F32),
                   jax.ShapeDtypeStruct((1, D), F32)],
        scratch_shapes=[pltpu.VMEM((tm, C), F32)],
        compiler_params=_cparams("arbitrary"),
    )(dy, m, c, lng, lnb, wpw2, g_post)


def _conv_bwd_a(dy, x, dc, y, p, g_pre, wdw, wpw1, sel, tm):
    T, D = x.shape
    C = y.shape[1]
    nc, E = wpw1.shape[0], wpw1.shape[-1]
    tm = min(tm, T)
    per = tm // HALO
    n_tiles = T // tm
    KP = wdw.shape[0]

    def body(dy_ref, x_ref, dc_ref, dcnext_ref, y_ref, yprev_ref, p_ref, gpre_ref, wdw_ref, w_ref,
             dx_ref, hn_ref, dp_ref, dwdw_ref, dgpre_ref, ybuf, dcbuf, dyg_scr, dw8_scr):
        i = pl.program_id(0)
        first = i == 0
        ybuf[0:HALO, :] = jnp.where(i > 0, yprev_ref[...], 0.0)
        ybuf[HALO:HALO + tm, :] = y_ref[...]
        dcbuf[0:tm, :] = dc_ref[...]
        dcbuf[tm:tm + HALO, :] = jnp.where(i < n_tiles - 1, dcnext_ref[...], 0.0)
        off = HALO - (CONV_W - 1)
        @pl.when(first)
        def _():
            dw8_scr[...] = jnp.zeros_like(dw8_scr)

        for r0 in range(0, tm, CONV_RB):
            for c0 in range(0, C, CONV_CB):
                cols = slice(c0, c0 + CONV_CB)
                dcb = dcbuf[r0:r0 + CONV_RB, cols]
                acc = jnp.zeros((CONV_RB, CONV_CB), F32)
                for s, win in _shifted_windows(dcbuf, r0, cols, 0, CONV_W):
                    k = CONV_W - 1 - s
                    acc = acc + wdw_ref[k:k + 1, cols] * win
                dyg_scr[r0:r0 + CONV_RB, cols] = acc
                for s, win in _shifted_windows(ybuf, r0, cols, off, off + CONV_W):
                    dw8_scr[s - off, :, cols] += jnp.sum((dcb * win).reshape(CONV_RB // 8, 8, CONV_CB), axis=0)

        @pl.when(i == n_tiles - 1)
        def _():
            dwdw_ref[...] = jnp.sum(dw8_scr[...], axis=1)

        dhn = None
        for c in range(nc // 2):
            cols = slice(c * E, (c + 1) * E)
            av = p_ref[c].astype(F32)
            sg = jax.nn.sigmoid(p_ref[c + nc // 2].astype(F32))
            dygc = dyg_scr[:, cols]
            da = (dygc * sg).astype(BF16)
            dgt = (dygc * av * sg * (1.0 - sg)).astype(BF16)
            dp_ref[c] = da
            dp_ref[c + nc // 2] = dgt
            t = _dot_nt(da, w_ref[c]) + _dot_nt(dgt, w_ref[c + nc // 2])
            dhn = t if dhn is None else dhn + t
        xh, rx = _rms_stats(x_ref[...])
        hn_ref[...] = (xh * gpre_ref[...]).astype(BF16)
        dxn, dgq = _rms_bwd(xh, rx, gpre_ref[...], dhn)
        dx_ref[...] = dy_ref[...] + dxn
        _acc_out(dgpre_ref, first, dgq)

    row = pl.BlockSpec((tm, D), lambda i: (i, 0))
    crow = pl.BlockSpec((tm, C), lambda i: (i, 0))
    prev = pl.BlockSpec((HALO, C), lambda i: (jnp.maximum(i * per - 1, 0), 0))
    nxt = pl.BlockSpec((HALO, C), lambda i: (jnp.minimum((i + 1) * per, T // HALO - 1), 0))
    chunks = pl.BlockSpec((nc, tm, E), lambda i: (0, i, 0))

    def whole(shape):
        return pl.BlockSpec(shape, lambda i: (0,) * len(shape))

    return pl.pallas_call(
        body,
        name="conv_bwd_a",
        grid=(n_tiles,),
        in_specs=[row, row, crow, nxt, crow, prev, chunks, _resident((1, D)), _resident(wdw.shape),
                  _chunks_spec(wpw1, sel)],
        out_specs=[row, row, chunks, whole((KP, C)), whole((1, D))],
        out_shape=[jax.ShapeDtypeStruct((T, D), F32), jax.ShapeDtypeStruct((T, D), BF16),
                   jax.ShapeDtypeStruct((nc, T, E), BF16), jax.ShapeDtypeStruct((KP, C), F32),
                   jax.ShapeDtypeStruct((1, D), F32)],
        scratch_shapes=[pltpu.VMEM((HALO + tm, C), F32), pltpu.VMEM((tm + HALO, C), F32),
                        pltpu.VMEM((tm, C), F32), pltpu.VMEM((KP, 8, C), F32)],
        compiler_params=_cparams("arbitrary"),
    )(dy, x, dc, dc, y, y, p, g_pre, wdw, wpw1)


def _loss_head(y, target, tm):
    T, D = y.shape
    tm = min(tm, T)

    def body(y_ref, t_ref, dy_ref, loss_ref):
        e = y_ref[...] - t_ref[...]
        dy_ref[...] = e * (1.0 / D)
        part = jnp.sum(jnp.sum(e * e, axis=-1, keepdims=True), axis=0, keepdims=True) * (0.5 / D)
        _acc_out(loss_ref, pl.program_id(0) == 0, jnp.broadcast_to(part, loss_ref.shape))

    row = pl.BlockSpec((tm, D), lambda i: (i, 0))
    return pl.pallas_call(
        body,
        name="loss_head",
        grid=(T // tm,),
        in_specs=[row, row],
        out_specs=[row, pl.BlockSpec((8, 128), lambda i: (0, 0))],
        out_shape=[jax.ShapeDtypeStruct((T, D), F32), jax.ShapeDtypeStruct((8, 128), F32)],
        compiler_params=_cparams("arbitrary"),
    )(y, target)


def _row_tile(rows, cols, itemsize_budget=2 * 1024 * 1024):
    want = max(16, itemsize_budget // (4 * cols))
    if rows <= want:
        return rows
    t = (want // 16) * 16
    while t > 16 and rows % t:
        t -= 16
    return t if rows % t == 0 else rows


def _sum_parts(parts):
    n, R, C = parts.shape
    tr = _row_tile(R, C * n // 2 if parts.dtype == BF16 else C * n)

    def body(p_ref, o_ref):
        acc = p_ref[0].astype(F32)
        for s in range(1, n):
            acc = acc + p_ref[s].astype(F32)
        o_ref[...] = acc

    return pl.pallas_call(
        body,
        name="sum_parts",
        grid=(R // tr,),
        in_specs=[pl.BlockSpec((n, tr, C), lambda i: (0, i, 0))],
        out_specs=pl.BlockSpec((tr, C), lambda i: (i, 0)),
        out_shape=jax.ShapeDtypeStruct((R, C), F32),
        compiler_params=_cparams("parallel"),
    )(parts)


def _cast_into_slot(w):
    _, R, C = w.shape
    tr = _row_tile(R, C)

    def body(w_ref, o_ref):
        o_ref[...] = w_ref[...].astype(BF16)

    def own_slot(h, i):
        return 2 * lax.axis_index("x") + lax.axis_index("y"), h, i, 0

    return pl.pallas_call(
        body,
        name="cast_into_slot",
        grid=(2, R // tr),
        in_specs=[pl.BlockSpec((None, tr, C), lambda h, i: (h, i, 0))],
        out_specs=pl.BlockSpec((None, None, tr, C), own_slot),
        out_shape=jax.ShapeDtypeStruct((N_CHIPS, 2, R, C), BF16),
        compiler_params=_cparams("parallel", "parallel"),
    )(w)


def _sum_with_own(arrived, own):
    n, R, C = arrived.shape
    tr = _row_tile(R, C * (n + 1) // 2)

    def body(a_ref, own_ref, o_ref):
        acc = own_ref[...].astype(F32)
        for s in range(n):
            acc = acc + a_ref[s].astype(F32)
        o_ref[...] = acc

    def own_piece(i):
        return 2 * lax.axis_index("x") + lax.axis_index("y"), lax.axis_index("c"), i, 0

    return pl.pallas_call(
        body,
        name="sum_with_own",
        grid=(R // tr,),
        in_specs=[pl.BlockSpec((n, tr, C), lambda i: (0, i, 0)), pl.BlockSpec((None, None, tr, C), own_piece)],
        out_specs=pl.BlockSpec((None, tr, C), lambda i: (lax.axis_index("c"), i, 0)),
        out_shape=jax.ShapeDtypeStruct((2, R, C), F32),
        compiler_params=_cparams("parallel"),
    )(arrived, own)


def _adamw(w, g, m, v):
    R, C = w.shape
    tr = _row_tile(R, C * 7 // 2)
    c1 = 1.0 - ADAM_B1 ** ADAM_STEP
    c2 = 1.0 - ADAM_B2 ** ADAM_STEP

    def body(w_ref, g_ref, m_ref, v_ref, d_ref, mo_ref, vo_ref):
        g = g_ref[...]
        mn = ADAM_B1 * m_ref[...] + (1.0 - ADAM_B1) * g
        vn = ADAM_B2 * v_ref[...] + (1.0 - ADAM_B2) * (g * g)
        mo_ref[...] = mn
        vo_ref[...] = vn
        d_ref[...] = -ADAM_LR * ((mn / c1) / (jnp.sqrt(vn / c2) + ADAM_EPS) + ADAM_WD * w_ref[...])

    blk = pl.BlockSpec((tr, C), lambda i: (i, 0))
    shp = jax.ShapeDtypeStruct((R, C), F32)
    return pl.pallas_call(
        body,
        name="adamw",
        grid=(R // tr,),
        in_specs=[blk, blk, blk, blk],
        out_specs=[blk, blk, blk],
        out_shape=[shp, shp, shp],
        compiler_params=_cparams("parallel"),
    )(w, g, m, v)


def _adamw_into(w, g, m, v, outs, sel):
    n, R, C = w.shape
    tr = _row_tile(R, C * 4)
    c1 = 1.0 - ADAM_B1 ** ADAM_STEP
    c2 = 1.0 - ADAM_B2 ** ADAM_STEP

    def body(w_ref, g_ref, m_ref, v_ref, *rest):
        go_ref, d_ref, mo_ref, vo_ref = rest[-4:]
        g = g_ref[...]
        mn = ADAM_B1 * m_ref[...] + (1.0 - ADAM_B1) * g
        vn = ADAM_B2 * v_ref[...] + (1.0 - ADAM_B2) * (g * g)
        go_ref[...] = g
        mo_ref[...] = mn
        vo_ref[...] = vn
        d_ref[...] = -ADAM_LR * ((mn / c1) / (jnp.sqrt(vn / c2) + ADAM_EPS) + ADAM_WD * w_ref[...])

    entry = pl.BlockSpec((None, tr, C), lambda i: (sel, i, 0))
    hbm = pl.BlockSpec(memory_space=pl.ANY)
    have = outs is not None
    shp = jax.ShapeDtypeStruct((n, R, C), F32)
    return pl.pallas_call(
        body,
        name="adamw_into",
        grid=(R // tr,),
        in_specs=[entry, pl.BlockSpec((tr, C), lambda i: (i, 0)), entry, entry] + ([hbm] * 4 if have else []),
        out_specs=[entry] * 4,
        out_shape=[shp] * 4,
        input_output_aliases={4 + t: t for t in range(4)} if have else {},
        compiler_params=_cparams("parallel"),
    )(w, g, m, v, *(outs if have else ()))


def _gather_weights(halved, whole):
    nh, nw = len(halved), len(whole)

    def body(*refs):
        w_in = refs[nh:nh + nw]
        h_out, w_out = refs[nh + nw:2 * nh + nw], refs[2 * nh + nw:2 * (nh + nw)]
        ws_send, ws_recv, loc_sem = refs[2 * (nh + nw):2 * (nh + nw) + 3]
        plan = _Gather(h_out, refs[2 * (nh + nw) + 3:])
        x, y, c = _my_place()
        me_chip = 2 * x + y
        plan.start()

        def small(a, j, slot, to):
            return pltpu.make_async_remote_copy(src_ref=w_in[a], dst_ref=w_out[a].at[slot],
                                                send_sem=ws_send.at[a, j], recv_sem=ws_recv.at[a, j],
                                                device_id=to, device_id_type=MESH)

        for a in range(nw):
            pltpu.make_async_copy(w_in[a], w_out[a].at[me_chip], loc_sem.at[a]).start()
            for j, ch in enumerate(plan.chips):
                small(a, j, me_chip, (*ch, c)).start()
        plan.forward()
        plan.finish()
        for a in range(nw):
            for j, ch in enumerate(plan.chips):
                cp = small(a, j, 2 * ch[0] + ch[1], (x, y, c))
                cp.wait_recv()
                cp.wait_send()
            pltpu.make_async_copy(w_in[a], w_out[a].at[me_chip], loc_sem.at[a]).wait()

    hbm = pl.BlockSpec(memory_space=pl.ANY)
    outs = pl.pallas_call(
        body,
        name="gather_weights",
        in_specs=[hbm] * (nh + nw),
        out_specs=[hbm] * (nh + nw),
        out_shape=[jax.ShapeDtypeStruct(a.shape, a.dtype) for a in halved]
        + [jax.ShapeDtypeStruct((N_CHIPS, *a.shape), a.dtype) for a in whole],
        input_output_aliases={a: a for a in range(nh)},
        scratch_shapes=[pltpu.SemaphoreType.DMA((max(nw, 1), 3)), pltpu.SemaphoreType.DMA((max(nw, 1), 3)),
                        pltpu.SemaphoreType.DMA((max(nw, 1),))] + _Gather.semaphores(nh),
    )(*halved, *whole)
    return outs[:nh], outs[nh:]


def _scatter_grads(grads, halves, into):
    n = len(grads)

    def body(*refs):
        plan = _Scatter(refs[:n], refs[2 * n:3 * n], refs[3 * n:], halves)
        plan.start()
        plan.finish()

    hbm = pl.BlockSpec(memory_space=pl.ANY)
    return pl.pallas_call(
        body,
        name="scatter_grads",
        in_specs=[hbm] * (2 * n),
        out_specs=[hbm] * n,
        out_shape=[jax.ShapeDtypeStruct(t.shape, t.dtype) for t in into],
        input_output_aliases={n + a: a for a in range(n)},
        scratch_shapes=_Scatter.semaphores(n),
    )(*grads, *into)


def _swap_halves(halves):
    n = len(halves)

    def body(*refs):
        h_out = refs[n:2 * n]
        send_sem, recv_sem = refs[2 * n:]
        x, y, c = _my_place()
        sib = (x, y, 1 - c)
        for a in range(n):
            pltpu.make_async_remote_copy(src_ref=h_out[a].at[c], dst_ref=h_out[a].at[c], send_sem=send_sem.at[a],
                                         recv_sem=recv_sem.at[a], device_id=sib, device_id_type=MESH).start()
        for a in range(n):
            cp = pltpu.make_async_remote_copy(src_ref=h_out[a].at[c], dst_ref=h_out[a].at[1 - c],
                                              send_sem=send_sem.at[a], recv_sem=recv_sem.at[a], device_id=sib,
                                              device_id_type=MESH)
            cp.wait_send()
            cp.wait_recv()

    hbm = pl.BlockSpec(memory_space=pl.ANY)
    return pl.pallas_call(
        body,
        name="swap_halves",
        in_specs=[hbm] * n,
        out_specs=[hbm] * n,
        out_shape=[jax.ShapeDtypeStruct(h.shape, h.dtype) for h in halves],
        input_output_aliases={a: a for a in range(n)},
        scratch_shapes=[pltpu.SemaphoreType.DMA((n,)), pltpu.SemaphoreType.DMA((n,))],
    )(*halves)


TM_FFN = 512
TM_FFN_FWD = 1024
RB_FFN_FWD = 512
RB_FFN_BWD = 256
TM_SGU = 256
TM_SGU_FWD = 256
TM_CONV = 256
TK_WGRAD = 4096
TM_LOSS = 1024


FFN_KINDS = ("ff_w_gate", "ff_w_up", "ff_w_down")


def _layer_kinds(i):
    return FFN_KINDS + (("sgu_w_in", "sgu_w_out") if i % 2 == 0 else ("conv_w_pw1", "conv_w_pw2"))


def _local_step(x, target, G, W, exchange=None, pack_small=None):
    depth = W["norm_g"].shape[0]
    G = [dict(g) for g in G]
    saved = []
    vec = lambda v: v.reshape(1, -1)

    def mixer_w(i, k):
        w = G[i][k]
        return w.reshape(w.shape[0], -1, w.shape[-1])
    wsm, wsmt, bsb = [], [], []
    n_sgu = W["sgu_w_spatial"].shape[0]
    causal = jnp.tril(jnp.ones((CHUNK, CHUNK), dtype=bool))
    dgrp = W["sgu_ln_g"].shape[1] // N_GROUPS
    for jx in range(n_sgu):
        ws = jnp.where(causal[None], W["sgu_w_spatial"][jx], 0.0).astype(BF16)
        wsm.append(ws)
        wsmt.append(jnp.swapaxes(ws, 1, 2))
        bsb.append(jnp.broadcast_to(W["sgu_b_spatial"][jx][:, :, None], (N_GROUPS, CHUNK, dgrp)))
    kp = HALO
    wdw = [jnp.pad(W["conv_w_dw"][jx], ((0, kp - CONV_W), (0, 0))) for jx in range(W["conv_w_dw"].shape[0])]

    def ffn(x, i, f_idx, gather=()):
        g = W["norm_g"][i]
        return _ffn_fwd(x, vec(g[4 * f_idx]), vec(g[4 * f_idx + 1]), G[i]["ff_w_gate"], G[i]["ff_w_up"],
                        G[i]["ff_w_down"], (f_idx,), TM_FFN_FWD, gather)

    for i in range(depth):
        g = W["norm_g"][i]
        rec = {"x0": x}
        if exchange is not None and i + 1 < depth:
            kinds = _layer_kinds(i + 1)
            x, rec["a1"], rec["b1"], rec["f1"], *filled = ffn(x, i, 0, [G[i + 1][k] for k in kinds])
            G[i + 1] = dict(zip(kinds, filled))
        else:
            x, rec["a1"], rec["b1"], rec["f1"] = ffn(x, i, 0)
        rec["x1"] = x
        j = i // 2
        if i % 2 == 0:
            x, rec["zp"], rec["m"] = _sgu_fwd(
                x, vec(g[2]), vec(g[3]), mixer_w(i, "sgu_w_in"), vec(W["sgu_ln_g"][j]), vec(W["sgu_ln_b"][j]),
                wsm[j], bsb[j], mixer_w(i, "sgu_w_out"), (), TM_SGU_FWD)
        else:
            rec["y"], rec["p"] = _conv_fwd_a(x, vec(g[2]), mixer_w(i, "conv_w_pw1"), (), TM_CONV)
            x, rec["c"], rec["m"] = _conv_fwd_b(
                x, rec["y"], wdw[j], vec(W["conv_b_dw"][j]), vec(W["conv_ln_g"][j]), vec(W["conv_ln_b"][j]),
                mixer_w(i, "conv_w_pw2"), vec(g[3]), (), TM_CONV)
        rec["x2"] = x
        x, rec["a2"], rec["b2"], rec["f2"] = ffn(x, i, 1)
        saved.append(rec)

    dx, loss_tile = _loss_head(x, target, TM_LOSS)
    loss = loss_tile[0, 0]

    big = [{k: None for k in _layer_kinds(i)} for i in range(depth)]
    small = {k: [None] * W[k].shape[0] for k in
             ("sgu_ln_g", "sgu_ln_b", "sgu_w_spatial", "sgu_b_spatial", "conv_w_dw", "conv_b_dw", "conv_ln_g",
              "conv_ln_b")}
    dnorm = [[None] * 6 for _ in range(depth)]
    pieces = [None] * depth
    waiting = []

    def wgrad(i, k, a, b, sel, rider=None):
        like = G[i][k] if sel else mixer_w(i, k)
        out = _tn_matmul(a, b, big[i][k], like, sel, TK_WGRAD, rider)
        big[i][k], rode = out if rider else (out, None)
        return rode

    def as_pieces(b):
        return b.reshape(N_CHIPS, 2, -1, b.shape[-1])

    def small_grads():
        out = {k: jnp.stack(v) for k, v in small.items()}
        out["norm_g"] = jnp.stack([jnp.stack(r) for r in dnorm])
        return out

    def ffn_back(dx, i, f_idx, xin, a, b, f, send, last=False):
        g = W["norm_g"][i]
        dx, h, dz, s, da, db, dgpre, dgpost, *arrived = _ffn_bwd(
            dx, xin, f, a, b, vec(g[4 * f_idx]), vec(g[4 * f_idx + 1]),
            G[i]["ff_w_gate"], G[i]["ff_w_up"], G[i]["ff_w_down"], (f_idx,), TM_FFN, list(send.values()))
        dnorm[i][4 * f_idx] = dgpre[0]
        dnorm[i][4 * f_idx + 1] = dgpost[0]
        arrived = dict(zip(send, arrived))
        if not last:
            wgrad(i, "ff_w_gate", h, da, (f_idx,))
            wgrad(i, "ff_w_up", h, db, (f_idx,))
            wgrad(i, "ff_w_down", s, dz, (f_idx,))
            return dx, arrived, None
        (shared,) = wgrad(i, "ff_w_gate", h, da, (f_idx,), _ShareRider(pack_small(small_grads())))
        for k, nxt, lhs, rhs in (("ff_w_gate", "ff_w_up", h, db), ("ff_w_up", "ff_w_down", s, dz)):
            (arrived[k],) = wgrad(i, nxt, lhs, rhs, (f_idx,),
                                  _ScatterRider([as_pieces(big[i][k])], [(f_idx,)], [arrived[k]]))
        (arrived["ff_w_down"],) = _scatter_grads([as_pieces(big[i]["ff_w_down"])], [(f_idx,)], [arrived["ff_w_down"]])
        return dx, arrived, shared

    for i in reversed(range(depth)):
        rec = saved[i]
        g = W["norm_g"][i]
        j = i // 2
        if exchange is not None and waiting:
            sent = waiting.pop()
            dx, arrived, _ = ffn_back(dx, i, 1, rec["x2"], rec["a2"], rec["b2"], rec["f2"],
                                      {k: (p, (0, 1)) for k, p in pieces[sent].items()})
            exchange(sent, pieces[sent], arrived)
        else:
            dx, _, _ = ffn_back(dx, i, 1, rec["x2"], rec["a2"], rec["b2"], rec["f2"], {})
        if i % 2 == 0:
            (dx, hn, dzp, gated, dm, dws, dbs_acc, dlng, dlnb, dgpre, dgpost) = _sgu_bwd(
                dx, rec["x1"], rec["m"], rec["zp"], vec(g[2]), vec(g[3]), mixer_w(i, "sgu_w_in"),
                vec(W["sgu_ln_g"][j]), vec(W["sgu_ln_b"][j]), wsm[j], wsmt[j], bsb[j], mixer_w(i, "sgu_w_out"), (),
                TM_SGU)
            wgrad(i, "sgu_w_in", hn, dzp, ())
            wgrad(i, "sgu_w_out", gated, dm, ())
            small["sgu_w_spatial"][j] = jnp.where(causal[None], dws, 0.0)
            small["sgu_b_spatial"][j] = dbs_acc.reshape(CHUNK, N_GROUPS, dgrp).sum(-1).T
            small["sgu_ln_g"][j] = dlng[0]
            small["sgu_ln_b"][j] = dlnb[0]
        else:
            dm, q, dc, dlng, dlnb, dbdw, dgpost = _conv_bwd_b(
                dx, rec["m"], rec["c"], vec(W["conv_ln_g"][j]), vec(W["conv_ln_b"][j]), mixer_w(i, "conv_w_pw2"),
                vec(g[3]), (), TM_CONV)
            dx, hn, dp, dwdw, dgpre = _conv_bwd_a(
                dx, rec["x1"], dc, rec["y"], rec["p"], vec(g[2]), wdw[j], mixer_w(i, "conv_w_pw1"), (), TM_CONV)
            wgrad(i, "conv_w_pw1", hn, dp, ())
            wgrad(i, "conv_w_pw2", q, dm, ())
            small["conv_w_dw"][j] = dwdw[:CONV_W]
            small["conv_b_dw"][j] = dbdw[0]
            small["conv_ln_g"][j] = dlng[0]
            small["conv_ln_b"][j] = dlnb[0]
        dnorm[i][2] = dgpre[0]
        dnorm[i][3] = dgpost[0]
        if exchange is not None and i == 0:
            dx, arrived, shared = ffn_back(
                dx, i, 0, rec["x0"], rec["a1"], rec["b1"], rec["f1"],
                {k: (as_pieces(b), (1,) if k in FFN_KINDS else (0, 1)) for k, b in big[i].items()}, last=True)
            pieces[i] = {k: as_pieces(b) for k, b in big[i].items()}
            exchange(i, pieces[i], arrived)
        else:
            dx, _, _ = ffn_back(dx, i, 0, rec["x0"], rec["a1"], rec["b1"], rec["f1"], {})
            pieces[i] = {k: as_pieces(b) for k, b in big[i].items()}
            waiting.append(i)

    return loss, dx, pieces, (shared if exchange is not None else small_grads())


BIG = ("ff_w_gate", "ff_w_up", "ff_w_down", "sgu_w_in", "sgu_w_out", "conv_w_pw1", "conv_w_pw2")
SHARDED_SMALL = ("norm_g", "conv_w_dw", "conv_b_dw", "conv_ln_g", "conv_ln_b")
REPLICATED = ("sgu_ln_g", "sgu_ln_b", "sgu_w_spatial", "sgu_b_spatial")
WEIGHTS = ("norm_g", "ff_w_gate", "ff_w_up", "ff_w_down", "sgu_w_in", "sgu_ln_g", "sgu_ln_b", "sgu_w_spatial",
           "sgu_b_spatial", "sgu_w_out", "conv_w_pw1", "conv_w_dw", "conv_b_dw", "conv_ln_g", "conv_ln_b",
           "conv_w_pw2")


def _rows8(a, width):
    r = a.reshape(-1, width)
    pad = (-r.shape[0]) % 8
    return jnp.pad(r, ((0, pad), (0, 0))) if pad else r


def _pack(arrs, width):
    parts = [_rows8(a, width) for a in arrs]
    return jnp.concatenate(parts, axis=0), [p.shape[0] for p in parts]


def _unpack(buf, like):
    out, r0 = [], 0
    width = buf.shape[-1]
    for a in like:
        n = -(-(a.size // width) // 8) * 8
        rows = a.size // width
        out.append(buf[..., r0:r0 + rows, :].reshape(*buf.shape[:-2], *a.shape))
        r0 += n
    return out


def kernel(x, norm_g, ff_w_gate, ff_w_up, ff_w_down, sgu_w_in, sgu_ln_g, sgu_ln_b, sgu_w_spatial, sgu_b_spatial, sgu_w_out, conv_w_pw1, conv_w_dw, conv_b_dw, conv_ln_g, conv_ln_b, conv_w_pw2, loss_target, m_norm_g, m_ff_w_gate, m_ff_w_up, m_ff_w_down, m_sgu_w_in, m_sgu_ln_g, m_sgu_ln_b, m_sgu_w_spatial, m_sgu_b_spatial, m_sgu_w_out, m_conv_w_pw1, m_conv_w_dw, m_conv_b_dw, m_conv_ln_g, m_conv_ln_b, m_conv_w_pw2, v_norm_g, v_ff_w_gate, v_ff_w_up, v_ff_w_down, v_sgu_w_in, v_sgu_ln_g, v_sgu_ln_b, v_sgu_w_spatial, v_sgu_b_spatial, v_sgu_w_out, v_conv_w_pw1, v_conv_w_dw, v_conv_b_dw, v_conv_ln_g, v_conv_ln_b, v_conv_w_pw2):
    w = dict(norm_g=norm_g, ff_w_gate=ff_w_gate, ff_w_up=ff_w_up, ff_w_down=ff_w_down, sgu_w_in=sgu_w_in,
             sgu_ln_g=sgu_ln_g, sgu_ln_b=sgu_ln_b, sgu_w_spatial=sgu_w_spatial, sgu_b_spatial=sgu_b_spatial,
             sgu_w_out=sgu_w_out, conv_w_pw1=conv_w_pw1, conv_w_dw=conv_w_dw, conv_b_dw=conv_b_dw,
             conv_ln_g=conv_ln_g, conv_ln_b=conv_ln_b, conv_w_pw2=conv_w_pw2)
    mom = dict(norm_g=m_norm_g, ff_w_gate=m_ff_w_gate, ff_w_up=m_ff_w_up, ff_w_down=m_ff_w_down,
               sgu_w_in=m_sgu_w_in, sgu_ln_g=m_sgu_ln_g, sgu_ln_b=m_sgu_ln_b, sgu_w_spatial=m_sgu_w_spatial,
               sgu_b_spatial=m_sgu_b_spatial, sgu_w_out=m_sgu_w_out, conv_w_pw1=m_conv_w_pw1,
               conv_w_dw=m_conv_w_dw, conv_b_dw=m_conv_b_dw, conv_ln_g=m_conv_ln_g, conv_ln_b=m_conv_ln_b,
               conv_w_pw2=m_conv_w_pw2)
    vel = dict(norm_g=v_norm_g, ff_w_gate=v_ff_w_gate, ff_w_up=v_ff_w_up, ff_w_down=v_ff_w_down,
               sgu_w_in=v_sgu_w_in, sgu_ln_g=v_sgu_ln_g, sgu_ln_b=v_sgu_ln_b, sgu_w_spatial=v_sgu_w_spatial,
               sgu_b_spatial=v_sgu_b_spatial, sgu_w_out=v_sgu_w_out, conv_w_pw1=v_conv_w_pw1,
               conv_w_dw=v_conv_w_dw, conv_b_dw=v_conv_b_dw, conv_ln_g=v_conv_ln_g, conv_ln_b=v_conv_ln_b,
               conv_w_pw2=v_conv_w_pw2)
    T, D = x.shape[1], x.shape[2]
    shard_w = conv_b_dw.shape[1]

    xi, yi, ci = _my_place()
    me_chip = (2 * xi + yi).astype(jnp.int32)
    me = (4 * xi + 2 * yi + ci).astype(jnp.int32)

    depth = norm_g.shape[0]

    def entry(k, i):
        return i if k in FFN_KINDS else i // 2

    def stacked(a):
        return a.reshape(a.shape[0], -1, a.shape[-1])

    G = []
    for i in range(depth):
        G.append({k: _cast_into_slot(w[k][entry(k, i)].reshape(2, -1, w[k].shape[-1])) for k in _layer_kinds(i)})
    small_buf, _ = _pack([w[k] for k in SHARDED_SMALL], shard_w)
    first, (small_all,) = _gather_weights(list(G[0].values()), [small_buf])
    G[0] = dict(zip(G[0], first))
    W = {}
    for k, part in zip(SHARDED_SMALL, _unpack(small_all, [w[k] for k in SHARDED_SMALL])):
        W[k] = jnp.moveaxis(part, 0, -2).reshape(*w[k].shape[:-1], N_CHIPS * shard_w)
    for k in REPLICATED:
        W[k] = w[k]

    results = {k: None for k in BIG}

    def reduce_and_update(i, pieces, arrived):
        kinds = list(pieces)
        both = _swap_halves([_sum_with_own(arrived[k], pieces[k]) for k in kinds])
        for k, g in zip(kinds, both):
            results[k] = _adamw_into(stacked(w[k]), g.reshape(-1, g.shape[-1]), stacked(mom[k]), stacked(vel[k]),
                                     results[k], entry(k, i))

    def pack_small(small):
        sbuf, _ = _pack([small[k] for k in SHARDED_SMALL + REPLICATED], D)
        return lax.dynamic_update_slice(jnp.zeros((N_DEV, *sbuf.shape), F32), sbuf[None], (me, 0, 0))

    loss, dx, _, shared = _local_step(x[0], loss_target[0], G, W, reduce_and_update, pack_small)
    loss = lax.psum(loss, ("x", "y", "c"))
    grads, delta, new_m, new_v = {}, {}, {}, {}
    for k in BIG:
        grads[k], delta[k], new_m[k], new_v[k] = (t.reshape(w[k].shape) for t in results[k])

    ssum = _sum_parts(shared)
    for k, gfull in zip(SHARDED_SMALL + REPLICATED, _unpack(ssum, [W[k] for k in SHARDED_SMALL + REPLICATED])):
        if k in SHARDED_SMALL:
            gfull = lax.dynamic_slice_in_dim(gfull, me_chip * shard_w, shard_w, axis=gfull.ndim - 1)
        grads[k] = gfull

    for names, width in ((SHARDED_SMALL, shard_w), (REPLICATED, CHUNK)):
        packed = [_pack([src[k] for k in names], width)[0] for src in (w, grads, mom, vel)]
        outs = _adamw(*packed)
        for res, out in zip((delta, new_m, new_v), outs):
            for k, a in zip(names, _unpack(out, [w[k] for k in names])):
                res[k] = a

    return (loss, dx[None], *[grads[k] for k in WEIGHTS], *[delta[k] for k in WEIGHTS],
            *[new_m[k] for k in WEIGHTS], *[new_v[k] for k in WEIGHTS])
```

```python
import functools

import jax
import jax.numpy as jnp
from jax import lax
from jax.experimental import pallas as pl
from jax.experimental.pallas import tpu as pltpu

F32 = jnp.float32
BF16 = jnp.bfloat16
EPS = 1e-6
N_CHIPS = 4
N_DEV = 8
N_GROUPS = 8
CHUNK = 128
CONV_W = 31
HALO = 32
CONV_RB = 64
CONV_CB = 256
VMEM_LIMIT_V7X = 60 * 1024 * 1024
MESH = pl.DeviceIdType.MESH

ADAM_LR = 0.001
ADAM_B1 = 0.9
ADAM_B2 = 0.999
ADAM_EPS = 1e-08
ADAM_WD = 0.01
ADAM_STEP = 10
FFN_SCALE = 0.5


def _cparams(*sem, **kw):
    return pltpu.CompilerParams(dimension_semantics=sem, vmem_limit_bytes=VMEM_LIMIT_V7X, **kw)


def _resident(shape):
    return pl.BlockSpec(shape, lambda *_: (0,) * len(shape), pipeline_mode=pl.Buffered(1))


def _dot(a, b):
    return jnp.dot(a, b, preferred_element_type=F32)


def _dot_nt(a, b):
    return lax.dot_general(a, b, (((1,), (1,)), ((), ())), preferred_element_type=F32)


def _dot_tn(a, b):
    return lax.dot_general(a, b, (((0,), (0,)), ((), ())), preferred_element_type=F32)


def _rms_stats(x):
    r = lax.rsqrt(jnp.mean(x * x, axis=-1, keepdims=True) + EPS)
    return x * r, r


def _rms_bwd(xh, r, g, dy):
    dxh = dy * g
    dx = r * (dxh - xh * jnp.mean(dxh * xh, axis=-1, keepdims=True))
    return dx, jnp.sum(dy * xh, axis=0, keepdims=True)


def _ln_stats(parts, width):
    mu = sum(jnp.sum(p, axis=-1, keepdims=True) for p in parts) / width
    cen = [p - mu for p in parts]
    var = sum(jnp.sum(c * c, axis=-1, keepdims=True) for c in cen) / width
    rstd = lax.rsqrt(var + EPS)
    return [c * rstd for c in cen], rstd


def _ln_bwd(vh_parts, rstd, dvh_parts, width):
    m1 = sum(jnp.sum(d, axis=-1, keepdims=True) for d in dvh_parts) / width
    m2 = sum(jnp.sum(d * v, axis=-1, keepdims=True) for d, v in zip(dvh_parts, vh_parts)) / width
    return [rstd * (d - m1 - v * m2) for d, v in zip(dvh_parts, vh_parts)]


_GELU_C = 0.7978845608028654
_GELU_A = 0.044715


def _gelu(x):
    return 0.5 * x * (1.0 + jnp.tanh(_GELU_C * (x + _GELU_A * x * x * x)))


def _gelu_pair(x):
    x2 = x * x
    t = jnp.tanh(_GELU_C * (x + _GELU_A * (x * x2)))
    half = 0.5 * (1.0 + t)
    return x * half, half + (0.5 * _GELU_C) * x * (1.0 - t * t) * (1.0 + (3.0 * _GELU_A) * x2)


def _sigmoid_pair(a):
    e = jnp.exp(jnp.minimum(-a, 80.0))
    sg = 1.0 / (1.0 + e)
    return sg, e * sg


def _acc_out(ref, first, val):
    @pl.when(first)
    def _():
        ref[...] = val

    @pl.when(jnp.logical_not(first))
    def _():
        ref[...] += val


def _my_place():
    return lax.axis_index("x"), lax.axis_index("y"), lax.axis_index("c")


class _Gather:
    def __init__(self, bufs, sems):
        self.bufs = bufs
        self.own_sems, self.fwd_sems = sems[:2], sems[2:4]
        self.x, self.y, self.c = _my_place()
        x, y = self.x, self.y
        self.chips = [(1 - x, y), (x, 1 - y), (1 - x, 1 - y)]

    def _copy(self, a, j, chip, half, to, sems):
        spot = self.bufs[a].at[2 * chip[0] + chip[1], pl.ds(half, 1)]
        return pltpu.make_async_remote_copy(src_ref=spot, dst_ref=spot, send_sem=sems[0].at[a, j],
                                            recv_sem=sems[1].at[a, j], device_id=to, device_id_type=MESH)

    def _own(self, a, j):
        return self._copy(a, j, (self.x, self.y), self.c, (*self.chips[j], self.c), self.own_sems)

    def _passed_on(self, a, j):
        return self._copy(a, j, self.chips[j], self.c, (self.x, self.y, 1 - self.c), self.fwd_sems)

    def start(self):
        for j in range(3):
            for a in range(len(self.bufs)):
                self._own(a, j).start()

    def forward(self):
        me = (self.x, self.y, self.c)
        for j in range(3):
            for a in range(len(self.bufs)):
                self._copy(a, j, self.chips[j], self.c, me, self.own_sems).wait_recv()
                self._passed_on(a, j).start()

    def finish(self):
        me = (self.x, self.y, self.c)
        for j in range(3):
            for a in range(len(self.bufs)):
                self._copy(a, j, self.chips[j], 1 - self.c, me, self.fwd_sems).wait_recv()
        for j in range(3):
            for a in range(len(self.bufs)):
                self._own(a, j).wait_send()
                self._passed_on(a, j).wait_send()

    @staticmethod
    def semaphores(n):
        return [pltpu.SemaphoreType.DMA((n, 3)) for _ in range(4)]


class _Scatter:
    def __init__(self, g_in, g_out, sems, halves):
        self.g_in, self.g_out = g_in, g_out
        self.send_sem, self.recv_sem = sems
        self.halves = halves
        x, y, c = _my_place()
        self.c = c
        self.me = 4 * x + 2 * y + c

    def _piece(self, a, d, slot):
        return pltpu.make_async_remote_copy(
            src_ref=self.g_in[a].at[d // 2, d % 2], dst_ref=self.g_out[a].at[slot],
            send_sem=self.send_sem.at[a, d], recv_sem=self.recv_sem.at[a, slot],
            device_id=(d // 4, (d // 2) % 2, d % 2), device_id_type=MESH)

    def _to(self, a):
        return [d for d in range(N_DEV) if d % 2 in self.halves[a]]

    def start(self):
        for a in range(len(self.g_in)):
            for d in self._to(a):
                @pl.when(d != self.me)
                def _():
                    self._piece(a, d, lax.rem(self.me - d - 1 + N_DEV, N_DEV)).start()

    def finish(self):
        for a in range(len(self.g_in)):
            for h in self.halves[a]:
                @pl.when(self.c == h)
                def _():
                    for slot in range(N_DEV - 1):
                        self._piece(a, 0, slot).wait_recv()
            for d in self._to(a):
                @pl.when(d != self.me)
                def _():
                    self._piece(a, d, 0).wait_send()

    @staticmethod
    def semaphores(n):
        return [pltpu.SemaphoreType.DMA((n, N_DEV)), pltpu.SemaphoreType.DMA((n, N_DEV - 1))]


class _ShareAll:
    def __init__(self, buf, sems):
        self.buf = buf
        self.send_sem, self.recv_sem = sems
        self.x, self.y, self.c = _my_place()
        self.me = 4 * self.x + 2 * self.y + self.c

    def start(self):
        mine = self.buf.at[self.me]
        for d in range(N_DEV):
            @pl.when(d != self.me)
            def _():
                pltpu.make_async_remote_copy(src_ref=mine, dst_ref=mine, send_sem=self.send_sem.at[d],
                                             recv_sem=self.recv_sem.at[self.me],
                                             device_id=(d // 4, (d // 2) % 2, d % 2), device_id_type=MESH).start()

    def finish(self):
        for d in range(N_DEV):
            @pl.when(d != self.me)
            def _():
                cp = pltpu.make_async_remote_copy(src_ref=self.buf.at[self.me], dst_ref=self.buf.at[d],
                                                  send_sem=self.send_sem.at[d], recv_sem=self.recv_sem.at[d],
                                                  device_id=(self.x, self.y, self.c), device_id_type=MESH)
                cp.wait_send()
                cp.wait_recv()

    @staticmethod
    def semaphores():
        return [pltpu.SemaphoreType.DMA((N_DEV,)), pltpu.SemaphoreType.DMA((N_DEV,))]


def _chunk_spec(sel, rows, cols):
    return pl.BlockSpec((None,) * (1 + len(sel)) + (rows, cols), lambda i, j: (j, *sel, 0, 0))


def _chunks_spec(w, sel):
    return pl.BlockSpec((w.shape[0],) + (None,) * len(sel) + w.shape[-2:], lambda *_: (0, *sel, 0, 0),
                        pipeline_mode=pl.Buffered(1))


def _ffn_fwd(x, g_pre, g_post, wg, wu, wd, sel, tm, gather=()):
    T, D = x.shape
    nj, F = wg.shape[0], wg.shape[-1]
    tm = min(tm, T)
    ni = T // tm
    rb = min(RB_FFN_FWD, tm)
    ng = len(gather)

    def body(*refs):
        x_ref, gpre_ref, gpost_ref, wg_ref, wu_ref, wd_ref = refs[:6]
        xo_ref, a_ref, b_ref, f_ref = refs[6 + ng:10 + ng]
        h_scr, acc_scr = refs[10 + 2 * ng:12 + 2 * ng]
        i = pl.program_id(0)
        j = pl.program_id(1)
        if ng:
            plan = _Gather(refs[10 + ng:10 + 2 * ng], refs[12 + 2 * ng:])
            pl.when(jnp.logical_and(i == 0, j == 0))(plan.start)
            pl.when(jnp.logical_and(i == (5 * ni) // 8, j == nj - 1))(plan.forward)

        @pl.when(j == 0)
        def _():
            xh, _ = _rms_stats(x_ref[...])
            h_scr[...] = (xh * gpre_ref[...]).astype(BF16)
            acc_scr[...] = jnp.zeros_like(acc_scr)

        for r0 in range(0, tm, rb):
            rows = slice(r0, r0 + rb)
            h = h_scr[rows, :]
            a = _dot(h, wg_ref[...]).astype(BF16)
            b = _dot(h, wu_ref[...]).astype(BF16)
            a_ref[rows, :] = a
            b_ref[rows, :] = b
            sg, _ = _sigmoid_pair(a)
            acc_scr[rows, :] += _dot((a * sg) * b, wd_ref[...])

        @pl.when(j == nj - 1)
        def _():
            f = acc_scr[...]
            f_ref[...] = f
            fh, _ = _rms_stats(f)
            xo_ref[...] = x_ref[...] + FFN_SCALE * (fh * gpost_ref[...])

        if ng:
            pl.when(jnp.logical_and(i == ni - 1, j == nj - 1))(plan.finish)

    row = pl.BlockSpec((tm, D), lambda i, j: (i, 0))
    vec = pl.BlockSpec((1, D), lambda i, j: (0, 0))
    w_in = _chunk_spec(sel, D, F)
    w_out = _chunk_spec(sel, F, D)
    act = pl.BlockSpec((None, tm, F), lambda i, j: (j, i, 0))
    hbm = pl.BlockSpec(memory_space=pl.ANY)
    return pl.pallas_call(
        body,
        name="ffn_fwd_gather" if ng else "ffn_fwd",
        grid=(ni, nj),
        in_specs=[row, vec, vec, w_in, w_in, w_out] + [hbm] * ng,
        out_specs=[row, act, act, row] + [hbm] * ng,
        out_shape=[
            jax.ShapeDtypeStruct((T, D), F32),
            jax.ShapeDtypeStruct((nj, T, F), BF16),
            jax.ShapeDtypeStruct((nj, T, F), BF16),
            jax.ShapeDtypeStruct((T, D), F32),
        ] + [jax.ShapeDtypeStruct(g.shape, g.dtype) for g in gather],
        input_output_aliases={6 + a: 4 + a for a in range(ng)},
        scratch_shapes=[pltpu.VMEM((tm, D), BF16), pltpu.VMEM((tm, D), F32)] + (_Gather.semaphores(ng) if ng else []),
        compiler_params=_cparams("arbitrary", "arbitrary"),
    )(x, g_pre, g_post, wg, wu, wd, *gather)


def _ffn_bwd(dy, x, f, a, b, g_pre, g_post, wg, wu, wd, sel, tm, scatter=()):
    T, D = x.shape
    nj, F = wg.shape[0], wg.shape[-1]
    tm = min(tm, T)
    ni = T // tm
    rb = min(RB_FFN_BWD, tm)
    ns = len(scatter)
    halves = [h for _, h in scatter]
    scatter = [g for g, _ in scatter]

    def body(*refs):
        dy_ref, x_ref, f_ref, a_ref, b_ref, gpre_ref, gpost_ref, wg_ref, wu_ref, wd_ref = refs[:10]
        dx_ref, h_ref, dz_ref, s_ref, da_ref, db_ref, dgpre_ref, dgpost_ref = refs[10 + ns:18 + ns]
        dh_scr = refs[18 + 2 * ns]
        i = pl.program_id(0)
        j = pl.program_id(1)
        if ns:
            plan = _Scatter(refs[10:10 + ns], refs[18 + ns:18 + 2 * ns], refs[19 + 2 * ns:], halves)
            pl.when(jnp.logical_and(i == 0, j == 0))(plan.start)

        @pl.when(j == 0)
        def _():
            fh, rf = _rms_stats(f_ref[...])
            dz, dg = _rms_bwd(fh, rf, gpost_ref[...], FFN_SCALE * dy_ref[...])
            dz_ref[...] = dz.astype(BF16)
            _acc_out(dgpost_ref, i == 0, dg)
            xh, _ = _rms_stats(x_ref[...])
            h_ref[...] = (xh * gpre_ref[...]).astype(BF16)
            dh_scr[...] = jnp.zeros_like(dh_scr)

        for r0 in range(0, tm, rb):
            rows = slice(r0, r0 + rb)
            ds = _dot_nt(dz_ref[rows, :], wd_ref[j]).astype(BF16)
            av = a_ref[rows, :]
            bv = b_ref[rows, :]
            sg, one_minus_sg = _sigmoid_pair(av)
            sl = av * sg
            s_ref[rows, :] = sl * bv
            da = (ds * bv) * (sg + sl * one_minus_sg)
            db = ds * sl
            da_ref[rows, :] = da
            db_ref[rows, :] = db
            dh_scr[rows, :] += _dot_nt(da, wg_ref[j]) + _dot_nt(db, wu_ref[j])

        @pl.when(j == nj - 1)
        def _():
            xh, rx = _rms_stats(x_ref[...])
            dxn, dg = _rms_bwd(xh, rx, gpre_ref[...], dh_scr[...])
            dx_ref[...] = dy_ref[...] + dxn
            _acc_out(dgpre_ref, i == 0, dg)

        if ns:
            pl.when(jnp.logical_and(i == ni - 1, j == nj - 1))(plan.finish)

    row = pl.BlockSpec((tm, D), lambda i, j: (i, 0))
    vec = pl.BlockSpec((1, D), lambda i, j: (0, 0))
    w_in = _chunks_spec(wg, sel)
    w_out = _chunks_spec(wd, sel)
    act = pl.BlockSpec((None, tm, F), lambda i, j: (j, i, 0))
    act_shape = jax.ShapeDtypeStruct((nj, T, F), BF16)
    hbm = pl.BlockSpec(memory_space=pl.ANY)
    return pl.pallas_call(
        body,
        name="ffn_bwd_scatter" if ns else "ffn_bwd",
        grid=(ni, nj),
        in_specs=[row, row, row, act, act, vec, vec, w_in, w_in, w_out] + [hbm] * ns,
        out_specs=[row, row, row, act, act, act, vec, vec] + [hbm] * ns,
        out_shape=[
            jax.ShapeDtypeStruct((T, D), F32),
            jax.ShapeDtypeStruct((T, D), BF16),
            jax.ShapeDtypeStruct((T, D), BF16),
            act_shape, act_shape, act_shape,
            jax.ShapeDtypeStruct((1, D), F32),
            jax.ShapeDtypeStruct((1, D), F32),
        ] + [jax.ShapeDtypeStruct((N_DEV - 1, *g.shape[2:]), g.dtype) for g in scatter],
        scratch_shapes=[pltpu.VMEM((tm, D), F32)] + (_Scatter.semaphores(ns) if ns else []),
        compiler_params=_cparams("arbitrary", "arbitrary"),
    )(dy, x, f, a, b, g_pre, g_post, wg, wu, wd, *scatter)


class _ScatterRider:
    name = "scatter"

    def __init__(self, grads, halves, into):
        n = len(grads)
        self.halves = halves
        self.operands = [*grads, *into]
        self.results = [jax.ShapeDtypeStruct(t.shape, t.dtype) for t in into]
        self.aliases = {n + a: a for a in range(n)}
        self.semaphores = _Scatter.semaphores(n)

    def plan(self, in_refs, out_refs, sems):
        return _Scatter(in_refs[:len(out_refs)], out_refs, sems, self.halves)


class _ShareRider:
    name = "share"

    def __init__(self, slots):
        self.operands = [slots]
        self.results = [jax.ShapeDtypeStruct(slots.shape, slots.dtype)]
        self.aliases = {0: 0}
        self.semaphores = _ShareAll.semaphores()

    def plan(self, in_refs, out_refs, sems):
        return _ShareAll(out_refs[0], sems)


def _tn_matmul(a, b, buf, like, sel, tk, rider=None):
    a_chunked = a.ndim == 3
    nj = a.shape[0] if a_chunked else b.shape[0]
    T, M, N = a.shape[-2], a.shape[-1], b.shape[-1]
    tk = min(tk, T)
    nk = T // tk
    have = buf is not None
    n_in = 2 + have + (len(rider.operands) if rider else 0)
    n_out = 1 + (len(rider.results) if rider else 0)

    def body(*refs):
        a_ref, b_ref = refs[:2]
        o_ref = refs[n_in]
        acc_scr = refs[n_in + n_out]
        j = pl.program_id(0)
        k = pl.program_id(1)
        if rider:
            plan = rider.plan(refs[2 + have:n_in], refs[n_in + 1:n_in + n_out], refs[n_in + n_out + 1:])
            pl.when(jnp.logical_and(j == 0, k == 0))(plan.start)

        @pl.when(k == 0)
        def _():
            acc_scr[...] = jnp.zeros_like(acc_scr)

        acc_scr[...] += _dot_tn(a_ref[...], b_ref[...])

        @pl.when(k == nk - 1)
        def _():
            o_ref[...] = acc_scr[...].astype(BF16)

        if rider:
            pl.when(jnp.logical_and(j == nj - 1, k == nk - 1))(plan.finish)

    def spec(chunked, width):
        if chunked:
            return pl.BlockSpec((None, tk, width), lambda j, k: (j, k, 0))
        return pl.BlockSpec((tk, width), lambda j, k: (k, 0))

    hbm = pl.BlockSpec(memory_space=pl.ANY)
    aliases = {2: 0} if have else {}
    if rider:
        aliases.update({2 + have + src: 1 + dst for src, dst in rider.aliases.items()})
    outs = pl.pallas_call(
        body,
        name="tn_matmul_" + rider.name if rider else "tn_matmul",
        grid=(nj, nk),
        in_specs=[spec(a_chunked, M), spec(not a_chunked, N)] + [hbm] * (n_in - 2),
        out_specs=[pl.BlockSpec((None,) * (1 + len(sel)) + (M, N), lambda j, k: (j, *sel, 0, 0))] + [hbm] * (n_out - 1),
        out_shape=[jax.ShapeDtypeStruct(like.shape, BF16)] + (list(rider.results) if rider else []),
        input_output_aliases=aliases,
        scratch_shapes=[pltpu.VMEM((M, N), F32)] + (rider.semaphores if rider else []),
        compiler_params=_cparams("arbitrary", "arbitrary"),
    )(a, b, *([buf] if have else []), *(rider.operands if rider else ()))
    return (outs[0], outs[1:]) if rider else outs[0]


def _sgu_fwd(x, g_pre, g_post, win, lng, lnb, wsm, bsb, wout, sel, tm):
    T, D = x.shape
    nc, E = win.shape[0], win.shape[-1]
    S = 2 * E
    dg = S // N_GROUPS
    wo_rows = wout.shape[-2]
    tm = min(tm, T)
    nq = tm // CHUNK

    def body(x_ref, gpre_ref, gpost_ref, win_ref, lng_ref, lnb_ref, ws_ref, bsb_ref, wout_ref,
             xo_ref, zp_ref, m_ref, u_scr, vn_scr, gt_scr):
        x = x_ref[...]
        xh, _ = _rms_stats(x)
        hn = (xh * gpre_ref[...]).astype(BF16)
        v_parts = []
        for c in range(nc):
            zp = _dot(hn, win_ref[c])
            zp_ref[c] = zp.astype(BF16)
            z = _gelu(zp)
            if c < nc // 2:
                u_scr[:, c * E:(c + 1) * E] = z
            else:
                v_parts.append(z)
        vh_parts, _ = _ln_stats(v_parts, S)
        for c, vh in enumerate(vh_parts):
            cols = slice(c * E, (c + 1) * E)
            vn_scr[:, cols] = (vh * lng_ref[:, cols] + lnb_ref[:, cols]).astype(BF16)
        for q in range(nq):
            rows = slice(q * CHUNK, (q + 1) * CHUNK)
            for g in range(N_GROUPS):
                cols = slice(g * dg, (g + 1) * dg)
                mixed = _dot(ws_ref[g], vn_scr[rows, cols]) + bsb_ref[g]
                gt_scr[rows, cols] = (u_scr[rows, cols] * mixed).astype(BF16)
        m = _dot(gt_scr[:, 0:wo_rows], wout_ref[0])
        for c in range(1, nc):
            m += _dot(gt_scr[:, c * wo_rows:(c + 1) * wo_rows], wout_ref[c])
        m_ref[...] = m
        mh, _ = _rms_stats(m)
        xo_ref[...] = x + mh * gpost_ref[...]

    row = pl.BlockSpec((tm, D), lambda i: (i, 0))
    return pl.pallas_call(
        body,
        name="sgu_fwd",
        grid=(T // tm,),
        in_specs=[row, _resident((1, D)), _resident((1, D)), _chunks_spec(win, sel), _resident((1, S)),
                  _resident((1, S)), _resident(wsm.shape), _resident(bsb.shape), _chunks_spec(wout, sel)],
        out_specs=[row, pl.BlockSpec((nc, tm, E), lambda i: (0, i, 0)), row],
        out_shape=[
            jax.ShapeDtypeStruct((T, D), F32),
            jax.ShapeDtypeStruct((nc, T, E), BF16),
            jax.ShapeDtypeStruct((T, D), F32),
        ],
        scratch_shapes=[pltpu.VMEM((tm, S), F32), pltpu.VMEM((tm, S), BF16), pltpu.VMEM((tm, S), BF16)],
        compiler_params=_cparams("parallel"),
    )(x, g_pre, g_post, win, lng, lnb, wsm, bsb, wout)


def _sgu_bwd(dy, x, m, zp, g_pre, g_post, win, lng, lnb, wsm, wsmt, bsb, wout, sel, tm):
    T, D = x.shape
    nc, E = win.shape[0], win.shape[-1]
    S = 2 * E
    dg = S // N_GROUPS
    wo_rows = wout.shape[-2]
    tm = min(tm, T)
    nq = tm // CHUNK

    def body(dy_ref, x_ref, m_ref, zp_ref, gpre_ref, gpost_ref, win_ref, lng_ref, lnb_ref, ws_ref, wst_ref,
             bsb_ref, wout_ref,
             dx_ref, hn_ref, dzp_ref, gated_ref, dm_ref, dws_ref, dbs_ref, dlng_ref, dlnb_ref, dgpre_ref,
             dgpost_ref, u_scr, d_scr, vh_scr, vn_scr, gg_scr):
        first = pl.program_id(0) == 0
        dy = dy_ref[...]
        mh, rm = _rms_stats(m_ref[...])
        dm, dgp = _rms_bwd(mh, rm, gpost_ref[...], dy)
        _acc_out(dgpost_ref, first, dgp)
        dm = dm.astype(BF16)
        dm_ref[...] = dm
        for c in range(nc):
            d_scr[:, c * wo_rows:(c + 1) * wo_rows] = _dot_nt(dm, wout_ref[c])
        v_parts = []
        for c in range(nc):
            z, gg_scr[c] = _gelu_pair(zp_ref[c])
            if c < nc // 2:
                u_scr[:, c * E:(c + 1) * E] = z.astype(F32)
            else:
                v_parts.append(z.astype(F32))
        vh_parts, rstd = _ln_stats(v_parts, S)
        for c, vh in enumerate(vh_parts):
            cols = slice(c * E, (c + 1) * E)
            vh_scr[:, cols] = vh
            vn_scr[:, cols] = (vh * lng_ref[:, cols] + lnb_ref[:, cols]).astype(BF16)

        @pl.when(first)
        def _():
            dws_ref[...] = jnp.zeros_like(dws_ref)
            dbs_ref[...] = jnp.zeros_like(dbs_ref)
            dlng_ref[...] = jnp.zeros_like(dlng_ref)
            dlnb_ref[...] = jnp.zeros_like(dlnb_ref)

        for q in range(nq):
            rows = slice(q * CHUNK, (q + 1) * CHUNK)
            for g in range(N_GROUPS):
                cols = slice(g * dg, (g + 1) * dg)
                vn = vn_scr[rows, cols]
                mixed = _dot(ws_ref[g], vn) + bsb_ref[g]
                u = u_scr[rows, cols]
                dgt = d_scr[rows, cols]
                gated_ref[(g * dg) // wo_rows, rows, (g * dg) % wo_rows:(g * dg) % wo_rows + dg] = (u * mixed).astype(BF16)
                dmix = dgt * u
                dbs_ref[:, cols] += dmix
                dmix = dmix.astype(BF16)
                dws_ref[g] += _dot_nt(dmix, vn)
                u_scr[rows, cols] = dgt * mixed
                d_scr[rows, cols] = _dot(wst_ref[g], dmix)
        dvn = [d_scr[:, c * E:(c + 1) * E] for c in range(nc // 2)]
        vh = [vh_scr[:, c * E:(c + 1) * E] for c in range(nc // 2)]
        for c, (d, v) in enumerate(zip(dvn, vh)):
            dlng_ref[:, c * E:(c + 1) * E] += jnp.sum(d * v, axis=0, keepdims=True)
            dlnb_ref[:, c * E:(c + 1) * E] += jnp.sum(d, axis=0, keepdims=True)
        dvh = [d * lng_ref[:, c * E:(c + 1) * E] for c, d in enumerate(dvn)]
        dv = _ln_bwd(vh, rstd, dvh, S)
        dhn = None
        for c in range(nc):
            dz = u_scr[:, c * E:(c + 1) * E] if c < nc // 2 else dv[c - nc // 2]
            dzp = dz.astype(BF16) * gg_scr[c]
            dzp_ref[c] = dzp
            t = _dot_nt(dzp, win_ref[c])
            dhn = t if dhn is None else dhn + t
        xh, rx = _rms_stats(x_ref[...])
        hn_ref[...] = (xh * gpre_ref[...]).astype(BF16)
        dxn, dgq = _rms_bwd(xh, rx, gpre_ref[...], dhn)
        dx_ref[...] = dy + dxn
        _acc_out(dgpre_ref, first, dgq)

    row = pl.BlockSpec((tm, D), lambda i: (i, 0))

    def whole(shape):
        return pl.BlockSpec(shape, lambda i: (0,) * len(shape))

    return pl.pallas_call(
        body,
        name="sgu_bwd",
        grid=(T // tm,),
        in_specs=[row, row, row, pl.BlockSpec((nc, tm, E), lambda i: (0, i, 0)), _resident((1, D)), _resident((1, D)),
                  _chunks_spec(win, sel), _resident((1, S)), _resident((1, S)), _resident(wsm.shape),
                  _resident(wsmt.shape), _resident(bsb.shape), _chunks_spec(wout, sel)],
        out_specs=[row, row, pl.BlockSpec((nc, tm, E), lambda i: (0, i, 0)),
                   pl.BlockSpec((nc, tm, wo_rows), lambda i: (0, i, 0)), row,
                   whole((N_GROUPS, CHUNK, CHUNK)), whole((CHUNK, S)), whole((1, S)), whole((1, S)),
                   whole((1, D)), whole((1, D))],
        out_shape=[
            jax.ShapeDtypeStruct((T, D), F32),
            jax.ShapeDtypeStruct((T, D), BF16),
            jax.ShapeDtypeStruct((nc, T, E), BF16),
            jax.ShapeDtypeStruct((nc, T, wo_rows), BF16),
            jax.ShapeDtypeStruct((T, D), BF16),
            jax.ShapeDtypeStruct((N_GROUPS, CHUNK, CHUNK), F32),
            jax.ShapeDtypeStruct((CHUNK, S), F32),
            jax.ShapeDtypeStruct((1, S), F32),
            jax.ShapeDtypeStruct((1, S), F32),
            jax.ShapeDtypeStruct((1, D), F32),
            jax.ShapeDtypeStruct((1, D), F32),
        ],
        scratch_shapes=[pltpu.VMEM((tm, S), F32), pltpu.VMEM((tm, S), F32), pltpu.VMEM((tm, S), F32),
                        pltpu.VMEM((tm, S), BF16), pltpu.VMEM((nc, tm, E), BF16)],
        compiler_params=_cparams("arbitrary"),
    )(dy, x, m, zp, g_pre, g_post, win, lng, lnb, wsm, wsmt, bsb, wout)


def _shifted_windows(buf, r0, cols, lo, hi):
    n = CONV_RB + HALO
    base = buf[r0:r0 + n, cols]
    for r in range(8):
        rolled = base if r == 0 else pltpu.roll(base, n - r, axis=0)
        for s in range(r, hi, 8):
            if s >= lo:
                yield s, rolled[s - r:s - r + CONV_RB]


def _conv_fwd_a(x, g_pre, wpw1, sel, tm):
    T, D = x.shape
    nc, E = wpw1.shape[0], wpw1.shape[-1]
    C = 2 * E
    tm = min(tm, T)

    def body(x_ref, gpre_ref, w_ref, y_ref, p_ref):
        xh, _ = _rms_stats(x_ref[...])
        hn = (xh * gpre_ref[...]).astype(BF16)
        ps = []
        for c in range(nc):
            p = _dot(hn, w_ref[c])
            p_ref[c] = p.astype(BF16)
            ps.append(p)
        for c in range(nc // 2):
            y_ref[:, c * E:(c + 1) * E] = ps[c] * jax.nn.sigmoid(ps[c + nc // 2])

    row = pl.BlockSpec((tm, D), lambda i: (i, 0))
    return pl.pallas_call(
        body,
        name="conv_fwd_a",
        grid=(T // tm,),
        in_specs=[row, _resident((1, D)), _chunks_spec(wpw1, sel)],
        out_specs=[pl.BlockSpec((tm, C), lambda i: (i, 0)), pl.BlockSpec((nc, tm, E), lambda i: (0, i, 0))],
        out_shape=[jax.ShapeDtypeStruct((T, C), F32), jax.ShapeDtypeStruct((nc, T, E), BF16)],
        compiler_params=_cparams("parallel"),
    )(x, g_pre, wpw1)


def _conv_fwd_b(x, y, wdw, bdw, lng, lnb, wpw2, g_post, sel, tm):
    T, D = x.shape
    C = y.shape[1]
    nc, E = wpw2.shape[0], wpw2.shape[-2]
    tm = min(tm, T)
    per = tm // HALO

    def body(x_ref, y_ref, yprev_ref, wdw_ref, bdw_ref, lng_ref, lnb_ref, w_ref, gpost_ref,
             xo_ref, c_ref, m_ref, ybuf):
        i = pl.program_id(0)
        ybuf[0:HALO, :] = jnp.where(i > 0, yprev_ref[...], 0.0)
        ybuf[HALO:HALO + tm, :] = y_ref[...]
        off = HALO - (CONV_W - 1)
        for r0 in range(0, tm, CONV_RB):
            for c0 in range(0, C, CONV_CB):
                cols = slice(c0, c0 + CONV_CB)
                acc = jnp.broadcast_to(bdw_ref[:, cols], (CONV_RB, CONV_CB))
                for s, win in _shifted_windows(ybuf, r0, cols, off, off + CONV_W):
                    acc = acc + wdw_ref[s - off:s - off + 1, cols] * win
                c_ref[r0:r0 + CONV_RB, cols] = acc
        (ch,), _ = _ln_stats([c_ref[...]], C)
        cn = ch * lng_ref[...] + lnb_ref[...]
        qv = (cn * jax.nn.sigmoid(cn)).astype(BF16)
        m = _dot(qv[:, 0:E], w_ref[0])
        for c in range(1, nc):
            m += _dot(qv[:, c * E:(c + 1) * E], w_ref[c])
        m_ref[...] = m
        mh, _ = _rms_stats(m)
        xo_ref[...] = x_ref[...] + mh * gpost_ref[...]

    row = pl.BlockSpec((tm, D), lambda i: (i, 0))
    crow = pl.BlockSpec((tm, C), lambda i: (i, 0))
    prev = pl.BlockSpec((HALO, C), lambda i: (jnp.maximum(i * per - 1, 0), 0))
    return pl.pallas_call(
        body,
        name="conv_fwd_b",
        grid=(T // tm,),
        in_specs=[row, crow, prev, _resident(wdw.shape), _resident((1, C)), _resident((1, C)), _resident((1, C)),
                  _chunks_spec(wpw2, sel), _resident((1, D))],
        out_specs=[row, crow, row],
        out_shape=[jax.ShapeDtypeStruct((T, D), F32), jax.ShapeDtypeStruct((T, C), F32),
                   jax.ShapeDtypeStruct((T, D), F32)],
        scratch_shapes=[pltpu.VMEM((HALO + tm, C), F32)],
        compiler_params=_cparams("parallel"),
    )(x, y, y, wdw, bdw, lng, lnb, wpw2, g_post)


def _conv_bwd_b(dy, m, c, lng, lnb, wpw2, g_post, sel, tm):
    T, D = dy.shape
    C = c.shape[1]
    nc, E = wpw2.shape[0], wpw2.shape[-2]
    tm = min(tm, T)

    def body(dy_ref, m_ref, c_ref, lng_ref, lnb_ref, w_ref, gpost_ref,
             dm_ref, q_ref, dc_ref, dlng_ref, dlnb_ref, dbdw_ref, dgpost_ref, dq_scr):
        first = pl.program_id(0) == 0
        mh, rm = _rms_stats(m_ref[...])
        dm, dgp = _rms_bwd(mh, rm, gpost_ref[...], dy_ref[...])
        _acc_out(dgpost_ref, first, dgp)
        dm = dm.astype(BF16)
        dm_ref[...] = dm
        for k in range(nc):
            dq_scr[:, k * E:(k + 1) * E] = _dot_nt(dm, w_ref[k])
        (ch,), rstd = _ln_stats([c_ref[...]], C)
        cn = ch * lng_ref[...] + lnb_ref[...]
        sg = jax.nn.sigmoid(cn)
        qv = (cn * sg).astype(BF16)
        for k in range(nc):
            q_ref[k] = qv[:, k * E:(k + 1) * E]
        dcn = dq_scr[...] * (sg * (1.0 + cn * (1.0 - sg)))
        _acc_out(dlng_ref, first, jnp.sum(dcn * ch, axis=0, keepdims=True))
        _acc_out(dlnb_ref, first, jnp.sum(dcn, axis=0, keepdims=True))
        (dc,) = _ln_bwd([ch], rstd, [dcn * lng_ref[...]], C)
        dc_ref[...] = dc
        _acc_out(dbdw_ref, first, jnp.sum(dc, axis=0, keepdims=True))

    row = pl.BlockSpec((tm, D), lambda i: (i, 0))
    crow = pl.BlockSpec((tm, C), lambda i: (i, 0))

    def whole(shape):
        return pl.BlockSpec(shape, lambda i: (0,) * len(shape))

    return pl.pallas_call(
        body,
        name="conv_bwd_b",
        grid=(T // tm,),
        in_specs=[row, row, crow, _resident((1, C)), _resident((1, C)), _chunks_spec(wpw2, sel), _resident((1, D))],
        out_specs=[row, pl.BlockSpec((nc, tm, E), lambda i: (0, i, 0)), crow, whole((1, C)), whole((1, C)),
                   whole((1, C)), whole((1, D))],
        out_shape=[jax.ShapeDtypeStruct((T, D), BF16), jax.ShapeDtypeStruct((nc, T, E), BF16),
                   jax.ShapeDtypeStruct((T, C), F32), jax.ShapeDtypeStruct((1, C), F32),
                   jax.ShapeDtypeStruct((1, C), F32), jax.ShapeDtypeStruct((1, C), F32),
                   jax.ShapeDtypeStruct((1, D), F32)],
        scratch_shapes=[pltpu.VMEM((tm, C), F32)],
        compiler_params=_cparams("arbitrary"),
    )(dy, m, c, lng, lnb, wpw2, g_post)


def _conv_bwd_a(dy, x, dc, y, p, g_pre, wdw, wpw1, sel, tm):
    T, D = x.shape
    C = y.shape[1]
    nc, E = wpw1.shape[0], wpw1.shape[-1]
    tm = min(tm, T)
    per = tm // HALO
    n_tiles = T // tm
    KP = wdw.shape[0]

    def body(dy_ref, x_ref, dc_ref, dcnext_ref, y_ref, yprev_ref, p_ref, gpre_ref, wdw_ref, w_ref,
             dx_ref, hn_ref, dp_ref, dwdw_ref, dgpre_ref, ybuf, dcbuf, dyg_scr, dw8_scr):
        i = pl.program_id(0)
        first = i == 0
        ybuf[0:HALO, :] = jnp.where(i > 0, yprev_ref[...], 0.0)
        ybuf[HALO:HALO + tm, :] = y_ref[...]
        dcbuf[0:tm, :] = dc_ref[...]
        dcbuf[tm:tm + HALO, :] = jnp.where(i < n_tiles - 1, dcnext_ref[...], 0.0)
        off = HALO - (CONV_W - 1)
        @pl.when(first)
        def _():
            dw8_scr[...] = jnp.zeros_like(dw8_scr)

        for r0 in range(0, tm, CONV_RB):
            for c0 in range(0, C, CONV_CB):
                cols = slice(c0, c0 + CONV_CB)
                dcb = dcbuf[r0:r0 + CONV_RB, cols]
                acc = jnp.zeros((CONV_RB, CONV_CB), F32)
                for s, win in _shifted_windows(dcbuf, r0, cols, 0, CONV_W):
                    k = CONV_W - 1 - s
                    acc = acc + wdw_ref[k:k + 1, cols] * win
                dyg_scr[r0:r0 + CONV_RB, cols] = acc
                for s, win in _shifted_windows(ybuf, r0, cols, off, off + CONV_W):
                    dw8_scr[s - off, :, cols] += jnp.sum((dcb * win).reshape(CONV_RB // 8, 8, CONV_CB), axis=0)

        @pl.when(i == n_tiles - 1)
        def _():
            dwdw_ref[...] = jnp.sum(dw8_scr[...], axis=1)

        dhn = None
        for c in range(nc // 2):
            cols = slice(c * E, (c + 1) * E)
            av = p_ref[c].astype(F32)
            sg = jax.nn.sigmoid(p_ref[c + nc // 2].astype(F32))
            dygc = dyg_scr[:, cols]
            da = (dygc * sg).astype(BF16)
            dgt = (dygc * av * sg * (1.0 - sg)).astype(BF16)
            dp_ref[c] = da
            dp_ref[c + nc // 2] = dgt
            t = _dot_nt(da, w_ref[c]) + _dot_nt(dgt, w_ref[c + nc // 2])
            dhn = t if dhn is None else dhn + t
        xh, rx = _rms_stats(x_ref[...])
        hn_ref[...] = (xh * gpre_ref[...]).astype(BF16)
        dxn, dgq = _rms_bwd(xh, rx, gpre_ref[...], dhn)
        dx_ref[...] = dy_ref[...] + dxn
        _acc_out(dgpre_ref, first, dgq)

    row = pl.BlockSpec((tm, D), lambda i: (i, 0))
    crow = pl.BlockSpec((tm, C), lambda i: (i, 0))
    prev = pl.BlockSpec((HALO, C), lambda i: (jnp.maximum(i * per - 1, 0), 0))
    nxt = pl.BlockSpec((HALO, C), lambda i: (jnp.minimum((i + 1) * per, T // HALO - 1), 0))
    chunks = pl.BlockSpec((nc, tm, E), lambda i: (0, i, 0))

    def whole(shape):
        return pl.BlockSpec(shape, lambda i: (0,) * len(shape))

    return pl.pallas_call(
        body,
        name="conv_bwd_a",
        grid=(n_tiles,),
        in_specs=[row, row, crow, nxt, crow, prev, chunks, _resident((1, D)), _resident(wdw.shape),
                  _chunks_spec(wpw1, sel)],
        out_specs=[row, row, chunks, whole((KP, C)), whole((1, D))],
        out_shape=[jax.ShapeDtypeStruct((T, D), F32), jax.ShapeDtypeStruct((T, D), BF16),
                   jax.ShapeDtypeStruct((nc, T, E), BF16), jax.ShapeDtypeStruct((KP, C), F32),
                   jax.ShapeDtypeStruct((1, D), F32)],
        scratch_shapes=[pltpu.VMEM((HALO + tm, C), F32), pltpu.VMEM((tm + HALO, C), F32),
                        pltpu.VMEM((tm, C), F32), pltpu.VMEM((KP, 8, C), F32)],
        compiler_params=_cparams("arbitrary"),
    )(dy, x, dc, dc, y, y, p, g_pre, wdw, wpw1)


def _loss_head(y, target, tm):
    T, D = y.shape
    tm = min(tm, T)

    def body(y_ref, t_ref, dy_ref, loss_ref):
        e = y_ref[...] - t_ref[...]
        dy_ref[...] = e * (1.0 / D)
        part = jnp.sum(jnp.sum(e * e, axis=-1, keepdims=True), axis=0, keepdims=True) * (0.5 / D)
        _acc_out(loss_ref, pl.program_id(0) == 0, jnp.broadcast_to(part, loss_ref.shape))

    row = pl.BlockSpec((tm, D), lambda i: (i, 0))
    return pl.pallas_call(
        body,
        name="loss_head",
        grid=(T // tm,),
        in_specs=[row, row],
        out_specs=[row, pl.BlockSpec((8, 128), lambda i: (0, 0))],
        out_shape=[jax.ShapeDtypeStruct((T, D), F32), jax.ShapeDtypeStruct((8, 128), F32)],
        compiler_params=_cparams("arbitrary"),
    )(y, target)


def _row_tile(rows, cols, itemsize_budget=2 * 1024 * 1024):
    want = max(16, itemsize_budget // (4 * cols))
    if rows <= want:
        return rows
    t = (want // 16) * 16
    while t > 16 and rows % t:
        t -= 16
    return t if rows % t == 0 else rows


def _sum_parts(parts):
    n, R, C = parts.shape
    tr = _row_tile(R, C * n // 2 if parts.dtype == BF16 else C * n)

    def body(p_ref, o_ref):
        acc = p_ref[0].astype(F32)
        for s in range(1, n):
            acc = acc + p_ref[s].astype(F32)
        o_ref[...] = acc

    return pl.pallas_call(
        body,
        name="sum_parts",
        grid=(R // tr,),
        in_specs=[pl.BlockSpec((n, tr, C), lambda i: (0, i, 0))],
        out_specs=pl.BlockSpec((tr, C), lambda i: (i, 0)),
        out_shape=jax.ShapeDtypeStruct((R, C), F32),
        compiler_params=_cparams("parallel"),
    )(parts)


def _cast_into_slot(w, entry):
    _, _, R, C = w.shape
    tr = _row_tile(R, C)

    def body(w_ref, o_ref):
        o_ref[...] = w_ref[...].astype(BF16)

    def own_slot(h, i):
        return 2 * lax.axis_index("x") + lax.axis_index("y"), h, i, 0

    return pl.pallas_call(
        body,
        name="cast_into_slot",
        grid=(2, R // tr),
        in_specs=[pl.BlockSpec((None, None, tr, C), lambda h, i: (entry, h, i, 0))],
        out_specs=pl.BlockSpec((None, None, tr, C), own_slot),
        out_shape=jax.ShapeDtypeStruct((N_CHIPS, 2, R, C), BF16),
        compiler_params=_cparams("parallel", "parallel"),
    )(w)


def _sum_with_own(arrived, own):
    n, R, C = arrived.shape
    tr = _row_tile(R, C * (n + 1) // 2)

    def body(a_ref, own_ref, o_ref):
        acc = own_ref[...].astype(F32)
        for s in range(n):
            acc = acc + a_ref[s].astype(F32)
        o_ref[...] = acc

    def own_piece(i):
        return 2 * lax.axis_index("x") + lax.axis_index("y"), lax.axis_index("c"), i, 0

    return pl.pallas_call(
        body,
        name="sum_with_own",
        grid=(R // tr,),
        in_specs=[pl.BlockSpec((n, tr, C), lambda i: (0, i, 0)), pl.BlockSpec((None, None, tr, C), own_piece)],
        out_specs=pl.BlockSpec((None, tr, C), lambda i: (lax.axis_index("c"), i, 0)),
        out_shape=jax.ShapeDtypeStruct((2, R, C), F32),
        compiler_params=_cparams("parallel"),
    )(arrived, own)


def _adamw(w, g, m, v):
    R, C = w.shape
    tr = _row_tile(R, C * 7 // 2)
    c1 = 1.0 - ADAM_B1 ** ADAM_STEP
    c2 = 1.0 - ADAM_B2 ** ADAM_STEP

    def body(w_ref, g_ref, m_ref, v_ref, d_ref, mo_ref, vo_ref):
        g = g_ref[...]
        mn = ADAM_B1 * m_ref[...] + (1.0 - ADAM_B1) * g
        vn = ADAM_B2 * v_ref[...] + (1.0 - ADAM_B2) * (g * g)
        mo_ref[...] = mn
        vo_ref[...] = vn
        d_ref[...] = -ADAM_LR * ((mn / c1) / (jnp.sqrt(vn / c2) + ADAM_EPS) + ADAM_WD * w_ref[...])

    blk = pl.BlockSpec((tr, C), lambda i: (i, 0))
    shp = jax.ShapeDtypeStruct((R, C), F32)
    return pl.pallas_call(
        body,
        name="adamw",
        grid=(R // tr,),
        in_specs=[blk, blk, blk, blk],
        out_specs=[blk, blk, blk],
        out_shape=[shp, shp, shp],
        compiler_params=_cparams("parallel"),
    )(w, g, m, v)


def _adamw_into(w, g, m, v, outs, sel):
    n, R, C = w.shape
    tr = _row_tile(R, C)
    c1 = 1.0 - ADAM_B1 ** ADAM_STEP
    c2 = 1.0 - ADAM_B2 ** ADAM_STEP

    def body(w_ref, g_ref, m_ref, v_ref, *rest):
        go_ref, d_ref, mo_ref, vo_ref = rest[-4:]
        g = g_ref[...]
        mn = ADAM_B1 * m_ref[...] + (1.0 - ADAM_B1) * g
        vn = ADAM_B2 * v_ref[...] + (1.0 - ADAM_B2) * (g * g)
        go_ref[...] = g
        mo_ref[...] = mn
        vo_ref[...] = vn
        d_ref[...] = -ADAM_LR * ((mn / c1) / (jnp.sqrt(vn / c2) + ADAM_EPS) + ADAM_WD * w_ref[...])

    entry = pl.BlockSpec((None, tr, C), lambda i: (sel, i, 0))
    hbm = pl.BlockSpec(memory_space=pl.ANY)
    have = outs is not None
    shp = jax.ShapeDtypeStruct((n, R, C), F32)
    return pl.pallas_call(
        body,
        name="adamw_into",
        grid=(R // tr,),
        in_specs=[entry, pl.BlockSpec((tr, C), lambda i: (i, 0)), entry, entry] + ([hbm] * 4 if have else []),
        out_specs=[entry] * 4,
        out_shape=[shp] * 4,
        input_output_aliases={4 + t: t for t in range(4)} if have else {},
        compiler_params=_cparams("parallel"),
    )(w, g, m, v, *(outs if have else ()))


def _gather_weights(halved, whole):
    nh, nw = len(halved), len(whole)

    def body(*refs):
        w_in = refs[nh:nh + nw]
        h_out, w_out = refs[nh + nw:2 * nh + nw], refs[2 * nh + nw:2 * (nh + nw)]
        ws_send, ws_recv, loc_sem = refs[2 * (nh + nw):2 * (nh + nw) + 3]
        plan = _Gather(h_out, refs[2 * (nh + nw) + 3:])
        x, y, c = _my_place()
        me_chip = 2 * x + y
        plan.start()

        def small(a, j, slot, to):
            return pltpu.make_async_remote_copy(src_ref=w_in[a], dst_ref=w_out[a].at[slot],
                                                send_sem=ws_send.at[a, j], recv_sem=ws_recv.at[a, j],
                                                device_id=to, device_id_type=MESH)

        for a in range(nw):
            pltpu.make_async_copy(w_in[a], w_out[a].at[me_chip], loc_sem.at[a]).start()
            for j, ch in enumerate(plan.chips):
                small(a, j, me_chip, (*ch, c)).start()
        plan.forward()
        plan.finish()
        for a in range(nw):
            for j, ch in enumerate(plan.chips):
                cp = small(a, j, 2 * ch[0] + ch[1], (x, y, c))
                cp.wait_recv()
                cp.wait_send()
            pltpu.make_async_copy(w_in[a], w_out[a].at[me_chip], loc_sem.at[a]).wait()

    hbm = pl.BlockSpec(memory_space=pl.ANY)
    outs = pl.pallas_call(
        body,
        name="gather_weights",
        in_specs=[hbm] * (nh + nw),
        out_specs=[hbm] * (nh + nw),
        out_shape=[jax.ShapeDtypeStruct(a.shape, a.dtype) for a in halved]
        + [jax.ShapeDtypeStruct((N_CHIPS, *a.shape), a.dtype) for a in whole],
        input_output_aliases={a: a for a in range(nh)},
        scratch_shapes=[pltpu.SemaphoreType.DMA((max(nw, 1), 3)), pltpu.SemaphoreType.DMA((max(nw, 1), 3)),
                        pltpu.SemaphoreType.DMA((max(nw, 1),))] + _Gather.semaphores(nh),
    )(*halved, *whole)
    return outs[:nh], outs[nh:]


def _scatter_grads(grads, halves, into):
    n = len(grads)

    def body(*refs):
        plan = _Scatter(refs[:n], refs[2 * n:3 * n], refs[3 * n:], halves)
        plan.start()
        plan.finish()

    hbm = pl.BlockSpec(memory_space=pl.ANY)
    return pl.pallas_call(
        body,
        name="scatter_grads",
        in_specs=[hbm] * (2 * n),
        out_specs=[hbm] * n,
        out_shape=[jax.ShapeDtypeStruct(t.shape, t.dtype) for t in into],
        input_output_aliases={n + a: a for a in range(n)},
        scratch_shapes=_Scatter.semaphores(n),
    )(*grads, *into)


def _swap_halves(halves):
    n = len(halves)

    def body(*refs):
        h_out = refs[n:2 * n]
        send_sem, recv_sem = refs[2 * n:]
        x, y, c = _my_place()
        sib = (x, y, 1 - c)
        for a in range(n):
            pltpu.make_async_remote_copy(src_ref=h_out[a].at[c], dst_ref=h_out[a].at[c], send_sem=send_sem.at[a],
                                         recv_sem=recv_sem.at[a], device_id=sib, device_id_type=MESH).start()
        for a in range(n):
            cp = pltpu.make_async_remote_copy(src_ref=h_out[a].at[c], dst_ref=h_out[a].at[1 - c],
                                              send_sem=send_sem.at[a], recv_sem=recv_sem.at[a], device_id=sib,
                                              device_id_type=MESH)
            cp.wait_send()
            cp.wait_recv()

    hbm = pl.BlockSpec(memory_space=pl.ANY)
    return pl.pallas_call(
        body,
        name="swap_halves",
        in_specs=[hbm] * n,
        out_specs=[hbm] * n,
        out_shape=[jax.ShapeDtypeStruct(h.shape, h.dtype) for h in halves],
        input_output_aliases={a: a for a in range(n)},
        scratch_shapes=[pltpu.SemaphoreType.DMA((n,)), pltpu.SemaphoreType.DMA((n,))],
    )(*halves)


TM_FFN = 512
TM_FFN_FWD = 1024
RB_FFN_FWD = 512
RB_FFN_BWD = 256
TM_SGU = 256
TM_SGU_FWD = 256
TM_CONV = 256
TK_WGRAD = 4096
TM_LOSS = 1024


FFN_KINDS = ("ff_w_gate", "ff_w_up", "ff_w_down")


def _layer_kinds(i):
    return FFN_KINDS + (("sgu_w_in", "sgu_w_out") if i % 2 == 0 else ("conv_w_pw1", "conv_w_pw2"))


def _local_step(x, target, G, W, exchange=None, pack_small=None):
    depth = W["norm_g"].shape[0]
    G = [dict(g) for g in G]
    saved = []
    vec = lambda v: v.reshape(1, -1)

    def mixer_w(i, k):
        w = G[i][k]
        return w.reshape(w.shape[0], -1, w.shape[-1])
    wsm, wsmt, bsb = [], [], []
    n_sgu = W["sgu_w_spatial"].shape[0]
    causal = jnp.tril(jnp.ones((CHUNK, CHUNK), dtype=bool))
    dgrp = W["sgu_ln_g"].shape[1] // N_GROUPS
    for jx in range(n_sgu):
        ws = jnp.where(causal[None], W["sgu_w_spatial"][jx], 0.0).astype(BF16)
        wsm.append(ws)
        wsmt.append(jnp.swapaxes(ws, 1, 2))
        bsb.append(jnp.broadcast_to(W["sgu_b_spatial"][jx][:, :, None], (N_GROUPS, CHUNK, dgrp)))
    kp = HALO
    wdw = [jnp.pad(W["conv_w_dw"][jx], ((0, kp - CONV_W), (0, 0))) for jx in range(W["conv_w_dw"].shape[0])]

    def ffn(x, i, f_idx, gather=()):
        g = W["norm_g"][i]
        return _ffn_fwd(x, vec(g[4 * f_idx]), vec(g[4 * f_idx + 1]), G[i]["ff_w_gate"], G[i]["ff_w_up"],
                        G[i]["ff_w_down"], (f_idx,), TM_FFN_FWD, gather)

    for i in range(depth):
        g = W["norm_g"][i]
        rec = {"x0": x}
        if exchange is not None and i + 1 < depth:
            kinds = _layer_kinds(i + 1)
            x, rec["a1"], rec["b1"], rec["f1"], *filled = ffn(x, i, 0, [G[i + 1][k] for k in kinds])
            G[i + 1] = dict(zip(kinds, filled))
        else:
            x, rec["a1"], rec["b1"], rec["f1"] = ffn(x, i, 0)
        rec["x1"] = x
        j = i // 2
        if i % 2 == 0:
            x, rec["zp"], rec["m"] = _sgu_fwd(
                x, vec(g[2]), vec(g[3]), mixer_w(i, "sgu_w_in"), vec(W["sgu_ln_g"][j]), vec(W["sgu_ln_b"][j]),
                wsm[j], bsb[j], mixer_w(i, "sgu_w_out"), (), TM_SGU_FWD)
        else:
            rec["y"], rec["p"] = _conv_fwd_a(x, vec(g[2]), mixer_w(i, "conv_w_pw1"), (), TM_CONV)
            x, rec["c"], rec["m"] = _conv_fwd_b(
                x, rec["y"], wdw[j], vec(W["conv_b_dw"][j]), vec(W["conv_ln_g"][j]), vec(W["conv_ln_b"][j]),
                mixer_w(i, "conv_w_pw2"), vec(g[3]), (), TM_CONV)
        rec["x2"] = x
        x, rec["a2"], rec["b2"], rec["f2"] = ffn(x, i, 1)
        saved.append(rec)

    dx, loss_tile = _loss_head(x, target, TM_LOSS)
    loss = loss_tile[0, 0]

    big = [{k: None for k in _layer_kinds(i)} for i in range(depth)]
    small = {k: [None] * W[k].shape[0] for k in
             ("sgu_ln_g", "sgu_ln_b", "sgu_w_spatial", "sgu_b_spatial", "conv_w_dw", "conv_b_dw", "conv_ln_g",
              "conv_ln_b")}
    dnorm = [[None] * 6 for _ in range(depth)]
    pieces = [None] * depth
    waiting = []

    def wgrad(i, k, a, b, sel, rider=None):
        like = G[i][k] if sel else mixer_w(i, k)
        out = _tn_matmul(a, b, big[i][k], like, sel, TK_WGRAD, rider)
        big[i][k], rode = out if rider else (out, None)
        return rode

    def as_pieces(b):
        return b.reshape(N_CHIPS, 2, -1, b.shape[-1])

    def small_grads():
        out = {k: jnp.stack(v) for k, v in small.items()}
        out["norm_g"] = jnp.stack([jnp.stack(r) for r in dnorm])
        return out

    def ffn_back(dx, i, f_idx, xin, a, b, f, send, last=False):
        g = W["norm_g"][i]
        dx, h, dz, s, da, db, dgpre, dgpost, *arrived = _ffn_bwd(
            dx, xin, f, a, b, vec(g[4 * f_idx]), vec(g[4 * f_idx + 1]),
            G[i]["ff_w_gate"], G[i]["ff_w_up"], G[i]["ff_w_down"], (f_idx,), TM_FFN, list(send.values()))
        dnorm[i][4 * f_idx] = dgpre[0]
        dnorm[i][4 * f_idx + 1] = dgpost[0]
        arrived = dict(zip(send, arrived))
        if not last:
            wgrad(i, "ff_w_gate", h, da, (f_idx,))
            wgrad(i, "ff_w_up", h, db, (f_idx,))
            wgrad(i, "ff_w_down", s, dz, (f_idx,))
            return dx, arrived, None
        (shared,) = wgrad(i, "ff_w_gate", h, da, (f_idx,), _ShareRider(pack_small(small_grads())))
        for k, nxt, lhs, rhs in (("ff_w_gate", "ff_w_up", h, db), ("ff_w_up", "ff_w_down", s, dz)):
            (arrived[k],) = wgrad(i, nxt, lhs, rhs, (f_idx,),
                                  _ScatterRider([as_pieces(big[i][k])], [(f_idx,)], [arrived[k]]))
        (arrived["ff_w_down"],) = _scatter_grads([as_pieces(big[i]["ff_w_down"])], [(f_idx,)], [arrived["ff_w_down"]])
        return dx, arrived, shared

    for i in reversed(range(depth)):
        rec = saved[i]
        g = W["norm_g"][i]
        j = i // 2
        if exchange is not None and waiting:
            sent = waiting.pop()
            dx, arrived, _ = ffn_back(dx, i, 1, rec["x2"], rec["a2"], rec["b2"], rec["f2"],
                                      {k: (p, (0, 1)) for k, p in pieces[sent].items()})
            exchange(sent, pieces[sent], arrived)
        else:
            dx, _, _ = ffn_back(dx, i, 1, rec["x2"], rec["a2"], rec["b2"], rec["f2"], {})
        if i % 2 == 0:
            (dx, hn, dzp, gated, dm, dws, dbs_acc, dlng, dlnb, dgpre, dgpost) = _sgu_bwd(
                dx, rec["x1"], rec["m"], rec["zp"], vec(g[2]), vec(g[3]), mixer_w(i, "sgu_w_in"),
                vec(W["sgu_ln_g"][j]), vec(W["sgu_ln_b"][j]), wsm[j], wsmt[j], bsb[j], mixer_w(i, "sgu_w_out"), (),
                TM_SGU)
            wgrad(i, "sgu_w_in", hn, dzp, ())
            wgrad(i, "sgu_w_out", gated, dm, ())
            small["sgu_w_spatial"][j] = jnp.where(causal[None], dws, 0.0)
            small["sgu_b_spatial"][j] = dbs_acc.reshape(CHUNK, N_GROUPS, dgrp).sum(-1).T
            small["sgu_ln_g"][j] = dlng[0]
            small["sgu_ln_b"][j] = dlnb[0]
        else:
            dm, q, dc, dlng, dlnb, dbdw, dgpost = _conv_bwd_b(
                dx, rec["m"], rec["c"], vec(W["conv_ln_g"][j]), vec(W["conv_ln_b"][j]), mixer_w(i, "conv_w_pw2"),
                vec(g[3]), (), TM_CONV)
            dx, hn, dp, dwdw, dgpre = _conv_bwd_a(
                dx, rec["x1"], dc, rec["y"], rec["p"], vec(g[2]), wdw[j], mixer_w(i, "conv_w_pw1"), (), TM_CONV)
            wgrad(i, "conv_w_pw1", hn, dp, ())
            wgrad(i, "conv_w_pw2", q, dm, ())
            small["conv_w_dw"][j] = dwdw[:CONV_W]
            small["conv_b_dw"][j] = dbdw[0]
            small["conv_ln_g"][j] = dlng[0]
            small["conv_ln_b"][j] = dlnb[0]
        dnorm[i][2] = dgpre[0]
        dnorm[i][3] = dgpost[0]
        if exchange is not None and i == 0:
            dx, arrived, shared = ffn_back(
                dx, i, 0, rec["x0"], rec["a1"], rec["b1"], rec["f1"],
                {k: (as_pieces(b), (1,) if k in FFN_KINDS else (0, 1)) for k, b in big[i].items()}, last=True)
            pieces[i] = {k: as_pieces(b) for k, b in big[i].items()}
            exchange(i, pieces[i], arrived)
        else:
            dx, _, _ = ffn_back(dx, i, 0, rec["x0"], rec["a1"], rec["b1"], rec["f1"], {})
            pieces[i] = {k: as_pieces(b) for k, b in big[i].items()}
            waiting.append(i)

    return loss, dx, pieces, (shared if exchange is not None else small_grads())


BIG = ("ff_w_gate", "ff_w_up", "ff_w_down", "sgu_w_in", "sgu_w_out", "conv_w_pw1", "conv_w_pw2")
SHARDED_SMALL = ("norm_g", "conv_w_dw", "conv_b_dw", "conv_ln_g", "conv_ln_b")
REPLICATED = ("sgu_ln_g", "sgu_ln_b", "sgu_w_spatial", "sgu_b_spatial")
WEIGHTS = ("norm_g", "ff_w_gate", "ff_w_up", "ff_w_down", "sgu_w_in", "sgu_ln_g", "sgu_ln_b", "sgu_w_spatial",
           "sgu_b_spatial", "sgu_w_out", "conv_w_pw1", "conv_w_dw", "conv_b_dw", "conv_ln_g", "conv_ln_b",
           "conv_w_pw2")


def _rows8(a, width):
    r = a.reshape(-1, width)
    pad = (-r.shape[0]) % 8
    return jnp.pad(r, ((0, pad), (0, 0))) if pad else r


def _pack(arrs, width):
    parts = [_rows8(a, width) for a in arrs]
    return jnp.concatenate(parts, axis=0), [p.shape[0] for p in parts]


def _unpack(buf, like):
    out, r0 = [], 0
    width = buf.shape[-1]
    for a in like:
        n = -(-(a.size // width) // 8) * 8
        rows = a.size // width
        out.append(buf[..., r0:r0 + rows, :].reshape(*buf.shape[:-2], *a.shape))
        r0 += n
    return out


def kernel(x, norm_g, ff_w_gate, ff_w_up, ff_w_down, sgu_w_in, sgu_ln_g, sgu_ln_b, sgu_w_spatial, sgu_b_spatial, sgu_w_out, conv_w_pw1, conv_w_dw, conv_b_dw, conv_ln_g, conv_ln_b, conv_w_pw2, loss_target, m_norm_g, m_ff_w_gate, m_ff_w_up, m_ff_w_down, m_sgu_w_in, m_sgu_ln_g, m_sgu_ln_b, m_sgu_w_spatial, m_sgu_b_spatial, m_sgu_w_out, m_conv_w_pw1, m_conv_w_dw, m_conv_b_dw, m_conv_ln_g, m_conv_ln_b, m_conv_w_pw2, v_norm_g, v_ff_w_gate, v_ff_w_up, v_ff_w_down, v_sgu_w_in, v_sgu_ln_g, v_sgu_ln_b, v_sgu_w_spatial, v_sgu_b_spatial, v_sgu_w_out, v_conv_w_pw1, v_conv_w_dw, v_conv_b_dw, v_conv_ln_g, v_conv_ln_b, v_conv_w_pw2):
    w = dict(norm_g=norm_g, ff_w_gate=ff_w_gate, ff_w_up=ff_w_up, ff_w_down=ff_w_down, sgu_w_in=sgu_w_in,
             sgu_ln_g=sgu_ln_g, sgu_ln_b=sgu_ln_b, sgu_w_spatial=sgu_w_spatial, sgu_b_spatial=sgu_b_spatial,
             sgu_w_out=sgu_w_out, conv_w_pw1=conv_w_pw1, conv_w_dw=conv_w_dw, conv_b_dw=conv_b_dw,
             conv_ln_g=conv_ln_g, conv_ln_b=conv_ln_b, conv_w_pw2=conv_w_pw2)
    mom = dict(norm_g=m_norm_g, ff_w_gate=m_ff_w_gate, ff_w_up=m_ff_w_up, ff_w_down=m_ff_w_down,
               sgu_w_in=m_sgu_w_in, sgu_ln_g=m_sgu_ln_g, sgu_ln_b=m_sgu_ln_b, sgu_w_spatial=m_sgu_w_spatial,
               sgu_b_spatial=m_sgu_b_spatial, sgu_w_out=m_sgu_w_out, conv_w_pw1=m_conv_w_pw1,
               conv_w_dw=m_conv_w_dw, conv_b_dw=m_conv_b_dw, conv_ln_g=m_conv_ln_g, conv_ln_b=m_conv_ln_b,
               conv_w_pw2=m_conv_w_pw2)
    vel = dict(norm_g=v_norm_g, ff_w_gate=v_ff_w_gate, ff_w_up=v_ff_w_up, ff_w_down=v_ff_w_down,
               sgu_w_in=v_sgu_w_in, sgu_ln_g=v_sgu_ln_g, sgu_ln_b=v_sgu_ln_b, sgu_w_spatial=v_sgu_w_spatial,
               sgu_b_spatial=v_sgu_b_spatial, sgu_w_out=v_sgu_w_out, conv_w_pw1=v_conv_w_pw1,
               conv_w_dw=v_conv_w_dw, conv_b_dw=v_conv_b_dw, conv_ln_g=v_conv_ln_g, conv_ln_b=v_conv_ln_b,
               conv_w_pw2=v_conv_w_pw2)
    T, D = x.shape[1], x.shape[2]
    shard_w = conv_b_dw.shape[1]

    xi, yi, ci = _my_place()
    me_chip = (2 * xi + yi).astype(jnp.int32)
    me = (4 * xi + 2 * yi + ci).astype(jnp.int32)

    depth = norm_g.shape[0]

    def entry(k, i):
        return i if k in FFN_KINDS else i // 2

    def stacked(a):
        return a.reshape(a.shape[0], -1, a.shape[-1])

    G = []
    for i in range(depth):
        G.append({k: _cast_into_slot(w[k].reshape(w[k].shape[0], 2, -1, w[k].shape[-1]), entry(k, i))
                  for k in _layer_kinds(i)})
    small_buf, _ = _pack([w[k] for k in SHARDED_SMALL], shard_w)
    first, (small_all,) = _gather_weights(list(G[0].values()), [small_buf])
    G[0] = dict(zip(G[0], first))
    W = {}
    for k, part in zip(SHARDED_SMALL, _unpack(small_all, [w[k] for k in SHARDED_SMALL])):
        W[k] = jnp.moveaxis(part, 0, -2).reshape(*w[k].shape[:-1], N_CHIPS * shard_w)
    for k in REPLICATED:
        W[k] = w[k]

    results = {k: None for k in BIG}

    def reduce_and_update(i, pieces, arrived):
        kinds = list(pieces)
        both = _swap_halves([_sum_with_own(arrived[k], pieces[k]) for k in kinds])
        for k, g in zip(kinds, both):
            results[k] = _adamw_into(stacked(w[k]), g.reshape(-1, g.shape[-1]), stacked(mom[k]), stacked(vel[k]),
                                     results[k], entry(k, i))

    def pack_small(small):
        sbuf, _ = _pack([small[k] for k in SHARDED_SMALL + REPLICATED], D)
        return lax.dynamic_update_slice(jnp.zeros((N_DEV, *sbuf.shape), F32), sbuf[None], (me, 0, 0))

    loss, dx, _, shared = _local_step(x[0], loss_target[0], G, W, reduce_and_update, pack_small)
    loss = lax.psum(loss, ("x", "y", "c"))
    grads, delta, new_m, new_v = {}, {}, {}, {}
    for k in BIG:
        grads[k], delta[k], new_m[k], new_v[k] = (t.reshape(w[k].shape) for t in results[k])

    ssum = _sum_parts(shared)
    for k, gfull in zip(SHARDED_SMALL + REPLICATED, _unpack(ssum, [W[k] for k in SHARDED_SMALL + REPLICATED])):
        if k in SHARDED_SMALL:
            gfull = lax.dynamic_slice_in_dim(gfull, me_chip * shard_w, shard_w, axis=gfull.ndim - 1)
        grads[k] = gfull

    for names, width in ((SHARDED_SMALL, shard_w), (REPLICATED, CHUNK)):
        packed = [_pack([src[k] for k in names], width)[0] for src in (w, grads, mom, vel)]
        outs = _adamw(*packed)
        for res, out in zip((delta, new_m, new_v), outs):
            for k, a in zip(names, _unpack(out, [w[k] for k in names])):
                res[k] = a

    return (loss, dx[None], *[grads[k] for k in WEIGHTS], *[delta[k] for k in WEIGHTS],
            *[new_m[k] for k in WEIGHTS], *[new_v[k] for k in WEIGHTS])
```

```python
import functools

import jax
import jax.numpy as jnp
from jax import lax
from jax.experimental import pallas as pl
from jax.experimental.pallas import tpu as pltpu

F32 = jnp.float32
BF16 = jnp.bfloat16
EPS = 1e-6
N_CHIPS = 4
N_DEV = 8
N_GROUPS = 8
CHUNK = 128
CONV_W = 31
HALO = 32
CONV_RB = 64
CONV_CB = 256
VMEM_LIMIT_V7X = 60 * 1024 * 1024
MESH = pl.DeviceIdType.MESH

ADAM_LR = 0.001
ADAM_B1 = 0.9
ADAM_B2 = 0.999
ADAM_EPS = 1e-08
ADAM_WD = 0.01
ADAM_STEP = 10
FFN_SCALE = 0.5


def _cparams(*sem, **kw):
    return pltpu.CompilerParams(dimension_semantics=sem, vmem_limit_bytes=VMEM_LIMIT_V7X, **kw)


def _resident(shape):
    return pl.BlockSpec(shape, lambda *_: (0,) * len(shape), pipeline_mode=pl.Buffered(1))


def _dot(a, b):
    return jnp.dot(a, b, preferred_element_type=F32)


def _dot_nt(a, b):
    return lax.dot_general(a, b, (((1,), (1,)), ((), ())), preferred_element_type=F32)


def _dot_tn(a, b):
    return lax.dot_general(a, b, (((0,), (0,)), ((), ())), preferred_element_type=F32)


def _rms_stats(x):
    r = lax.rsqrt(jnp.mean(x * x, axis=-1, keepdims=True) + EPS)
    return x * r, r


def _rms_bwd(xh, r, g, dy):
    dxh = dy * g
    dx = r * (dxh - xh * jnp.mean(dxh * xh, axis=-1, keepdims=True))
    return dx, jnp.sum(dy * xh, axis=0, keepdims=True)


def _ln_stats(parts, width):
    mu = sum(jnp.sum(p, axis=-1, keepdims=True) for p in parts) / width
    cen = [p - mu for p in parts]
    var = sum(jnp.sum(c * c, axis=-1, keepdims=True) for c in cen) / width
    rstd = lax.rsqrt(var + EPS)
    return [c * rstd for c in cen], rstd


def _ln_bwd(vh_parts, rstd, dvh_parts, width):
    m1 = sum(jnp.sum(d, axis=-1, keepdims=True) for d in dvh_parts) / width
    m2 = sum(jnp.sum(d * v, axis=-1, keepdims=True) for d, v in zip(dvh_parts, vh_parts)) / width
    return [rstd * (d - m1 - v * m2) for d, v in zip(dvh_parts, vh_parts)]


_GELU_C = 0.7978845608028654
_GELU_A = 0.044715


def _gelu(x):
    return 0.5 * x * (1.0 + jnp.tanh(_GELU_C * (x + _GELU_A * x * x * x)))


def _gelu_pair(x):
    x2 = x * x
    t = jnp.tanh(_GELU_C * (x + _GELU_A * (x * x2)))
    half = 0.5 * (1.0 + t)
    return x * half, half + (0.5 * _GELU_C) * x * (1.0 - t * t) * (1.0 + (3.0 * _GELU_A) * x2)


def _sigmoid_pair(a):
    e = jnp.exp(jnp.minimum(-a, 80.0))
    sg = 1.0 / (1.0 + e)
    return sg, e * sg


def _acc_out(ref, first, val):
    @pl.when(first)
    def _():
        ref[...] = val

    @pl.when(jnp.logical_not(first))
    def _():
        ref[...] += val


def _my_place():
    return lax.axis_index("x"), lax.axis_index("y"), lax.axis_index("c")


class _Gather:
    def __init__(self, bufs, sems):
        self.bufs = bufs
        self.own_sems, self.fwd_sems = sems[:2], sems[2:4]
        self.x, self.y, self.c = _my_place()
        x, y = self.x, self.y
        self.chips = [(1 - x, y), (x, 1 - y), (1 - x, 1 - y)]

    def _copy(self, a, j, chip, half, to, sems):
        spot = self.bufs[a].at[2 * chip[0] + chip[1], pl.ds(half, 1)]
        return pltpu.make_async_remote_copy(src_ref=spot, dst_ref=spot, send_sem=sems[0].at[a, j],
                                            recv_sem=sems[1].at[a, j], device_id=to, device_id_type=MESH)

    def _own(self, a, j):
        return self._copy(a, j, (self.x, self.y), self.c, (*self.chips[j], self.c), self.own_sems)

    def _passed_on(self, a, j):
        return self._copy(a, j, self.chips[j], self.c, (self.x, self.y, 1 - self.c), self.fwd_sems)

    def start(self):
        for j in range(3):
            for a in range(len(self.bufs)):
                self._own(a, j).start()

    def forward(self):
        me = (self.x, self.y, self.c)
        for j in range(3):
            for a in range(len(self.bufs)):
                self._copy(a, j, self.chips[j], self.c, me, self.own_sems).wait_recv()
                self._passed_on(a, j).start()

    def finish(self):
        me = (self.x, self.y, self.c)
        for j in range(3):
            for a in range(len(self.bufs)):
                self._copy(a, j, self.chips[j], 1 - self.c, me, self.fwd_sems).wait_recv()
        for j in range(3):
            for a in range(len(self.bufs)):
                self._own(a, j).wait_send()
                self._passed_on(a, j).wait_send()

    @staticmethod
    def semaphores(n):
        return [pltpu.SemaphoreType.DMA((n, 3)) for _ in range(4)]


class _Scatter:
    def __init__(self, g_in, g_out, sems, halves):
        self.g_in, self.g_out = g_in, g_out
        self.send_sem, self.recv_sem = sems
        self.halves = halves
        x, y, c = _my_place()
        self.c = c
        self.me = 4 * x + 2 * y + c

    def _piece(self, a, d, slot):
        return pltpu.make_async_remote_copy(
            src_ref=self.g_in[a].at[d // 2, d % 2], dst_ref=self.g_out[a].at[slot],
            send_sem=self.send_sem.at[a, d], recv_sem=self.recv_sem.at[a, slot],
            device_id=(d // 4, (d // 2) % 2, d % 2), device_id_type=MESH)

    def _to(self, a):
        return [d for d in range(N_DEV) if d % 2 in self.halves[a]]

    def start(self):
        for a in range(len(self.g_in)):
            for d in self._to(a):
                @pl.when(d != self.me)
                def _():
                    self._piece(a, d, lax.rem(self.me - d - 1 + N_DEV, N_DEV)).start()

    def finish(self):
        for a in range(len(self.g_in)):
            for h in self.halves[a]:
                @pl.when(self.c == h)
                def _():
                    for slot in range(N_DEV - 1):
                        self._piece(a, 0, slot).wait_recv()
            for d in self._to(a):
                @pl.when(d != self.me)
                def _():
                    self._piece(a, d, 0).wait_send()

    @staticmethod
    def semaphores(n):
        return [pltpu.SemaphoreType.DMA((n, N_DEV)), pltpu.SemaphoreType.DMA((n, N_DEV - 1))]


class _ShareAll:
    def __init__(self, buf, sems):
        self.buf = buf
        self.send_sem, self.recv_sem = sems
        self.x, self.y, self.c = _my_place()
        self.me = 4 * self.x + 2 * self.y + self.c

    def start(self):
        mine = self.buf.at[self.me]
        for d in range(N_DEV):
            @pl.when(d != self.me)
            def _():
                pltpu.make_async_remote_copy(src_ref=mine, dst_ref=mine, send_sem=self.send_sem.at[d],
                                             recv_sem=self.recv_sem.at[self.me],
                                             device_id=(d // 4, (d // 2) % 2, d % 2), device_id_type=MESH).start()

    def finish(self):
        for d in range(N_DEV):
            @pl.when(d != self.me)
            def _():
                cp = pltpu.make_async_remote_copy(src_ref=self.buf.at[self.me], dst_ref=self.buf.at[d],
                                                  send_sem=self.send_sem.at[d], recv_sem=self.recv_sem.at[d],
                                                  device_id=(self.x, self.y, self.c), device_id_type=MESH)
                cp.wait_send()
                cp.wait_recv()

    @staticmethod
    def semaphores():
        return [pltpu.SemaphoreType.DMA((N_DEV,)), pltpu.SemaphoreType.DMA((N_DEV,))]


def _chunk_spec(sel, rows, cols):
    return pl.BlockSpec((None,) * (1 + len(sel)) + (rows, cols), lambda i, j: (j, *sel, 0, 0))


def _chunks_spec(w, sel):
    return pl.BlockSpec((w.shape[0],) + (None,) * len(sel) + w.shape[-2:], lambda *_: (0, *sel, 0, 0),
                        pipeline_mode=pl.Buffered(1))


def _ffn_fwd(x, g_pre, g_post, wg, wu, wd, sel, tm, gather=()):
    T, D = x.shape
    nj, F = wg.shape[0], wg.shape[-1]
    tm = min(tm, T)
    ni = T // tm
    rb = min(RB_FFN_FWD, tm)
    ng = len(gather)

    def body(*refs):
        x_ref, gpre_ref, gpost_ref, wg_ref, wu_ref, wd_ref = refs[:6]
        xo_ref, a_ref, b_ref, f_ref = refs[6 + ng:10 + ng]
        h_ref = refs[10 + 2 * ng]
        acc_scr = refs[11 + 2 * ng]
        i = pl.program_id(0)
        j = pl.program_id(1)
        if ng:
            plan = _Gather(refs[10 + ng:10 + 2 * ng], refs[12 + 2 * ng:])
            pl.when(jnp.logical_and(i == 0, j == 0))(plan.start)
            pl.when(jnp.logical_and(i == (5 * ni) // 8, j == nj - 1))(plan.forward)

        @pl.when(j == 0)
        def _():
            xh, _ = _rms_stats(x_ref[...])
            h_ref[...] = (xh * gpre_ref[...]).astype(BF16)
            acc_scr[...] = jnp.zeros_like(acc_scr)

        for r0 in range(0, tm, rb):
            rows = slice(r0, r0 + rb)
            h = h_ref[rows, :]
            a = _dot(h, wg_ref[...]).astype(BF16)
            b = _dot(h, wu_ref[...]).astype(BF16)
            a_ref[rows, :] = a
            b_ref[rows, :] = b
            sg, _ = _sigmoid_pair(a)
            acc_scr[rows, :] += _dot((a * sg) * b, wd_ref[...])

        @pl.when(j == nj - 1)
        def _():
            f = acc_scr[...]
            f_ref[...] = f
            fh, _ = _rms_stats(f)
            xo_ref[...] = x_ref[...] + FFN_SCALE * (fh * gpost_ref[...])

        if ng:
            pl.when(jnp.logical_and(i == ni - 1, j == nj - 1))(plan.finish)

    row = pl.BlockSpec((tm, D), lambda i, j: (i, 0))
    vec = pl.BlockSpec((1, D), lambda i, j: (0, 0))
    w_in = _chunk_spec(sel, D, F)
    w_out = _chunk_spec(sel, F, D)
    act = pl.BlockSpec((None, tm, F), lambda i, j: (j, i, 0))
    hbm = pl.BlockSpec(memory_space=pl.ANY)
    return pl.pallas_call(
        body,
        name="ffn_fwd_gather" if ng else "ffn_fwd",
        grid=(ni, nj),
        in_specs=[row, vec, vec, w_in, w_in, w_out] + [hbm] * ng,
        out_specs=[row, act, act, row] + [hbm] * ng + [row],
        out_shape=[
            jax.ShapeDtypeStruct((T, D), F32),
            jax.ShapeDtypeStruct((nj, T, F), BF16),
            jax.ShapeDtypeStruct((nj, T, F), BF16),
            jax.ShapeDtypeStruct((T, D), F32),
        ] + [jax.ShapeDtypeStruct(g.shape, g.dtype) for g in gather] + [jax.ShapeDtypeStruct((T, D), BF16)],
        input_output_aliases={6 + a: 4 + a for a in range(ng)},
        scratch_shapes=[pltpu.VMEM((tm, D), F32)] + (_Gather.semaphores(ng) if ng else []),
        compiler_params=_cparams("arbitrary", "arbitrary"),
    )(x, g_pre, g_post, wg, wu, wd, *gather)


def _ffn_bwd(dy, x, f, a, b, g_pre, g_post, wg, wu, wd, sel, tm, scatter=()):
    T, D = x.shape
    nj, F = wg.shape[0], wg.shape[-1]
    tm = min(tm, T)
    ni = T // tm
    rb = min(RB_FFN_BWD, tm)
    ns = len(scatter)
    halves = [h for _, h in scatter]
    scatter = [g for g, _ in scatter]

    def body(*refs):
        dy_ref, x_ref, f_ref, a_ref, b_ref, gpre_ref, gpost_ref, wg_ref, wu_ref, wd_ref = refs[:10]
        dx_ref, dz_ref, s_ref, da_ref, db_ref, dgpre_ref, dgpost_ref = refs[10 + ns:17 + ns]
        dh_scr = refs[17 + 2 * ns]
        i = pl.program_id(0)
        j = pl.program_id(1)
        if ns:
            plan = _Scatter(refs[10:10 + ns], refs[17 + ns:17 + 2 * ns], refs[18 + 2 * ns:], halves)
            pl.when(jnp.logical_and(i == 0, j == 0))(plan.start)

        @pl.when(j == 0)
        def _():
            fh, rf = _rms_stats(f_ref[...])
            dz, dg = _rms_bwd(fh, rf, gpost_ref[...], FFN_SCALE * dy_ref[...])
            dz_ref[...] = dz.astype(BF16)
            _acc_out(dgpost_ref, i == 0, dg)
            dh_scr[...] = jnp.zeros_like(dh_scr)

        for r0 in range(0, tm, rb):
            rows = slice(r0, r0 + rb)
            ds = _dot_nt(dz_ref[rows, :], wd_ref[j]).astype(BF16)
            av = a_ref[rows, :]
            bv = b_ref[rows, :]
            sg, one_minus_sg = _sigmoid_pair(av)
            sl = av * sg
            s_ref[rows, :] = sl * bv
            da = (ds * bv) * (sg + sl * one_minus_sg)
            db = ds * sl
            da_ref[rows, :] = da
            db_ref[rows, :] = db
            dh_scr[rows, :] += _dot_nt(da, wg_ref[j]) + _dot_nt(db, wu_ref[j])

        @pl.when(j == nj - 1)
        def _():
            xh, rx = _rms_stats(x_ref[...])
            dxn, dg = _rms_bwd(xh, rx, gpre_ref[...], dh_scr[...])
            dx_ref[...] = dy_ref[...] + dxn
            _acc_out(dgpre_ref, i == 0, dg)

        if ns:
            pl.when(jnp.logical_and(i == ni - 1, j == nj - 1))(plan.finish)

    row = pl.BlockSpec((tm, D), lambda i, j: (i, 0))
    vec = pl.BlockSpec((1, D), lambda i, j: (0, 0))
    w_in = _chunks_spec(wg, sel)
    w_out = _chunks_spec(wd, sel)
    act = pl.BlockSpec((None, tm, F), lambda i, j: (j, i, 0))
    act_shape = jax.ShapeDtypeStruct((nj, T, F), BF16)
    hbm = pl.BlockSpec(memory_space=pl.ANY)
    return pl.pallas_call(
        body,
        name="ffn_bwd_scatter" if ns else "ffn_bwd",
        grid=(ni, nj),
        in_specs=[row, row, row, act, act, vec, vec, w_in, w_in, w_out] + [hbm] * ns,
        out_specs=[row, row, act, act, act, vec, vec] + [hbm] * ns,
        out_shape=[
            jax.ShapeDtypeStruct((T, D), F32),
            jax.ShapeDtypeStruct((T, D), BF16),
            act_shape, act_shape, act_shape,
            jax.ShapeDtypeStruct((1, D), F32),
            jax.ShapeDtypeStruct((1, D), F32),
        ] + [jax.ShapeDtypeStruct((N_DEV - 1, *g.shape[2:]), g.dtype) for g in scatter],
        scratch_shapes=[pltpu.VMEM((tm, D), F32)] + (_Scatter.semaphores(ns) if ns else []),
        compiler_params=_cparams("arbitrary", "arbitrary"),
    )(dy, x, f, a, b, g_pre, g_post, wg, wu, wd, *scatter)


class _ScatterRider:
    name = "scatter"

    def __init__(self, grads, halves, into):
        n = len(grads)
        self.halves = halves
        self.operands = [*grads, *into]
        self.results = [jax.ShapeDtypeStruct(t.shape, t.dtype) for t in into]
        self.aliases = {n + a: a for a in range(n)}
        self.semaphores = _Scatter.semaphores(n)

    def plan(self, in_refs, out_refs, sems):
        return _Scatter(in_refs[:len(out_refs)], out_refs, sems, self.halves)


class _ShareRider:
    name = "share"

    def __init__(self, slots):
        self.operands = [slots]
        self.results = [jax.ShapeDtypeStruct(slots.shape, slots.dtype)]
        self.aliases = {0: 0}
        self.semaphores = _ShareAll.semaphores()

    def plan(self, in_refs, out_refs, sems):
        return _ShareAll(out_refs[0], sems)


def _tn_matmul(a, b, buf, like, sel, tk, rider=None):
    a_chunked = a.ndim == 3
    nj = a.shape[0] if a_chunked else b.shape[0]
    T, M, N = a.shape[-2], a.shape[-1], b.shape[-1]
    tk = min(tk, T)
    nk = T // tk
    have = buf is not None
    n_in = 2 + have + (len(rider.operands) if rider else 0)
    n_out = 1 + (len(rider.results) if rider else 0)

    def body(*refs):
        a_ref, b_ref = refs[:2]
        o_ref = refs[n_in]
        acc_scr = refs[n_in + n_out]
        j = pl.program_id(0)
        k = pl.program_id(1)
        if rider:
            plan = rider.plan(refs[2 + have:n_in], refs[n_in + 1:n_in + n_out], refs[n_in + n_out + 1:])
            pl.when(jnp.logical_and(j == 0, k == 0))(plan.start)

        @pl.when(k == 0)
        def _():
            acc_scr[...] = jnp.zeros_like(acc_scr)

        acc_scr[...] += _dot_tn(a_ref[...], b_ref[...])

        @pl.when(k == nk - 1)
        def _():
            o_ref[...] = acc_scr[...].astype(BF16)

        if rider:
            pl.when(jnp.logical_and(j == nj - 1, k == nk - 1))(plan.finish)

    def spec(chunked, width):
        if chunked:
            return pl.BlockSpec((None, tk, width), lambda j, k: (j, k, 0))
        return pl.BlockSpec((tk, width), lambda j, k: (k, 0))

    hbm = pl.BlockSpec(memory_space=pl.ANY)
    aliases = {2: 0} if have else {}
    if rider:
        aliases.update({2 + have + src: 1 + dst for src, dst in rider.aliases.items()})
    outs = pl.pallas_call(
        body,
        name="tn_matmul_" + rider.name if rider else "tn_matmul",
        grid=(nj, nk),
        in_specs=[spec(a_chunked, M), spec(not a_chunked, N)] + [hbm] * (n_in - 2),
        out_specs=[pl.BlockSpec((None,) * (1 + len(sel)) + (M, N), lambda j, k: (j, *sel, 0, 0))] + [hbm] * (n_out - 1),
        out_shape=[jax.ShapeDtypeStruct(like.shape, BF16)] + (list(rider.results) if rider else []),
        input_output_aliases=aliases,
        scratch_shapes=[pltpu.VMEM((M, N), F32)] + (rider.semaphores if rider else []),
        compiler_params=_cparams("arbitrary", "arbitrary"),
    )(a, b, *([buf] if have else []), *(rider.operands if rider else ()))
    return (outs[0], outs[1:]) if rider else outs[0]


def _sgu_fwd(x, g_pre, g_post, win, lng, lnb, wsm, bsb, wout, sel, tm):
    T, D = x.shape
    nc, E = win.shape[0], win.shape[-1]
    S = 2 * E
    dg = S // N_GROUPS
    wo_rows = wout.shape[-2]
    tm = min(tm, T)
    nq = tm // CHUNK

    def body(x_ref, gpre_ref, gpost_ref, win_ref, lng_ref, lnb_ref, ws_ref, bsb_ref, wout_ref,
             xo_ref, zp_ref, m_ref, u_scr, vn_scr, gt_scr):
        x = x_ref[...]
        xh, _ = _rms_stats(x)
        hn = (xh * gpre_ref[...]).astype(BF16)
        v_parts = []
        for c in range(nc):
            zp = _dot(hn, win_ref[c])
            zp_ref[c] = zp.astype(BF16)
            z = _gelu(zp)
            if c < nc // 2:
                u_scr[:, c * E:(c + 1) * E] = z
            else:
                v_parts.append(z)
        vh_parts, _ = _ln_stats(v_parts, S)
        for c, vh in enumerate(vh_parts):
            cols = slice(c * E, (c + 1) * E)
            vn_scr[:, cols] = (vh * lng_ref[:, cols] + lnb_ref[:, cols]).astype(BF16)
        for q in range(nq):
            rows = slice(q * CHUNK, (q + 1) * CHUNK)
            for g in range(N_GROUPS):
                cols = slice(g * dg, (g + 1) * dg)
                mixed = _dot(ws_ref[g], vn_scr[rows, cols]) + bsb_ref[g]
                gt_scr[rows, cols] = (u_scr[rows, cols] * mixed).astype(BF16)
        m = _dot(gt_scr[:, 0:wo_rows], wout_ref[0])
        for c in range(1, nc):
            m += _dot(gt_scr[:, c * wo_rows:(c + 1) * wo_rows], wout_ref[c])
        m_ref[...] = m
        mh, _ = _rms_stats(m)
        xo_ref[...] = x + mh * gpost_ref[...]

    row = pl.BlockSpec((tm, D), lambda i: (i, 0))
    return pl.pallas_call(
        body,
        name="sgu_fwd",
        grid=(T // tm,),
        in_specs=[row, _resident((1, D)), _resident((1, D)), _chunks_spec(win, sel), _resident((1, S)),
                  _resident((1, S)), _resident(wsm.shape), _resident(bsb.shape), _chunks_spec(wout, sel)],
        out_specs=[row, pl.BlockSpec((nc, tm, E), lambda i: (0, i, 0)), row],
        out_shape=[
            jax.ShapeDtypeStruct((T, D), F32),
            jax.ShapeDtypeStruct((nc, T, E), BF16),
            jax.ShapeDtypeStruct((T, D), F32),
        ],
        scratch_shapes=[pltpu.VMEM((tm, S), F32), pltpu.VMEM((tm, S), BF16), pltpu.VMEM((tm, S), BF16)],
        compiler_params=_cparams("parallel"),
    )(x, g_pre, g_post, win, lng, lnb, wsm, bsb, wout)


def _sgu_bwd(dy, x, m, zp, g_pre, g_post, win, lng, lnb, wsm, wsmt, bsb, wout, sel, tm):
    T, D = x.shape
    nc, E = win.shape[0], win.shape[-1]
    S = 2 * E
    dg = S // N_GROUPS
    wo_rows = wout.shape[-2]
    tm = min(tm, T)
    nq = tm // CHUNK

    def body(dy_ref, x_ref, m_ref, zp_ref, gpre_ref, gpost_ref, win_ref, lng_ref, lnb_ref, ws_ref, wst_ref,
             bsb_ref, wout_ref,
             dx_ref, hn_ref, dzp_ref, gated_ref, dm_ref, dws_ref, dbs_ref, dlng_ref, dlnb_ref, dgpre_ref,
             dgpost_ref, u_scr, d_scr, vh_scr, vn_scr, gg_scr):
        first = pl.program_id(0) == 0
        dy = dy_ref[...]
        mh, rm = _rms_stats(m_ref[...])
        dm, dgp = _rms_bwd(mh, rm, gpost_ref[...], dy)
        _acc_out(dgpost_ref, first, dgp)
        dm = dm.astype(BF16)
        dm_ref[...] = dm
        for c in range(nc):
            d_scr[:, c * wo_rows:(c + 1) * wo_rows] = _dot_nt(dm, wout_ref[c])
        v_parts = []
        for c in range(nc):
            z, gg_scr[c] = _gelu_pair(zp_ref[c])
            if c < nc // 2:
                u_scr[:, c * E:(c + 1) * E] = z.astype(F32)
            else:
                v_parts.append(z.astype(F32))
        vh_parts, rstd = _ln_stats(v_parts, S)
        for c, vh in enumerate(vh_parts):
            cols = slice(c * E, (c + 1) * E)
            vh_scr[:, cols] = vh
            vn_scr[:, cols] = (vh * lng_ref[:, cols] + lnb_ref[:, cols]).astype(BF16)

        @pl.when(first)
        def _():
            dws_ref[...] = jnp.zeros_like(dws_ref)
            dbs_ref[...] = jnp.zeros_like(dbs_ref)
            dlng_ref[...] = jnp.zeros_like(dlng_ref)
            dlnb_ref[...] = jnp.zeros_like(dlnb_ref)

        for q in range(nq):
            rows = slice(q * CHUNK, (q + 1) * CHUNK)
            for g in range(N_GROUPS):
                cols = slice(g * dg, (g + 1) * dg)
                vn = vn_scr[rows, cols]
                mixed = _dot(ws_ref[g], vn) + bsb_ref[g]
                u = u_scr[rows, cols]
                dgt = d_scr[rows, cols]
                gated_ref[(g * dg) // wo_rows, rows, (g * dg) % wo_rows:(g * dg) % wo_rows + dg] = (u * mixed).astype(BF16)
                dmix = dgt * u
                dbs_ref[:, cols] += dmix
                dmix = dmix.astype(BF16)
                dws_ref[g] += _dot_nt(dmix, vn)
                u_scr[rows, cols] = dgt * mixed
                d_scr[rows, cols] = _dot(wst_ref[g], dmix)
        dvn = [d_scr[:, c * E:(c + 1) * E] for c in range(nc // 2)]
        vh = [vh_scr[:, c * E:(c + 1) * E] for c in range(nc // 2)]
        for c, (d, v) in enumerate(zip(dvn, vh)):
            dlng_ref[:, c * E:(c + 1) * E] += jnp.sum(d * v, axis=0, keepdims=True)
            dlnb_ref[:, c * E:(c + 1) * E] += jnp.sum(d, axis=0, keepdims=True)
        dvh = [d * lng_ref[:, c * E:(c + 1) * E] for c, d in enumerate(dvn)]
        dv = _ln_bwd(vh, rstd, dvh, S)
        dhn = None
        for c in range(nc):
            dz = u_scr[:, c * E:(c + 1) * E] if c < nc // 2 else dv[c - nc // 2]
            dzp = dz.astype(BF16) * gg_scr[c]
            dzp_ref[c] = dzp
            t = _dot_nt(dzp, win_ref[c])
            dhn = t if dhn is None else dhn + t
        xh, rx = _rms_stats(x_ref[...])
        hn_ref[...] = (xh * gpre_ref[...]).astype(BF16)
        dxn, dgq = _rms_bwd(xh, rx, gpre_ref[...], dhn)
        dx_ref[...] = dy + dxn
        _acc_out(dgpre_ref, first, dgq)

    row = pl.BlockSpec((tm, D), lambda i: (i, 0))

    def whole(shape):
        return pl.BlockSpec(shape, lambda i: (0,) * len(shape))

    return pl.pallas_call(
        body,
        name="sgu_bwd",
        grid=(T // tm,),
        in_specs=[row, row, row, pl.BlockSpec((nc, tm, E), lambda i: (0, i, 0)), _resident((1, D)), _resident((1, D)),
                  _chunks_spec(win, sel), _resident((1, S)), _resident((1, S)), _resident(wsm.shape),
                  _resident(wsmt.shape), _resident(bsb.shape), _chunks_spec(wout, sel)],
        out_specs=[row, row, pl.BlockSpec((nc, tm, E), lambda i: (0, i, 0)),
                   pl.BlockSpec((nc, tm, wo_rows), lambda i: (0, i, 0)), row,
                   whole((N_GROUPS, CHUNK, CHUNK)), whole((CHUNK, S)), whole((1, S)), whole((1, S)),
                   whole((1, D)), whole((1, D))],
        out_shape=[
            jax.ShapeDtypeStruct((T, D), F32),
            jax.ShapeDtypeStruct((T, D), BF16),
            jax.ShapeDtypeStruct((nc, T, E), BF16),
            jax.ShapeDtypeStruct((nc, T, wo_rows), BF16),
            jax.ShapeDtypeStruct((T, D), BF16),
            jax.ShapeDtypeStruct((N_GROUPS, CHUNK, CHUNK), F32),
            jax.ShapeDtypeStruct((CHUNK, S), F32),
            jax.ShapeDtypeStruct((1, S), F32),
            jax.ShapeDtypeStruct((1, S), F32),
            jax.ShapeDtypeStruct((1, D), F32),
            jax.ShapeDtypeStruct((1, D), F32),
        ],
        scratch_shapes=[pltpu.VMEM((tm, S), F32), pltpu.VMEM((tm, S), F32), pltpu.VMEM((tm, S), F32),
                        pltpu.VMEM((tm, S), BF16), pltpu.VMEM((nc, tm, E), BF16)],
        compiler_params=_cparams("arbitrary"),
    )(dy, x, m, zp, g_pre, g_post, win, lng, lnb, wsm, wsmt, bsb, wout)


def _shifted_windows(buf, r0, cols, lo, hi):
    n = CONV_RB + HALO
    base = buf[r0:r0 + n, cols]
    for r in range(8):
        rolled = base if r == 0 else pltpu.roll(base, n - r, axis=0)
        for s in range(r, hi, 8):
            if s >= lo:
                yield s, rolled[s - r:s - r + CONV_RB]


def _conv_fwd_a(x, g_pre, wpw1, sel, tm):
    T, D = x.shape
    nc, E = wpw1.shape[0], wpw1.shape[-1]
    C = 2 * E
    tm = min(tm, T)

    def body(x_ref, gpre_ref, w_ref, y_ref, p_ref):
        xh, _ = _rms_stats(x_ref[...])
        hn = (xh * gpre_ref[...]).astype(BF16)
        ps = []
        for c in range(nc):
            p = _dot(hn, w_ref[c])
            p_ref[c] = p.astype(BF16)
            ps.append(p)
        for c in range(nc // 2):
            y_ref[:, c * E:(c + 1) * E] = ps[c] * jax.nn.sigmoid(ps[c + nc // 2])

    row = pl.BlockSpec((tm, D), lambda i: (i, 0))
    return pl.pallas_call(
        body,
        name="conv_fwd_a",
        grid=(T // tm,),
        in_specs=[row, _resident((1, D)), _chunks_spec(wpw1, sel)],
        out_specs=[pl.BlockSpec((tm, C), lambda i: (i, 0)), pl.BlockSpec((nc, tm, E), lambda i: (0, i, 0))],
        out_shape=[jax.ShapeDtypeStruct((T, C), F32), jax.ShapeDtypeStruct((nc, T, E), BF16)],
        compiler_params=_cparams("parallel"),
    )(x, g_pre, wpw1)


def _conv_fwd_b(x, y, wdw, bdw, lng, lnb, wpw2, g_post, sel, tm):
    T, D = x.shape
    C = y.shape[1]
    nc, E = wpw2.shape[0], wpw2.shape[-2]
    tm = min(tm, T)
    per = tm // HALO

    def body(x_ref, y_ref, yprev_ref, wdw_ref, bdw_ref, lng_ref, lnb_ref, w_ref, gpost_ref,
             xo_ref, c_ref, m_ref, ybuf):
        i = pl.program_id(0)
        ybuf[0:HALO, :] = jnp.where(i > 0, yprev_ref[...], 0.0)
        ybuf[HALO:HALO + tm, :] = y_ref[...]
        off = HALO - (CONV_W - 1)
        for r0 in range(0, tm, CONV_RB):
            for c0 in range(0, C, CONV_CB):
                cols = slice(c0, c0 + CONV_CB)
                acc = jnp.broadcast_to(bdw_ref[:, cols], (CONV_RB, CONV_CB))
                for s, win in _shifted_windows(ybuf, r0, cols, off, off + CONV_W):
                    acc = acc + wdw_ref[s - off:s - off + 1, cols] * win
                c_ref[r0:r0 + CONV_RB, cols] = acc
        (ch,), _ = _ln_stats([c_ref[...]], C)
        cn = ch * lng_ref[...] + lnb_ref[...]
        qv = (cn * jax.nn.sigmoid(cn)).astype(BF16)
        m = _dot(qv[:, 0:E], w_ref[0])
        for c in range(1, nc):
            m += _dot(qv[:, c * E:(c + 1) * E], w_ref[c])
        m_ref[...] = m
        mh, _ = _rms_stats(m)
        xo_ref[...] = x_ref[...] + mh * gpost_ref[...]

    row = pl.BlockSpec((tm, D), lambda i: (i, 0))
    crow = pl.BlockSpec((tm, C), lambda i: (i, 0))
    prev = pl.BlockSpec((HALO, C), lambda i: (jnp.maximum(i * per - 1, 0), 0))
    return pl.pallas_call(
        body,
        name="conv_fwd_b",
        grid=(T // tm,),
        in_specs=[row, crow, prev, _resident(wdw.shape), _resident((1, C)), _resident((1, C)), _resident((1, C)),
                  _chunks_spec(wpw2, sel), _resident((1, D))],
        out_specs=[row, crow, row],
        out_shape=[jax.ShapeDtypeStruct((T, D), F32), jax.ShapeDtypeStruct((T, C), F32),
                   jax.ShapeDtypeStruct((T, D), F32)],
        scratch_shapes=[pltpu.VMEM((HALO + tm, C), F32)],
        compiler_params=_cparams("parallel"),
    )(x, y, y, wdw, bdw, lng, lnb, wpw2, g_post)


def _conv_bwd_b(dy, m, c, lng, lnb, wpw2, g_post, sel, tm):
    T, D = dy.shape
    C = c.shape[1]
    nc, E = wpw2.shape[0], wpw2.shape[-2]
    tm = min(tm, T)

    def body(dy_ref, m_ref, c_ref, lng_ref, lnb_ref, w_ref, gpost_ref,
             dm_ref, q_ref, dc_ref, dlng_ref, dlnb_ref, dbdw_ref, dgpost_ref, dq_scr):
        first = pl.program_id(0) == 0
        mh, rm = _rms_stats(m_ref[...])
        dm, dgp = _rms_bwd(mh, rm, gpost_ref[...], dy_ref[...])
        _acc_out(dgpost_ref, first, dgp)
        dm = dm.astype(BF16)
        dm_ref[...] = dm
        for k in range(nc):
            dq_scr[:, k * E:(k + 1) * E] = _dot_nt(dm, w_ref[k])
        (ch,), rstd = _ln_stats([c_ref[...]], C)
        cn = ch * lng_ref[...] + lnb_ref[...]
        sg = jax.nn.sigmoid(cn)
        qv = (cn * sg).astype(BF16)
        for k in range(nc):
            q_ref[k] = qv[:, k * E:(k + 1) * E]
        dcn = dq_scr[...] * (sg * (1.0 + cn * (1.0 - sg)))
        _acc_out(dlng_ref, first, jnp.sum(dcn * ch, axis=0, keepdims=True))
        _acc_out(dlnb_ref, first, jnp.sum(dcn, axis=0, keepdims=True))
        (dc,) = _ln_bwd([ch], rstd, [dcn * lng_ref[...]], C)
        dc_ref[...] = dc
        _acc_out(dbdw_ref, first, jnp.sum(dc, axis=0, keepdims=True))

    row = pl.BlockSpec((tm, D), lambda i: (i, 0))
    crow = pl.BlockSpec((tm, C), lambda i: (i, 0))

    def whole(shape):
        return pl.BlockSpec(shape, lambda i: (0,) * len(shape))

    return pl.pallas_call(
        body,
        name="conv_bwd_b",
        grid=(T // tm,),
        in_specs=[row, row, crow, _resident((1, C)), _resident((1, C)), _chunks_spec(wpw2, sel), _resident((1, D))],
        out_specs=[row, pl.BlockSpec((nc, tm, E), lambda i: (0, i, 0)), crow, whole((1, C)), whole((1, C)),
                   whole((1, C)), whole((1, D))],
        out_shape=[jax.ShapeDtypeStruct((T, D), BF16), jax.ShapeDtypeStruct((nc, T, E), BF16),
                   jax.ShapeDtypeStruct((T, C), F32), jax.ShapeDtypeStruct((1, C), F32),
                   jax.ShapeDtypeStruct((1, C), F32), jax.ShapeDtypeStruct((1, C), F32),
                   jax.ShapeDtypeStruct((1, D), F32)],
        scratch_shapes=[pltpu.VMEM((tm, C), F32)],
        compiler_params=_cparams("arbitrary"),
    )(dy, m, c, lng, lnb, wpw2, g_post)


def _conv_bwd_a(dy, x, dc, y, p, g_pre, wdw, wpw1, sel, tm):
    T, D = x.shape
    C = y.shape[1]
    nc, E = wpw1.shape[0], wpw1.shape[-1]
    tm = min(tm, T)
    per = tm // HALO
    n_tiles = T // tm
    KP = wdw.shape[0]

    def body(dy_ref, x_ref, dc_ref, dcnext_ref, y_ref, yprev_ref, p_ref, gpre_ref, wdw_ref, w_ref,
             dx_ref, hn_ref, dp_ref, dwdw_ref, dgpre_ref, ybuf, dcbuf, dyg_scr, dw8_scr):
        i = pl.program_id(0)
        first = i == 0
        ybuf[0:HALO, :] = jnp.where(i > 0, yprev_ref[...], 0.0)
        ybuf[HALO:HALO + tm, :] = y_ref[...]
        dcbuf[0:tm, :] = dc_ref[...]
        dcbuf[tm:tm + HALO, :] = jnp.where(i < n_tiles - 1, dcnext_ref[...], 0.0)
        off = HALO - (CONV_W - 1)
        @pl.when(first)
        def _():
            dw8_scr[...] = jnp.zeros_like(dw8_scr)

        for r0 in range(0, tm, CONV_RB):
            for c0 in range(0, C, CONV_CB):
                cols = slice(c0, c0 + CONV_CB)
                dcb = dcbuf[r0:r0 + CONV_RB, cols]
                acc = jnp.zeros((CONV_RB, CONV_CB), F32)
                for s, win in _shifted_windows(dcbuf, r0, cols, 0, CONV_W):
                    k = CONV_W - 1 - s
                    acc = acc + wdw_ref[k:k + 1, cols] * win
                dyg_scr[r0:r0 + CONV_RB, cols] = acc
                for s, win in _shifted_windows(ybuf, r0, cols, off, off + CONV_W):
                    dw8_scr[s - off, :, cols] += jnp.sum((dcb * win).reshape(CONV_RB // 8, 8, CONV_CB), axis=0)

        @pl.when(i == n_tiles - 1)
        def _():
            dwdw_ref[...] = jnp.sum(dw8_scr[...], axis=1)

        dhn = None
        for c in range(nc // 2):
            cols = slice(c * E, (c + 1) * E)
            av = p_ref[c].astype(F32)
            sg = jax.nn.sigmoid(p_ref[c + nc // 2].astype(F32))
            dygc = dyg_scr[:, cols]
            da = (dygc * sg).astype(BF16)
            dgt = (dygc * av * sg * (1.0 - sg)).astype(BF16)
            dp_ref[c] = da
            dp_ref[c + nc // 2] = dgt
            t = _dot_nt(da, w_ref[c]) + _dot_nt(dgt, w_ref[c + nc // 2])
            dhn = t if dhn is None else dhn + t
        xh, rx = _rms_stats(x_ref[...])
        hn_ref[...] = (xh * gpre_ref[...]).astype(BF16)
        dxn, dgq = _rms_bwd(xh, rx, gpre_ref[...], dhn)
        dx_ref[...] = dy_ref[...] + dxn
        _acc_out(dgpre_ref, first, dgq)

    row = pl.BlockSpec((tm, D), lambda i: (i, 0))
    crow = pl.BlockSpec((tm, C), lambda i: (i, 0))
    prev = pl.BlockSpec((HALO, C), lambda i: (jnp.maximum(i * per - 1, 0), 0))
    nxt = pl.BlockSpec((HALO, C), lambda i: (jnp.minimum((i + 1) * per, T // HALO - 1), 0))
    chunks = pl.BlockSpec((nc, tm, E), lambda i: (0, i, 0))

    def whole(shape):
        return pl.BlockSpec(shape, lambda i: (0,) * len(shape))

    return pl.pallas_call(
        body,
        name="conv_bwd_a",
        grid=(n_tiles,),
        in_specs=[row, row, crow, nxt, crow, prev, chunks, _resident((1, D)), _resident(wdw.shape),
                  _chunks_spec(wpw1, sel)],
        out_specs=[row, row, chunks, whole((KP, C)), whole((1, D))],
        out_shape=[jax.ShapeDtypeStruct((T, D), F32), jax.ShapeDtypeStruct((T, D), BF16),
                   jax.ShapeDtypeStruct((nc, T, E), BF16), jax.ShapeDtypeStruct((KP, C), F32),
                   jax.ShapeDtypeStruct((1, D), F32)],
        scratch_shapes=[pltpu.VMEM((HALO + tm, C), F32), pltpu.VMEM((tm + HALO, C), F32),
                        pltpu.VMEM((tm, C), F32), pltpu.VMEM((KP, 8, C), F32)],
        compiler_params=_cparams("arbitrary"),
    )(dy, x, dc, dc, y, y, p, g_pre, wdw, wpw1)


def _loss_head(y, target, tm):
    T, D = y.shape
    tm = min(tm, T)

    def body(y_ref, t_ref, dy_ref, loss_ref):
        e = y_ref[...] - t_ref[...]
        dy_ref[...] = e * (1.0 / D)
        part = jnp.sum(jnp.sum(e * e, axis=-1, keepdims=True), axis=0, keepdims=True) * (0.5 / D)
        _acc_out(loss_ref, pl.program_id(0) == 0, jnp.broadcast_to(part, loss_ref.shape))

    row = pl.BlockSpec((tm, D), lambda i: (i, 0))
    return pl.pallas_call(
        body,
        name="loss_head",
        grid=(T // tm,),
        in_specs=[row, row],
        out_specs=[row, pl.BlockSpec((8, 128), lambda i: (0, 0))],
        out_shape=[jax.ShapeDtypeStruct((T, D), F32), jax.ShapeDtypeStruct((8, 128), F32)],
        compiler_params=_cparams("arbitrary"),
    )(y, target)


def _row_tile(rows, cols, itemsize_budget=2 * 1024 * 1024):
    want = max(16, itemsize_budget // (4 * cols))
    if rows <= want:
        return rows
    t = (want // 16) * 16
    while t > 16 and rows % t:
        t -= 16
    return t if rows % t == 0 else rows


def _sum_parts(parts):
    n, R, C = parts.shape
    tr = _row_tile(R, C * n // 2 if parts.dtype == BF16 else C * n)

    def body(p_ref, o_ref):
        acc = p_ref[0].astype(F32)
        for s in range(1, n):
            acc = acc + p_ref[s].astype(F32)
        o_ref[...] = acc

    return pl.pallas_call(
        body,
        name="sum_parts",
        grid=(R // tr,),
        in_specs=[pl.BlockSpec((n, tr, C), lambda i: (0, i, 0))],
        out_specs=pl.BlockSpec((tr, C), lambda i: (i, 0)),
        out_shape=jax.ShapeDtypeStruct((R, C), F32),
        compiler_params=_cparams("parallel"),
    )(parts)


def _cast_into_slot(w, entry):
    _, _, R, C = w.shape
    tr = _row_tile(R, C)

    def body(w_ref, o_ref):
        o_ref[...] = w_ref[...].astype(BF16)

    def own_slot(h, i):
        return 2 * lax.axis_index("x") + lax.axis_index("y"), h, i, 0

    return pl.pallas_call(
        body,
        name="cast_into_slot",
        grid=(2, R // tr),
        in_specs=[pl.BlockSpec((None, None, tr, C), lambda h, i: (entry, h, i, 0))],
        out_specs=pl.BlockSpec((None, None, tr, C), own_slot),
        out_shape=jax.ShapeDtypeStruct((N_CHIPS, 2, R, C), BF16),
        compiler_params=_cparams("parallel", "parallel"),
    )(w)


def _sum_with_own(arrived, own):
    n, R, C = arrived.shape
    tr = _row_tile(R, C * (n + 1) // 2)

    def body(a_ref, own_ref, o_ref):
        acc = own_ref[...].astype(F32)
        for s in range(n):
            acc = acc + a_ref[s].astype(F32)
        o_ref[...] = acc

    def own_piece(i):
        return 2 * lax.axis_index("x") + lax.axis_index("y"), lax.axis_index("c"), i, 0

    return pl.pallas_call(
        body,
        name="sum_with_own",
        grid=(R // tr,),
        in_specs=[pl.BlockSpec((n, tr, C), lambda i: (0, i, 0)), pl.BlockSpec((None, None, tr, C), own_piece)],
        out_specs=pl.BlockSpec((None, tr, C), lambda i: (lax.axis_index("c"), i, 0)),
        out_shape=jax.ShapeDtypeStruct((2, R, C), F32),
        compiler_params=_cparams("parallel"),
    )(arrived, own)


def _adamw(w, g, m, v):
    R, C = w.shape
    tr = _row_tile(R, C * 7 // 2)
    c1 = 1.0 - ADAM_B1 ** ADAM_STEP
    c2 = 1.0 - ADAM_B2 ** ADAM_STEP

    def body(w_ref, g_ref, m_ref, v_ref, d_ref, mo_ref, vo_ref):
        g = g_ref[...]
        mn = ADAM_B1 * m_ref[...] + (1.0 - ADAM_B1) * g
        vn = ADAM_B2 * v_ref[...] + (1.0 - ADAM_B2) * (g * g)
        mo_ref[...] = mn
        vo_ref[...] = vn
        d_ref[...] = -ADAM_LR * ((mn / c1) / (jnp.sqrt(vn / c2) + ADAM_EPS) + ADAM_WD * w_ref[...])

    blk = pl.BlockSpec((tr, C), lambda i: (i, 0))
    shp = jax.ShapeDtypeStruct((R, C), F32)
    return pl.pallas_call(
        body,
        name="adamw",
        grid=(R // tr,),
        in_specs=[blk, blk, blk, blk],
        out_specs=[blk, blk, blk],
        out_shape=[shp, shp, shp],
        compiler_params=_cparams("parallel"),
    )(w, g, m, v)


def _adamw_into(w, g, m, v, outs, sel):
    n, R, C = w.shape
    tr = _row_tile(R, C)
    c1 = 1.0 - ADAM_B1 ** ADAM_STEP
    c2 = 1.0 - ADAM_B2 ** ADAM_STEP

    def body(w_ref, g_ref, m_ref, v_ref, *rest):
        go_ref, d_ref, mo_ref, vo_ref = rest[-4:]
        g = g_ref[...]
        mn = ADAM_B1 * m_ref[...] + (1.0 - ADAM_B1) * g
        vn = ADAM_B2 * v_ref[...] + (1.0 - ADAM_B2) * (g * g)
        go_ref[...] = g
        mo_ref[...] = mn
        vo_ref[...] = vn
        d_ref[...] = -ADAM_LR * ((mn / c1) / (jnp.sqrt(vn / c2) + ADAM_EPS) + ADAM_WD * w_ref[...])

    entry = pl.BlockSpec((None, tr, C), lambda i: (sel, i, 0))
    hbm = pl.BlockSpec(memory_space=pl.ANY)
    have = outs is not None
    shp = jax.ShapeDtypeStruct((n, R, C), F32)
    return pl.pallas_call(
        body,
        name="adamw_into",
        grid=(R // tr,),
        in_specs=[entry, pl.BlockSpec((tr, C), lambda i: (i, 0)), entry, entry] + ([hbm] * 4 if have else []),
        out_specs=[entry] * 4,
        out_shape=[shp] * 4,
        input_output_aliases={4 + t: t for t in range(4)} if have else {},
        compiler_params=_cparams("parallel"),
    )(w, g, m, v, *(outs if have else ()))


def _gather_weights(halved, whole):
    nh, nw = len(halved), len(whole)

    def body(*refs):
        w_in = refs[nh:nh + nw]
        h_out, w_out = refs[nh + nw:2 * nh + nw], refs[2 * nh + nw:2 * (nh + nw)]
        ws_send, ws_recv, loc_sem = refs[2 * (nh + nw):2 * (nh + nw) + 3]
        plan = _Gather(h_out, refs[2 * (nh + nw) + 3:])
        x, y, c = _my_place()
        me_chip = 2 * x + y
        plan.start()

        def small(a, j, slot, to):
            return pltpu.make_async_remote_copy(src_ref=w_in[a], dst_ref=w_out[a].at[slot],
                                                send_sem=ws_send.at[a, j], recv_sem=ws_recv.at[a, j],
                                                device_id=to, device_id_type=MESH)

        for a in range(nw):
            pltpu.make_async_copy(w_in[a], w_out[a].at[me_chip], loc_sem.at[a]).start()
            for j, ch in enumerate(plan.chips):
                small(a, j, me_chip, (*ch, c)).start()
        plan.forward()
        plan.finish()
        for a in range(nw):
            for j, ch in enumerate(plan.chips):
                cp = small(a, j, 2 * ch[0] + ch[1], (x, y, c))
                cp.wait_recv()
                cp.wait_send()
            pltpu.make_async_copy(w_in[a], w_out[a].at[me_chip], loc_sem.at[a]).wait()

    hbm = pl.BlockSpec(memory_space=pl.ANY)
    outs = pl.pallas_call(
        body,
        name="gather_weights",
        in_specs=[hbm] * (nh + nw),
        out_specs=[hbm] * (nh + nw),
        out_shape=[jax.ShapeDtypeStruct(a.shape, a.dtype) for a in halved]
        + [jax.ShapeDtypeStruct((N_CHIPS, *a.shape), a.dtype) for a in whole],
        input_output_aliases={a: a for a in range(nh)},
        scratch_shapes=[pltpu.SemaphoreType.DMA((max(nw, 1), 3)), pltpu.SemaphoreType.DMA((max(nw, 1), 3)),
                        pltpu.SemaphoreType.DMA((max(nw, 1),))] + _Gather.semaphores(nh),
    )(*halved, *whole)
    return outs[:nh], outs[nh:]


def _scatter_grads(grads, halves, into):
    n = len(grads)

    def body(*refs):
        plan = _Scatter(refs[:n], refs[2 * n:3 * n], refs[3 * n:], halves)
        plan.start()
        plan.finish()

    hbm = pl.BlockSpec(memory_space=pl.ANY)
    return pl.pallas_call(
        body,
        name="scatter_grads",
        in_specs=[hbm] * (2 * n),
        out_specs=[hbm] * n,
        out_shape=[jax.ShapeDtypeStruct(t.shape, t.dtype) for t in into],
        input_output_aliases={n + a: a for a in range(n)},
        scratch_shapes=_Scatter.semaphores(n),
    )(*grads, *into)


def _swap_halves(halves):
    n = len(halves)

    def body(*refs):
        h_out = refs[n:2 * n]
        send_sem, recv_sem = refs[2 * n:]
        x, y, c = _my_place()
        sib = (x, y, 1 - c)
        for a in range(n):
            pltpu.make_async_remote_copy(src_ref=h_out[a].at[c], dst_ref=h_out[a].at[c], send_sem=send_sem.at[a],
                                         recv_sem=recv_sem.at[a], device_id=sib, device_id_type=MESH).start()
        for a in range(n):
            cp = pltpu.make_async_remote_copy(src_ref=h_out[a].at[c], dst_ref=h_out[a].at[1 - c],
                                              send_sem=send_sem.at[a], recv_sem=recv_sem.at[a], device_id=sib,
                                              device_id_type=MESH)
            cp.wait_send()
            cp.wait_recv()

    hbm = pl.BlockSpec(memory_space=pl.ANY)
    return pl.pallas_call(
        body,
        name="swap_halves",
        in_specs=[hbm] * n,
        out_specs=[hbm] * n,
        out_shape=[jax.ShapeDtypeStruct(h.shape, h.dtype) for h in halves],
        input_output_aliases={a: a for a in range(n)},
        scratch_shapes=[pltpu.SemaphoreType.DMA((n,)), pltpu.SemaphoreType.DMA((n,))],
    )(*halves)


TM_FFN = 512
TM_FFN_FWD = 1024
RB_FFN_FWD = 512
RB_FFN_BWD = 256
TM_SGU = 256
TM_SGU_FWD = 256
TM_CONV = 256
TK_WGRAD = 4096
TM_LOSS = 1024


FFN_KINDS = ("ff_w_gate", "ff_w_up", "ff_w_down")


def _layer_kinds(i):
    return FFN_KINDS + (("sgu_w_in", "sgu_w_out") if i % 2 == 0 else ("conv_w_pw1", "conv_w_pw2"))


def _local_step(x, target, G, W, exchange=None, pack_small=None):
    depth = W["norm_g"].shape[0]
    G = [dict(g) for g in G]
    saved = []
    vec = lambda v: v.reshape(1, -1)

    def mixer_w(i, k):
        w = G[i][k]
        return w.reshape(w.shape[0], -1, w.shape[-1])
    wsm, wsmt, bsb = [], [], []
    n_sgu = W["sgu_w_spatial"].shape[0]
    causal = jnp.tril(jnp.ones((CHUNK, CHUNK), dtype=bool))
    dgrp = W["sgu_ln_g"].shape[1] // N_GROUPS
    for jx in range(n_sgu):
        ws = jnp.where(causal[None], W["sgu_w_spatial"][jx], 0.0).astype(BF16)
        wsm.append(ws)
        wsmt.append(jnp.swapaxes(ws, 1, 2))
        bsb.append(jnp.broadcast_to(W["sgu_b_spatial"][jx][:, :, None], (N_GROUPS, CHUNK, dgrp)))
    kp = HALO
    wdw = [jnp.pad(W["conv_w_dw"][jx], ((0, kp - CONV_W), (0, 0))) for jx in range(W["conv_w_dw"].shape[0])]

    def ffn(x, i, f_idx, gather=()):
        g = W["norm_g"][i]
        return _ffn_fwd(x, vec(g[4 * f_idx]), vec(g[4 * f_idx + 1]), G[i]["ff_w_gate"], G[i]["ff_w_up"],
                        G[i]["ff_w_down"], (f_idx,), TM_FFN_FWD, gather)

    for i in range(depth):
        g = W["norm_g"][i]
        rec = {"x0": x}
        if exchange is not None and i + 1 < depth:
            kinds = _layer_kinds(i + 1)
            x, rec["a1"], rec["b1"], rec["f1"], *filled, rec["h1"] = ffn(x, i, 0, [G[i + 1][k] for k in kinds])
            G[i + 1] = dict(zip(kinds, filled))
        else:
            x, rec["a1"], rec["b1"], rec["f1"], rec["h1"] = ffn(x, i, 0)
        rec["x1"] = x
        j = i // 2
        if i % 2 == 0:
            x, rec["zp"], rec["m"] = _sgu_fwd(
                x, vec(g[2]), vec(g[3]), mixer_w(i, "sgu_w_in"), vec(W["sgu_ln_g"][j]), vec(W["sgu_ln_b"][j]),
                wsm[j], bsb[j], mixer_w(i, "sgu_w_out"), (), TM_SGU_FWD)
        else:
            rec["y"], rec["p"] = _conv_fwd_a(x, vec(g[2]), mixer_w(i, "conv_w_pw1"), (), TM_CONV)
            x, rec["c"], rec["m"] = _conv_fwd_b(
                x, rec["y"], wdw[j], vec(W["conv_b_dw"][j]), vec(W["conv_ln_g"][j]), vec(W["conv_ln_b"][j]),
                mixer_w(i, "conv_w_pw2"), vec(g[3]), (), TM_CONV)
        rec["x2"] = x
        x, rec["a2"], rec["b2"], rec["f2"], rec["h2"] = ffn(x, i, 1)
        saved.append(rec)

    dx, loss_tile = _loss_head(x, target, TM_LOSS)
    loss = loss_tile[0, 0]

    big = [{k: None for k in _layer_kinds(i)} for i in range(depth)]
    small = {k: [None] * W[k].shape[0] for k in
             ("sgu_ln_g", "sgu_ln_b", "sgu_w_spatial", "sgu_b_spatial", "conv_w_dw", "conv_b_dw", "conv_ln_g",
              "conv_ln_b")}
    dnorm = [[None] * 6 for _ in range(depth)]
    pieces = [None] * depth
    waiting = []

    def wgrad(i, k, a, b, sel, rider=None):
        like = G[i][k] if sel else mixer_w(i, k)
        out = _tn_matmul(a, b, big[i][k], like, sel, TK_WGRAD, rider)
        big[i][k], rode = out if rider else (out, None)
        return rode

    def as_pieces(b):
        return b.reshape(N_CHIPS, 2, -1, b.shape[-1])

    def small_grads():
        out = {k: jnp.stack(v) for k, v in small.items()}
        out["norm_g"] = jnp.stack([jnp.stack(r) for r in dnorm])
        return out

    def ffn_back(dx, i, f_idx, xin, a, b, f, h, send, last=False):
        g = W["norm_g"][i]
        dx, dz, s, da, db, dgpre, dgpost, *arrived = _ffn_bwd(
            dx, xin, f, a, b, vec(g[4 * f_idx]), vec(g[4 * f_idx + 1]),
            G[i]["ff_w_gate"], G[i]["ff_w_up"], G[i]["ff_w_down"], (f_idx,), TM_FFN, list(send.values()))
        dnorm[i][4 * f_idx] = dgpre[0]
        dnorm[i][4 * f_idx + 1] = dgpost[0]
        arrived = dict(zip(send, arrived))
        if not last:
            wgrad(i, "ff_w_gate", h, da, (f_idx,))
            wgrad(i, "ff_w_up", h, db, (f_idx,))
            wgrad(i, "ff_w_down", s, dz, (f_idx,))
            return dx, arrived, None
        (shared,) = wgrad(i, "ff_w_gate", h, da, (f_idx,), _ShareRider(pack_small(small_grads())))
        for k, nxt, lhs, rhs in (("ff_w_gate", "ff_w_up", h, db), ("ff_w_up", "ff_w_down", s, dz)):
            (arrived[k],) = wgrad(i, nxt, lhs, rhs, (f_idx,),
                                  _ScatterRider([as_pieces(big[i][k])], [(f_idx,)], [arrived[k]]))
        (arrived["ff_w_down"],) = _scatter_grads([as_pieces(big[i]["ff_w_down"])], [(f_idx,)], [arrived["ff_w_down"]])
        return dx, arrived, shared

    for i in reversed(range(depth)):
        rec = saved[i]
        g = W["norm_g"][i]
        j = i // 2
        if exchange is not None and waiting:
            sent = waiting.pop()
            dx, arrived, _ = ffn_back(dx, i, 1, rec["x2"], rec["a2"], rec["b2"], rec["f2"], rec["h2"],
                                      {k: (p, (0, 1)) for k, p in pieces[sent].items()})
            exchange(sent, pieces[sent], arrived)
        else:
            dx, _, _ = ffn_back(dx, i, 1, rec["x2"], rec["a2"], rec["b2"], rec["f2"], rec["h2"], {})
        if i % 2 == 0:
            (dx, hn, dzp, gated, dm, dws, dbs_acc, dlng, dlnb, dgpre, dgpost) = _sgu_bwd(
                dx, rec["x1"], rec["m"], rec["zp"], vec(g[2]), vec(g[3]), mixer_w(i, "sgu_w_in"),
                vec(W["sgu_ln_g"][j]), vec(W["sgu_ln_b"][j]), wsm[j], wsmt[j], bsb[j], mixer_w(i, "sgu_w_out"), (),
                TM_SGU)
            wgrad(i, "sgu_w_in", hn, dzp, ())
            wgrad(i, "sgu_w_out", gated, dm, ())
            small["sgu_w_spatial"][j] = jnp.where(causal[None], dws, 0.0)
            small["sgu_b_spatial"][j] = dbs_acc.reshape(CHUNK, N_GROUPS, dgrp).sum(-1).T
            small["sgu_ln_g"][j] = dlng[0]
            small["sgu_ln_b"][j] = dlnb[0]
        else:
            dm, q, dc, dlng, dlnb, dbdw, dgpost = _conv_bwd_b(
                dx, rec["m"], rec["c"], vec(W["conv_ln_g"][j]), vec(W["conv_ln_b"][j]), mixer_w(i, "conv_w_pw2"),
                vec(g[3]), (), TM_CONV)
            dx, hn, dp, dwdw, dgpre = _conv_bwd_a(
                dx, rec["x1"], dc, rec["y"], rec["p"], vec(g[2]), wdw[j], mixer_w(i, "conv_w_pw1"), (), TM_CONV)
            wgrad(i, "conv_w_pw1", hn, dp, ())
            wgrad(i, "conv_w_pw2", q, dm, ())
            small["conv_w_dw"][j] = dwdw[:CONV_W]
            small["conv_b_dw"][j] = dbdw[0]
            small["conv_ln_g"][j] = dlng[0]
            small["conv_ln_b"][j] = dlnb[0]
        dnorm[i][2] = dgpre[0]
        dnorm[i][3] = dgpost[0]
        if exchange is not None and i == 0:
            dx, arrived, shared = ffn_back(
                dx, i, 0, rec["x0"], rec["a1"], rec["b1"], rec["f1"], rec["h1"],
                {k: (as_pieces(b), (1,) if k in FFN_KINDS else (0, 1)) for k, b in big[i].items()}, last=True)
            pieces[i] = {k: as_pieces(b) for k, b in big[i].items()}
            exchange(i, pieces[i], arrived)
        else:
            dx, _, _ = ffn_back(dx, i, 0, rec["x0"], rec["a1"], rec["b1"], rec["f1"], rec["h1"], {})
            pieces[i] = {k: as_pieces(b) for k, b in big[i].items()}
            waiting.append(i)

    return loss, dx, pieces, (shared if exchange is not None else small_grads())


BIG = ("ff_w_gate", "ff_w_up", "ff_w_down", "sgu_w_in", "sgu_w_out", "conv_w_pw1", "conv_w_pw2")
SHARDED_SMALL = ("norm_g", "conv_w_dw", "conv_b_dw", "conv_ln_g", "conv_ln_b")
REPLICATED = ("sgu_ln_g", "sgu_ln_b", "sgu_w_spatial", "sgu_b_spatial")
WEIGHTS = ("norm_g", "ff_w_gate", "ff_w_up", "ff_w_down", "sgu_w_in", "sgu_ln_g", "sgu_ln_b", "sgu_w_spatial",
           "sgu_b_spatial", "sgu_w_out", "conv_w_pw1", "conv_w_dw", "conv_b_dw", "conv_ln_g", "conv_ln_b",
           "conv_w_pw2")


def _rows8(a, width):
    r = a.reshape(-1, width)
    pad = (-r.shape[0]) % 8
    return jnp.pad(r, ((0, pad), (0, 0))) if pad else r


def _pack(arrs, width):
    parts = [_rows8(a, width) for a in arrs]
    return jnp.concatenate(parts, axis=0), [p.shape[0] for p in parts]


def _unpack(buf, like):
    out, r0 = [], 0
    width = buf.shape[-1]
    for a in like:
        n = -(-(a.size // width) // 8) * 8
        rows = a.size // width
        out.append(buf[..., r0:r0 + rows, :].reshape(*buf.shape[:-2], *a.shape))
        r0 += n
    return out


def kernel(x, norm_g, ff_w_gate, ff_w_up, ff_w_down, sgu_w_in, sgu_ln_g, sgu_ln_b, sgu_w_spatial, sgu_b_spatial, sgu_w_out, conv_w_pw1, conv_w_dw, conv_b_dw, conv_ln_g, conv_ln_b, conv_w_pw2, loss_target, m_norm_g, m_ff_w_gate, m_ff_w_up, m_ff_w_down, m_sgu_w_in, m_sgu_ln_g, m_sgu_ln_b, m_sgu_w_spatial, m_sgu_b_spatial, m_sgu_w_out, m_conv_w_pw1, m_conv_w_dw, m_conv_b_dw, m_conv_ln_g, m_conv_ln_b, m_conv_w_pw2, v_norm_g, v_ff_w_gate, v_ff_w_up, v_ff_w_down, v_sgu_w_in, v_sgu_ln_g, v_sgu_ln_b, v_sgu_w_spatial, v_sgu_b_spatial, v_sgu_w_out, v_conv_w_pw1, v_conv_w_dw, v_conv_b_dw, v_conv_ln_g, v_conv_ln_b, v_conv_w_pw2):
    w = dict(norm_g=norm_g, ff_w_gate=ff_w_gate, ff_w_up=ff_w_up, ff_w_down=ff_w_down, sgu_w_in=sgu_w_in,
             sgu_ln_g=sgu_ln_g, sgu_ln_b=sgu_ln_b, sgu_w_spatial=sgu_w_spatial, sgu_b_spatial=sgu_b_spatial,
             sgu_w_out=sgu_w_out, conv_w_pw1=conv_w_pw1, conv_w_dw=conv_w_dw, conv_b_dw=conv_b_dw,
             conv_ln_g=conv_ln_g, conv_ln_b=conv_ln_b, conv_w_pw2=conv_w_pw2)
    mom = dict(norm_g=m_norm_g, ff_w_gate=m_ff_w_gate, ff_w_up=m_ff_w_up, ff_w_down=m_ff_w_down,
               sgu_w_in=m_sgu_w_in, sgu_ln_g=m_sgu_ln_g, sgu_ln_b=m_sgu_ln_b, sgu_w_spatial=m_sgu_w_spatial,
               sgu_b_spatial=m_sgu_b_spatial, sgu_w_out=m_sgu_w_out, conv_w_pw1=m_conv_w_pw1,
               conv_w_dw=m_conv_w_dw, conv_b_dw=m_conv_b_dw, conv_ln_g=m_conv_ln_g, conv_ln_b=m_conv_ln_b,
               conv_w_pw2=m_conv_w_pw2)
    vel = dict(norm_g=v_norm_g, ff_w_gate=v_ff_w_gate, ff_w_up=v_ff_w_up, ff_w_down=v_ff_w_down,
               sgu_w_in=v_sgu_w_in, sgu_ln_g=v_sgu_ln_g, sgu_ln_b=v_sgu_ln_b, sgu_w_spatial=v_sgu_w_spatial,
               sgu_b_spatial=v_sgu_b_spatial, sgu_w_out=v_sgu_w_out, conv_w_pw1=v_conv_w_pw1,
               conv_w_dw=v_conv_w_dw, conv_b_dw=v_conv_b_dw, conv_ln_g=v_conv_ln_g, conv_ln_b=v_conv_ln_b,
               conv_w_pw2=v_conv_w_pw2)
    T, D = x.shape[1], x.shape[2]
    shard_w = conv_b_dw.shape[1]

    xi, yi, ci = _my_place()
    me_chip = (2 * xi + yi).astype(jnp.int32)
    me = (4 * xi + 2 * yi + ci).astype(jnp.int32)

    depth = norm_g.shape[0]

    def entry(k, i):
        return i if k in FFN_KINDS else i // 2

    def stacked(a):
        return a.reshape(a.shape[0], -1, a.shape[-1])

    G = []
    for i in range(depth):
        G.append({k: _cast_into_slot(w[k].reshape(w[k].shape[0], 2, -1, w[k].shape[-1]), entry(k, i))
                  for k in _layer_kinds(i)})
    small_buf, _ = _pack([w[k] for k in SHARDED_SMALL], shard_w)
    first, (small_all,) = _gather_weights(list(G[0].values()), [small_buf])
    G[0] = dict(zip(G[0], first))
    W = {}
    for k, part in zip(SHARDED_SMALL, _unpack(small_all, [w[k] for k in SHARDED_SMALL])):
        W[k] = jnp.moveaxis(part, 0, -2).reshape(*w[k].shape[:-1], N_CHIPS * shard_w)
    for k in REPLICATED:
        W[k] = w[k]

    results = {k: None for k in BIG}

    def reduce_and_update(i, pieces, arrived):
        kinds = list(pieces)
        both = _swap_halves([_sum_with_own(arrived[k], pieces[k]) for k in kinds])
        for k, g in zip(kinds, both):
            results[k] = _adamw_into(stacked(w[k]), g.reshape(-1, g.shape[-1]), stacked(mom[k]), stacked(vel[k]),
                                     results[k], entry(k, i))

    def pack_small(small):
        sbuf, _ = _pack([small[k] for k in SHARDED_SMALL + REPLICATED], D)
        return lax.dynamic_update_slice(jnp.zeros((N_DEV, *sbuf.shape), F32), sbuf[None], (me, 0, 0))

    loss, dx, _, shared = _local_step(x[0], loss_target[0], G, W, reduce_and_update, pack_small)
    loss = lax.psum(loss, ("x", "y", "c"))
    grads, delta, new_m, new_v = {}, {}, {}, {}
    for k in BIG:
        grads[k], delta[k], new_m[k], new_v[k] = (t.reshape(w[k].shape) for t in results[k])

    ssum = _sum_parts(shared)
    for k, gfull in zip(SHARDED_SMALL + REPLICATED, _unpack(ssum, [W[k] for k in SHARDED_SMALL + REPLICATED])):
        if k in SHARDED_SMALL:
            gfull = lax.dynamic_slice_in_dim(gfull, me_chip * shard_w, shard_w, axis=gfull.ndim - 1)
        grads[k] = gfull

    for names, width in ((SHARDED_SMALL, shard_w), (REPLICATED, CHUNK)):
        packed = [_pack([src[k] for k in names], width)[0] for src in (w, grads, mom, vel)]
        outs = _adamw(*packed)
        for res, out in zip((delta, new_m, new_v), outs):
            for k, a in zip(names, _unpack(out, [w[k] for k in names])):
                res[k] = a

    return (loss, dx[None], *[grads[k] for k in WEIGHTS], *[delta[k] for k in WEIGHTS],
            *[new_m[k] for k in WEIGHTS], *[new_v[k] for k in WEIGHTS])
```

```python
import functools

import jax
import jax.numpy as jnp
from jax import lax
from jax.experimental import pallas as pl
from jax.experimental.pallas import tpu as pltpu

F32 = jnp.float32
BF16 = jnp.bfloat16
EPS = 1e-6
N_CHIPS = 4
N_DEV = 8
N_GROUPS = 8
CHUNK = 128
CONV_W = 31
HALO = 32
CONV_RB = 64
CONV_CB = 256
VMEM_LIMIT_V7X = 60 * 1024 * 1024
MESH = pl.DeviceIdType.MESH

ADAM_LR = 0.001
ADAM_B1 = 0.9
ADAM_B2 = 0.999
ADAM_EPS = 1e-08
ADAM_WD = 0.01
ADAM_STEP = 10
FFN_SCALE = 0.5


def _cparams(*sem, **kw):
    return pltpu.CompilerParams(dimension_semantics=sem, vmem_limit_bytes=VMEM_LIMIT_V7X, **kw)


def _resident(shape):
    return pl.BlockSpec(shape, lambda *_: (0,) * len(shape), pipeline_mode=pl.Buffered(1))


def _dot(a, b):
    return jnp.dot(a, b, preferred_element_type=F32)


def _dot_nt(a, b):
    return lax.dot_general(a, b, (((1,), (1,)), ((), ())), preferred_element_type=F32)


def _dot_tn(a, b):
    return lax.dot_general(a, b, (((0,), (0,)), ((), ())), preferred_element_type=F32)


def _rms_stats(x):
    r = lax.rsqrt(jnp.mean(x * x, axis=-1, keepdims=True) + EPS)
    return x * r, r


def _rms_bwd(xh, r, g, dy):
    dxh = dy * g
    dx = r * (dxh - xh * jnp.mean(dxh * xh, axis=-1, keepdims=True))
    return dx, jnp.sum(dy * xh, axis=0, keepdims=True)


def _ln_stats(parts, width):
    mu = sum(jnp.sum(p, axis=-1, keepdims=True) for p in parts) / width
    cen = [p - mu for p in parts]
    var = sum(jnp.sum(c * c, axis=-1, keepdims=True) for c in cen) / width
    rstd = lax.rsqrt(var + EPS)
    return [c * rstd for c in cen], rstd


def _ln_bwd(vh_parts, rstd, dvh_parts, width):
    m1 = sum(jnp.sum(d, axis=-1, keepdims=True) for d in dvh_parts) / width
    m2 = sum(jnp.sum(d * v, axis=-1, keepdims=True) for d, v in zip(dvh_parts, vh_parts)) / width
    return [rstd * (d - m1 - v * m2) for d, v in zip(dvh_parts, vh_parts)]


_GELU_C = 0.7978845608028654
_GELU_A = 0.044715


def _gelu(x):
    return 0.5 * x * (1.0 + jnp.tanh(_GELU_C * (x + _GELU_A * x * x * x)))


def _gelu_pair(x):
    x2 = x * x
    t = jnp.tanh(_GELU_C * (x + _GELU_A * (x * x2)))
    half = 0.5 * (1.0 + t)
    return x * half, half + (0.5 * _GELU_C) * x * (1.0 - t * t) * (1.0 + (3.0 * _GELU_A) * x2)


def _sigmoid_pair(a):
    e = jnp.exp(jnp.minimum(-a, 80.0))
    sg = 1.0 / (1.0 + e)
    return sg, e * sg


def _acc_out(ref, first, val):
    @pl.when(first)
    def _():
        ref[...] = val

    @pl.when(jnp.logical_not(first))
    def _():
        ref[...] += val


def _my_place():
    return lax.axis_index("x"), lax.axis_index("y"), lax.axis_index("c")


class _Gather:
    def __init__(self, bufs, sems):
        self.bufs = bufs
        self.own_sems, self.fwd_sems = sems[:2], sems[2:4]
        self.x, self.y, self.c = _my_place()
        x, y = self.x, self.y
        self.chips = [(1 - x, y), (x, 1 - y), (1 - x, 1 - y)]

    def _copy(self, a, j, chip, half, to, sems):
        spot = self.bufs[a].at[2 * chip[0] + chip[1], pl.ds(half, 1)]
        return pltpu.make_async_remote_copy(src_ref=spot, dst_ref=spot, send_sem=sems[0].at[a, j],
                                            recv_sem=sems[1].at[a, j], device_id=to, device_id_type=MESH)

    def _own(self, a, j):
        return self._copy(a, j, (self.x, self.y), self.c, (*self.chips[j], self.c), self.own_sems)

    def _passed_on(self, a, j):
        return self._copy(a, j, self.chips[j], self.c, (self.x, self.y, 1 - self.c), self.fwd_sems)

    def start(self):
        for j in range(3):
            for a in range(len(self.bufs)):
                self._own(a, j).start()

    def forward(self):
        me = (self.x, self.y, self.c)
        for j in range(3):
            for a in range(len(self.bufs)):
                self._copy(a, j, self.chips[j], self.c, me, self.own_sems).wait_recv()
                self._passed_on(a, j).start()

    def finish(self):
        me = (self.x, self.y, self.c)
        for j in range(3):
            for a in range(len(self.bufs)):
                self._copy(a, j, self.chips[j], 1 - self.c, me, self.fwd_sems).wait_recv()
        for j in range(3):
            for a in range(len(self.bufs)):
                self._own(a, j).wait_send()
                self._passed_on(a, j).wait_send()

    @staticmethod
    def semaphores(n):
        return [pltpu.SemaphoreType.DMA((n, 3)) for _ in range(4)]


class _Scatter:
    def __init__(self, g_in, g_out, sems, halves):
        self.g_in, self.g_out = g_in, g_out
        self.send_sem, self.recv_sem = sems
        self.halves = halves
        x, y, c = _my_place()
        self.c = c
        self.me = 4 * x + 2 * y + c

    def _piece(self, a, d, slot):
        return pltpu.make_async_remote_copy(
            src_ref=self.g_in[a].at[d // 2, d % 2], dst_ref=self.g_out[a].at[slot],
            send_sem=self.send_sem.at[a, d], recv_sem=self.recv_sem.at[a, slot],
            device_id=(d // 4, (d // 2) % 2, d % 2), device_id_type=MESH)

    def _to(self, a):
        return [d for d in range(N_DEV) if d % 2 in self.halves[a]]

    def start(self):
        for a in range(len(self.g_in)):
            for d in self._to(a):
                @pl.when(d != self.me)
                def _():
                    self._piece(a, d, lax.rem(self.me - d - 1 + N_DEV, N_DEV)).start()

    def finish(self):
        for a in range(len(self.g_in)):
            for h in self.halves[a]:
                @pl.when(self.c == h)
                def _():
                    for slot in range(N_DEV - 1):
                        self._piece(a, 0, slot).wait_recv()
            for d in self._to(a):
                @pl.when(d != self.me)
                def _():
                    self._piece(a, d, 0).wait_send()

    @staticmethod
    def semaphores(n):
        return [pltpu.SemaphoreType.DMA((n, N_DEV)), pltpu.SemaphoreType.DMA((n, N_DEV - 1))]


class _ShareAll:
    def __init__(self, buf, sems):
        self.buf = buf
        self.send_sem, self.recv_sem = sems
        self.x, self.y, self.c = _my_place()
        self.me = 4 * self.x + 2 * self.y + self.c

    def start(self):
        mine = self.buf.at[self.me]
        for d in range(N_DEV):
            @pl.when(d != self.me)
            def _():
                pltpu.make_async_remote_copy(src_ref=mine, dst_ref=mine, send_sem=self.send_sem.at[d],
                                             recv_sem=self.recv_sem.at[self.me],
                                             device_id=(d // 4, (d // 2) % 2, d % 2), device_id_type=MESH).start()

    def finish(self):
        for d in range(N_DEV):
            @pl.when(d != self.me)
            def _():
                cp = pltpu.make_async_remote_copy(src_ref=self.buf.at[self.me], dst_ref=self.buf.at[d],
                                                  send_sem=self.send_sem.at[d], recv_sem=self.recv_sem.at[d],
                                                  device_id=(self.x, self.y, self.c), device_id_type=MESH)
                cp.wait_send()
                cp.wait_recv()

    @staticmethod
    def semaphores():
        return [pltpu.SemaphoreType.DMA((N_DEV,)), pltpu.SemaphoreType.DMA((N_DEV,))]


def _chunk_spec(sel, rows, cols):
    return pl.BlockSpec((None,) * (1 + len(sel)) + (rows, cols), lambda i, j: (j, *sel, 0, 0))


def _chunks_spec(w, sel):
    return pl.BlockSpec((w.shape[0],) + (None,) * len(sel) + w.shape[-2:], lambda *_: (0, *sel, 0, 0),
                        pipeline_mode=pl.Buffered(1))


def _ffn_fwd(x, g_pre, g_post, wg, wu, wd, sel, tm, gather=(), target=None):
    T, D = x.shape
    nj, F = wg.shape[0], wg.shape[-1]
    tm = min(tm, T)
    ni = T // tm
    rb = min(RB_FFN_FWD, tm)
    ng = len(gather)
    lt = int(target is not None)

    def body(*refs):
        x_ref, gpre_ref, gpost_ref, wg_ref, wu_ref, wd_ref = refs[:6]
        o0 = 6 + ng + lt
        xo_ref, a_ref, b_ref, f_ref = refs[o0:o0 + 4]
        h_ref = refs[o0 + 4 + ng]
        acc_scr = refs[o0 + 5 + ng + 2 * lt]
        i = pl.program_id(0)
        j = pl.program_id(1)
        if ng:
            plan = _Gather(refs[o0 + 4:o0 + 4 + ng], refs[o0 + 6 + ng + 2 * lt:])
            pl.when(jnp.logical_and(i == 0, j == 0))(plan.start)
            pl.when(jnp.logical_and(i == (5 * ni) // 8, j == nj - 1))(plan.forward)

        @pl.when(j == 0)
        def _():
            xh, _ = _rms_stats(x_ref[...])
            h_ref[...] = (xh * gpre_ref[...]).astype(BF16)
            acc_scr[...] = jnp.zeros_like(acc_scr)

        for r0 in range(0, tm, rb):
            rows = slice(r0, r0 + rb)
            h = h_ref[rows, :]
            a = _dot(h, wg_ref[...]).astype(BF16)
            b = _dot(h, wu_ref[...]).astype(BF16)
            a_ref[rows, :] = a
            b_ref[rows, :] = b
            sg, _ = _sigmoid_pair(a)
            acc_scr[rows, :] += _dot((a * sg) * b, wd_ref[...])

        @pl.when(j == nj - 1)
        def _():
            f = acc_scr[...]
            f_ref[...] = f
            fh, _ = _rms_stats(f)
            xo = x_ref[...] + FFN_SCALE * (fh * gpost_ref[...])
            xo_ref[...] = xo
            if lt:
                e = xo - refs[6 + ng][...]
                refs[o0 + 5 + ng][...] = e * (1.0 / D)
                part = jnp.sum(jnp.sum(e * e, axis=-1, keepdims=True), axis=0, keepdims=True) * (0.5 / D)
                loss_ref = refs[o0 + 6 + ng]
                _acc_out(loss_ref, i == 0, jnp.broadcast_to(part, loss_ref.shape))

        if ng:
            pl.when(jnp.logical_and(i == ni - 1, j == nj - 1))(plan.finish)

    row = pl.BlockSpec((tm, D), lambda i, j: (i, 0))
    vec = pl.BlockSpec((1, D), lambda i, j: (0, 0))
    w_in = _chunk_spec(sel, D, F)
    w_out = _chunk_spec(sel, F, D)
    act = pl.BlockSpec((None, tm, F), lambda i, j: (j, i, 0))
    hbm = pl.BlockSpec(memory_space=pl.ANY)
    return pl.pallas_call(
        body,
        name="ffn_fwd_gather" if ng else "ffn_fwd",
        grid=(ni, nj),
        in_specs=[row, vec, vec, w_in, w_in, w_out] + [hbm] * ng + [row] * lt,
        out_specs=[row, act, act, row] + [hbm] * ng + [row] + [row, pl.BlockSpec((8, 128), lambda i, j: (0, 0))] * lt,
        out_shape=[
            jax.ShapeDtypeStruct((T, D), F32),
            jax.ShapeDtypeStruct((nj, T, F), BF16),
            jax.ShapeDtypeStruct((nj, T, F), BF16),
            jax.ShapeDtypeStruct((T, D), F32),
        ] + [jax.ShapeDtypeStruct(g.shape, g.dtype) for g in gather] + [jax.ShapeDtypeStruct((T, D), BF16)]
        + [jax.ShapeDtypeStruct((T, D), F32), jax.ShapeDtypeStruct((8, 128), F32)] * lt,
        input_output_aliases={6 + a: 4 + a for a in range(ng)},
        scratch_shapes=[pltpu.VMEM((tm, D), F32)] + (_Gather.semaphores(ng) if ng else []),
        compiler_params=_cparams("arbitrary", "arbitrary"),
    )(x, g_pre, g_post, wg, wu, wd, *gather, *([target] if lt else []))


def _ffn_bwd(dy, x, f, a, b, g_pre, g_post, wg, wu, wd, sel, tm, scatter=()):
    T, D = x.shape
    nj, F = wg.shape[0], wg.shape[-1]
    tm = min(tm, T)
    ni = T // tm
    rb = min(RB_FFN_BWD, tm)
    ns = len(scatter)
    halves = [h for _, h in scatter]
    scatter = [g for g, _ in scatter]

    def body(*refs):
        dy_ref, x_ref, f_ref, a_ref, b_ref, gpre_ref, gpost_ref, wg_ref, wu_ref, wd_ref = refs[:10]
        dx_ref, dz_ref, s_ref, da_ref, db_ref, dgpre_ref, dgpost_ref = refs[10 + ns:17 + ns]
        dh_scr = refs[17 + 2 * ns]
        i = pl.program_id(0)
        j = pl.program_id(1)
        if ns:
            plan = _Scatter(refs[10:10 + ns], refs[17 + ns:17 + 2 * ns], refs[18 + 2 * ns:], halves)
            pl.when(jnp.logical_and(i == 0, j == 0))(plan.start)

        @pl.when(j == 0)
        def _():
            fh, rf = _rms_stats(f_ref[...])
            dz, dg = _rms_bwd(fh, rf, gpost_ref[...], FFN_SCALE * dy_ref[...])
            dz_ref[...] = dz.astype(BF16)
            _acc_out(dgpost_ref, i == 0, dg)
            dh_scr[...] = jnp.zeros_like(dh_scr)

        for r0 in range(0, tm, rb):
            rows = slice(r0, r0 + rb)
            ds = _dot_nt(dz_ref[rows, :], wd_ref[j]).astype(BF16)
            av = a_ref[rows, :]
            bv = b_ref[rows, :]
            sg, one_minus_sg = _sigmoid_pair(av)
            sl = av * sg
            s_ref[rows, :] = sl * bv
            da = (ds * bv) * (sg + sl * one_minus_sg)
            db = ds * sl
            da_ref[rows, :] = da
            db_ref[rows, :] = db
            dh_scr[rows, :] += _dot_nt(da, wg_ref[j]) + _dot_nt(db, wu_ref[j])

        @pl.when(j == nj - 1)
        def _():
            xh, rx = _rms_stats(x_ref[...])
            dxn, dg = _rms_bwd(xh, rx, gpre_ref[...], dh_scr[...])
            dx_ref[...] = dy_ref[...] + dxn
            _acc_out(dgpre_ref, i == 0, dg)

        if ns:
            pl.when(jnp.logical_and(i == ni - 1, j == nj - 1))(plan.finish)

    row = pl.BlockSpec((tm, D), lambda i, j: (i, 0))
    vec = pl.BlockSpec((1, D), lambda i, j: (0, 0))
    w_in = _chunks_spec(wg, sel)
    w_out = _chunks_spec(wd, sel)
    act = pl.BlockSpec((None, tm, F), lambda i, j: (j, i, 0))
    act_shape = jax.ShapeDtypeStruct((nj, T, F), BF16)
    hbm = pl.BlockSpec(memory_space=pl.ANY)
    return pl.pallas_call(
        body,
        name="ffn_bwd_scatter" if ns else "ffn_bwd",
        grid=(ni, nj),
        in_specs=[row, row, row, act, act, vec, vec, w_in, w_in, w_out] + [hbm] * ns,
        out_specs=[row, row, act, act, act, vec, vec] + [hbm] * ns,
        out_shape=[
            jax.ShapeDtypeStruct((T, D), F32),
            jax.ShapeDtypeStruct((T, D), BF16),
            act_shape, act_shape, act_shape,
            jax.ShapeDtypeStruct((1, D), F32),
            jax.ShapeDtypeStruct((1, D), F32),
        ] + [jax.ShapeDtypeStruct((N_DEV - 1, *g.shape[2:]), g.dtype) for g in scatter],
        scratch_shapes=[pltpu.VMEM((tm, D), F32)] + (_Scatter.semaphores(ns) if ns else []),
        compiler_params=_cparams("arbitrary", "arbitrary"),
    )(dy, x, f, a, b, g_pre, g_post, wg, wu, wd, *scatter)


class _ScatterRider:
    name = "scatter"

    def __init__(self, grads, halves, into):
        n = len(grads)
        self.halves = halves
        self.operands = [*grads, *into]
        self.results = [jax.ShapeDtypeStruct(t.shape, t.dtype) for t in into]
        self.aliases = {n + a: a for a in range(n)}
        self.semaphores = _Scatter.semaphores(n)

    def plan(self, in_refs, out_refs, sems):
        return _Scatter(in_refs[:len(out_refs)], out_refs, sems, self.halves)


class _ShareRider:
    name = "share"

    def __init__(self, slots):
        self.operands = [slots]
        self.results = [jax.ShapeDtypeStruct(slots.shape, slots.dtype)]
        self.aliases = {0: 0}
        self.semaphores = _ShareAll.semaphores()

    def plan(self, in_refs, out_refs, sems):
        return _ShareAll(out_refs[0], sems)


def _tn_matmul(a, b, buf, like, sel, tk, rider=None):
    a_chunked = a.ndim == 3
    nj = a.shape[0] if a_chunked else b.shape[0]
    T, M, N = a.shape[-2], a.shape[-1], b.shape[-1]
    tk = min(tk, T)
    nk = T // tk
    have = buf is not None
    n_in = 2 + have + (len(rider.operands) if rider else 0)
    n_out = 1 + (len(rider.results) if rider else 0)

    def body(*refs):
        a_ref, b_ref = refs[:2]
        o_ref = refs[n_in]
        acc_scr = refs[n_in + n_out]
        j = pl.program_id(0)
        k = pl.program_id(1)
        if rider:
            plan = rider.plan(refs[2 + have:n_in], refs[n_in + 1:n_in + n_out], refs[n_in + n_out + 1:])
            pl.when(jnp.logical_and(j == 0, k == 0))(plan.start)

        @pl.when(k == 0)
        def _():
            acc_scr[...] = jnp.zeros_like(acc_scr)

        acc_scr[...] += _dot_tn(a_ref[...], b_ref[...])

        @pl.when(k == nk - 1)
        def _():
            o_ref[...] = acc_scr[...].astype(BF16)

        if rider:
            pl.when(jnp.logical_and(j == nj - 1, k == nk - 1))(plan.finish)

    def spec(chunked, width):
        if chunked:
            return pl.BlockSpec((None, tk, width), lambda j, k: (j, k, 0))
        return pl.BlockSpec((tk, width), lambda j, k: (k, 0))

    hbm = pl.BlockSpec(memory_space=pl.ANY)
    aliases = {2: 0} if have else {}
    if rider:
        aliases.update({2 + have + src: 1 + dst for src, dst in rider.aliases.items()})
    outs = pl.pallas_call(
        body,
        name="tn_matmul_" + rider.name if rider else "tn_matmul",
        grid=(nj, nk),
        in_specs=[spec(a_chunked, M), spec(not a_chunked, N)] + [hbm] * (n_in - 2),
        out_specs=[pl.BlockSpec((None,) * (1 + len(sel)) + (M, N), lambda j, k: (j, *sel, 0, 0))] + [hbm] * (n_out - 1),
        out_shape=[jax.ShapeDtypeStruct(like.shape, BF16)] + (list(rider.results) if rider else []),
        input_output_aliases=aliases,
        scratch_shapes=[pltpu.VMEM((M, N), F32)] + (rider.semaphores if rider else []),
        compiler_params=_cparams("arbitrary", "arbitrary"),
    )(a, b, *([buf] if have else []), *(rider.operands if rider else ()))
    return (outs[0], outs[1:]) if rider else outs[0]


def _sgu_fwd(x, g_pre, g_post, win, lng, lnb, wsm, bsb, wout, sel, tm):
    T, D = x.shape
    nc, E = win.shape[0], win.shape[-1]
    S = 2 * E
    dg = S // N_GROUPS
    wo_rows = wout.shape[-2]
    tm = min(tm, T)
    nq = tm // CHUNK

    def body(x_ref, gpre_ref, gpost_ref, win_ref, lng_ref, lnb_ref, ws_ref, bsb_ref, wout_ref,
             xo_ref, zp_ref, m_ref, u_scr, vn_scr, gt_scr):
        x = x_ref[...]
        xh, _ = _rms_stats(x)
        hn = (xh * gpre_ref[...]).astype(BF16)
        v_parts = []
        for c in range(nc):
            zp = _dot(hn, win_ref[c])
            zp_ref[c] = zp.astype(BF16)
            z = _gelu(zp)
            if c < nc // 2:
                u_scr[:, c * E:(c + 1) * E] = z
            else:
                v_parts.append(z)
        vh_parts, _ = _ln_stats(v_parts, S)
        for c, vh in enumerate(vh_parts):
            cols = slice(c * E, (c + 1) * E)
            vn_scr[:, cols] = (vh * lng_ref[:, cols] + lnb_ref[:, cols]).astype(BF16)
        for q in range(nq):
            rows = slice(q * CHUNK, (q + 1) * CHUNK)
            for g in range(N_GROUPS):
                cols = slice(g * dg, (g + 1) * dg)
                mixed = _dot(ws_ref[g], vn_scr[rows, cols]) + bsb_ref[g]
                gt_scr[rows, cols] = (u_scr[rows, cols] * mixed).astype(BF16)
        m = _dot(gt_scr[:, 0:wo_rows], wout_ref[0])
        for c in range(1, nc):
            m += _dot(gt_scr[:, c * wo_rows:(c + 1) * wo_rows], wout_ref[c])
        m_ref[...] = m
        mh, _ = _rms_stats(m)
        xo_ref[...] = x + mh * gpost_ref[...]

    row = pl.BlockSpec((tm, D), lambda i: (i, 0))
    return pl.pallas_call(
        body,
        name="sgu_fwd",
        grid=(T // tm,),
        in_specs=[row, _resident((1, D)), _resident((1, D)), _chunks_spec(win, sel), _resident((1, S)),
                  _resident((1, S)), _resident(wsm.shape), _resident(bsb.shape), _chunks_spec(wout, sel)],
        out_specs=[row, pl.BlockSpec((nc, tm, E), lambda i: (0, i, 0)), row],
        out_shape=[
            jax.ShapeDtypeStruct((T, D), F32),
            jax.ShapeDtypeStruct((nc, T, E), BF16),
            jax.ShapeDtypeStruct((T, D), F32),
        ],
        scratch_shapes=[pltpu.VMEM((tm, S), F32), pltpu.VMEM((tm, S), BF16), pltpu.VMEM((tm, S), BF16)],
        compiler_params=_cparams("parallel"),
    )(x, g_pre, g_post, win, lng, lnb, wsm, bsb, wout)


def _sgu_bwd(dy, x, m, zp, g_pre, g_post, win, lng, lnb, wsm, wsmt, bsb, wout, sel, tm):
    T, D = x.shape
    nc, E = win.shape[0], win.shape[-1]
    S = 2 * E
    dg = S // N_GROUPS
    wo_rows = wout.shape[-2]
    tm = min(tm, T)
    nq = tm // CHUNK

    def body(dy_ref, x_ref, m_ref, zp_ref, gpre_ref, gpost_ref, win_ref, lng_ref, lnb_ref, ws_ref, wst_ref,
             bsb_ref, wout_ref,
             dx_ref, hn_ref, dzp_ref, gated_ref, dm_ref, dws_ref, dbs_ref, dlng_ref, dlnb_ref, dgpre_ref,
             dgpost_ref, u_scr, d_scr, vh_scr, vn_scr, gg_scr):
        first = pl.program_id(0) == 0
        dy = dy_ref[...]
        mh, rm = _rms_stats(m_ref[...])
        dm, dgp = _rms_bwd(mh, rm, gpost_ref[...], dy)
        _acc_out(dgpost_ref, first, dgp)
        dm = dm.astype(BF16)
        dm_ref[...] = dm
        for c in range(nc):
            d_scr[:, c * wo_rows:(c + 1) * wo_rows] = _dot_nt(dm, wout_ref[c])
        v_parts = []
        for c in range(nc):
            z, gg_scr[c] = _gelu_pair(zp_ref[c])
            if c < nc // 2:
                u_scr[:, c * E:(c + 1) * E] = z.astype(F32)
            else:
                v_parts.append(z.astype(F32))
        vh_parts, rstd = _ln_stats(v_parts, S)
        for c, vh in enumerate(vh_parts):
            cols = slice(c * E, (c + 1) * E)
            vh_scr[:, cols] = vh
            vn_scr[:, cols] = (vh * lng_ref[:, cols] + lnb_ref[:, cols]).astype(BF16)

        @pl.when(first)
        def _():
            dws_ref[...] = jnp.zeros_like(dws_ref)
            dbs_ref[...] = jnp.zeros_like(dbs_ref)
            dlng_ref[...] = jnp.zeros_like(dlng_ref)
            dlnb_ref[...] = jnp.zeros_like(dlnb_ref)

        for q in range(nq):
            rows = slice(q * CHUNK, (q + 1) * CHUNK)
            for g in range(N_GROUPS):
                cols = slice(g * dg, (g + 1) * dg)
                vn = vn_scr[rows, cols]
                mixed = _dot(ws_ref[g], vn) + bsb_ref[g]
                u = u_scr[rows, cols]
                dgt = d_scr[rows, cols]
                gated_ref[(g * dg) // wo_rows, rows, (g * dg) % wo_rows:(g * dg) % wo_rows + dg] = (u * mixed).astype(BF16)
                dmix = dgt * u
                dbs_ref[:, cols] += dmix
                dmix = dmix.astype(BF16)
                dws_ref[g] += _dot_nt(dmix, vn)
                u_scr[rows, cols] = dgt * mixed
                d_scr[rows, cols] = _dot(wst_ref[g], dmix)
        dvn = [d_scr[:, c * E:(c + 1) * E] for c in range(nc // 2)]
        vh = [vh_scr[:, c * E:(c + 1) * E] for c in range(nc // 2)]
        for c, (d, v) in enumerate(zip(dvn, vh)):
            dlng_ref[:, c * E:(c + 1) * E] += jnp.sum(d * v, axis=0, keepdims=True)
            dlnb_ref[:, c * E:(c + 1) * E] += jnp.sum(d, axis=0, keepdims=True)
        dvh = [d * lng_ref[:, c * E:(c + 1) * E] for c, d in enumerate(dvn)]
        dv = _ln_bwd(vh, rstd, dvh, S)
        dhn = None
        for c in range(nc):
            dz = u_scr[:, c * E:(c + 1) * E] if c < nc // 2 else dv[c - nc // 2]
            dzp = dz.astype(BF16) * gg_scr[c]
            dzp_ref[c] = dzp
            t = _dot_nt(dzp, win_ref[c])
            dhn = t if dhn is None else dhn + t
        xh, rx = _rms_stats(x_ref[...])
        hn_ref[...] = (xh * gpre_ref[...]).astype(BF16)
        dxn, dgq = _rms_bwd(xh, rx, gpre_ref[...], dhn)
        dx_ref[...] = dy + dxn
        _acc_out(dgpre_ref, first, dgq)

    row = pl.BlockSpec((tm, D), lambda i: (i, 0))

    def whole(shape):
        return pl.BlockSpec(shape, lambda i: (0,) * len(shape))

    return pl.pallas_call(
        body,
        name="sgu_bwd",
        grid=(T // tm,),
        in_specs=[row, row, row, pl.BlockSpec((nc, tm, E), lambda i: (0, i, 0)), _resident((1, D)), _resident((1, D)),
                  _chunks_spec(win, sel), _resident((1, S)), _resident((1, S)), _resident(wsm.shape),
                  _resident(wsmt.shape), _resident(bsb.shape), _chunks_spec(wout, sel)],
        out_specs=[row, row, pl.BlockSpec((nc, tm, E), lambda i: (0, i, 0)),
                   pl.BlockSpec((nc, tm, wo_rows), lambda i: (0, i, 0)), row,
                   whole((N_GROUPS, CHUNK, CHUNK)), whole((CHUNK, S)), whole((1, S)), whole((1, S)),
                   whole((1, D)), whole((1, D))],
        out_shape=[
            jax.ShapeDtypeStruct((T, D), F32),
            jax.ShapeDtypeStruct((T, D), BF16),
            jax.ShapeDtypeStruct((nc, T, E), BF16),
            jax.ShapeDtypeStruct((nc, T, wo_rows), BF16),
            jax.ShapeDtypeStruct((T, D), BF16),
            jax.ShapeDtypeStruct((N_GROUPS, CHUNK, CHUNK), F32),
            jax.ShapeDtypeStruct((CHUNK, S), F32),
            jax.ShapeDtypeStruct((1, S), F32),
            jax.ShapeDtypeStruct((1, S), F32),
            jax.ShapeDtypeStruct((1, D), F32),
            jax.ShapeDtypeStruct((1, D), F32),
        ],
        scratch_shapes=[pltpu.VMEM((tm, S), F32), pltpu.VMEM((tm, S), F32), pltpu.VMEM((tm, S), F32),
                        pltpu.VMEM((tm, S), BF16), pltpu.VMEM((nc, tm, E), BF16)],
        compiler_params=_cparams("arbitrary"),
    )(dy, x, m, zp, g_pre, g_post, win, lng, lnb, wsm, wsmt, bsb, wout)


def _shifted_windows(buf, r0, cols, lo, hi):
    n = CONV_RB + HALO
    base = buf[r0:r0 + n, cols]
    for r in range(8):
        rolled = base if r == 0 else pltpu.roll(base, n - r, axis=0)
        for s in range(r, hi, 8):
            if s >= lo:
                yield s, rolled[s - r:s - r + CONV_RB]


def _conv_fwd_a(x, g_pre, wpw1, sel, tm):
    T, D = x.shape
    nc, E = wpw1.shape[0], wpw1.shape[-1]
    C = 2 * E
    tm = min(tm, T)

    def body(x_ref, gpre_ref, w_ref, y_ref, p_ref):
        xh, _ = _rms_stats(x_ref[...])
        hn = (xh * gpre_ref[...]).astype(BF16)
        ps = []
        for c in range(nc):
            p = _dot(hn, w_ref[c])
            p_ref[c] = p.astype(BF16)
            ps.append(p)
        for c in range(nc // 2):
            y_ref[:, c * E:(c + 1) * E] = ps[c] * jax.nn.sigmoid(ps[c + nc // 2])

    row = pl.BlockSpec((tm, D), lambda i: (i, 0))
    return pl.pallas_call(
        body,
        name="conv_fwd_a",
        grid=(T // tm,),
        in_specs=[row, _resident((1, D)), _chunks_spec(wpw1, sel)],
        out_specs=[pl.BlockSpec((tm, C), lambda i: (i, 0)), pl.BlockSpec((nc, tm, E), lambda i: (0, i, 0))],
        out_shape=[jax.ShapeDtypeStruct((T, C), F32), jax.ShapeDtypeStruct((nc, T, E), BF16)],
        compiler_params=_cparams("parallel"),
    )(x, g_pre, wpw1)


def _conv_fwd_b(x, y, wdw, bdw, lng, lnb, wpw2, g_post, sel, tm):
    T, D = x.shape
    C = y.shape[1]
    nc, E = wpw2.shape[0], wpw2.shape[-2]
    tm = min(tm, T)
    per = tm // HALO

    def body(x_ref, y_ref, yprev_ref, wdw_ref, bdw_ref, lng_ref, lnb_ref, w_ref, gpost_ref,
             xo_ref, c_ref, m_ref, ybuf):
        i = pl.program_id(0)
        ybuf[0:HALO, :] = jnp.where(i > 0, yprev_ref[...], 0.0)
        ybuf[HALO:HALO + tm, :] = y_ref[...]
        off = HALO - (CONV_W - 1)
        for r0 in range(0, tm, CONV_RB):
            for c0 in range(0, C, CONV_CB):
                cols = slice(c0, c0 + CONV_CB)
                acc = jnp.broadcast_to(bdw_ref[:, cols], (CONV_RB, CONV_CB))
                for s, win in _shifted_windows(ybuf, r0, cols, off, off + CONV_W):
                    acc = acc + wdw_ref[s - off:s - off + 1, cols] * win
                c_ref[r0:r0 + CONV_RB, cols] = acc
        (ch,), _ = _ln_stats([c_ref[...]], C)
        cn = ch * lng_ref[...] + lnb_ref[...]
        qv = (cn * jax.nn.sigmoid(cn)).astype(BF16)
        m = _dot(qv[:, 0:E], w_ref[0])
        for c in range(1, nc):
            m += _dot(qv[:, c * E:(c + 1) * E], w_ref[c])
        m_ref[...] = m
        mh, _ = _rms_stats(m)
        xo_ref[...] = x_ref[...] + mh * gpost_ref[...]

    row = pl.BlockSpec((tm, D), lambda i: (i, 0))
    crow = pl.BlockSpec((tm, C), lambda i: (i, 0))
    prev = pl.BlockSpec((HALO, C), lambda i: (jnp.maximum(i * per - 1, 0), 0))
    return pl.pallas_call(
        body,
        name="conv_fwd_b",
        grid=(T // tm,),
        in_specs=[row, crow, prev, _resident(wdw.shape), _resident((1, C)), _resident((1, C)), _resident((1, C)),
                  _chunks_spec(wpw2, sel), _resident((1, D))],
        out_specs=[row, crow, row],
        out_shape=[jax.ShapeDtypeStruct((T, D), F32), jax.ShapeDtypeStruct((T, C), F32),
                   jax.ShapeDtypeStruct((T, D), F32)],
        scratch_shapes=[pltpu.VMEM((HALO + tm, C), F32)],
        compiler_params=_cparams("parallel"),
    )(x, y, y, wdw, bdw, lng, lnb, wpw2, g_post)


def _conv_bwd_b(dy, m, c, lng, lnb, wpw2, g_post, sel, tm):
    T, D = dy.shape
    C = c.shape[1]
    nc, E = wpw2.shape[0], wpw2.shape[-2]
    tm = min(tm, T)

    def body(dy_ref, m_ref, c_ref, lng_ref, lnb_ref, w_ref, gpost_ref,
             dm_ref, q_ref, dc_ref, dlng_ref, dlnb_ref, dbdw_ref, dgpost_ref, dq_scr):
        first = pl.program_id(0) == 0
        mh, rm = _rms_stats(m_ref[...])
        dm, dgp = _rms_bwd(mh, rm, gpost_ref[...], dy_ref[...])
        _acc_out(dgpost_ref, first, dgp)
        dm = dm.astype(BF16)
        dm_ref[...] = dm
        for k in range(nc):
            dq_scr[:, k * E:(k + 1) * E] = _dot_nt(dm, w_ref[k])
        (ch,), rstd = _ln_stats([c_ref[...]], C)
        cn = ch * lng_ref[...] + lnb_ref[...]
        sg = jax.nn.sigmoid(cn)
        qv = (cn * sg).astype(BF16)
        for k in range(nc):
            q_ref[k] = qv[:, k * E:(k + 1) * E]
        dcn = dq_scr[...] * (sg * (1.0 + cn * (1.0 - sg)))
        _acc_out(dlng_ref, first, jnp.sum(dcn * ch, axis=0, keepdims=True))
        _acc_out(dlnb_ref, first, jnp.sum(dcn, axis=0, keepdims=True))
        (dc,) = _ln_bwd([ch], rstd, [dcn * lng_ref[...]], C)
        dc_ref[...] = dc
        _acc_out(dbdw_ref, first, jnp.sum(dc, axis=0, keepdims=True))

    row = pl.BlockSpec((tm, D), lambda i: (i, 0))
    crow = pl.BlockSpec((tm, C), lambda i: (i, 0))

    def whole(shape):
        return pl.BlockSpec(shape, lambda i: (0,) * len(shape))

    return pl.pallas_call(
        body,
        name="conv_bwd_b",
        grid=(T // tm,),
        in_specs=[row, row, crow, _resident((1, C)), _resident((1, C)), _chunks_spec(wpw2, sel), _resident((1, D))],
        out_specs=[row, pl.BlockSpec((nc, tm, E), lambda i: (0, i, 0)), crow, whole((1, C)), whole((1, C)),
                   whole((1, C)), whole((1, D))],
        out_shape=[jax.ShapeDtypeStruct((T, D), BF16), jax.ShapeDtypeStruct((nc, T, E), BF16),
                   jax.ShapeDtypeStruct((T, C), F32), jax.ShapeDtypeStruct((1, C), F32),
                   jax.ShapeDtypeStruct((1, C), F32), jax.ShapeDtypeStruct((1, C), F32),
                   jax.ShapeDtypeStruct((1, D), F32)],
        scratch_shapes=[pltpu.VMEM((tm, C), F32)],
        compiler_params=_cparams("arbitrary"),
    )(dy, m, c, lng, lnb, wpw2, g_post)


def _conv_bwd_a(dy, x, dc, y, p, g_pre, wdw, wpw1, sel, tm):
    T, D = x.shape
    C = y.shape[1]
    nc, E = wpw1.shape[0], wpw1.shape[-1]
    tm = min(tm, T)
    per = tm // HALO
    n_tiles = T // tm
    KP = wdw.shape[0]

    def body(dy_ref, x_ref, dc_ref, dcnext_ref, y_ref, yprev_ref, p_ref, gpre_ref, wdw_ref, w_ref,
             dx_ref, hn_ref, dp_ref, dwdw_ref, dgpre_ref, ybuf, dcbuf, dyg_scr, dw8_scr):
        i = pl.program_id(0)
        first = i == 0
        ybuf[0:HALO, :] = jnp.where(i > 0, yprev_ref[...], 0.0)
        ybuf[HALO:HALO + tm, :] = y_ref[...]
        dcbuf[0:tm, :] = dc_ref[...]
        dcbuf[tm:tm + HALO, :] = jnp.where(i < n_tiles - 1, dcnext_ref[...], 0.0)
        off = HALO - (CONV_W - 1)
        @pl.when(first)
        def _():
            dw8_scr[...] = jnp.zeros_like(dw8_scr)

        for r0 in range(0, tm, CONV_RB):
            for c0 in range(0, C, CONV_CB):
                cols = slice(c0, c0 + CONV_CB)
                dcb = dcbuf[r0:r0 + CONV_RB, cols]
                acc = jnp.zeros((CONV_RB, CONV_CB), F32)
                for s, win in _shifted_windows(dcbuf, r0, cols, 0, CONV_W):
                    k = CONV_W - 1 - s
                    acc = acc + wdw_ref[k:k + 1, cols] * win
                dyg_scr[r0:r0 + CONV_RB, cols] = acc
                for s, win in _shifted_windows(ybuf, r0, cols, off, off + CONV_W):
                    dw8_scr[s - off, :, cols] += jnp.sum((dcb * win).reshape(CONV_RB // 8, 8, CONV_CB), axis=0)

        @pl.when(i == n_tiles - 1)
        def _():
            dwdw_ref[...] = jnp.sum(dw8_scr[...], axis=1)

        dhn = None
        for c in range(nc // 2):
            cols = slice(c * E, (c + 1) * E)
            av = p_ref[c].astype(F32)
            sg = jax.nn.sigmoid(p_ref[c + nc // 2].astype(F32))
            dygc = dyg_scr[:, cols]
            da = (dygc * sg).astype(BF16)
            dgt = (dygc * av * sg * (1.0 - sg)).astype(BF16)
            dp_ref[c] = da
            dp_ref[c + nc // 2] = dgt
            t = _dot_nt(da, w_ref[c]) + _dot_nt(dgt, w_ref[c + nc // 2])
            dhn = t if dhn is None else dhn + t
        xh, rx = _rms_stats(x_ref[...])
        hn_ref[...] = (xh * gpre_ref[...]).astype(BF16)
        dxn, dgq = _rms_bwd(xh, rx, gpre_ref[...], dhn)
        dx_ref[...] = dy_ref[...] + dxn
        _acc_out(dgpre_ref, first, dgq)

    row = pl.BlockSpec((tm, D), lambda i: (i, 0))
    crow = pl.BlockSpec((tm, C), lambda i: (i, 0))
    prev = pl.BlockSpec((HALO, C), lambda i: (jnp.maximum(i * per - 1, 0), 0))
    nxt = pl.BlockSpec((HALO, C), lambda i: (jnp.minimum((i + 1) * per, T // HALO - 1), 0))
    chunks = pl.BlockSpec((nc, tm, E), lambda i: (0, i, 0))

    def whole(shape):
        return pl.BlockSpec(shape, lambda i: (0,) * len(shape))

    return pl.pallas_call(
        body,
        name="conv_bwd_a",
        grid=(n_tiles,),
        in_specs=[row, row, crow, nxt, crow, prev, chunks, _resident((1, D)), _resident(wdw.shape),
                  _chunks_spec(wpw1, sel)],
        out_specs=[row, row, chunks, whole((KP, C)), whole((1, D))],
        out_shape=[jax.ShapeDtypeStruct((T, D), F32), jax.ShapeDtypeStruct((T, D), BF16),
                   jax.ShapeDtypeStruct((nc, T, E), BF16), jax.ShapeDtypeStruct((KP, C), F32),
                   jax.ShapeDtypeStruct((1, D), F32)],
        scratch_shapes=[pltpu.VMEM((HALO + tm, C), F32), pltpu.VMEM((tm + HALO, C), F32),
                        pltpu.VMEM((tm, C), F32), pltpu.VMEM((KP, 8, C), F32)],
        compiler_params=_cparams("arbitrary"),
    )(dy, x, dc, dc, y, y, p, g_pre, wdw, wpw1)


def _loss_head(y, target, tm):
    T, D = y.shape
    tm = min(tm, T)

    def body(y_ref, t_ref, dy_ref, loss_ref):
        e = y_ref[...] - t_ref[...]
        dy_ref[...] = e * (1.0 / D)
        part = jnp.sum(jnp.sum(e * e, axis=-1, keepdims=True), axis=0, keepdims=True) * (0.5 / D)
        _acc_out(loss_ref, pl.program_id(0) == 0, jnp.broadcast_to(part, loss_ref.shape))

    row = pl.BlockSpec((tm, D), lambda i: (i, 0))
    return pl.pallas_call(
        body,
        name="loss_head",
        grid=(T // tm,),
        in_specs=[row, row],
        out_specs=[row, pl.BlockSpec((8, 128), lambda i: (0, 0))],
        out_shape=[jax.ShapeDtypeStruct((T, D), F32), jax.ShapeDtypeStruct((8, 128), F32)],
        compiler_params=_cparams("arbitrary"),
    )(y, target)


def _row_tile(rows, cols, itemsize_budget=2 * 1024 * 1024):
    want = max(16, itemsize_budget // (4 * cols))
    if rows <= want:
        return rows
    t = (want // 16) * 16
    while t > 16 and rows % t:
        t -= 16
    return t if rows % t == 0 else rows


def _sum_parts(parts):
    n, R, C = parts.shape
    tr = _row_tile(R, C * n // 2 if parts.dtype == BF16 else C * n)

    def body(p_ref, o_ref):
        acc = p_ref[0].astype(F32)
        for s in range(1, n):
            acc = acc + p_ref[s].astype(F32)
        o_ref[...] = acc

    return pl.pallas_call(
        body,
        name="sum_parts",
        grid=(R // tr,),
        in_specs=[pl.BlockSpec((n, tr, C), lambda i: (0, i, 0))],
        out_specs=pl.BlockSpec((tr, C), lambda i: (i, 0)),
        out_shape=jax.ShapeDtypeStruct((R, C), F32),
        compiler_params=_cparams("parallel"),
    )(parts)


def _cast_into_slot(w, entry):
    _, _, R, C = w.shape
    tr = _row_tile(R, C)

    def body(w_ref, o_ref):
        o_ref[...] = w_ref[...].astype(BF16)

    def own_slot(h, i):
        return 2 * lax.axis_index("x") + lax.axis_index("y"), h, i, 0

    return pl.pallas_call(
        body,
        name="cast_into_slot",
        grid=(2, R // tr),
        in_specs=[pl.BlockSpec((None, None, tr, C), lambda h, i: (entry, h, i, 0))],
        out_specs=pl.BlockSpec((None, None, tr, C), own_slot),
        out_shape=jax.ShapeDtypeStruct((N_CHIPS, 2, R, C), BF16),
        compiler_params=_cparams("parallel", "parallel"),
    )(w)


def _sum_with_own(arrived, own):
    n, R, C = arrived.shape
    tr = _row_tile(R, C * (n + 1) // 2)

    def body(a_ref, own_ref, o_ref):
        acc = own_ref[...].astype(F32)
        for s in range(n):
            acc = acc + a_ref[s].astype(F32)
        o_ref[...] = acc

    def own_piece(i):
        return 2 * lax.axis_index("x") + lax.axis_index("y"), lax.axis_index("c"), i, 0

    return pl.pallas_call(
        body,
        name="sum_with_own",
        grid=(R // tr,),
        in_specs=[pl.BlockSpec((n, tr, C), lambda i: (0, i, 0)), pl.BlockSpec((None, None, tr, C), own_piece)],
        out_specs=pl.BlockSpec((None, tr, C), lambda i: (lax.axis_index("c"), i, 0)),
        out_shape=jax.ShapeDtypeStruct((2, R, C), F32),
        compiler_params=_cparams("parallel"),
    )(arrived, own)


def _adamw(w, g, m, v):
    R, C = w.shape
    tr = _row_tile(R, C * 7 // 2)
    c1 = 1.0 - ADAM_B1 ** ADAM_STEP
    c2 = 1.0 - ADAM_B2 ** ADAM_STEP

    def body(w_ref, g_ref, m_ref, v_ref, d_ref, mo_ref, vo_ref):
        g = g_ref[...]
        mn = ADAM_B1 * m_ref[...] + (1.0 - ADAM_B1) * g
        vn = ADAM_B2 * v_ref[...] + (1.0 - ADAM_B2) * (g * g)
        mo_ref[...] = mn
        vo_ref[...] = vn
        d_ref[...] = -ADAM_LR * ((mn / c1) / (jnp.sqrt(vn / c2) + ADAM_EPS) + ADAM_WD * w_ref[...])

    blk = pl.BlockSpec((tr, C), lambda i: (i, 0))
    shp = jax.ShapeDtypeStruct((R, C), F32)
    return pl.pallas_call(
        body,
        name="adamw",
        grid=(R // tr,),
        in_specs=[blk, blk, blk, blk],
        out_specs=[blk, blk, blk],
        out_shape=[shp, shp, shp],
        compiler_params=_cparams("parallel"),
    )(w, g, m, v)


def _adamw_into(w, g, m, v, outs, sel):
    n, R, C = w.shape
    tr = _row_tile(R, C)
    c1 = 1.0 - ADAM_B1 ** ADAM_STEP
    c2 = 1.0 - ADAM_B2 ** ADAM_STEP

    def body(w_ref, g_ref, m_ref, v_ref, *rest):
        go_ref, d_ref, mo_ref, vo_ref = rest[-4:]
        g = g_ref[...]
        mn = ADAM_B1 * m_ref[...] + (1.0 - ADAM_B1) * g
        vn = ADAM_B2 * v_ref[...] + (1.0 - ADAM_B2) * (g * g)
        go_ref[...] = g
        mo_ref[...] = mn
        vo_ref[...] = vn
        d_ref[...] = -ADAM_LR * ((mn / c1) / (jnp.sqrt(vn / c2) + ADAM_EPS) + ADAM_WD * w_ref[...])

    entry = pl.BlockSpec((None, tr, C), lambda i: (sel, i, 0))
    hbm = pl.BlockSpec(memory_space=pl.ANY)
    have = outs is not None
    shp = jax.ShapeDtypeStruct((n, R, C), F32)
    return pl.pallas_call(
        body,
        name="adamw_into",
        grid=(R // tr,),
        in_specs=[entry, pl.BlockSpec((tr, C), lambda i: (i, 0)), entry, entry] + ([hbm] * 4 if have else []),
        out_specs=[entry] * 4,
        out_shape=[shp] * 4,
        input_output_aliases={4 + t: t for t in range(4)} if have else {},
        compiler_params=_cparams("parallel"),
    )(w, g, m, v, *(outs if have else ()))


def _gather_weights(halved, whole):
    nh, nw = len(halved), len(whole)

    def body(*refs):
        w_in = refs[nh:nh + nw]
        h_out, w_out = refs[nh + nw:2 * nh + nw], refs[2 * nh + nw:2 * (nh + nw)]
        ws_send, ws_recv, loc_sem = refs[2 * (nh + nw):2 * (nh + nw) + 3]
        plan = _Gather(h_out, refs[2 * (nh + nw) + 3:])
        x, y, c = _my_place()
        me_chip = 2 * x + y
        plan.start()

        def small(a, j, slot, to):
            return pltpu.make_async_remote_copy(src_ref=w_in[a], dst_ref=w_out[a].at[slot],
                                                send_sem=ws_send.at[a, j], recv_sem=ws_recv.at[a, j],
                                                device_id=to, device_id_type=MESH)

        for a in range(nw):
            pltpu.make_async_copy(w_in[a], w_out[a].at[me_chip], loc_sem.at[a]).start()
            for j, ch in enumerate(plan.chips):
                small(a, j, me_chip, (*ch, c)).start()
        plan.forward()
        plan.finish()
        for a in range(nw):
            for j, ch in enumerate(plan.chips):
                cp = small(a, j, 2 * ch[0] + ch[1], (x, y, c))
                cp.wait_recv()
                cp.wait_send()
            pltpu.make_async_copy(w_in[a], w_out[a].at[me_chip], loc_sem.at[a]).wait()

    hbm = pl.BlockSpec(memory_space=pl.ANY)
    outs = pl.pallas_call(
        body,
        name="gather_weights",
        in_specs=[hbm] * (nh + nw),
        out_specs=[hbm] * (nh + nw),
        out_shape=[jax.ShapeDtypeStruct(a.shape, a.dtype) for a in halved]
        + [jax.ShapeDtypeStruct((N_CHIPS, *a.shape), a.dtype) for a in whole],
        input_output_aliases={a: a for a in range(nh)},
        scratch_shapes=[pltpu.SemaphoreType.DMA((max(nw, 1), 3)), pltpu.SemaphoreType.DMA((max(nw, 1), 3)),
                        pltpu.SemaphoreType.DMA((max(nw, 1),))] + _Gather.semaphores(nh),
    )(*halved, *whole)
    return outs[:nh], outs[nh:]


def _scatter_grads(grads, halves, into):
    n = len(grads)

    def body(*refs):
        plan = _Scatter(refs[:n], refs[2 * n:3 * n], refs[3 * n:], halves)
        plan.start()
        plan.finish()

    hbm = pl.BlockSpec(memory_space=pl.ANY)
    return pl.pallas_call(
        body,
        name="scatter_grads",
        in_specs=[hbm] * (2 * n),
        out_specs=[hbm] * n,
        out_shape=[jax.ShapeDtypeStruct(t.shape, t.dtype) for t in into],
        input_output_aliases={n + a: a for a in range(n)},
        scratch_shapes=_Scatter.semaphores(n),
    )(*grads, *into)


def _swap_halves(halves):
    n = len(halves)

    def body(*refs):
        h_out = refs[n:2 * n]
        send_sem, recv_sem = refs[2 * n:]
        x, y, c = _my_place()
        sib = (x, y, 1 - c)
        for a in range(n):
            pltpu.make_async_remote_copy(src_ref=h_out[a].at[c], dst_ref=h_out[a].at[c], send_sem=send_sem.at[a],
                                         recv_sem=recv_sem.at[a], device_id=sib, device_id_type=MESH).start()
        for a in range(n):
            cp = pltpu.make_async_remote_copy(src_ref=h_out[a].at[c], dst_ref=h_out[a].at[1 - c],
                                              send_sem=send_sem.at[a], recv_sem=recv_sem.at[a], device_id=sib,
                                              device_id_type=MESH)
            cp.wait_send()
            cp.wait_recv()

    hbm = pl.BlockSpec(memory_space=pl.ANY)
    return pl.pallas_call(
        body,
        name="swap_halves",
        in_specs=[hbm] * n,
        out_specs=[hbm] * n,
        out_shape=[jax.ShapeDtypeStruct(h.shape, h.dtype) for h in halves],
        input_output_aliases={a: a for a in range(n)},
        scratch_shapes=[pltpu.SemaphoreType.DMA((n,)), pltpu.SemaphoreType.DMA((n,))],
    )(*halves)


TM_FFN = 512
TM_FFN_FWD = 1024
RB_FFN_FWD = 512
RB_FFN_BWD = 256
TM_SGU = 256
TM_SGU_FWD = 256
TM_CONV = 256
TK_WGRAD = 4096
TM_LOSS = 1024


FFN_KINDS = ("ff_w_gate", "ff_w_up", "ff_w_down")


def _layer_kinds(i):
    return FFN_KINDS + (("sgu_w_in", "sgu_w_out") if i % 2 == 0 else ("conv_w_pw1", "conv_w_pw2"))


def _local_step(x, target, G, W, exchange=None, pack_small=None):
    depth = W["norm_g"].shape[0]
    G = [dict(g) for g in G]
    saved = []
    vec = lambda v: v.reshape(1, -1)

    def mixer_w(i, k):
        w = G[i][k]
        return w.reshape(w.shape[0], -1, w.shape[-1])
    wsm, wsmt, bsb = [], [], []
    n_sgu = W["sgu_w_spatial"].shape[0]
    causal = jnp.tril(jnp.ones((CHUNK, CHUNK), dtype=bool))
    dgrp = W["sgu_ln_g"].shape[1] // N_GROUPS
    for jx in range(n_sgu):
        ws = jnp.where(causal[None], W["sgu_w_spatial"][jx], 0.0).astype(BF16)
        wsm.append(ws)
        wsmt.append(jnp.swapaxes(ws, 1, 2))
        bsb.append(jnp.broadcast_to(W["sgu_b_spatial"][jx][:, :, None], (N_GROUPS, CHUNK, dgrp)))
    kp = HALO
    wdw = [jnp.pad(W["conv_w_dw"][jx], ((0, kp - CONV_W), (0, 0))) for jx in range(W["conv_w_dw"].shape[0])]

    def ffn(x, i, f_idx, gather=()):
        g = W["norm_g"][i]
        return _ffn_fwd(x, vec(g[4 * f_idx]), vec(g[4 * f_idx + 1]), G[i]["ff_w_gate"], G[i]["ff_w_up"],
                        G[i]["ff_w_down"], (f_idx,), TM_FFN_FWD, gather)

    for i in range(depth):
        g = W["norm_g"][i]
        rec = {"x0": x}
        if exchange is not None and i + 1 < depth:
            kinds = _layer_kinds(i + 1)
            x, rec["a1"], rec["b1"], rec["f1"], *filled, rec["h1"] = ffn(x, i, 0, [G[i + 1][k] for k in kinds])
            G[i + 1] = dict(zip(kinds, filled))
        else:
            x, rec["a1"], rec["b1"], rec["f1"], rec["h1"] = ffn(x, i, 0)
        rec["x1"] = x
        j = i // 2
        if i % 2 == 0:
            x, rec["zp"], rec["m"] = _sgu_fwd(
                x, vec(g[2]), vec(g[3]), mixer_w(i, "sgu_w_in"), vec(W["sgu_ln_g"][j]), vec(W["sgu_ln_b"][j]),
                wsm[j], bsb[j], mixer_w(i, "sgu_w_out"), (), TM_SGU_FWD)
        else:
            rec["y"], rec["p"] = _conv_fwd_a(x, vec(g[2]), mixer_w(i, "conv_w_pw1"), (), TM_CONV)
            x, rec["c"], rec["m"] = _conv_fwd_b(
                x, rec["y"], wdw[j], vec(W["conv_b_dw"][j]), vec(W["conv_ln_g"][j]), vec(W["conv_ln_b"][j]),
                mixer_w(i, "conv_w_pw2"), vec(g[3]), (), TM_CONV)
        rec["x2"] = x
        if i + 1 < depth:
            x, rec["a2"], rec["b2"], rec["f2"], rec["h2"] = ffn(x, i, 1)
        else:
            g = W["norm_g"][i]
            x, rec["a2"], rec["b2"], rec["f2"], rec["h2"], dx, loss_tile = _ffn_fwd(
                x, vec(g[4]), vec(g[5]), G[i]["ff_w_gate"], G[i]["ff_w_up"], G[i]["ff_w_down"], (1,), TM_FFN,
                (), target)
        saved.append(rec)

    loss = loss_tile[0, 0]

    big = [{k: None for k in _layer_kinds(i)} for i in range(depth)]
    small = {k: [None] * W[k].shape[0] for k in
             ("sgu_ln_g", "sgu_ln_b", "sgu_w_spatial", "sgu_b_spatial", "conv_w_dw", "conv_b_dw", "conv_ln_g",
              "conv_ln_b")}
    dnorm = [[None] * 6 for _ in range(depth)]
    pieces = [None] * depth
    waiting = []

    def wgrad(i, k, a, b, sel, rider=None):
        like = G[i][k] if sel else mixer_w(i, k)
        out = _tn_matmul(a, b, big[i][k], like, sel, TK_WGRAD, rider)
        big[i][k], rode = out if rider else (out, None)
        return rode

    def as_pieces(b):
        return b.reshape(N_CHIPS, 2, -1, b.shape[-1])

    def small_grads():
        out = {k: jnp.stack(v) for k, v in small.items()}
        out["norm_g"] = jnp.stack([jnp.stack(r) for r in dnorm])
        return out

    def ffn_back(dx, i, f_idx, xin, a, b, f, h, send, last=False):
        g = W["norm_g"][i]
        dx, dz, s, da, db, dgpre, dgpost, *arrived = _ffn_bwd(
            dx, xin, f, a, b, vec(g[4 * f_idx]), vec(g[4 * f_idx + 1]),
            G[i]["ff_w_gate"], G[i]["ff_w_up"], G[i]["ff_w_down"], (f_idx,), TM_FFN, list(send.values()))
        dnorm[i][4 * f_idx] = dgpre[0]
        dnorm[i][4 * f_idx + 1] = dgpost[0]
        arrived = dict(zip(send, arrived))
        if not last:
            wgrad(i, "ff_w_gate", h, da, (f_idx,))
            wgrad(i, "ff_w_up", h, db, (f_idx,))
            wgrad(i, "ff_w_down", s, dz, (f_idx,))
            return dx, arrived, None
        (shared,) = wgrad(i, "ff_w_gate", h, da, (f_idx,), _ShareRider(pack_small(small_grads())))
        for k, nxt, lhs, rhs in (("ff_w_gate", "ff_w_up", h, db), ("ff_w_up", "ff_w_down", s, dz)):
            (arrived[k],) = wgrad(i, nxt, lhs, rhs, (f_idx,),
                                  _ScatterRider([as_pieces(big[i][k])], [(f_idx,)], [arrived[k]]))
        (arrived["ff_w_down"],) = _scatter_grads([as_pieces(big[i]["ff_w_down"])], [(f_idx,)], [arrived["ff_w_down"]])
        return dx, arrived, shared

    for i in reversed(range(depth)):
        rec = saved[i]
        g = W["norm_g"][i]
        j = i // 2
        if exchange is not None and waiting:
            sent = waiting.pop()
            dx, arrived, _ = ffn_back(dx, i, 1, rec["x2"], rec["a2"], rec["b2"], rec["f2"], rec["h2"],
                                      {k: (p, (0, 1)) for k, p in pieces[sent].items()})
            exchange(sent, pieces[sent], arrived)
        else:
            dx, _, _ = ffn_back(dx, i, 1, rec["x2"], rec["a2"], rec["b2"], rec["f2"], rec["h2"], {})
        if i % 2 == 0:
            (dx, hn, dzp, gated, dm, dws, dbs_acc, dlng, dlnb, dgpre, dgpost) = _sgu_bwd(
                dx, rec["x1"], rec["m"], rec["zp"], vec(g[2]), vec(g[3]), mixer_w(i, "sgu_w_in"),
                vec(W["sgu_ln_g"][j]), vec(W["sgu_ln_b"][j]), wsm[j], wsmt[j], bsb[j], mixer_w(i, "sgu_w_out"), (),
                TM_SGU)
            wgrad(i, "sgu_w_in", hn, dzp, ())
            wgrad(i, "sgu_w_out", gated, dm, ())
            small["sgu_w_spatial"][j] = jnp.where(causal[None], dws, 0.0)
            small["sgu_b_spatial"][j] = dbs_acc.reshape(CHUNK, N_GROUPS, dgrp).sum(-1).T
            small["sgu_ln_g"][j] = dlng[0]
            small["sgu_ln_b"][j] = dlnb[0]
        else:
            dm, q, dc, dlng, dlnb, dbdw, dgpost = _conv_bwd_b(
                dx, rec["m"], rec["c"], vec(W["conv_ln_g"][j]), vec(W["conv_ln_b"][j]), mixer_w(i, "conv_w_pw2"),
                vec(g[3]), (), TM_CONV)
            dx, hn, dp, dwdw, dgpre = _conv_bwd_a(
                dx, rec["x1"], dc, rec["y"], rec["p"], vec(g[2]), wdw[j], mixer_w(i, "conv_w_pw1"), (), TM_CONV)
            wgrad(i, "conv_w_pw1", hn, dp, ())
            wgrad(i, "conv_w_pw2", q, dm, ())
            small["conv_w_dw"][j] = dwdw[:CONV_W]
            small["conv_b_dw"][j] = dbdw[0]
            small["conv_ln_g"][j] = dlng[0]
            small["conv_ln_b"][j] = dlnb[0]
        dnorm[i][2] = dgpre[0]
        dnorm[i][3] = dgpost[0]
        if exchange is not None and i == 0:
            dx, arrived, shared = ffn_back(
                dx, i, 0, rec["x0"], rec["a1"], rec["b1"], rec["f1"], rec["h1"],
                {k: (as_pieces(b), (1,) if k in FFN_KINDS else (0, 1)) for k, b in big[i].items()}, last=True)
            pieces[i] = {k: as_pieces(b) for k, b in big[i].items()}
            exchange(i, pieces[i], arrived)
        else:
            dx, _, _ = ffn_back(dx, i, 0, rec["x0"], rec["a1"], rec["b1"], rec["f1"], rec["h1"], {})
            pieces[i] = {k: as_pieces(b) for k, b in big[i].items()}
            waiting.append(i)

    return loss, dx, pieces, (shared if exchange is not None else small_grads())


BIG = ("ff_w_gate", "ff_w_up", "ff_w_down", "sgu_w_in", "sgu_w_out", "conv_w_pw1", "conv_w_pw2")
SHARDED_SMALL = ("norm_g", "conv_w_dw", "conv_b_dw", "conv_ln_g", "conv_ln_b")
REPLICATED = ("sgu_ln_g", "sgu_ln_b", "sgu_w_spatial", "sgu_b_spatial")
WEIGHTS = ("norm_g", "ff_w_gate", "ff_w_up", "ff_w_down", "sgu_w_in", "sgu_ln_g", "sgu_ln_b", "sgu_w_spatial",
           "sgu_b_spatial", "sgu_w_out", "conv_w_pw1", "conv_w_dw", "conv_b_dw", "conv_ln_g", "conv_ln_b",
           "conv_w_pw2")


def _rows8(a, width):
    r = a.reshape(-1, width)
    pad = (-r.shape[0]) % 8
    return jnp.pad(r, ((0, pad), (0, 0))) if pad else r


def _pack(arrs, width):
    parts = [_rows8(a, width) for a in arrs]
    return jnp.concatenate(parts, axis=0), [p.shape[0] for p in parts]


def _unpack(buf, like):
    out, r0 = [], 0
    width = buf.shape[-1]
    for a in like:
        n = -(-(a.size // width) // 8) * 8
        rows = a.size // width
        out.append(buf[..., r0:r0 + rows, :].reshape(*buf.shape[:-2], *a.shape))
        r0 += n
    return out


def kernel(x, norm_g, ff_w_gate, ff_w_up, ff_w_down, sgu_w_in, sgu_ln_g, sgu_ln_b, sgu_w_spatial, sgu_b_spatial, sgu_w_out, conv_w_pw1, conv_w_dw, conv_b_dw, conv_ln_g, conv_ln_b, conv_w_pw2, loss_target, m_norm_g, m_ff_w_gate, m_ff_w_up, m_ff_w_down, m_sgu_w_in, m_sgu_ln_g, m_sgu_ln_b, m_sgu_w_spatial, m_sgu_b_spatial, m_sgu_w_out, m_conv_w_pw1, m_conv_w_dw, m_conv_b_dw, m_conv_ln_g, m_conv_ln_b, m_conv_w_pw2, v_norm_g, v_ff_w_gate, v_ff_w_up, v_ff_w_down, v_sgu_w_in, v_sgu_ln_g, v_sgu_ln_b, v_sgu_w_spatial, v_sgu_b_spatial, v_sgu_w_out, v_conv_w_pw1, v_conv_w_dw, v_conv_b_dw, v_conv_ln_g, v_conv_ln_b, v_conv_w_pw2):
    w = dict(norm_g=norm_g, ff_w_gate=ff_w_gate, ff_w_up=ff_w_up, ff_w_down=ff_w_down, sgu_w_in=sgu_w_in,
             sgu_ln_g=sgu_ln_g, sgu_ln_b=sgu_ln_b, sgu_w_spatial=sgu_w_spatial, sgu_b_spatial=sgu_b_spatial,
             sgu_w_out=sgu_w_out, conv_w_pw1=conv_w_pw1, conv_w_dw=conv_w_dw, conv_b_dw=conv_b_dw,
             conv_ln_g=conv_ln_g, conv_ln_b=conv_ln_b, conv_w_pw2=conv_w_pw2)
    mom = dict(norm_g=m_norm_g, ff_w_gate=m_ff_w_gate, ff_w_up=m_ff_w_up, ff_w_down=m_ff_w_down,
               sgu_w_in=m_sgu_w_in, sgu_ln_g=m_sgu_ln_g, sgu_ln_b=m_sgu_ln_b, sgu_w_spatial=m_sgu_w_spatial,
               sgu_b_spatial=m_sgu_b_spatial, sgu_w_out=m_sgu_w_out, conv_w_pw1=m_conv_w_pw1,
               conv_w_dw=m_conv_w_dw, conv_b_dw=m_conv_b_dw, conv_ln_g=m_conv_ln_g, conv_ln_b=m_conv_ln_b,
               conv_w_pw2=m_conv_w_pw2)
    vel = dict(norm_g=v_norm_g, ff_w_gate=v_ff_w_gate, ff_w_up=v_ff_w_up, ff_w_down=v_ff_w_down,
               sgu_w_in=v_sgu_w_in, sgu_ln_g=v_sgu_ln_g, sgu_ln_b=v_sgu_ln_b, sgu_w_spatial=v_sgu_w_spatial,
               sgu_b_spatial=v_sgu_b_spatial, sgu_w_out=v_sgu_w_out, conv_w_pw1=v_conv_w_pw1,
               conv_w_dw=v_conv_w_dw, conv_b_dw=v_conv_b_dw, conv_ln_g=v_conv_ln_g, conv_ln_b=v_conv_ln_b,
               conv_w_pw2=v_conv_w_pw2)
    T, D = x.shape[1], x.shape[2]
    shard_w = conv_b_dw.shape[1]

    xi, yi, ci = _my_place()
    me_chip = (2 * xi + yi).astype(jnp.int32)
    me = (4 * xi + 2 * yi + ci).astype(jnp.int32)

    depth = norm_g.shape[0]

    def entry(k, i):
        return i if k in FFN_KINDS else i // 2

    def stacked(a):
        return a.reshape(a.shape[0], -1, a.shape[-1])

    G = []
    for i in range(depth):
        G.append({k: _cast_into_slot(w[k].reshape(w[k].shape[0], 2, -1, w[k].shape[-1]), entry(k, i))
                  for k in _layer_kinds(i)})
    small_buf, _ = _pack([w[k] for k in SHARDED_SMALL], shard_w)
    first, (small_all,) = _gather_weights(list(G[0].values()), [small_buf])
    G[0] = dict(zip(G[0], first))
    W = {}
    for k, part in zip(SHARDED_SMALL, _unpack(small_all, [w[k] for k in SHARDED_SMALL])):
        W[k] = jnp.moveaxis(part, 0, -2).reshape(*w[k].shape[:-1], N_CHIPS * shard_w)
    for k in REPLICATED:
        W[k] = w[k]

    results = {k: None for k in BIG}

    def reduce_and_update(i, pieces, arrived):
        kinds = list(pieces)
        both = _swap_halves([_sum_with_own(arrived[k], pieces[k]) for k in kinds])
        for k, g in zip(kinds, both):
            results[k] = _adamw_into(stacked(w[k]), g.reshape(-1, g.shape[-1]), stacked(mom[k]), stacked(vel[k]),
                                     results[k], entry(k, i))

    def pack_small(small):
        sbuf, _ = _pack([small[k] for k in SHARDED_SMALL + REPLICATED], D)
        return lax.dynamic_update_slice(jnp.zeros((N_DEV, *sbuf.shape), F32), sbuf[None], (me, 0, 0))

    loss, dx, _, shared = _local_step(x[0], loss_target[0], G, W, reduce_and_update, pack_small)
    loss = lax.psum(loss, ("x", "y", "c"))
    grads, delta, new_m, new_v = {}, {}, {}, {}
    for k in BIG:
        grads[k], delta[k], new_m[k], new_v[k] = (t.reshape(w[k].shape) for t in results[k])

    ssum = _sum_parts(shared)
    for k, gfull in zip(SHARDED_SMALL + REPLICATED, _unpack(ssum, [W[k] for k in SHARDED_SMALL + REPLICATED])):
        if k in SHARDED_SMALL:
            gfull = lax.dynamic_slice_in_dim(gfull, me_chip * shard_w, shard_w, axis=gfull.ndim - 1)
        grads[k] = gfull

    for names, width in ((SHARDED_SMALL, shard_w), (REPLICATED, CHUNK)):
        packed = [_pack([src[k] for k in names], width)[0] for src in (w, grads, mom, vel)]
        outs = _adamw(*packed)
        for res, out in zip((delta, new_m, new_v), outs):
            for k, a in zip(names, _unpack(out, [w[k] for k in names])):
                res[k] = a

    return (loss, dx[None], *[grads[k] for k in WEIGHTS], *[delta[k] for k in WEIGHTS],
            *[new_m[k] for k in WEIGHTS], *[new_v[k] for k in WEIGHTS])
```
